```python
import math
import jax
import jax.numpy as jnp
from jax import lax
import numpy as np

D_MODEL = 2048
BATCH = 2
SEQ = 8192
DEPTH = 2

GRID_W = 64
CTX_LEN = 256
N_BRANCH = 4
BRANCH_W = D_MODEL // N_BRANCH
NORM_EPS = 1e-6
NEG_INF = -1e30
Q_BLOCK = 128

NA_DIM = 64
NA_HEADS = BRANCH_W // NA_DIM
NA_WIN_R = 8
NA_WIN_C = 16

RW_DIM = 64
RW_HEADS = BRANCH_W // RW_DIM
RW_LORA_W = 64
RW_LORA_A = 64
RW_GN_EPS = 64e-5
RW_SHIFT_W = 3 * BRANCH_W + 2 * RW_LORA_W + 2 * RW_LORA_A

SSM_HEAD_DIM = 64
SSM_HEADS = BRANCH_W // SSM_HEAD_DIM
SSM_GROUPS = 2
SSM_STATE = 128
SSM_CONV = 3
SSM_CHUNK = 128
SSM_CONV_CH = BRANCH_W + 2 * SSM_GROUPS * SSM_STATE

DA_DIM = 64
DA_HEADS = BRANCH_W // (2 * DA_DIM)
ROPE_BASE = 10000.0

BRANCH_COLS = (
    4 * BRANCH_W,
    RW_SHIFT_W + BRANCH_W,
    SSM_CONV_CH + 2 * SSM_HEADS + BRANCH_W,
    4 * BRANCH_W,
)
IN_W = sum(BRANCH_COLS)

kernel_name = 'hybrid_diffusion_trunk'

F32 = jnp.float32


def split_cols(p, sizes):
    idx = np.cumsum(sizes)[:-1].tolist()
    return jnp.split(p, idx, axis=-1)


def rms_norm(x, w, eps=NORM_EPS):
    xf = x.astype(F32)
    y = xf * lax.rsqrt(jnp.mean(xf * xf, axis=-1, keepdims=True) + eps)
    return (y * w).astype(x.dtype)


def grid_positions(length):
    t = jnp.arange(length, dtype=jnp.int32)
    return t // GRID_W, t % GRID_W


def axial_rope(x, rows, cols):
    d = x.shape[-1]
    nf = d // 4
    inv = ROPE_BASE ** (-jnp.arange(nf, dtype=F32) / nf)
    ang_r = rows.astype(F32)[:, None] * inv
    ang_c = cols.astype(F32)[:, None] * inv
    ang = jnp.concatenate([ang_r, ang_r, ang_c, ang_c], axis=-1)
    shape = (1, x.shape[1]) + (1,) * (x.ndim - 3) + (d,)
    cos = jnp.cos(ang).reshape(shape)
    sin = jnp.sin(ang).reshape(shape)
    xf = x.astype(F32)
    x1, x2, x3, x4 = jnp.split(xf, 4, axis=-1)
    rot = jnp.concatenate([-x2, x1, -x4, x3], axis=-1)
    return (xf * cos + rot * sin).astype(x.dtype)


def dense_softmax_attn(q, k, v, scale):
    s = jnp.einsum('bqhd,bkhd->bhqk', q, k).astype(F32) * scale
    p = jax.nn.softmax(s, axis=-1).astype(v.dtype)
    return jnp.einsum('bhqk,bkhe->bqhe', p, v)


def centred_token_shift(x, mu):
    xp = jnp.pad(x, ((0, 0), (1, 1), (0, 0)))
    nb = 0.5 * (xp[:, :-2] + xp[:, 2:])
    return x + (nb - x) * mu


def centred_depthwise_conv(x, w, b):
    y = lax.conv_general_dilated(x, w[:, None, :], window_strides=(1,), padding='SAME',
                                 dimension_numbers=('NWC', 'WIO', 'NWC'),
                                 feature_group_count=x.shape[-1])
    return y + b


def neighbourhood_attention(pc, px, q_norm_w, k_norm_w, rpb, need_ctx_out):
    def heads(p):
        b, l, _ = p.shape
        qkv, g = split_cols(p, (3 * BRANCH_W, BRANCH_W))
        qkv = qkv.reshape(b, l, 3, NA_HEADS, NA_DIM)
        return rms_norm(qkv[:, :, 0], q_norm_w), rms_norm(qkv[:, :, 1], k_norm_w), qkv[:, :, 2], g

    qc, kc, vc, gc = heads(pc)
    qx, kx, vx, gx = heads(px)
    scale = NA_DIM ** -0.5
    B, L = px.shape[:2]
    n_rows = L // GRID_W
    wr = min(NA_WIN_R, n_rows)
    grid = (B, n_rows, GRID_W, NA_HEADS, NA_DIM)
    qg, kg, vg = qx.reshape(grid), kx.reshape(grid), vx.reshape(grid)
    r = jnp.arange(n_rows)
    r0 = jnp.clip(r - wr // 2, 0, n_rows - wr)
    row_idx = r0[:, None] + jnp.arange(wr)[None, :]
    kb = kg[:, row_idx]
    vb = vg[:, row_idx]
    col = jnp.arange(GRID_W)
    c0 = jnp.clip(col - NA_WIN_C // 2, 0, GRID_W - NA_WIN_C)
    col_ok = (col[None, :] >= c0[:, None]) & (col[None, :] < c0[:, None] + NA_WIN_C)
    d_row = row_idx - r[:, None] + (NA_WIN_R - 1)
    d_col = jnp.clip(col[None, :] - col[:, None] + (NA_WIN_C - 1), 0, 2 * NA_WIN_C - 2)
    bias = rpb[:, d_row[:, None, :, None], d_col[None, :, None, :]]
    s_loc = jnp.einsum('brqhd,bruwhd->bhrquw', qg, kb).astype(F32) * scale + bias[None].astype(F32)
    s_loc = jnp.where(col_ok[:, None, :], s_loc, NEG_INF)
    s_ctx = jnp.einsum('brqhd,bchd->bhrqc', qg, kc).astype(F32) * scale
    n_loc = wr * GRID_W
    s = jnp.concatenate([s_loc.reshape(B, NA_HEADS, n_rows, GRID_W, n_loc), s_ctx], axis=-1)
    p = jax.nn.softmax(s, axis=-1).astype(vx.dtype)
    p_loc = p[..., :n_loc].reshape(B, NA_HEADS, n_rows, GRID_W, wr, GRID_W)
    p_ctx = p[..., n_loc:]
    ox = jnp.einsum('bhrquw,bruwhd->brqhd', p_loc, vb) + jnp.einsum('bhrqc,bchd->brqhd', p_ctx, vc)
    ox = ox.reshape(B, L, BRANCH_W) * jax.nn.silu(gx)
    oc = None
    if need_ctx_out:
        oc = dense_softmax_attn(qc, kc, vc, scale).reshape(B, pc.shape[1], BRANCH_W) * jax.nn.silu(gc)
    return oc, ox


def rwkv7_scan(s0, r, w, k, v, a, b, reverse, emit):
    def step(S, inp):
        r_t, w_t, k_t, v_t, a_t, b_t = inp
        sa = jnp.einsum('bhvk,bhk->bhv', S, a_t)
        S = S * w_t[:, :, None, :] + sa[..., None] * b_t[:, :, None, :] + v_t[..., None] * k_t[:, :, None, :]
        y = jnp.einsum('bhvk,bhk->bhv', S, r_t) if emit else None
        return S, y

    xs = tuple(jnp.moveaxis(t, 1, 0) for t in (r, w, k, v, a, b))
    s_fin, ys = lax.scan(step, s0, xs, reverse=reverse)
    return s_fin, (jnp.moveaxis(ys, 0, 1) if emit else None)


def rwkv7_time_mix(pc, px, mu, w0, w2, a0, a2, k_k, k_a, r_k, ln_w, ln_b, need_ctx_out):
    def heads(t):
        return t.reshape(t.shape[:-1] + (RW_HEADS, RW_DIM))

    def prep(p):
        b, l, _ = p.shape
        core, g = split_cols(p, (RW_SHIFT_W, BRANCH_W))
        core = centred_token_shift(core, mu)
        r, k, v, wd, ad = split_cols(core, (BRANCH_W, BRANCH_W, BRANCH_W, 2 * RW_LORA_W, 2 * RW_LORA_A))
        wd = jnp.tanh(wd.reshape(b, l, 2, RW_LORA_W))
        ad = ad.reshape(b, l, 2, RW_LORA_A)
        w_log = (w0 + jnp.einsum('bldr,drc->bldc', wd, w2)).astype(F32)
        decay = jnp.exp(-jnp.exp(-jax.nn.softplus(-w_log) - 0.5))
        a = jax.nn.sigmoid((a0 + jnp.einsum('bldr,drc->bldc', ad, a2)).astype(F32))
        kk = heads((k * k_k).astype(F32))
        kk = kk / jnp.maximum(jnp.sqrt(jnp.sum(kk * kk, axis=-1, keepdims=True)), 1e-12)
        kd = k.astype(F32)[:, :, None] * (1.0 + (a - 1.0) * k_a)
        return heads(r.astype(F32)), heads(v.astype(F32)), kk, heads(decay), heads(kd), heads(a), g

    def finish(y, r, kd, v, g):
        b, l = y.shape[:2]
        mean = jnp.mean(y, axis=-1, keepdims=True)
        var = jnp.mean(jnp.square(y - mean), axis=-1, keepdims=True)
        yn = ((y - mean) * lax.rsqrt(var + RW_GN_EPS)).reshape(b, l, BRANCH_W) * ln_w + ln_b
        bonus = jnp.sum(r[:, :, None] * kd * r_k, axis=(2, 4))[..., None] * v
        out = (yn + bonus.reshape(b, l, BRANCH_W)) * jax.nn.silu(g.astype(F32))
        return out.astype(g.dtype)

    rc, vc, kkc, wc, kdc, ac, gc = prep(pc)
    rx, vx, kkx, wx, kdx, ax, gx = prep(px)
    s_zero = jnp.zeros((px.shape[0], RW_HEADS, RW_DIM, RW_DIM), F32)
    y_x, y_c = None, None
    for d in range(2):
        rev = d == 1
        s_ctx, yc_d = rwkv7_scan(s_zero, rc, wc[:, :, d], kdc[:, :, d], vc, -kkc, kkc * ac[:, :, d], rev, need_ctx_out)
        _, yx_d = rwkv7_scan(s_ctx, rx, wx[:, :, d], kdx[:, :, d], vx, -kkx, kkx * ax[:, :, d], rev, True)
        y_x = yx_d if d == 0 else y_x + yx_d
        if need_ctx_out:
            y_c = yc_d if d == 0 else y_c + yc_d
    ox = finish(y_x, rx, kdx, vx, gx)
    oc = finish(y_c, rc, kdc, vc, gc) if need_ctx_out else None
    return oc, ox


def ssd_chunked(x, dt, A, Bm, Cm, s0, emit):
    b, l, H, P = x.shape
    N = Bm.shape[-1]
    Q = SSM_CHUNK
    nc = l // Q
    xq = (x * dt[..., None]).reshape(b, nc, Q, H, P)
    a_cum = jnp.cumsum((dt * A).reshape(b, nc, Q, H), axis=2)
    Bq = Bm.reshape(b, nc, Q, H, N)
    Cq = Cm.reshape(b, nc, Q, H, N)
    decay_end = jnp.exp(a_cum[:, :, -1:] - a_cum)
    states = jnp.einsum('bcjhn,bcjh,bcjhp->bchpn', Bq, decay_end, xq)
    chunk_decay = jnp.exp(a_cum[:, :, -1])

    def step(s, inp):
        st, dec = inp
        return s * dec[:, :, None, None] + st, s

    s_fin, s_start = lax.scan(step, s0, (jnp.moveaxis(states, 1, 0), jnp.moveaxis(chunk_decay, 1, 0)))
    if not emit:
        return None, s_fin
    s_start = jnp.moveaxis(s_start, 0, 1)
    i = jnp.arange(Q)
    causal = i[:, None] >= i[None, :]
    seg = a_cum[:, :, :, None, :] - a_cum[:, :, None, :, :]
    l_mat = jnp.exp(jnp.where(causal[:, :, None], seg, NEG_INF))
    g = jnp.einsum('bcihn,bcjhn->bcijh', Cq, Bq) * l_mat
    y = jnp.einsum('bcijh,bcjhp->bcihp', g, xq)
    y = y + jnp.einsum('bcihn,bchpn->bcihp', Cq, s_start) * jnp.exp(a_cum)[..., None]
    return y.reshape(b, l, H, P), s_fin


def mamba2_mixer(pc, px, conv_w, conv_b, dt_bias, A_log, D_skip, norm_w, need_ctx_out):
    A = -jnp.exp(A_log.astype(F32))
    rep = SSM_HEADS // SSM_GROUPS

    def prep(p):
        b, l, _ = p.shape
        xbc, dt, z = split_cols(p, (SSM_CONV_CH, 2 * SSM_HEADS, BRANCH_W))
        xbc = jax.nn.silu(centred_depthwise_conv(xbc, conv_w, conv_b)).astype(F32)
        xs, Bm, Cm = split_cols(xbc, (BRANCH_W, SSM_GROUPS * SSM_STATE, SSM_GROUPS * SSM_STATE))
        Bm = jnp.repeat(Bm.reshape(b, l, SSM_GROUPS, SSM_STATE), rep, axis=2)
        Cm = jnp.repeat(Cm.reshape(b, l, SSM_GROUPS, SSM_STATE), rep, axis=2)
        dt = jax.nn.softplus(dt.astype(F32).reshape(b, l, 2, SSM_HEADS) + dt_bias)
        return xs.reshape(b, l, SSM_HEADS, SSM_HEAD_DIM), dt, Bm, Cm, z

    def finish(y, z):
        b, l = y.shape[:2]
        g = y.reshape(b, l, BRANCH_W) * jax.nn.silu(z.astype(F32))
        g = rms_norm(g.reshape(b, l, SSM_GROUPS, BRANCH_W // SSM_GROUPS), norm_w.reshape(SSM_GROUPS, -1))
        return g.reshape(b, l, BRANCH_W).astype(z.dtype)

    xc, dtc, Bc, Cc, zc = prep(pc)
    xx, dtx, Bx, Cx, zx = prep(px)
    s_zero = jnp.zeros((px.shape[0], SSM_HEADS, SSM_HEAD_DIM, SSM_STATE), F32)
    y_x = D_skip[:, None] * xx
    y_c = D_skip[:, None] * xc if need_ctx_out else None
    for d in range(2):
        orient = (lambda t: t) if d == 0 else (lambda t: jnp.flip(t, axis=1))
        yc_d, s_ctx = ssd_chunked(orient(xc), orient(dtc[:, :, d]), A[d], orient(Bc), orient(Cc), s_zero, need_ctx_out)
        yx_d, _ = ssd_chunked(orient(xx), orient(dtx[:, :, d]), A[d], orient(Bx), orient(Cx), s_ctx, True)
        y_x = y_x + orient(yx_d)
        if need_ctx_out:
            y_c = y_c + orient(yc_d)
    ox = finish(y_x, zx)
    oc = finish(y_c, zc) if need_ctx_out else None
    return oc, ox


def diff_attention(pc, px, q_norm_w, k_norm_w, lq1, lk1, lq2, lk2, subln_w, lam_init, need_ctx_out):
    def heads(p):
        b, l, _ = p.shape
        q, k, v, g = split_cols(p, (BRANCH_W, BRANCH_W, BRANCH_W, BRANCH_W))
        q = rms_norm(q.reshape(b, l, DA_HEADS, 2, DA_DIM), q_norm_w)
        k = rms_norm(k.reshape(b, l, DA_HEADS, 2, DA_DIM), k_norm_w)
        return q, k, v.reshape(b, l, DA_HEADS, 2 * DA_DIM), g

    qc, kc, vc, gc = heads(pc)
    qx, kx, vx, gx = heads(px)
    B, L = px.shape[:2]
    rows, cols = grid_positions(L)
    qx = axial_rope(qx, rows, cols)
    kx = axial_rope(kx, rows, cols)
    lam = (jnp.exp(jnp.sum(lq1.astype(F32) * lk1.astype(F32)))
           - jnp.exp(jnp.sum(lq2.astype(F32) * lk2.astype(F32))) + lam_init)
    scale = DA_DIM ** -0.5

    def attend(q, k, v):
        s = jnp.einsum('bqhcd,bkhcd->bhcqk', q, k).astype(F32) * scale
        p = jax.nn.softmax(s, axis=-1)
        p = p[:, :, 0] - lam * p[:, :, 1]
        return jnp.einsum('bhqk,bkhe->bqhe', p.astype(v.dtype), v)

    def finish(o, g):
        b, l = o.shape[:2]
        o = rms_norm(o, subln_w) * (1.0 - lam_init)
        return o.reshape(b, l, BRANCH_W) * jax.nn.silu(g)

    k_all = jnp.concatenate([kc, kx], axis=1)
    v_all = jnp.concatenate([vc, vx], axis=1)
    q_blocks = jnp.moveaxis(qx.reshape(B, L // Q_BLOCK, Q_BLOCK, DA_HEADS, 2, DA_DIM), 1, 0)
    ox = lax.map(lambda qb: attend(qb, k_all, v_all), q_blocks)
    ox = jnp.moveaxis(ox, 0, 1).reshape(B, L, DA_HEADS, 2 * DA_DIM)
    oc = finish(attend(qc, kc, vc), gc) if need_ctx_out else None
    return oc, finish(ox, gx)


def merge_branches(h, ys, w_gate, w_up, w_out):
    acc = jax.nn.sigmoid(h @ w_gate[0]) * (ys[0] @ w_up[0])
    for i in range(1, N_BRANCH):
        acc = acc + jax.nn.sigmoid(h @ w_gate[i]) * (ys[i] @ w_up[i])
    return acc @ w_out


def setup_inputs(seed: int = 0) -> dict:
    key = jax.random.key(seed)
    keys = iter(jax.random.split(key, 64))
    D, W, Lr = D_MODEL, BRANCH_W, DEPTH

    def normal(shape, scale):
        return jax.random.normal(next(keys), shape, F32) * scale

    def gain(shape):
        return 1.0 + normal(shape, 0.02)

    dt0 = jnp.exp(jax.random.uniform(next(keys), (Lr, 2, SSM_HEADS), F32, math.log(1e-3), math.log(1e-1)))
    return {
        'x': normal((BATCH, SEQ, D), 1.0),
        'c': normal((BATCH, D), 1.0),
        'ctx': normal((BATCH, CTX_LEN, D), 1.0),
        'c_ctx': normal((D,), 1.0),
        'norm_w': gain((Lr, D)),
        'w_ada': normal((Lr, D, 3 * D), 0.5 * D ** -0.5),
        'b_ada': normal((Lr, 3 * D), 0.01),
        'w_in': normal((Lr, D, IN_W), D ** -0.5),
        'na_q_norm': gain((Lr, NA_DIM)),
        'na_k_norm': gain((Lr, NA_DIM)),
        'na_rpb': normal((Lr, NA_HEADS, 2 * NA_WIN_R - 1, 2 * NA_WIN_C - 1), 0.02),
        'rw_mu': jax.random.uniform(next(keys), (Lr, RW_SHIFT_W), F32, 0.2, 0.8),
        'rw_w0': jax.random.uniform(next(keys), (Lr, 2, W), F32, -6.0, -1.0),
        'rw_w2': normal((Lr, 2, RW_LORA_W, W), 0.1),
        'rw_a0': normal((Lr, 2, W), 0.1),
        'rw_a2': normal((Lr, 2, RW_LORA_A, W), RW_LORA_A ** -0.5),
        'rw_k_k': 0.85 + normal((Lr, W), 0.02),
        'rw_k_a': gain((Lr, W)),
        'rw_r_k': normal((Lr, RW_HEADS, RW_DIM), 0.1),
        'rw_ln_w': gain((Lr, W)),
        'rw_ln_b': normal((Lr, W), 0.01),
        'ssm_conv_w': normal((Lr, SSM_CONV, SSM_CONV_CH), SSM_CONV ** -0.5),
        'ssm_conv_b': normal((Lr, SSM_CONV_CH), 0.01),
        'ssm_dt_bias': dt0 + jnp.log(-jnp.expm1(-dt0)),
        'ssm_A_log': jnp.log(jax.random.uniform(next(keys), (Lr, 2, SSM_HEADS), F32, 1.0, 16.0)),
        'ssm_D': gain((Lr, SSM_HEADS)),
        'ssm_norm_w': gain((Lr, W)),
        'da_q_norm': gain((Lr, DA_DIM)),
        'da_k_norm': gain((Lr, DA_DIM)),
        'da_lq1': normal((Lr, DA_DIM), 0.1),
        'da_lk1': normal((Lr, DA_DIM), 0.1),
        'da_lq2': normal((Lr, DA_DIM), 0.1),
        'da_lk2': normal((Lr, DA_DIM), 0.1),
        'da_subln': gain((Lr, 2 * DA_DIM)),
        'w_gate': normal((Lr, N_BRANCH, D, D), D ** -0.5),
        'w_up': normal((Lr, N_BRANCH, W, D), W ** -0.5),
        'w_out': normal((Lr, D, D), D ** -0.5),
    }


def reference(x, c, ctx, c_ctx, norm_w, w_ada, b_ada, w_in,
              na_q_norm, na_k_norm, na_rpb,
              rw_mu, rw_w0, rw_w2, rw_a0, rw_a2, rw_k_k, rw_k_a, rw_r_k, rw_ln_w, rw_ln_b,
              ssm_conv_w, ssm_conv_b, ssm_dt_bias, ssm_A_log, ssm_D, ssm_norm_w,
              da_q_norm, da_k_norm, da_lq1, da_lk1, da_lq2, da_lk2, da_subln,
              w_gate, w_up, w_out):
    c_act = jax.nn.silu(c)
    cc_act = jax.nn.silu(c_ctx)
    xc = ctx
    for l in range(DEPTH):
        need_ctx = l < DEPTH - 1
        lam_init = 0.8 - 0.6 * math.exp(-0.3 * l)
        shift, scale, gate = jnp.split(c_act @ w_ada[l] + b_ada[l], 3, axis=-1)
        shift_c, scale_c, gate_c = jnp.split(cc_act @ w_ada[l] + b_ada[l], 3, axis=-1)
        h = rms_norm(x, norm_w[l]) * (1.0 + scale[:, None]) + shift[:, None]
        hc = rms_norm(xc, norm_w[l]) * (1.0 + scale_c) + shift_c
        pa_x, pb_x, pm_x, pd_x = split_cols(h @ w_in[l], BRANCH_COLS)
        pa_c, pb_c, pm_c, pd_c = split_cols(hc @ w_in[l], BRANCH_COLS)
        oa_c, oa_x = neighbourhood_attention(pa_c, pa_x, na_q_norm[l], na_k_norm[l], na_rpb[l], need_ctx)
        ob_c, ob_x = rwkv7_time_mix(pb_c, pb_x, rw_mu[l], rw_w0[l], rw_w2[l], rw_a0[l], rw_a2[l],
                                    rw_k_k[l], rw_k_a[l], rw_r_k[l], rw_ln_w[l], rw_ln_b[l], need_ctx)
        om_c, om_x = mamba2_mixer(pm_c, pm_x, ssm_conv_w[l], ssm_conv_b[l], ssm_dt_bias[l], ssm_A_log[l],
                                  ssm_D[l], ssm_norm_w[l], need_ctx)
        od_c, od_x = diff_attention(pd_c, pd_x, da_q_norm[l], da_k_norm[l], da_lq1[l], da_lk1[l],
                                    da_lq2[l], da_lk2[l], da_subln[l], lam_init, need_ctx)
        x = x + gate[:, None] * merge_branches(h, (oa_x, ob_x, om_x, od_x), w_gate[l], w_up[l], w_out[l])
        if need_ctx:
            xc = xc + gate_c * merge_branches(hc, (oa_c, ob_c, om_c, od_c), w_gate[l], w_up[l], w_out[l])
    return x
```

```python
import functools
import math

import numpy as np
import jax
import jax.numpy as jnp
from jax import lax
from jax.experimental import pallas as pl
from jax.experimental.pallas import tpu as pltpu

F32 = jnp.float32
BF16 = jnp.bfloat16

D_MODEL = 2048
DEPTH = 2
GRID_W = 64
CTX_LEN = 256
N_BRANCH = 4
BRANCH_W = D_MODEL // N_BRANCH
NORM_EPS = 1e-6
NEG_INF = -1e30

NA_DIM = 64
NA_HEADS = BRANCH_W // NA_DIM
NA_WIN_R = 8
NA_WIN_C = 16

RW_DIM = 64
RW_HEADS = BRANCH_W // RW_DIM
RW_LORA_W = 64
RW_LORA_A = 64
RW_GN_EPS = 64e-5
RW_SHIFT_W = 3 * BRANCH_W + 2 * RW_LORA_W + 2 * RW_LORA_A
RW_CHUNK = 64

SSM_HEAD_DIM = 64
SSM_HEADS = BRANCH_W // SSM_HEAD_DIM
SSM_GROUPS = 2
SSM_STATE = 128
SSM_CHUNK = 128
SSM_CONV_CH = BRANCH_W + 2 * SSM_GROUPS * SSM_STATE

DA_DIM = 64
DA_HEADS = BRANCH_W // (2 * DA_DIM)
ROPE_BASE = 10000.0

COL_A = 0
COL_B = COL_A + 4 * BRANCH_W
COL_XBC = COL_B + RW_SHIFT_W + BRANCH_W
COL_Z = COL_XBC + SSM_CONV_CH
COL_D = COL_Z + BRANCH_W
COL_DT = COL_D + 4 * BRANCH_W
IN_W_PAD = COL_DT + 128

V7X_VMEM_BYTES = 64 * 1024 * 1024
VMEM_LIMIT = V7X_VMEM_BYTES * 7 // 8


def _cparams(sem):
    return pltpu.CompilerParams(dimension_semantics=sem, vmem_limit_bytes=VMEM_LIMIT)


def _dot(a, b):
    return jnp.dot(a, b, preferred_element_type=F32)


def _dot_nt(a, b):
    return lax.dot_general(a, b, (((1,), (1,)), ((), ())), preferred_element_type=F32)


def _dot_tn(a, b):
    return lax.dot_general(a, b, (((0,), (0,)), ((), ())), preferred_element_type=F32)


def _split3(x):
    hi = x.astype(BF16)
    r1 = x - hi.astype(F32)
    mid = r1.astype(BF16)
    lo = (r1 - mid.astype(F32)).astype(BF16)
    return hi, mid, lo


def _dot_exact_lhs(m_bf16, x):
    hi, mid, lo = _split3(x)
    return _dot(m_bf16, hi) + _dot(m_bf16, mid) + _dot(m_bf16, lo)


def _dot_exact_rhs(x, m_bf16):
    hi, mid, lo = _split3(x)
    return _dot(hi, m_bf16) + _dot(mid, m_bf16) + _dot(lo, m_bf16)


def _dot_nt_exact_lhs_f32(x, m_bf16):
    hi, mid, lo = _split3(x)
    return _dot_nt(hi, m_bf16) + _dot_nt(mid, m_bf16) + _dot_nt(lo, m_bf16)


def _sigmoid(x):
    return 1.0 / (1.0 + jnp.exp(-x))


def _silu(x):
    return x * _sigmoid(x)


def _small_mm_kernel(a_ref, w_ref, b_ref, o_ref):
    o_ref[...] = _dot(a_ref[...].astype(BF16), w_ref[...].astype(BF16)) + b_ref[...]


def small_matmul(a, w, b, tn=512):
    m, k = a.shape
    n = w.shape[1]
    return pl.pallas_call(
        _small_mm_kernel,
        grid=(n // tn,),
        in_specs=[pl.BlockSpec((m, k), lambda j: (0, 0)),
                  pl.BlockSpec((k, tn), lambda j: (0, j)),
                  pl.BlockSpec((1, tn), lambda j: (0, j))],
        out_specs=pl.BlockSpec((m, tn), lambda j: (0, j)),
        out_shape=jax.ShapeDtypeStruct((m, n), F32),
        compiler_params=_cparams(("arbitrary",)),
        name="adaln_mm",
    )(a, w, b.reshape(1, n))


def _inproj_kernel(x_ref, nw_ref, sc_ref, sh_ref, w_ref, p_ref, h_ref, hs_ref):
    @pl.when(pl.program_id(1) == 0)
    def _():
        xf = x_ref[...]
        ms = jnp.mean(xf * xf, axis=-1, keepdims=True)
        y = xf * lax.rsqrt(ms + NORM_EPS) * nw_ref[...]
        h = (y * (1.0 + sc_ref[0]) + sh_ref[0]).astype(BF16)
        hs_ref[...] = h
        h_ref[...] = h

    p_ref[...] = _dot(hs_ref[...], w_ref[...])


def in_projection(x2, norm_w, scale, shift, w_bf16, rows_per_batch, tm, tn):
    m, d = x2.shape
    n = w_bf16.shape[1]
    tpb = rows_per_batch // tm
    nb = scale.shape[0]
    return pl.pallas_call(
        _inproj_kernel,
        grid=(m // tm, n // tn),
        in_specs=[pl.BlockSpec((tm, d), lambda i, j: (i, 0)),
                  pl.BlockSpec((1, d), lambda i, j: (0, 0)),
                  pl.BlockSpec((1, 1, d), lambda i, j: (i // tpb, 0, 0)),
                  pl.BlockSpec((1, 1, d), lambda i, j: (i // tpb, 0, 0)),
                  pl.BlockSpec((d, tn), lambda i, j: (0, j))],
        out_specs=[pl.BlockSpec((tm, tn), lambda i, j: (i, j)),
                   pl.BlockSpec((tm, d), lambda i, j: (i, 0))],
        out_shape=[jax.ShapeDtypeStruct((m, n), F32),
                   jax.ShapeDtypeStruct((m, d), BF16)],
        scratch_shapes=[pltpu.VMEM((tm, d), BF16)],
        compiler_params=_cparams(("arbitrary", "arbitrary")),
        name="in_proj",
    )(x2, norm_w.reshape(1, d), scale.reshape(nb, 1, d), shift.reshape(nb, 1, d), w_bf16)


def _gate_up_kernel(h_ref, ya_ref, yb_ref, yc_ref, yd_ref, wg_ref, wu_ref, o_ref):
    h = h_ref[...]
    acc = None
    for i, y_ref in enumerate((ya_ref, yb_ref, yc_ref, yd_ref)):
        g = _dot(h, wg_ref[i])
        u = _dot(y_ref[...].astype(BF16), wu_ref[i])
        t = _sigmoid(g) * u
        acc = t if acc is None else acc + t
    o_ref[...] = acc.astype(BF16)


def gate_up(h, ys, wg_bf16, wu_bf16, tm, tn):
    m, d = h.shape
    w = ys[0].shape[1]
    y_spec = pl.BlockSpec((tm, w), lambda i, j: (i, 0))
    return pl.pallas_call(
        _gate_up_kernel,
        grid=(m // tm, d // tn),
        in_specs=[pl.BlockSpec((tm, d), lambda i, j: (i, 0)), y_spec, y_spec, y_spec, y_spec,
                  pl.BlockSpec((N_BRANCH, d, tn), lambda i, j: (0, 0, j)),
                  pl.BlockSpec((N_BRANCH, w, tn), lambda i, j: (0, 0, j))],
        out_specs=pl.BlockSpec((tm, tn), lambda i, j: (i, j)),
        out_shape=jax.ShapeDtypeStruct((m, d), BF16),
        compiler_params=_cparams(("arbitrary", "arbitrary")),
        name="gate_up",
    )(h, *ys, wg_bf16, wu_bf16)


def _out_proj_kernel(a_ref, w_ref, x_ref, g_ref, o_ref):
    o_ref[...] = x_ref[...] + g_ref[0] * _dot(a_ref[...], w_ref[...])


def out_projection(acc, w_bf16, x2, gate, rows_per_batch, tm, tn):
    m, d = x2.shape
    tpb = rows_per_batch // tm
    nb = gate.shape[0]
    return pl.pallas_call(
        _out_proj_kernel,
        grid=(m // tm, d // tn),
        in_specs=[pl.BlockSpec((tm, d), lambda i, j: (i, 0)),
                  pl.BlockSpec((d, tn), lambda i, j: (0, j)),
                  pl.BlockSpec((tm, tn), lambda i, j: (i, j)),
                  pl.BlockSpec((1, 1, tn), lambda i, j: (i // tpb, 0, j))],
        out_specs=pl.BlockSpec((tm, tn), lambda i, j: (i, j)),
        out_shape=jax.ShapeDtypeStruct((m, d), F32),
        compiler_params=_cparams(("arbitrary", "arbitrary")),
        name="out_proj",
    )(acc, w_bf16, x2, gate.reshape(nb, 1, d))


def _flash_kernel(lam_ref, q_ref, k_ref, v_ref, o_ref, *, nh, ncomp, dv, tk, nk):
    tq = q_ref.shape[1]
    for h in range(nh):
        comps = []
        for c in range(ncomp):
            col = (h * ncomp + c) * 64
            qh = q_ref[0, :, col:col + 64]

            def body(i, carry, col=col, qh=qh, h=h):
                m, l, acc = carry
                start = pl.multiple_of(i * tk, tk)
                kk = k_ref[0, pl.ds(start, tk), col:col + 64]
                vv = v_ref[0, pl.ds(start, tk), h * dv:(h + 1) * dv]
                s = _dot_nt(qh, kk)
                m_new = jnp.maximum(m, jnp.max(s, axis=-1, keepdims=True))
                alpha = jnp.exp(m - m_new)
                p = jnp.exp(s - m_new)
                l = alpha * l + jnp.sum(p, axis=-1, keepdims=True)
                acc = alpha * acc + _dot(p.astype(BF16), vv)
                return m_new, l, acc

            init = (jnp.full((tq, 1), NEG_INF, F32), jnp.zeros((tq, 1), F32), jnp.zeros((tq, dv), F32))
            _, l, acc = lax.fori_loop(0, nk, body, init)
            comps.append(acc / l)
        o = comps[0] if ncomp == 1 else comps[0] - lam_ref[0] * comps[1]
        o_ref[0, :, h * dv:(h + 1) * dv] = o


def flash_attention(q, k, v, lam, nh, ncomp, dv, tq, tk):
    b, lq, wq = q.shape
    lk = k.shape[1]
    wv = v.shape[2]
    kern = functools.partial(_flash_kernel, nh=nh, ncomp=ncomp, dv=dv, tk=tk, nk=lk // tk)
    return pl.pallas_call(
        kern,
        grid=(b, lq // tq),
        in_specs=[pl.BlockSpec(memory_space=pltpu.SMEM),
                  pl.BlockSpec((1, tq, wq), lambda bi, qi: (bi, qi, 0)),
                  pl.BlockSpec((1, lk, wq), lambda bi, qi: (bi, 0, 0)),
                  pl.BlockSpec((1, lk, wv), lambda bi, qi: (bi, 0, 0))],
        out_specs=pl.BlockSpec((1, tq, wv), lambda bi, qi: (bi, qi, 0)),
        out_shape=jax.ShapeDtypeStruct((b, lq, wv), F32),
        compiler_params=_cparams(("arbitrary", "arbitrary")),
        name="flash_attn",
    )(lam.reshape(1).astype(F32), q, k, v)


def _na_kernel(q_ref, k_ref, v_ref, kc_ref, vc_ref, bias_ref, g_ref, o_ref, *, n_rows):
    r = pl.program_id(1)
    r0 = jnp.clip(r - NA_WIN_R // 2, 0, n_rows - NA_WIN_R)
    start = pl.multiple_of(r0 * GRID_W, GRID_W)
    n_loc = NA_WIN_R * GRID_W
    g = g_ref[0]
    for h in range(NA_HEADS):
        sl = slice(h * NA_DIM, (h + 1) * NA_DIM)
        qh = q_ref[0, :, sl]
        kw = k_ref[0, pl.ds(start, n_loc), sl]
        vw = v_ref[0, pl.ds(start, n_loc), sl]
        s_loc = _dot_nt(qh, kw) + bias_ref[0, h]
        s_ctx = _dot_nt(qh, kc_ref[0, :, sl])
        m = jnp.maximum(jnp.max(s_loc, axis=-1, keepdims=True), jnp.max(s_ctx, axis=-1, keepdims=True))
        p_loc = jnp.exp(s_loc - m)
        p_ctx = jnp.exp(s_ctx - m)
        l = jnp.sum(p_loc, axis=-1, keepdims=True) + jnp.sum(p_ctx, axis=-1, keepdims=True)
        o = _dot(p_loc.astype(BF16), vw) + _dot(p_ctx.astype(BF16), vc_ref[0, :, sl])
        o_ref[0, :, sl] = (o / l) * _silu(g[:, sl])


def neighbourhood_attention_x(q, k, v, kc, vc, bias_tbl, g):
    b, l, w = q.shape
    n_rows = l // GRID_W
    lc = kc.shape[1]
    half = NA_WIN_R // 2

    def bias_idx(bi, r):
        return (r - jnp.clip(r - half, 0, n_rows - NA_WIN_R), 0, 0, 0)

    row_spec = pl.BlockSpec((1, GRID_W, w), lambda bi, r: (bi, r, 0))
    full_spec = pl.BlockSpec((1, l, w), lambda bi, r: (bi, 0, 0))
    ctx_spec = pl.BlockSpec((1, lc, w), lambda bi, r: (bi, 0, 0))
    return pl.pallas_call(
        functools.partial(_na_kernel, n_rows=n_rows),
        grid=(b, n_rows),
        in_specs=[row_spec, full_spec, full_spec, ctx_spec, ctx_spec,
                  pl.BlockSpec((1, NA_HEADS, GRID_W, NA_WIN_R * GRID_W), bias_idx),
                  row_spec],
        out_specs=row_spec,
        out_shape=jax.ShapeDtypeStruct((b, l, w), F32),
        compiler_params=_cparams(("arbitrary", "arbitrary")),
        name="nbr_attn",
    )(q, k, v, kc, vc, bias_tbl, g)


def na_bias_table(rpb):
    col = np.arange(GRID_W)
    c0 = np.clip(col - NA_WIN_C // 2, 0, GRID_W - NA_WIN_C)
    col_ok = (col[None, :] >= c0[:, None]) & (col[None, :] < c0[:, None] + NA_WIN_C)
    d_col = np.clip(col[None, :] - col[:, None] + (NA_WIN_C - 1), 0, 2 * NA_WIN_C - 2)
    o = np.arange(NA_WIN_R)
    u = np.arange(NA_WIN_R)
    d_row = u[None, :] - o[:, None] + (NA_WIN_R - 1)
    bias = rpb[:, d_row[:, None, :, None], d_col[None, :, None, :]]
    bias = jnp.where(jnp.asarray(col_ok)[None, None, :, None, :], bias.astype(F32), NEG_INF)
    bias = jnp.transpose(bias, (1, 0, 2, 3, 4))
    return bias.reshape(NA_WIN_R, NA_HEADS, GRID_W, NA_WIN_R * GRID_W)


def _rwkv_kernel(tri_ref, ms_ref, mi_ref, r_ref, v_ref, a_ref, lw_ref, kd_ref, b_ref, y_ref, s_ref):
    c = RW_CHUNK

    @pl.when(pl.program_id(2) == 0)
    def _():
        s_ref[...] = jnp.zeros_like(s_ref)

    tri = tri_ref[0]
    m_strict = ms_ref[0] > 0.5
    m_incl = mi_ref[0] > 0.5
    lw = lw_ref[0, 0]
    cum = _dot_exact_lhs(tri, lw)
    tot = jnp.sum(lw, axis=0, keepdims=True)
    e_p = jnp.exp(cum)
    e_m = jnp.exp(-cum)
    e_e = jnp.exp(cum - lw)
    e_t = jnp.exp(tot - cum)
    e_tot = jnp.exp(tot)
    a_in = a_ref[0]
    b_in = b_ref[0, 0]
    k_in = kd_ref[0, 0]
    alpha = (a_in * e_e).astype(BF16)
    rho_f = r_ref[0] * e_p
    rho = rho_f.astype(BF16)
    beta = (b_in * e_m).astype(BF16)
    kappa = (k_in * e_m).astype(BF16)
    beta_c = (b_in * e_t).astype(BF16)
    kappa_c = (k_in * e_t).astype(BF16)
    v_b = v_ref[0].astype(BF16)
    rows = lax.broadcasted_iota(jnp.int32, (c, c), 0)
    cols = lax.broadcasted_iota(jnp.int32, (c, c), 1)
    eye = (rows == cols).astype(F32)
    same_half = (rows >= c // 2) == (cols >= c // 2)

    for h in range(RW_HEADS):
        sl = slice(h * RW_DIM, (h + 1) * RW_DIM)
        al, rh, be, ka, vh = alpha[:, sl], rho[:, sl], beta[:, sl], kappa[:, sl], v_b[:, sl]
        l_ab = jnp.where(m_strict, _dot_nt(al, be), 0.0)
        l_ak = jnp.where(m_strict, _dot_nt(al, ka), 0.0)
        t_rb = jnp.where(m_incl, _dot_nt(rh, be), 0.0).astype(BF16)
        t_rk = jnp.where(m_incl, _dot_nt(rh, ka), 0.0).astype(BF16)
        l_d = jnp.where(same_half, l_ab, 0.0)
        l_o = (l_ab - l_d).astype(BF16)
        pw = l_d.astype(BF16)
        td = eye + l_d
        for _ in range(4):
            pw = _dot(pw, pw).astype(BF16)
            td = td + _dot(td.astype(BF16), pw)
        td_b = td.astype(BF16)
        tinv_b = (td + _dot(_dot(td_b, l_o).astype(BF16), td_b)).astype(BF16)
        a_hat = _dot(tinv_b, al).astype(BF16)
        v_hat = _dot(tinv_b, _dot(l_ak.astype(BF16), vh).astype(BF16)).astype(BF16)
        r_hat = (rho_f[:, sl] + _dot(t_rb, a_hat)).astype(BF16)
        y_hat = _dot(t_rb, v_hat) + _dot(t_rk, vh)
        q_mat = _dot_tn(a_hat, beta_c[:, sl]).astype(BF16)
        n_mat = _dot_tn(v_hat, beta_c[:, sl]) + _dot_tn(vh, kappa_c[:, sl])
        s0 = s_ref[h]
        s0_b = s0.astype(BF16)
        y_ref[0, 0, :, sl] = _dot_nt(r_hat, s0_b) + y_hat
        s_ref[h] = s0 * e_tot[:, sl] + _dot(s0_b, q_mat) + n_mat


def _scan_chunk_index(d, s, n_ctx_chunks, n_chunks):
    rev = jnp.where(s < n_ctx_chunks, n_ctx_chunks - 1 - s, n_chunks + n_ctx_chunks - 1 - s)
    return jnp.where(d == 0, s, rev)


def _direction_masks(c):
    i = np.arange(c)
    lower = (i[None, :] <= i[:, None]).astype(np.float32)
    tri = np.stack([lower, lower.T])
    strict = np.stack([lower - np.eye(c, dtype=np.float32), lower.T - np.eye(c, dtype=np.float32)])
    return tri, strict


def rwkv_scan(r, v, a, lw, kd, b, n_ctx):
    bsz, lt, w = r.shape
    c = RW_CHUNK
    nck = lt // c
    ncc = n_ctx // c
    tri, strict = _direction_masks(c)
    shared = pl.BlockSpec((1, c, w), lambda d, bi, s: (bi, _scan_chunk_index(d, s, ncc, nck), 0))
    perdir = pl.BlockSpec((1, 1, c, w), lambda d, bi, s: (d, bi, _scan_chunk_index(d, s, ncc, nck), 0))
    mask_spec = pl.BlockSpec((1, c, c), lambda d, bi, s: (d, 0, 0))
    return pl.pallas_call(
        _rwkv_kernel,
        grid=(2, bsz, nck),
        in_specs=[mask_spec, mask_spec, mask_spec, shared, shared, shared, perdir, perdir, perdir],
        out_specs=perdir,
        out_shape=jax.ShapeDtypeStruct((2, bsz, lt, w), F32),
        scratch_shapes=[pltpu.VMEM((RW_HEADS, RW_DIM, RW_DIM), F32)],
        compiler_params=_cparams(("arbitrary", "arbitrary", "arbitrary")),
        name="rwkv_scan",
    )(jnp.asarray(tri, BF16), jnp.asarray(strict, F32), jnp.asarray(tri, F32), r, v, a, lw, kd, b)


def _ssd_kernel(tri_ref, mi_ref, xq_ref, b_ref, c_ref, bt_ref, a_ref, at_ref, y_ref, s_ref):
    q = SSM_CHUNK
    rep = SSM_HEADS // SSM_GROUPS

    @pl.when(pl.program_id(2) == 0)
    def _():
        s_ref[...] = jnp.zeros_like(s_ref)

    tri = tri_ref[0]
    mask = mi_ref[0] > 0.5
    a = a_ref[0, 0]
    a_t = at_ref[0, 0]
    xq = xq_ref[0, 0]
    bm = b_ref[0].astype(BF16)
    cm = c_ref[0].astype(BF16)
    bt = bt_ref[0].astype(BF16)
    for gi in range(SSM_GROUPS):
        gs = slice(gi * SSM_STATE, (gi + 1) * SSM_STATE)
        cb = _dot_nt(cm[:, gs], bm[:, gs])
        for hh in range(rep):
            h = gi * rep + hh
            a_col = jnp.broadcast_to(a[:, h:h + 1], (q, q))
            ac = _dot_exact_lhs(tri, a_col)
            tot = jnp.sum(a_col, axis=0, keepdims=True)
            ar = _dot_nt_exact_lhs_f32(jnp.broadcast_to(a_t[h:h + 1, :], (q, q)), tri)
            lmat = jnp.exp(jnp.where(mask, ac - ar, NEG_INF))
            gmat = (cb * lmat).astype(BF16)
            sl = slice(h * SSM_HEAD_DIM, (h + 1) * SSM_HEAD_DIM)
            xh = xq[:, sl]
            ac64 = ac[:, :SSM_HEAD_DIM]
            tot64 = tot[:, :SSM_HEAD_DIM]
            s0 = s_ref[h]
            y = _dot(gmat, xh.astype(BF16)) + _dot(cm[:, gs], s0.astype(BF16)) * jnp.exp(ac64)
            y_ref[0, 0, :, sl] = y
            xd = (xh * jnp.exp(tot64 - ac64)).astype(BF16)
            s_ref[h] = s0 * jnp.exp(tot64) + _dot(bt[gs, :], xd)


def ssd_scan(xq, bm, cm, a, n_ctx):
    _, bsz, lt, w = xq.shape
    q = SSM_CHUNK
    nck = lt // q
    ncc = n_ctx // q
    gw = bm.shape[-1]
    tri, _ = _direction_masks(q)
    bt = jnp.swapaxes(bm, 1, 2)
    a_t = jnp.swapaxes(a, 2, 3)

    def cidx(d, s):
        return _scan_chunk_index(d, s, ncc, nck)

    mask_spec = pl.BlockSpec((1, q, q), lambda d, bi, s: (d, 0, 0))
    return pl.pallas_call(
        _ssd_kernel,
        grid=(2, bsz, nck),
        in_specs=[mask_spec, mask_spec,
                  pl.BlockSpec((1, 1, q, w), lambda d, bi, s: (d, bi, cidx(d, s), 0)),
                  pl.BlockSpec((1, q, gw), lambda d, bi, s: (bi, cidx(d, s), 0)),
                  pl.BlockSpec((1, q, gw), lambda d, bi, s: (bi, cidx(d, s), 0)),
                  pl.BlockSpec((1, gw, q), lambda d, bi, s: (bi, 0, cidx(d, s))),
                  pl.BlockSpec((1, 1, q, SSM_HEADS), lambda d, bi, s: (d, bi, cidx(d, s), 0)),
                  pl.BlockSpec((1, 1, SSM_HEADS, q), lambda d, bi, s: (d, bi, 0, cidx(d, s)))],
        out_specs=pl.BlockSpec((1, 1, q, w), lambda d, bi, s: (d, bi, cidx(d, s), 0)),
        out_shape=jax.ShapeDtypeStruct((2, bsz, lt, w), F32),
        scratch_shapes=[pltpu.VMEM((SSM_HEADS, SSM_STATE, SSM_HEAD_DIM), F32)],
        compiler_params=_cparams(("arbitrary", "arbitrary", "arbitrary")),
        name="ssd_scan",
    )(jnp.asarray(tri, BF16), jnp.asarray(tri, F32), xq, bm, cm, bt, a, a_t)


def _rms_norm(x, w, eps=NORM_EPS):
    y = x * lax.rsqrt(jnp.mean(x * x, axis=-1, keepdims=True) + eps)
    return y * w


def _head_norm(p, w, dim):
    shp = p.shape
    return _rms_norm(p.reshape(shp[:-1] + (shp[-1] // dim, dim)), w).reshape(shp)


def _axial_rope(x, length):
    d = x.shape[-1]
    nf = d // 4
    t = jnp.arange(length, dtype=jnp.int32)
    rows, cols = t // GRID_W, t % GRID_W
    inv = ROPE_BASE ** (-jnp.arange(nf, dtype=F32) / nf)
    ang_r = rows.astype(F32)[:, None] * inv
    ang_c = cols.astype(F32)[:, None] * inv
    ang = jnp.concatenate([ang_r, ang_r, ang_c, ang_c], axis=-1)
    cos = jnp.cos(ang)[None, :, None, :]
    sin = jnp.sin(ang)[None, :, None, :]
    x1, x2, x3, x4 = jnp.split(x, 4, axis=-1)
    rot = jnp.concatenate([-x2, x1, -x4, x3], axis=-1)
    return x * cos + rot * sin


def _token_shift(x, mu):
    xp = jnp.pad(x, ((0, 0), (1, 1), (0, 0)))
    nb = 0.5 * (xp[:, :-2] + xp[:, 2:])
    return x + (nb - x) * mu


def _depthwise_conv3(x, w, b):
    xp = jnp.pad(x, ((0, 0), (1, 1), (0, 0)))
    return xp[:, :-2] * w[0] + xp[:, 1:-1] * w[1] + xp[:, 2:] * w[2] + b


def _mixer_a(pc, px, q_norm_w, k_norm_w, rpb, need_ctx):
    w = BRANCH_W
    scale = NA_DIM ** -0.5

    def prep(p):
        q = (_head_norm(p[..., :w], q_norm_w, NA_DIM) * scale).astype(BF16)
        k = _head_norm(p[..., w:2 * w], k_norm_w, NA_DIM).astype(BF16)
        return q, k, p[..., 2 * w:3 * w].astype(BF16), p[..., 3 * w:4 * w]

    qc, kc, vc, gc = prep(pc)
    qx, kx, vx, gx = prep(px)
    ox = neighbourhood_attention_x(qx, kx, vx, kc, vc, na_bias_table(rpb), gx)
    oc = None
    if need_ctx:
        oc = flash_attention(qc, kc, vc, jnp.zeros((), F32), NA_HEADS, 1, NA_DIM, CTX_LEN, CTX_LEN)
        oc = oc * _silu(gc)
    return oc, ox


def _mixer_b(pc, px, mu, w0, w2, a0, a2, k_k, k_a, r_k, ln_w, ln_b, need_ctx):
    w = BRANCH_W
    n_ctx = pc.shape[1]

    def prep(p):
        bsz, l, _ = p.shape
        core = _token_shift(p[..., :RW_SHIFT_W], mu)
        g = p[..., RW_SHIFT_W:]
        r, k, v = core[..., :w], core[..., w:2 * w], core[..., 2 * w:3 * w]
        wd = jnp.tanh(core[..., 3 * w:3 * w + 2 * RW_LORA_W].reshape(bsz, l, 2, RW_LORA_W))
        ad = core[..., 3 * w + 2 * RW_LORA_W:].reshape(bsz, l, 2, RW_LORA_A)
        w_log = w0 + jnp.einsum('bldr,drc->bldc', wd, w2)
        log_decay = -jnp.exp(-jax.nn.softplus(-w_log) - 0.5)
        a = jax.nn.sigmoid(a0 + jnp.einsum('bldr,drc->bldc', ad, a2))
        kk = (k * k_k).reshape(bsz, l, RW_HEADS, RW_DIM)
        kk = kk / jnp.maximum(jnp.sqrt(jnp.sum(kk * kk, axis=-1, keepdims=True)), 1e-12)
        kk = kk.reshape(bsz, l, w)
        kd = k[:, :, None] * (1.0 + (a - 1.0) * k_a)
        return r, v, kk, log_decay, kd, a, g

    rc, vc, kkc, lwc, kdc, ac, gc = prep(pc)
    rx, vx, kkx, lwx, kdx, ax, gx = prep(px)
    cat = lambda c_, x_: jnp.concatenate([c_, x_], axis=1)
    r = cat(rc, rx)
    v = cat(vc, vx)
    kk = cat(kkc, kkx)
    dirs = lambda t: jnp.moveaxis(t, 2, 0)
    lw = dirs(cat(lwc, lwx))
    kd = dirs(cat(kdc, kdx))
    a = dirs(cat(ac, ax))
    y = rwkv_scan(r, v, -kk, lw, kd, kk[None] * a, n_ctx)
    y = y[0] + y[1]

    def finish(y, r, kd, v, g):
        bsz, l = y.shape[:2]
        yh = y.reshape(bsz, l, RW_HEADS, RW_DIM)
        mean = jnp.mean(yh, axis=-1, keepdims=True)
        var = jnp.mean(jnp.square(yh - mean), axis=-1, keepdims=True)
        yn = ((yh - mean) * lax.rsqrt(var + RW_GN_EPS)).reshape(bsz, l, w) * ln_w + ln_b
        rh = r.reshape(bsz, l, 1, RW_HEADS, RW_DIM)
        kdh = kd.reshape(2, bsz, l, RW_HEADS, RW_DIM)
        kdh = jnp.moveaxis(kdh, 0, 2)
        bonus = jnp.sum(rh * kdh * r_k, axis=(2, 4))[..., None] * v.reshape(bsz, l, RW_HEADS, RW_DIM)
        return (yn + bonus.reshape(bsz, l, w)) * _silu(g)

    ox = finish(y[:, n_ctx:], rx, kd[:, :, n_ctx:], vx, gx)
    oc = finish(y[:, :n_ctx], rc, kd[:, :, :n_ctx], vc, gc) if need_ctx else None
    return oc, ox


def _mixer_c(xbc_c, z_c, dt_c, xbc_x, z_x, dt_x, conv_w, conv_b, dt_bias, a_log, d_skip, norm_w, need_ctx):
    w = BRANCH_W
    n_ctx = xbc_c.shape[1]
    a_neg = -jnp.exp(a_log)
    gs = SSM_GROUPS * SSM_STATE

    def prep(xbc, dt):
        bsz, l, _ = xbc.shape
        xbc = _silu(_depthwise_conv3(xbc, conv_w, conv_b))
        dt = jax.nn.softplus(dt.reshape(bsz, l, 2, SSM_HEADS) + dt_bias)
        return xbc[..., :w], xbc[..., w:w + gs], xbc[..., w + gs:], dt

    xc, bc, cc, dtc = prep(xbc_c, dt_c)
    xx, bx, cx, dtx = prep(xbc_x, dt_x)
    cat = lambda c_, x_: jnp.concatenate([c_, x_], axis=1)
    xs = cat(xc, xx)
    dt = jnp.moveaxis(cat(dtc, dtx), 2, 0)
    bsz, lt = xs.shape[:2]
    xq = xs.reshape(1, bsz, lt, SSM_HEADS, SSM_HEAD_DIM) * dt[..., None]
    xq = xq.reshape(2, bsz, lt, w)
    a = dt * a_neg[:, None, None, :]
    y = ssd_scan(xq, cat(bc, bx), cat(cc, cx), a, n_ctx)
    y = y[0] + y[1] + (xs.reshape(bsz, lt, SSM_HEADS, SSM_HEAD_DIM) * d_skip[:, None]).reshape(bsz, lt, w)

    def finish(y, z):
        bsz, l = y.shape[:2]
        g = y * _silu(z)
        g = _rms_norm(g.reshape(bsz, l, SSM_GROUPS, w // SSM_GROUPS), norm_w.reshape(SSM_GROUPS, -1))
        return g.reshape(bsz, l, w)

    ox = finish(y[:, n_ctx:], z_x)
    oc = finish(y[:, :n_ctx], z_c) if need_ctx else None
    return oc, ox


def _mixer_d(pc, px, q_norm_w, k_norm_w, lq1, lk1, lq2, lk2, subln_w, lam_init, need_ctx):
    w = BRANCH_W
    scale = DA_DIM ** -0.5
    lam = jnp.exp(jnp.sum(lq1 * lk1)) - jnp.exp(jnp.sum(lq2 * lk2)) + lam_init

    def prep(p, rope):
        bsz, l, _ = p.shape
        q = _head_norm(p[..., :w], q_norm_w, DA_DIM)
        k = _head_norm(p[..., w:2 * w], k_norm_w, DA_DIM)
        if rope:
            q = _axial_rope(q.reshape(bsz, l, 2 * DA_HEADS, DA_DIM), l).reshape(bsz, l, w)
            k = _axial_rope(k.reshape(bsz, l, 2 * DA_HEADS, DA_DIM), l).reshape(bsz, l, w)
        return (q * scale).astype(BF16), k.astype(BF16), p[..., 2 * w:3 * w].astype(BF16), p[..., 3 * w:]

    qc, kc, vc, gc = prep(pc, False)
    qx, kx, vx, gx = prep(px, True)

    def finish(o, g):
        bsz, l = o.shape[:2]
        o = _rms_norm(o.reshape(bsz, l, DA_HEADS, 2 * DA_DIM), subln_w) * (1.0 - lam_init)
        return o.reshape(bsz, l, w) * _silu(g)

    k_all = jnp.concatenate([kc, kx], axis=1)
    v_all = jnp.concatenate([vc, vx], axis=1)
    ox = flash_attention(qx, k_all, v_all, lam, DA_HEADS, 2, 2 * DA_DIM, 256, 384)
    oc = None
    if need_ctx:
        oc = finish(flash_attention(qc, kc, vc, lam, DA_HEADS, 2, 2 * DA_DIM, CTX_LEN, CTX_LEN), gc)
    return oc, finish(ox, gx)


def _permute_w_in(w_in_l):
    old_b = 4 * BRANCH_W
    old_c = old_b + RW_SHIFT_W + BRANCH_W
    old_dt = old_c + SSM_CONV_CH
    old_z = old_dt + 2 * SSM_HEADS
    old_d = old_z + BRANCH_W
    old_end = old_d + 4 * BRANCH_W
    perm = np.concatenate([np.arange(0, old_dt), np.arange(old_z, old_end), np.arange(old_dt, old_z)])
    wp = w_in_l[:, perm]
    return jnp.pad(wp, ((0, 0), (0, IN_W_PAD - wp.shape[1]))).astype(BF16)


def kernel(x, c, ctx, c_ctx, norm_w, w_ada, b_ada, w_in, na_q_norm, na_k_norm, na_rpb, rw_mu, rw_w0, rw_w2, rw_a0, rw_a2, rw_k_k, rw_k_a, rw_r_k, rw_ln_w, rw_ln_b, ssm_conv_w, ssm_conv_b, ssm_dt_bias, ssm_A_log, ssm_D, ssm_norm_w, da_q_norm, da_k_norm, da_lq1, da_lk1, da_lq2, da_lk2, da_subln, w_gate, w_up, w_out):
    bsz, seq, d = x.shape
    n_ctx = ctx.shape[1]
    depth = w_in.shape[0]
    cond = jnp.concatenate([_silu(c), _silu(c_ctx)[None], jnp.zeros((8 - bsz - 1, d), F32)], axis=0)
    x2 = x.reshape(bsz * seq, d)
    xc2 = ctx.reshape(bsz * n_ctx, d)
    for l in range(depth):
        need_ctx = l < depth - 1
        lam_init = 0.8 - 0.6 * math.exp(-0.3 * l)
        mod = small_matmul(cond, w_ada[l], b_ada[l])
        shift, scale, gate = jnp.split(mod[:bsz], 3, axis=-1)
        shift_c, scale_c, gate_c = [jnp.broadcast_to(t, (bsz, d)) for t in jnp.split(mod[bsz:bsz + 1], 3, axis=-1)]
        w_in_b = _permute_w_in(w_in[l])
        wg_b = w_gate[l].astype(BF16)
        wu_b = w_up[l].astype(BF16)
        wo_b = w_out[l].astype(BF16)

        px, hx = in_projection(x2, norm_w[l], scale, shift, w_in_b, seq, 512, 1152)
        pc, hc = in_projection(xc2, norm_w[l], scale_c, shift_c, w_in_b, n_ctx, 256, 1152)
        px = px.reshape(bsz, seq, IN_W_PAD)
        pc = pc.reshape(bsz, n_ctx, IN_W_PAD)

        def cols(p, lo, width):
            return p[..., lo:lo + width]

        oa_c, oa_x = _mixer_a(cols(pc, COL_A, 4 * BRANCH_W), cols(px, COL_A, 4 * BRANCH_W),
                              na_q_norm[l], na_k_norm[l], na_rpb[l], need_ctx)
        ob_c, ob_x = _mixer_b(cols(pc, COL_B, RW_SHIFT_W + BRANCH_W), cols(px, COL_B, RW_SHIFT_W + BRANCH_W),
                              rw_mu[l], rw_w0[l], rw_w2[l], rw_a0[l], rw_a2[l], rw_k_k[l], rw_k_a[l],
                              rw_r_k[l], rw_ln_w[l], rw_ln_b[l], need_ctx)
        om_c, om_x = _mixer_c(cols(pc, COL_XBC, SSM_CONV_CH), cols(pc, COL_Z, BRANCH_W), cols(pc, COL_DT, 2 * SSM_HEADS),
                              cols(px, COL_XBC, SSM_CONV_CH), cols(px, COL_Z, BRANCH_W), cols(px, COL_DT, 2 * SSM_HEADS),
                              ssm_conv_w[l], ssm_conv_b[l], ssm_dt_bias[l], ssm_A_log[l], ssm_D[l],
                              ssm_norm_w[l], need_ctx)
        od_c, od_x = _mixer_d(cols(pc, COL_D, 4 * BRANCH_W), cols(px, COL_D, 4 * BRANCH_W),
                              da_q_norm[l], da_k_norm[l], da_lq1[l], da_lk1[l], da_lq2[l], da_lk2[l],
                              da_subln[l], lam_init, need_ctx)

        ys_x = [t.reshape(bsz * seq, BRANCH_W) for t in (oa_x, ob_x, om_x, od_x)]
        acc_x = gate_up(hx, ys_x, wg_b, wu_b, 512, 512)
        x2 = out_projection(acc_x, wo_b, x2, gate, seq, 512, 512)
        if need_ctx:
            ys_c = [t.reshape(bsz * n_ctx, BRANCH_W) for t in (oa_c, ob_c, om_c, od_c)]
            acc_c = gate_up(hc, ys_c, wg_b, wu_b, 256, 512)
            xc2 = out_projection(acc_c, wo_b, xc2, gate_c, n_ctx, 256, 512)
    return x2.reshape(bsz, seq, d)
```

```python
import functools
import math

import numpy as np
import jax
import jax.numpy as jnp
from jax import lax
from jax.experimental import pallas as pl
from jax.experimental.pallas import tpu as pltpu

F32 = jnp.float32
BF16 = jnp.bfloat16

D_MODEL = 2048
DEPTH = 2
GRID_W = 64
CTX_LEN = 256
N_BRANCH = 4
BRANCH_W = D_MODEL // N_BRANCH
NORM_EPS = 1e-6
NEG_INF = -1e30

NA_DIM = 64
NA_HEADS = BRANCH_W // NA_DIM
NA_WIN_R = 8
NA_WIN_C = 16

RW_DIM = 64
RW_HEADS = BRANCH_W // RW_DIM
RW_LORA_W = 64
RW_LORA_A = 64
RW_GN_EPS = 64e-5
RW_SHIFT_W = 3 * BRANCH_W + 2 * RW_LORA_W + 2 * RW_LORA_A
RW_CHUNK = 64
RW_NSUB = 2

SSM_HEAD_DIM = 64
SSM_HEADS = BRANCH_W // SSM_HEAD_DIM
SSM_GROUPS = 2
SSM_STATE = 128
SSM_CHUNK = 128
SSM_CONV_CH = BRANCH_W + 2 * SSM_GROUPS * SSM_STATE

DA_DIM = 64
DA_HEADS = BRANCH_W // (2 * DA_DIM)
ROPE_BASE = 10000.0
DA_TQ = 256
DA_TK = 384
DA_NSUB = 2

COL_A = 0
COL_B = COL_A + 4 * BRANCH_W
COL_XBC = COL_B + RW_SHIFT_W + BRANCH_W
COL_Z = COL_XBC + SSM_CONV_CH
COL_D = COL_Z + BRANCH_W
COL_DT = COL_D + 4 * BRANCH_W
IN_W_PAD = COL_DT + 128

V7X_VMEM_BYTES = 64 * 1024 * 1024
VMEM_LIMIT = V7X_VMEM_BYTES * 7 // 8


def _cparams(sem):
    return pltpu.CompilerParams(dimension_semantics=sem, vmem_limit_bytes=VMEM_LIMIT)


def _dot(a, b):
    return jnp.dot(a, b, preferred_element_type=F32)


def _dot_nt(a, b):
    return lax.dot_general(a, b, (((1,), (1,)), ((), ())), preferred_element_type=F32)


def _dot_tn(a, b):
    return lax.dot_general(a, b, (((0,), (0,)), ((), ())), preferred_element_type=F32)


def _split3(x):
    hi = x.astype(BF16)
    r1 = x - hi.astype(F32)
    mid = r1.astype(BF16)
    lo = (r1 - mid.astype(F32)).astype(BF16)
    return hi, mid, lo


def _dot_exact_lhs(m_bf16, x):
    hi, mid, lo = _split3(x)
    return _dot(m_bf16, hi) + _dot(m_bf16, mid) + _dot(m_bf16, lo)


def _dot_exact_rhs(x, m_bf16):
    hi, mid, lo = _split3(x)
    return _dot(hi, m_bf16) + _dot(mid, m_bf16) + _dot(lo, m_bf16)


def _dot_nt_exact_lhs_f32(x, m_bf16):
    hi, mid, lo = _split3(x)
    return _dot_nt(hi, m_bf16) + _dot_nt(mid, m_bf16) + _dot_nt(lo, m_bf16)


def _sigmoid(x):
    return 1.0 / (1.0 + jnp.exp(-x))


def _silu(x):
    return x * _sigmoid(x)


def _small_mm_kernel(a_ref, w_ref, b_ref, o_ref):
    o_ref[...] = _dot(a_ref[...].astype(BF16), w_ref[...].astype(BF16)) + b_ref[...]


def small_matmul(a, w, b, tn=512):
    m, k = a.shape
    n = w.shape[1]
    return pl.pallas_call(
        _small_mm_kernel,
        grid=(n // tn,),
        in_specs=[pl.BlockSpec((m, k), lambda j: (0, 0)),
                  pl.BlockSpec((k, tn), lambda j: (0, j)),
                  pl.BlockSpec((1, tn), lambda j: (0, j))],
        out_specs=pl.BlockSpec((m, tn), lambda j: (0, j)),
        out_shape=jax.ShapeDtypeStruct((m, n), F32),
        compiler_params=_cparams(("arbitrary",)),
        name="adaln_mm",
    )(a, w, b.reshape(1, n))


def _inproj_kernel(x_ref, nw_ref, sc_ref, sh_ref, w_ref, p_ref, h_ref, hs_ref):
    @pl.when(pl.program_id(1) == 0)
    def _():
        xf = x_ref[...]
        ms = jnp.mean(xf * xf, axis=-1, keepdims=True)
        y = xf * lax.rsqrt(ms + NORM_EPS) * nw_ref[...]
        h = (y * (1.0 + sc_ref[0]) + sh_ref[0]).astype(BF16)
        hs_ref[...] = h
        h_ref[...] = h

    p_ref[...] = _dot(hs_ref[...], w_ref[...])


def in_projection(x2, norm_w, scale, shift, w_bf16, rows_per_batch, tm, tn):
    m, d = x2.shape
    n = w_bf16.shape[1]
    tpb = rows_per_batch // tm
    nb = scale.shape[0]
    return pl.pallas_call(
        _inproj_kernel,
        grid=(m // tm, n // tn),
        in_specs=[pl.BlockSpec((tm, d), lambda i, j: (i, 0)),
                  pl.BlockSpec((1, d), lambda i, j: (0, 0)),
                  pl.BlockSpec((1, 1, d), lambda i, j: (i // tpb, 0, 0)),
                  pl.BlockSpec((1, 1, d), lambda i, j: (i // tpb, 0, 0)),
                  pl.BlockSpec((d, tn), lambda i, j: (0, j))],
        out_specs=[pl.BlockSpec((tm, tn), lambda i, j: (i, j)),
                   pl.BlockSpec((tm, d), lambda i, j: (i, 0))],
        out_shape=[jax.ShapeDtypeStruct((m, n), F32),
                   jax.ShapeDtypeStruct((m, d), BF16)],
        scratch_shapes=[pltpu.VMEM((tm, d), BF16)],
        compiler_params=_cparams(("arbitrary", "arbitrary")),
        name="in_proj",
    )(x2, norm_w.reshape(1, d), scale.reshape(nb, 1, d), shift.reshape(nb, 1, d), w_bf16)


def _gate_up_kernel(h_ref, ya_ref, yb_ref, yc_ref, yd_ref, wg_ref, wu_ref, o_ref):
    h = h_ref[...]
    acc = None
    for i, y_ref in enumerate((ya_ref, yb_ref, yc_ref, yd_ref)):
        g = _dot(h, wg_ref[i])
        u = _dot(y_ref[...].astype(BF16), wu_ref[i])
        t = _sigmoid(g) * u
        acc = t if acc is None else acc + t
    o_ref[...] = acc.astype(BF16)


def gate_up(h, ys, wg_bf16, wu_bf16, tm, tn):
    m, d = h.shape
    w = ys[0].shape[1]
    y_spec = pl.BlockSpec((tm, w), lambda i, j: (i, 0))
    return pl.pallas_call(
        _gate_up_kernel,
        grid=(m // tm, d // tn),
        in_specs=[pl.BlockSpec((tm, d), lambda i, j: (i, 0)), y_spec, y_spec, y_spec, y_spec,
                  pl.BlockSpec((N_BRANCH, d, tn), lambda i, j: (0, 0, j)),
                  pl.BlockSpec((N_BRANCH, w, tn), lambda i, j: (0, 0, j))],
        out_specs=pl.BlockSpec((tm, tn), lambda i, j: (i, j)),
        out_shape=jax.ShapeDtypeStruct((m, d), BF16),
        compiler_params=_cparams(("arbitrary", "arbitrary")),
        name="gate_up",
    )(h, *ys, wg_bf16, wu_bf16)


def _out_proj_kernel(a_ref, w_ref, x_ref, g_ref, o_ref):
    o_ref[...] = x_ref[...] + g_ref[0] * _dot(a_ref[...], w_ref[...])


def out_projection(acc, w_bf16, x2, gate, rows_per_batch, tm, tn):
    m, d = x2.shape
    tpb = rows_per_batch // tm
    nb = gate.shape[0]
    return pl.pallas_call(
        _out_proj_kernel,
        grid=(m // tm, d // tn),
        in_specs=[pl.BlockSpec((tm, d), lambda i, j: (i, 0)),
                  pl.BlockSpec((d, tn), lambda i, j: (0, j)),
                  pl.BlockSpec((tm, tn), lambda i, j: (i, j)),
                  pl.BlockSpec((1, 1, tn), lambda i, j: (i // tpb, 0, j))],
        out_specs=pl.BlockSpec((tm, tn), lambda i, j: (i, j)),
        out_shape=jax.ShapeDtypeStruct((m, d), F32),
        compiler_params=_cparams(("arbitrary", "arbitrary")),
        name="out_proj",
    )(acc, w_bf16, x2, gate.reshape(nb, 1, d))


def _flash_kernel(lam_ref, qt_ref, k_ref, vt_ref, o_ref, *, ncomp, nk, nsub):
    tq = qt_ref.shape[2]
    dv = vt_ref.shape[3]
    qts = [qt_ref[0, c * 64:(c + 1) * 64, :] for c in range(ncomp)]

    def body(i, carry):
        carry = list(carry)
        sts = [[_dot(k_ref[0, c, i * nsub + j], qts[c]) for c in range(ncomp)] for j in range(nsub)]
        for j in range(nsub):
            vt = vt_ref[0, 0, i * nsub + j]
            for c in range(ncomp):
                m, l, acc = carry[c]
                st = sts[j][c]
                m_new = jnp.maximum(m, jnp.max(st, axis=0, keepdims=True))
                alpha = jnp.exp2(m - m_new)
                pt = jnp.exp2(st - m_new)
                l = alpha * l + jnp.sum(pt, axis=0, keepdims=True)
                acc = alpha * acc + _dot(vt, pt.astype(BF16))
                carry[c] = (m_new, l, acc)
        return tuple(carry)

    init = tuple((jnp.full((1, tq), NEG_INF, F32), jnp.zeros((1, tq), F32), jnp.zeros((dv, tq), F32))
                 for _ in range(ncomp))
    res = lax.fori_loop(0, nk // nsub, body, init)
    outs = [acc / l for (_, l, acc) in res]
    o_ref[0] = outs[0] if ncomp == 1 else outs[0] - lam_ref[0] * outs[1]


def flash_attention(q, k, v, lam, nh, ncomp, dv, tq, tk, nsub):
    b, lq, _ = q.shape
    lk = k.shape[1]
    nk = lk // tk
    qt = jnp.swapaxes(q * math.log2(math.e), 1, 2).astype(BF16)
    kh = k.astype(BF16).reshape(b, nk, tk, nh * ncomp, 64).transpose(0, 3, 1, 2, 4)
    vt = v.astype(BF16).reshape(b, nk, tk, nh, dv).transpose(0, 3, 1, 4, 2)
    kern = functools.partial(_flash_kernel, ncomp=ncomp, nk=nk, nsub=nsub)
    ot = pl.pallas_call(
        kern,
        grid=(b, nh, lq // tq),
        in_specs=[pl.BlockSpec(memory_space=pltpu.SMEM),
                  pl.BlockSpec((1, ncomp * 64, tq), lambda bi, h, qi: (bi, h, qi)),
                  pl.BlockSpec((1, ncomp, nk, tk, 64), lambda bi, h, qi: (bi, h, 0, 0, 0)),
                  pl.BlockSpec((1, 1, nk, dv, tk), lambda bi, h, qi: (bi, h, 0, 0, 0))],
        out_specs=pl.BlockSpec((1, dv, tq), lambda bi, h, qi: (bi, h, qi)),
        out_shape=jax.ShapeDtypeStruct((b, nh * dv, lq), F32),
        compiler_params=_cparams(("arbitrary", "arbitrary", "arbitrary")),
        name="flash_attn",
    )(lam.reshape(1).astype(F32), qt, kh, vt)
    return jnp.swapaxes(ot, 1, 2)


def _na_kernel(q_ref, k_ref, v_ref, kc_ref, vc_ref, bias_ref, g_ref, o_ref, *, n_rows):
    r = pl.program_id(1)
    r0 = jnp.clip(r - NA_WIN_R // 2, 0, n_rows - NA_WIN_R)
    start = pl.multiple_of(r0 * GRID_W, GRID_W)
    n_loc = NA_WIN_R * GRID_W
    g = g_ref[0]
    for h in range(NA_HEADS):
        sl = slice(h * NA_DIM, (h + 1) * NA_DIM)
        qh = q_ref[0, :, sl]
        kw = k_ref[0, pl.ds(start, n_loc), sl]
        vw = v_ref[0, pl.ds(start, n_loc), sl]
        s_loc = _dot_nt(qh, kw) + bias_ref[0, h]
        s_ctx = _dot_nt(qh, kc_ref[0, :, sl])
        m = jnp.maximum(jnp.max(s_loc, axis=-1, keepdims=True), jnp.max(s_ctx, axis=-1, keepdims=True))
        p_loc = jnp.exp(s_loc - m)
        p_ctx = jnp.exp(s_ctx - m)
        l = jnp.sum(p_loc, axis=-1, keepdims=True) + jnp.sum(p_ctx, axis=-1, keepdims=True)
        o = _dot(p_loc.astype(BF16), vw) + _dot(p_ctx.astype(BF16), vc_ref[0, :, sl])
        o_ref[0, :, sl] = (o / l) * _silu(g[:, sl])


def neighbourhood_attention_x(q, k, v, kc, vc, bias_tbl, g):
    b, l, w = q.shape
    n_rows = l // GRID_W
    lc = kc.shape[1]
    half = NA_WIN_R // 2

    def bias_idx(bi, r):
        return (r - jnp.clip(r - half, 0, n_rows - NA_WIN_R), 0, 0, 0)

    row_spec = pl.BlockSpec((1, GRID_W, w), lambda bi, r: (bi, r, 0))
    full_spec = pl.BlockSpec((1, l, w), lambda bi, r: (bi, 0, 0))
    ctx_spec = pl.BlockSpec((1, lc, w), lambda bi, r: (bi, 0, 0))
    return pl.pallas_call(
        functools.partial(_na_kernel, n_rows=n_rows),
        grid=(b, n_rows),
        in_specs=[row_spec, full_spec, full_spec, ctx_spec, ctx_spec,
                  pl.BlockSpec((1, NA_HEADS, GRID_W, NA_WIN_R * GRID_W), bias_idx),
                  row_spec],
        out_specs=row_spec,
        out_shape=jax.ShapeDtypeStruct((b, l, w), F32),
        compiler_params=_cparams(("arbitrary", "arbitrary")),
        name="nbr_attn",
    )(q, k, v, kc, vc, bias_tbl, g)


def na_bias_table(rpb):
    col = np.arange(GRID_W)
    c0 = np.clip(col - NA_WIN_C // 2, 0, GRID_W - NA_WIN_C)
    col_ok = (col[None, :] >= c0[:, None]) & (col[None, :] < c0[:, None] + NA_WIN_C)
    d_col = np.clip(col[None, :] - col[:, None] + (NA_WIN_C - 1), 0, 2 * NA_WIN_C - 2)
    o = np.arange(NA_WIN_R)
    u = np.arange(NA_WIN_R)
    d_row = u[None, :] - o[:, None] + (NA_WIN_R - 1)
    bias = rpb[:, d_row[:, None, :, None], d_col[None, :, None, :]]
    bias = jnp.where(jnp.asarray(col_ok)[None, None, :, None, :], bias.astype(F32), NEG_INF)
    bias = jnp.transpose(bias, (1, 0, 2, 3, 4))
    return bias.reshape(NA_WIN_R, NA_HEADS, GRID_W, NA_WIN_R * GRID_W)


def _rwkv_kernel(tri_ref, ms_ref, mi_ref, r_ref, v_ref, a_ref, lw_ref, kd_ref, b_ref, y_ref, s_ref):
    c = RW_CHUNK
    nsub = r_ref.shape[1] // c
    d = pl.program_id(0)

    @pl.when(pl.program_id(2) == 0)
    def _():
        s_ref[...] = jnp.zeros_like(s_ref)

    tri = tri_ref[0]
    m_strict = ms_ref[0] > 0.5
    m_incl = mi_ref[0] > 0.5
    rows = lax.broadcasted_iota(jnp.int32, (c, c), 0)
    cols = lax.broadcasted_iota(jnp.int32, (c, c), 1)
    eye = (rows == cols).astype(F32)
    same_half = (rows >= c // 2) == (cols >= c // 2)
    heads = [slice(h * RW_DIM, (h + 1) * RW_DIM) for h in range(RW_HEADS)]

    row_sl, e_tot = [], []
    al, rh, rf, be, ka, bc, kc, vb = [], [], [], [], [], [], [], []
    for j in range(nsub):
        off = pl.multiple_of(jnp.where(d == 0, j, nsub - 1 - j) * c, c)
        rs = pl.ds(off, c)
        row_sl.append(rs)
        lw = lw_ref[0, 0, rs, :]
        cum = _dot_exact_lhs(tri, lw)
        tot = jnp.sum(lw, axis=0, keepdims=True)
        e_m = jnp.exp(-cum)
        e_t = jnp.exp(tot - cum)
        e_tot.append(jnp.exp(tot))
        b_in = b_ref[0, 0, rs, :]
        k_in = kd_ref[0, 0, rs, :]
        alpha = (a_ref[0, rs, :] * jnp.exp(cum - lw)).astype(BF16)
        rho_f = r_ref[0, rs, :] * jnp.exp(cum)
        rho = rho_f.astype(BF16)
        beta = (b_in * e_m).astype(BF16)
        kappa = (k_in * e_m).astype(BF16)
        beta_c = (b_in * e_t).astype(BF16)
        kappa_c = (k_in * e_t).astype(BF16)
        v_b = v_ref[0, rs, :].astype(BF16)
        for sl in heads:
            al.append(alpha[:, sl])
            rh.append(rho[:, sl])
            rf.append(rho_f[:, sl])
            be.append(beta[:, sl])
            ka.append(kappa[:, sl])
            bc.append(beta_c[:, sl])
            kc.append(kappa_c[:, sl])
            vb.append(v_b[:, sl])

    units = range(nsub * RW_HEADS)
    l_ab = [jnp.where(m_strict, _dot_nt(al[u], be[u]), 0.0) for u in units]
    l_ak = [jnp.where(m_strict, _dot_nt(al[u], ka[u]), 0.0).astype(BF16) for u in units]
    t_rb = [jnp.where(m_incl, _dot_nt(rh[u], be[u]), 0.0).astype(BF16) for u in units]
    t_rk = [jnp.where(m_incl, _dot_nt(rh[u], ka[u]), 0.0).astype(BF16) for u in units]
    l_d = [jnp.where(same_half, l_ab[u], 0.0) for u in units]
    l_o = [(l_ab[u] - l_d[u]).astype(BF16) for u in units]
    pw = [l_d[u].astype(BF16) for u in units]
    td = [eye + l_d[u] for u in units]
    for _ in range(int(math.log2(c)) - 2):
        pw = [_dot(pw[u], pw[u]).astype(BF16) for u in units]
        td = [td[u] + _dot(td[u].astype(BF16), pw[u]) for u in units]
    td_b = [td[u].astype(BF16) for u in units]
    x_o = [_dot(td_b[u], l_o[u]).astype(BF16) for u in units]
    tinv = [(td[u] + _dot(x_o[u], td_b[u])).astype(BF16) for u in units]
    akv = [_dot(l_ak[u], vb[u]).astype(BF16) for u in units]
    a_hat = [_dot(tinv[u], al[u]).astype(BF16) for u in units]
    v_hat = [_dot(tinv[u], akv[u]).astype(BF16) for u in units]
    r_hat = [(rf[u] + _dot(t_rb[u], a_hat[u])).astype(BF16) for u in units]
    y_hat = [_dot(t_rb[u], v_hat[u]) + _dot(t_rk[u], vb[u]) for u in units]
    q_mat = [_dot_tn(a_hat[u], bc[u]).astype(BF16) for u in units]
    n_mat = [_dot_tn(v_hat[u], bc[u]) + _dot_tn(vb[u], kc[u]) for u in units]

    state = [s_ref[h] for h in range(RW_HEADS)]
    for j in range(nsub):
        for h, sl in enumerate(heads):
            u = j * RW_HEADS + h
            s_b = state[h].astype(BF16)
            y_ref[0, 0, row_sl[j], sl] = _dot_nt(r_hat[u], s_b) + y_hat[u]
            state[h] = state[h] * e_tot[j][:, sl] + _dot(s_b, q_mat[u]) + n_mat[u]
    for h in range(RW_HEADS):
        s_ref[h] = state[h]


def _scan_chunk_index(d, s, n_ctx_chunks, n_chunks):
    rev = jnp.where(s < n_ctx_chunks, n_ctx_chunks - 1 - s, n_chunks + n_ctx_chunks - 1 - s)
    return jnp.where(d == 0, s, rev)


def _direction_masks(c):
    i = np.arange(c)
    lower = (i[None, :] <= i[:, None]).astype(np.float32)
    tri = np.stack([lower, lower.T])
    strict = np.stack([lower - np.eye(c, dtype=np.float32), lower.T - np.eye(c, dtype=np.float32)])
    return tri, strict


def rwkv_scan(r, v, a, lw, kd, b, n_ctx):
    bsz, lt, w = r.shape
    c = RW_CHUNK
    blk = RW_NSUB * c
    nck = lt // blk
    ncc = n_ctx // blk
    tri, strict = _direction_masks(c)
    shared = pl.BlockSpec((1, blk, w), lambda d, bi, s: (bi, _scan_chunk_index(d, s, ncc, nck), 0))
    perdir = pl.BlockSpec((1, 1, blk, w), lambda d, bi, s: (d, bi, _scan_chunk_index(d, s, ncc, nck), 0))
    mask_spec = pl.BlockSpec((1, c, c), lambda d, bi, s: (d, 0, 0))
    return pl.pallas_call(
        _rwkv_kernel,
        grid=(2, bsz, nck),
        in_specs=[mask_spec, mask_spec, mask_spec, shared, shared, shared, perdir, perdir, perdir],
        out_specs=perdir,
        out_shape=jax.ShapeDtypeStruct((2, bsz, lt, w), F32),
        scratch_shapes=[pltpu.VMEM((RW_HEADS, RW_DIM, RW_DIM), F32)],
        compiler_params=_cparams(("arbitrary", "arbitrary", "arbitrary")),
        name="rwkv_scan",
    )(jnp.asarray(tri, BF16), jnp.asarray(strict, F32), jnp.asarray(tri, F32), r, v, a, lw, kd, b)


def _ssd_kernel(tri_ref, mi_ref, xq_ref, b_ref, c_ref, bt_ref, a_ref, at_ref, y_ref, s_ref):
    q = SSM_CHUNK
    rep = SSM_HEADS // SSM_GROUPS

    @pl.when(pl.program_id(2) == 0)
    def _():
        s_ref[...] = jnp.zeros_like(s_ref)

    tri = tri_ref[0]
    mask = mi_ref[0] > 0.5
    a = a_ref[0, 0]
    a_t = at_ref[0, 0]
    xq = xq_ref[0, 0]
    bm = b_ref[0].astype(BF16)
    cm = c_ref[0].astype(BF16)
    bt = bt_ref[0].astype(BF16)
    for gi in range(SSM_GROUPS):
        gs = slice(gi * SSM_STATE, (gi + 1) * SSM_STATE)
        cb = _dot_nt(cm[:, gs], bm[:, gs])
        for hh in range(rep):
            h = gi * rep + hh
            a_col = jnp.broadcast_to(a[:, h:h + 1], (q, q))
            ac = _dot_exact_lhs(tri, a_col)
            tot = jnp.sum(a_col, axis=0, keepdims=True)
            ar = _dot_nt_exact_lhs_f32(jnp.broadcast_to(a_t[h:h + 1, :], (q, q)), tri)
            lmat = jnp.exp(jnp.where(mask, ac - ar, NEG_INF))
            gmat = (cb * lmat).astype(BF16)
            sl = slice(h * SSM_HEAD_DIM, (h + 1) * SSM_HEAD_DIM)
            xh = xq[:, sl]
            ac64 = ac[:, :SSM_HEAD_DIM]
            tot64 = tot[:, :SSM_HEAD_DIM]
            s0 = s_ref[h]
            y = _dot(gmat, xh.astype(BF16)) + _dot(cm[:, gs], s0.astype(BF16)) * jnp.exp(ac64)
            y_ref[0, 0, :, sl] = y
            xd = (xh * jnp.exp(tot64 - ac64)).astype(BF16)
            s_ref[h] = s0 * jnp.exp(tot64) + _dot(bt[gs, :], xd)


def ssd_scan(xq, bm, cm, a, n_ctx):
    _, bsz, lt, w = xq.shape
    q = SSM_CHUNK
    nck = lt // q
    ncc = n_ctx // q
    gw = bm.shape[-1]
    tri, _ = _direction_masks(q)
    bt = jnp.swapaxes(bm, 1, 2)
    a_t = jnp.swapaxes(a, 2, 3)

    def cidx(d, s):
        return _scan_chunk_index(d, s, ncc, nck)

    mask_spec = pl.BlockSpec((1, q, q), lambda d, bi, s: (d, 0, 0))
    return pl.pallas_call(
        _ssd_kernel,
        grid=(2, bsz, nck),
        in_specs=[mask_spec, mask_spec,
                  pl.BlockSpec((1, 1, q, w), lambda d, bi, s: (d, bi, cidx(d, s), 0)),
                  pl.BlockSpec((1, q, gw), lambda d, bi, s: (bi, cidx(d, s), 0)),
                  pl.BlockSpec((1, q, gw), lambda d, bi, s: (bi, cidx(d, s), 0)),
                  pl.BlockSpec((1, gw, q), lambda d, bi, s: (bi, 0, cidx(d, s))),
                  pl.BlockSpec((1, 1, q, SSM_HEADS), lambda d, bi, s: (d, bi, cidx(d, s), 0)),
                  pl.BlockSpec((1, 1, SSM_HEADS, q), lambda d, bi, s: (d, bi, 0, cidx(d, s)))],
        out_specs=pl.BlockSpec((1, 1, q, w), lambda d, bi, s: (d, bi, cidx(d, s), 0)),
        out_shape=jax.ShapeDtypeStruct((2, bsz, lt, w), F32),
        scratch_shapes=[pltpu.VMEM((SSM_HEADS, SSM_STATE, SSM_HEAD_DIM), F32)],
        compiler_params=_cparams(("arbitrary", "arbitrary", "arbitrary")),
        name="ssd_scan",
    )(jnp.asarray(tri, BF16), jnp.asarray(tri, F32), xq, bm, cm, bt, a, a_t)


def _rms_norm(x, w, eps=NORM_EPS):
    y = x * lax.rsqrt(jnp.mean(x * x, axis=-1, keepdims=True) + eps)
    return y * w


def _head_norm(p, w, dim):
    shp = p.shape
    return _rms_norm(p.reshape(shp[:-1] + (shp[-1] // dim, dim)), w).reshape(shp)


def _axial_rope(x, length):
    d = x.shape[-1]
    nf = d // 4
    t = jnp.arange(length, dtype=jnp.int32)
    rows, cols = t // GRID_W, t % GRID_W
    inv = ROPE_BASE ** (-jnp.arange(nf, dtype=F32) / nf)
    ang_r = rows.astype(F32)[:, None] * inv
    ang_c = cols.astype(F32)[:, None] * inv
    ang = jnp.concatenate([ang_r, ang_r, ang_c, ang_c], axis=-1)
    cos = jnp.cos(ang)[None, :, None, :]
    sin = jnp.sin(ang)[None, :, None, :]
    x1, x2, x3, x4 = jnp.split(x, 4, axis=-1)
    rot = jnp.concatenate([-x2, x1, -x4, x3], axis=-1)
    return x * cos + rot * sin


def _token_shift(x, mu):
    xp = jnp.pad(x, ((0, 0), (1, 1), (0, 0)))
    nb = 0.5 * (xp[:, :-2] + xp[:, 2:])
    return x + (nb - x) * mu


def _depthwise_conv3(x, w, b):
    xp = jnp.pad(x, ((0, 0), (1, 1), (0, 0)))
    return xp[:, :-2] * w[0] + xp[:, 1:-1] * w[1] + xp[:, 2:] * w[2] + b


def _mixer_a(pc, px, q_norm_w, k_norm_w, rpb, need_ctx):
    w = BRANCH_W
    scale = NA_DIM ** -0.5

    def prep(p):
        q = _head_norm(p[..., :w], q_norm_w, NA_DIM) * scale
        k = _head_norm(p[..., w:2 * w], k_norm_w, NA_DIM)
        return q, k, p[..., 2 * w:3 * w], p[..., 3 * w:4 * w]

    qc, kc, vc, gc = prep(pc)
    qx, kx, vx, gx = prep(px)
    bf = lambda t: t.astype(BF16)
    ox = neighbourhood_attention_x(bf(qx), bf(kx), bf(vx), bf(kc), bf(vc), na_bias_table(rpb), gx)
    oc = None
    if need_ctx:
        oc = flash_attention(qc, kc, vc, jnp.zeros((), F32), NA_HEADS, 1, NA_DIM, CTX_LEN, CTX_LEN, 1)
        oc = oc * _silu(gc)
    return oc, ox


def _mixer_b(pc, px, mu, w0, w2, a0, a2, k_k, k_a, r_k, ln_w, ln_b, need_ctx):
    w = BRANCH_W
    n_ctx = pc.shape[1]

    def prep(p):
        bsz, l, _ = p.shape
        core = _token_shift(p[..., :RW_SHIFT_W], mu)
        g = p[..., RW_SHIFT_W:]
        r, k, v = core[..., :w], core[..., w:2 * w], core[..., 2 * w:3 * w]
        wd = jnp.tanh(core[..., 3 * w:3 * w + 2 * RW_LORA_W].reshape(bsz, l, 2, RW_LORA_W))
        ad = core[..., 3 * w + 2 * RW_LORA_W:].reshape(bsz, l, 2, RW_LORA_A)
        w_log = w0 + jnp.einsum('bldr,drc->bldc', wd, w2)
        log_decay = -jnp.exp(-jax.nn.softplus(-w_log) - 0.5)
        a = jax.nn.sigmoid(a0 + jnp.einsum('bldr,drc->bldc', ad, a2))
        kk = (k * k_k).reshape(bsz, l, RW_HEADS, RW_DIM)
        kk = kk / jnp.maximum(jnp.sqrt(jnp.sum(kk * kk, axis=-1, keepdims=True)), 1e-12)
        kk = kk.reshape(bsz, l, w)
        kd = k[:, :, None] * (1.0 + (a - 1.0) * k_a)
        return r, v, kk, log_decay, kd, a, g

    rc, vc, kkc, lwc, kdc, ac, gc = prep(pc)
    rx, vx, kkx, lwx, kdx, ax, gx = prep(px)
    cat = lambda c_, x_: jnp.concatenate([c_, x_], axis=1)
    r = cat(rc, rx)
    v = cat(vc, vx)
    kk = cat(kkc, kkx)
    dirs = lambda t: jnp.moveaxis(t, 2, 0)
    lw = dirs(cat(lwc, lwx))
    kd = dirs(cat(kdc, kdx))
    a = dirs(cat(ac, ax))
    y = rwkv_scan(r, v, -kk, lw, kd, kk[None] * a, n_ctx)
    y = y[0] + y[1]

    def finish(y, r, kd, v, g):
        bsz, l = y.shape[:2]
        yh = y.reshape(bsz, l, RW_HEADS, RW_DIM)
        mean = jnp.mean(yh, axis=-1, keepdims=True)
        var = jnp.mean(jnp.square(yh - mean), axis=-1, keepdims=True)
        yn = ((yh - mean) * lax.rsqrt(var + RW_GN_EPS)).reshape(bsz, l, w) * ln_w + ln_b
        rh = r.reshape(bsz, l, 1, RW_HEADS, RW_DIM)
        kdh = kd.reshape(2, bsz, l, RW_HEADS, RW_DIM)
        kdh = jnp.moveaxis(kdh, 0, 2)
        bonus = jnp.sum(rh * kdh * r_k, axis=(2, 4))[..., None] * v.reshape(bsz, l, RW_HEADS, RW_DIM)
        return (yn + bonus.reshape(bsz, l, w)) * _silu(g)

    ox = finish(y[:, n_ctx:], rx, kd[:, :, n_ctx:], vx, gx)
    oc = finish(y[:, :n_ctx], rc, kd[:, :, :n_ctx], vc, gc) if need_ctx else None
    return oc, ox


def _mixer_c(xbc_c, z_c, dt_c, xbc_x, z_x, dt_x, conv_w, conv_b, dt_bias, a_log, d_skip, norm_w, need_ctx):
    w = BRANCH_W
    n_ctx = xbc_c.shape[1]
    a_neg = -jnp.exp(a_log)
    gs = SSM_GROUPS * SSM_STATE

    def prep(xbc, dt):
        bsz, l, _ = xbc.shape
        xbc = _silu(_depthwise_conv3(xbc, conv_w, conv_b))
        dt = jax.nn.softplus(dt.reshape(bsz, l, 2, SSM_HEADS) + dt_bias)
        return xbc[..., :w], xbc[..., w:w + gs], xbc[..., w + gs:], dt

    xc, bc, cc, dtc = prep(xbc_c, dt_c)
    xx, bx, cx, dtx = prep(xbc_x, dt_x)
    cat = lambda c_, x_: jnp.concatenate([c_, x_], axis=1)
    xs = cat(xc, xx)
    dt = jnp.moveaxis(cat(dtc, dtx), 2, 0)
    bsz, lt = xs.shape[:2]
    xq = xs.reshape(1, bsz, lt, SSM_HEADS, SSM_HEAD_DIM) * dt[..., None]
    xq = xq.reshape(2, bsz, lt, w)
    a = dt * a_neg[:, None, None, :]
    y = ssd_scan(xq, cat(bc, bx), cat(cc, cx), a, n_ctx)
    y = y[0] + y[1] + (xs.reshape(bsz, lt, SSM_HEADS, SSM_HEAD_DIM) * d_skip[:, None]).reshape(bsz, lt, w)

    def finish(y, z):
        bsz, l = y.shape[:2]
        g = y * _silu(z)
        g = _rms_norm(g.reshape(bsz, l, SSM_GROUPS, w // SSM_GROUPS), norm_w.reshape(SSM_GROUPS, -1))
        return g.reshape(bsz, l, w)

    ox = finish(y[:, n_ctx:], z_x)
    oc = finish(y[:, :n_ctx], z_c) if need_ctx else None
    return oc, ox


def _mixer_d(pc, px, q_norm_w, k_norm_w, lq1, lk1, lq2, lk2, subln_w, lam_init, need_ctx):
    w = BRANCH_W
    scale = DA_DIM ** -0.5
    lam = jnp.exp(jnp.sum(lq1 * lk1)) - jnp.exp(jnp.sum(lq2 * lk2)) + lam_init

    def prep(p, rope):
        bsz, l, _ = p.shape
        q = _head_norm(p[..., :w], q_norm_w, DA_DIM)
        k = _head_norm(p[..., w:2 * w], k_norm_w, DA_DIM)
        if rope:
            q = _axial_rope(q.reshape(bsz, l, 2 * DA_HEADS, DA_DIM), l).reshape(bsz, l, w)
            k = _axial_rope(k.reshape(bsz, l, 2 * DA_HEADS, DA_DIM), l).reshape(bsz, l, w)
        return q * scale, k, p[..., 2 * w:3 * w], p[..., 3 * w:]

    qc, kc, vc, gc = prep(pc, False)
    qx, kx, vx, gx = prep(px, True)

    def finish(o, g):
        bsz, l = o.shape[:2]
        o = _rms_norm(o.reshape(bsz, l, DA_HEADS, 2 * DA_DIM), subln_w) * (1.0 - lam_init)
        return o.reshape(bsz, l, w) * _silu(g)

    k_all = jnp.concatenate([kc, kx], axis=1)
    v_all = jnp.concatenate([vc, vx], axis=1)
    ox = flash_attention(qx, k_all, v_all, lam, DA_HEADS, 2, 2 * DA_DIM, DA_TQ, DA_TK, DA_NSUB)
    oc = None
    if need_ctx:
        oc = finish(flash_attention(qc, kc, vc, lam, DA_HEADS, 2, 2 * DA_DIM, CTX_LEN, CTX_LEN, 1), gc)
    return oc, finish(ox, gx)


def _permute_w_in(w_in_l):
    old_b = 4 * BRANCH_W
    old_c = old_b + RW_SHIFT_W + BRANCH_W
    old_dt = old_c + SSM_CONV_CH
    old_z = old_dt + 2 * SSM_HEADS
    old_d = old_z + BRANCH_W
    old_end = old_d + 4 * BRANCH_W
    pad = jnp.zeros((w_in_l.shape[0], IN_W_PAD - old_end), BF16)
    parts = [w_in_l[:, :old_dt], w_in_l[:, old_z:old_end], w_in_l[:, old_dt:old_z]]
    return jnp.concatenate([p.astype(BF16) for p in parts] + [pad], axis=1)


def kernel(x, c, ctx, c_ctx, norm_w, w_ada, b_ada, w_in, na_q_norm, na_k_norm, na_rpb, rw_mu, rw_w0, rw_w2, rw_a0, rw_a2, rw_k_k, rw_k_a, rw_r_k, rw_ln_w, rw_ln_b, ssm_conv_w, ssm_conv_b, ssm_dt_bias, ssm_A_log, ssm_D, ssm_norm_w, da_q_norm, da_k_norm, da_lq1, da_lk1, da_lq2, da_lk2, da_subln, w_gate, w_up, w_out):
    bsz, seq, d = x.shape
    n_ctx = ctx.shape[1]
    depth = w_in.shape[0]
    cond = jnp.concatenate([_silu(c), _silu(c_ctx)[None], jnp.zeros((8 - bsz - 1, d), F32)], axis=0)
    x2 = x.reshape(bsz * seq, d)
    xc2 = ctx.reshape(bsz * n_ctx, d)
    for l in range(depth):
        need_ctx = l < depth - 1
        lam_init = 0.8 - 0.6 * math.exp(-0.3 * l)
        mod = small_matmul(cond, w_ada[l], b_ada[l])
        shift, scale, gate = jnp.split(mod[:bsz], 3, axis=-1)
        shift_c, scale_c, gate_c = [jnp.broadcast_to(t, (bsz, d)) for t in jnp.split(mod[bsz:bsz + 1], 3, axis=-1)]
        w_in_b = _permute_w_in(w_in[l])
        wg_b = w_gate[l].astype(BF16)
        wu_b = w_up[l].astype(BF16)
        wo_b = w_out[l].astype(BF16)

        px, hx = in_projection(x2, norm_w[l], scale, shift, w_in_b, seq, 512, 1152)
        pc, hc = in_projection(xc2, norm_w[l], scale_c, shift_c, w_in_b, n_ctx, 256, 1152)
        px = px.reshape(bsz, seq, IN_W_PAD)
        pc = pc.reshape(bsz, n_ctx, IN_W_PAD)

        def cols(p, lo, width):
            return p[..., lo:lo + width]

        oa_c, oa_x = _mixer_a(cols(pc, COL_A, 4 * BRANCH_W), cols(px, COL_A, 4 * BRANCH_W),
                              na_q_norm[l], na_k_norm[l], na_rpb[l], need_ctx)
        ob_c, ob_x = _mixer_b(cols(pc, COL_B, RW_SHIFT_W + BRANCH_W), cols(px, COL_B, RW_SHIFT_W + BRANCH_W),
                              rw_mu[l], rw_w0[l], rw_w2[l], rw_a0[l], rw_a2[l], rw_k_k[l], rw_k_a[l],
                              rw_r_k[l], rw_ln_w[l], rw_ln_b[l], need_ctx)
        om_c, om_x = _mixer_c(cols(pc, COL_XBC, SSM_CONV_CH), cols(pc, COL_Z, BRANCH_W), cols(pc, COL_DT, 2 * SSM_HEADS),
                              cols(px, COL_XBC, SSM_CONV_CH), cols(px, COL_Z, BRANCH_W), cols(px, COL_DT, 2 * SSM_HEADS),
                              ssm_conv_w[l], ssm_conv_b[l], ssm_dt_bias[l], ssm_A_log[l], ssm_D[l],
                              ssm_norm_w[l], need_ctx)
        od_c, od_x = _mixer_d(cols(pc, COL_D, 4 * BRANCH_W), cols(px, COL_D, 4 * BRANCH_W),
                              da_q_norm[l], da_k_norm[l], da_lq1[l], da_lk1[l], da_lq2[l], da_lk2[l],
                              da_subln[l], lam_init, need_ctx)

        ys_x = [t.reshape(bsz * seq, BRANCH_W) for t in (oa_x, ob_x, om_x, od_x)]
        acc_x = gate_up(hx, ys_x, wg_b, wu_b, 512, 512)
        x2 = out_projection(acc_x, wo_b, x2, gate, seq, 512, 512)
        if need_ctx:
            ys_c = [t.reshape(bsz * n_ctx, BRANCH_W) for t in (oa_c, ob_c, om_c, od_c)]
            acc_c = gate_up(hc, ys_c, wg_b, wu_b, 256, 512)
            xc2 = out_projection(acc_c, wo_b, xc2, gate_c, n_ctx, 256, 512)
    return x2.reshape(bsz, seq, d)
```

```python
import functools
import math

import numpy as np
import jax
import jax.numpy as jnp
from jax import lax
from jax.experimental import pallas as pl
from jax.experimental.pallas import tpu as pltpu

F32 = jnp.float32
BF16 = jnp.bfloat16

D_MODEL = 2048
DEPTH = 2
GRID_W = 64
CTX_LEN = 256
N_BRANCH = 4
BRANCH_W = D_MODEL // N_BRANCH
NORM_EPS = 1e-6
NEG_INF = -1e30

NA_DIM = 64
NA_HEADS = BRANCH_W // NA_DIM
NA_WIN_R = 8
NA_WIN_C = 16

RW_DIM = 64
RW_HEADS = BRANCH_W // RW_DIM
RW_LORA_W = 64
RW_LORA_A = 64
RW_GN_EPS = 64e-5
RW_SHIFT_W = 3 * BRANCH_W + 2 * RW_LORA_W + 2 * RW_LORA_A
RW_CHUNK = 64
RW_NSUB = 2

SSM_HEAD_DIM = 64
SSM_HEADS = BRANCH_W // SSM_HEAD_DIM
SSM_GROUPS = 2
SSM_STATE = 128
SSM_CHUNK = 128
SSM_CONV_CH = BRANCH_W + 2 * SSM_GROUPS * SSM_STATE

DA_DIM = 64
DA_HEADS = BRANCH_W // (2 * DA_DIM)
ROPE_BASE = 10000.0
DA_TQ = 256
DA_TK = 256
DA_NSUB = 3

COL_A = 0
COL_B = COL_A + 4 * BRANCH_W
COL_XBC = COL_B + RW_SHIFT_W + BRANCH_W
COL_Z = COL_XBC + SSM_CONV_CH
COL_D = COL_Z + BRANCH_W
COL_DT = COL_D + 4 * BRANCH_W
IN_W_PAD = COL_DT + 128

V7X_VMEM_BYTES = 64 * 1024 * 1024
VMEM_LIMIT = V7X_VMEM_BYTES * 7 // 8


def _cparams(sem):
    return pltpu.CompilerParams(dimension_semantics=sem, vmem_limit_bytes=VMEM_LIMIT)


def _dot(a, b):
    return jnp.dot(a, b, preferred_element_type=F32)


def _dot_nt(a, b):
    return lax.dot_general(a, b, (((1,), (1,)), ((), ())), preferred_element_type=F32)


def _dot_tn(a, b):
    return lax.dot_general(a, b, (((0,), (0,)), ((), ())), preferred_element_type=F32)


def _split3(x):
    hi = x.astype(BF16)
    r1 = x - hi.astype(F32)
    mid = r1.astype(BF16)
    lo = (r1 - mid.astype(F32)).astype(BF16)
    return hi, mid, lo


def _dot_exact_lhs(m_bf16, x):
    hi, mid, lo = _split3(x)
    return _dot(m_bf16, hi) + _dot(m_bf16, mid) + _dot(m_bf16, lo)


def _dot_exact_rhs(x, m_bf16):
    hi, mid, lo = _split3(x)
    return _dot(hi, m_bf16) + _dot(mid, m_bf16) + _dot(lo, m_bf16)


def _dot_nt_exact_lhs_f32(x, m_bf16):
    hi, mid, lo = _split3(x)
    return _dot_nt(hi, m_bf16) + _dot_nt(mid, m_bf16) + _dot_nt(lo, m_bf16)


def _sigmoid(x):
    return 1.0 / (1.0 + jnp.exp(-x))


def _silu(x):
    return x * _sigmoid(x)


def _small_mm_kernel(a_ref, w_ref, b_ref, o_ref):
    o_ref[...] = _dot(a_ref[...].astype(BF16), w_ref[...].astype(BF16)) + b_ref[...]


def small_matmul(a, w, b, tn=512):
    m, k = a.shape
    n = w.shape[1]
    return pl.pallas_call(
        _small_mm_kernel,
        grid=(n // tn,),
        in_specs=[pl.BlockSpec((m, k), lambda j: (0, 0)),
                  pl.BlockSpec((k, tn), lambda j: (0, j)),
                  pl.BlockSpec((1, tn), lambda j: (0, j))],
        out_specs=pl.BlockSpec((m, tn), lambda j: (0, j)),
        out_shape=jax.ShapeDtypeStruct((m, n), F32),
        compiler_params=_cparams(("arbitrary",)),
        name="adaln_mm",
    )(a, w, b.reshape(1, n))


def _inproj_kernel(x_ref, nw_ref, sc_ref, sh_ref, w_ref, p_ref, h_ref, hs_ref):
    @pl.when(pl.program_id(1) == 0)
    def _():
        xf = x_ref[...]
        ms = jnp.mean(xf * xf, axis=-1, keepdims=True)
        y = xf * lax.rsqrt(ms + NORM_EPS) * nw_ref[...]
        h = (y * (1.0 + sc_ref[0]) + sh_ref[0]).astype(BF16)
        hs_ref[...] = h
        h_ref[...] = h

    p_ref[...] = _dot(hs_ref[...], w_ref[...])


def in_projection(x2, norm_w, scale, shift, w_bf16, rows_per_batch, tm, tn):
    m, d = x2.shape
    n = w_bf16.shape[1]
    tpb = rows_per_batch // tm
    nb = scale.shape[0]
    return pl.pallas_call(
        _inproj_kernel,
        grid=(m // tm, n // tn),
        in_specs=[pl.BlockSpec((tm, d), lambda i, j: (i, 0)),
                  pl.BlockSpec((1, d), lambda i, j: (0, 0)),
                  pl.BlockSpec((1, 1, d), lambda i, j: (i // tpb, 0, 0)),
                  pl.BlockSpec((1, 1, d), lambda i, j: (i // tpb, 0, 0)),
                  pl.BlockSpec((d, tn), lambda i, j: (0, j))],
        out_specs=[pl.BlockSpec((tm, tn), lambda i, j: (i, j)),
                   pl.BlockSpec((tm, d), lambda i, j: (i, 0))],
        out_shape=[jax.ShapeDtypeStruct((m, n), F32),
                   jax.ShapeDtypeStruct((m, d), BF16)],
        scratch_shapes=[pltpu.VMEM((tm, d), BF16)],
        compiler_params=_cparams(("arbitrary", "arbitrary")),
        name="in_proj",
    )(x2, norm_w.reshape(1, d), scale.reshape(nb, 1, d), shift.reshape(nb, 1, d), w_bf16)


def _gate_up_kernel(h_ref, ya_ref, yb_ref, yc_ref, yd_ref, wg_ref, wu_ref, o_ref):
    h = h_ref[...]
    acc = None
    for i, y_ref in enumerate((ya_ref, yb_ref, yc_ref, yd_ref)):
        g = _dot(h, wg_ref[i])
        u = _dot(y_ref[...].astype(BF16), wu_ref[i])
        t = _sigmoid(g) * u
        acc = t if acc is None else acc + t
    o_ref[...] = acc.astype(BF16)


def gate_up(h, ys, wg_bf16, wu_bf16, tm, tn):
    m, d = h.shape
    w = ys[0].shape[1]
    y_spec = pl.BlockSpec((tm, w), lambda i, j: (i, 0))
    return pl.pallas_call(
        _gate_up_kernel,
        grid=(m // tm, d // tn),
        in_specs=[pl.BlockSpec((tm, d), lambda i, j: (i, 0)), y_spec, y_spec, y_spec, y_spec,
                  pl.BlockSpec((N_BRANCH, d, tn), lambda i, j: (0, 0, j)),
                  pl.BlockSpec((N_BRANCH, w, tn), lambda i, j: (0, 0, j))],
        out_specs=pl.BlockSpec((tm, tn), lambda i, j: (i, j)),
        out_shape=jax.ShapeDtypeStruct((m, d), BF16),
        compiler_params=_cparams(("arbitrary", "arbitrary")),
        name="gate_up",
    )(h, *ys, wg_bf16, wu_bf16)


def _out_proj_kernel(a_ref, w_ref, x_ref, g_ref, o_ref):
    o_ref[...] = x_ref[...] + g_ref[0] * _dot(a_ref[...], w_ref[...])


def out_projection(acc, w_bf16, x2, gate, rows_per_batch, tm, tn):
    m, d = x2.shape
    tpb = rows_per_batch // tm
    nb = gate.shape[0]
    return pl.pallas_call(
        _out_proj_kernel,
        grid=(m // tm, d // tn),
        in_specs=[pl.BlockSpec((tm, d), lambda i, j: (i, 0)),
                  pl.BlockSpec((d, tn), lambda i, j: (0, j)),
                  pl.BlockSpec((tm, tn), lambda i, j: (i, j)),
                  pl.BlockSpec((1, 1, tn), lambda i, j: (i // tpb, 0, j))],
        out_specs=pl.BlockSpec((tm, tn), lambda i, j: (i, j)),
        out_shape=jax.ShapeDtypeStruct((m, d), F32),
        compiler_params=_cparams(("arbitrary", "arbitrary")),
        name="out_proj",
    )(acc, w_bf16, x2, gate.reshape(nb, 1, d))


def _flash_kernel(lam_ref, qt_ref, k_ref, vt_ref, o_ref, *, ncomp, nk, nsub):
    tq = qt_ref.shape[2]
    dv = vt_ref.shape[3]
    qts = [qt_ref[0, c * 64:(c + 1) * 64, :] for c in range(ncomp)]

    def body(i, carry):
        carry = list(carry)
        sts = [[_dot(k_ref[0, c, i * nsub + j], qts[c]) for c in range(ncomp)] for j in range(nsub)]
        for j in range(nsub):
            vt = vt_ref[0, 0, i * nsub + j]
            for c in range(ncomp):
                m, l, acc = carry[c]
                st = sts[j][c]
                m_new = jnp.maximum(m, jnp.max(st, axis=0, keepdims=True))
                alpha = jnp.exp2(m - m_new)
                pt = jnp.exp2(st - m_new)
                l = alpha * l + jnp.sum(pt, axis=0, keepdims=True)
                acc = alpha * acc + _dot(vt, pt.astype(BF16))
                carry[c] = (m_new, l, acc)
        return tuple(carry)

    init = tuple((jnp.full((1, tq), NEG_INF, F32), jnp.zeros((1, tq), F32), jnp.zeros((dv, tq), F32))
                 for _ in range(ncomp))
    res = lax.fori_loop(0, nk // nsub, body, init)
    outs = [acc / l for (_, l, acc) in res]
    o_ref[0] = outs[0] if ncomp == 1 else outs[0] - lam_ref[0] * outs[1]


def flash_attention(q, k, v, lam, nh, ncomp, dv, tq, tk, nsub):
    b, lq, _ = q.shape
    lk = k.shape[1]
    nk = lk // tk
    qt = jnp.swapaxes(q * math.log2(math.e), 1, 2).astype(BF16)
    kh = k.astype(BF16).reshape(b, nk, tk, nh * ncomp, 64).transpose(0, 3, 1, 2, 4)
    vt = v.astype(BF16).reshape(b, nk, tk, nh, dv).transpose(0, 3, 1, 4, 2)
    kern = functools.partial(_flash_kernel, ncomp=ncomp, nk=nk, nsub=nsub)
    ot = pl.pallas_call(
        kern,
        grid=(b, nh, lq // tq),
        in_specs=[pl.BlockSpec(memory_space=pltpu.SMEM),
                  pl.BlockSpec((1, ncomp * 64, tq), lambda bi, h, qi: (bi, h, qi)),
                  pl.BlockSpec((1, ncomp, nk, tk, 64), lambda bi, h, qi: (bi, h, 0, 0, 0)),
                  pl.BlockSpec((1, 1, nk, dv, tk), lambda bi, h, qi: (bi, h, 0, 0, 0))],
        out_specs=pl.BlockSpec((1, dv, tq), lambda bi, h, qi: (bi, h, qi)),
        out_shape=jax.ShapeDtypeStruct((b, nh * dv, lq), F32),
        compiler_params=_cparams(("arbitrary", "arbitrary", "arbitrary")),
        name="flash_attn",
    )(lam.reshape(1).astype(F32), qt, kh, vt)
    return jnp.swapaxes(ot, 1, 2)


def _na_kernel(q_ref, k_ref, v_ref, kc_ref, vc_ref, bias_ref, g_ref, o_ref, *, n_rows):
    r = pl.program_id(1)
    r0 = jnp.clip(r - NA_WIN_R // 2, 0, n_rows - NA_WIN_R)
    start = pl.multiple_of(r0 * GRID_W, GRID_W)
    n_loc = NA_WIN_R * GRID_W
    g = g_ref[0]
    hs = range(NA_HEADS)
    sls = [slice(h * NA_DIM, (h + 1) * NA_DIM) for h in hs]
    qh = [q_ref[0, :, sl] for sl in sls]
    s_loc = [_dot_nt(qh[h], k_ref[0, pl.ds(start, n_loc), sls[h]]) + bias_ref[0, h] for h in hs]
    s_ctx = [_dot_nt(qh[h], kc_ref[0, :, sls[h]]) for h in hs]
    m = [jnp.maximum(jnp.max(s_loc[h], axis=-1, keepdims=True), jnp.max(s_ctx[h], axis=-1, keepdims=True))
         for h in hs]
    p_loc = [jnp.exp(s_loc[h] - m[h]) for h in hs]
    p_ctx = [jnp.exp(s_ctx[h] - m[h]) for h in hs]
    l = [jnp.sum(p_loc[h], axis=-1, keepdims=True) + jnp.sum(p_ctx[h], axis=-1, keepdims=True) for h in hs]
    o = [_dot(p_loc[h].astype(BF16), v_ref[0, pl.ds(start, n_loc), sls[h]])
         + _dot(p_ctx[h].astype(BF16), vc_ref[0, :, sls[h]]) for h in hs]
    for h in hs:
        o_ref[0, :, sls[h]] = (o[h] / l[h]) * _silu(g[:, sls[h]])


def neighbourhood_attention_x(q, k, v, kc, vc, bias_tbl, g):
    b, l, w = q.shape
    n_rows = l // GRID_W
    lc = kc.shape[1]
    half = NA_WIN_R // 2

    def bias_idx(bi, r):
        return (r - jnp.clip(r - half, 0, n_rows - NA_WIN_R), 0, 0, 0)

    row_spec = pl.BlockSpec((1, GRID_W, w), lambda bi, r: (bi, r, 0))
    full_spec = pl.BlockSpec((1, l, w), lambda bi, r: (bi, 0, 0))
    ctx_spec = pl.BlockSpec((1, lc, w), lambda bi, r: (bi, 0, 0))
    return pl.pallas_call(
        functools.partial(_na_kernel, n_rows=n_rows),
        grid=(b, n_rows),
        in_specs=[row_spec, full_spec, full_spec, ctx_spec, ctx_spec,
                  pl.BlockSpec((1, NA_HEADS, GRID_W, NA_WIN_R * GRID_W), bias_idx),
                  row_spec],
        out_specs=row_spec,
        out_shape=jax.ShapeDtypeStruct((b, l, w), F32),
        compiler_params=_cparams(("arbitrary", "arbitrary")),
        name="nbr_attn",
    )(q, k, v, kc, vc, bias_tbl, g)


def na_bias_table(rpb):
    col = np.arange(GRID_W)
    c0 = np.clip(col - NA_WIN_C // 2, 0, GRID_W - NA_WIN_C)
    col_ok = (col[None, :] >= c0[:, None]) & (col[None, :] < c0[:, None] + NA_WIN_C)
    d_col = np.clip(col[None, :] - col[:, None] + (NA_WIN_C - 1), 0, 2 * NA_WIN_C - 2)
    onehot = (d_col[:, :, None] == np.arange(2 * NA_WIN_C - 1)).astype(np.float32)
    by_col = jnp.einsum('hrc,qwc->hrqw', rpb.astype(F32), jnp.asarray(onehot),
                        precision=lax.Precision.HIGHEST)
    by_col = jnp.where(jnp.asarray(col_ok)[None, None], by_col, NEG_INF)
    tbl = jnp.stack([by_col[:, NA_WIN_R - 1 - o:2 * NA_WIN_R - 1 - o] for o in range(NA_WIN_R)])
    tbl = jnp.transpose(tbl, (0, 1, 3, 2, 4))
    return tbl.reshape(NA_WIN_R, NA_HEADS, GRID_W, NA_WIN_R * GRID_W)


def _rwkv_kernel(tri_ref, ms_ref, mi_ref, r_ref, v_ref, a_ref, lw_ref, kd_ref, b_ref, y_ref, s_ref):
    c = RW_CHUNK
    nsub = r_ref.shape[1] // c
    d = pl.program_id(0)

    @pl.when(pl.program_id(2) == 0)
    def _():
        s_ref[...] = jnp.zeros_like(s_ref)

    tri = tri_ref[0]
    m_strict = ms_ref[0] > 0.5
    m_incl = mi_ref[0] > 0.5
    rows = lax.broadcasted_iota(jnp.int32, (c, c), 0)
    cols = lax.broadcasted_iota(jnp.int32, (c, c), 1)
    eye = (rows == cols).astype(F32)
    same_half = (rows >= c // 2) == (cols >= c // 2)
    heads = [slice(h * RW_DIM, (h + 1) * RW_DIM) for h in range(RW_HEADS)]

    row_sl, e_tot = [], []
    al, rh, rf, be, ka, bc, kc, vb = [], [], [], [], [], [], [], []
    for j in range(nsub):
        off = pl.multiple_of(jnp.where(d == 0, j, nsub - 1 - j) * c, c)
        rs = pl.ds(off, c)
        row_sl.append(rs)
        lw = lw_ref[0, 0, rs, :]
        cum = _dot_exact_lhs(tri, lw)
        tot = jnp.sum(lw, axis=0, keepdims=True)
        e_m = jnp.exp(-cum)
        e_t = jnp.exp(tot - cum)
        e_tot.append(jnp.exp(tot))
        b_in = b_ref[0, 0, rs, :]
        k_in = kd_ref[0, 0, rs, :]
        alpha = (a_ref[0, rs, :] * jnp.exp(cum - lw)).astype(BF16)
        rho_f = r_ref[0, rs, :] * jnp.exp(cum)
        rho = rho_f.astype(BF16)
        beta = (b_in * e_m).astype(BF16)
        kappa = (k_in * e_m).astype(BF16)
        beta_c = (b_in * e_t).astype(BF16)
        kappa_c = (k_in * e_t).astype(BF16)
        v_b = v_ref[0, rs, :].astype(BF16)
        for sl in heads:
            al.append(alpha[:, sl])
            rh.append(rho[:, sl])
            rf.append(rho_f[:, sl])
            be.append(beta[:, sl])
            ka.append(kappa[:, sl])
            bc.append(beta_c[:, sl])
            kc.append(kappa_c[:, sl])
            vb.append(v_b[:, sl])

    units = range(nsub * RW_HEADS)
    l_ab = [jnp.where(m_strict, _dot_nt(al[u], be[u]), 0.0) for u in units]
    l_ak = [jnp.where(m_strict, _dot_nt(al[u], ka[u]), 0.0).astype(BF16) for u in units]
    t_rb = [jnp.where(m_incl, _dot_nt(rh[u], be[u]), 0.0).astype(BF16) for u in units]
    t_rk = [jnp.where(m_incl, _dot_nt(rh[u], ka[u]), 0.0).astype(BF16) for u in units]
    l_d = [jnp.where(same_half, l_ab[u], 0.0) for u in units]
    l_o = [(l_ab[u] - l_d[u]).astype(BF16) for u in units]
    pw = [l_d[u].astype(BF16) for u in units]
    td = [eye + l_d[u] for u in units]
    for _ in range(int(math.log2(c)) - 2):
        pw = [_dot(pw[u], pw[u]).astype(BF16) for u in units]
        td = [td[u] + _dot(td[u].astype(BF16), pw[u]) for u in units]
    td_b = [td[u].astype(BF16) for u in units]
    x_o = [_dot(td_b[u], l_o[u]).astype(BF16) for u in units]
    tinv = [(td[u] + _dot(x_o[u], td_b[u])).astype(BF16) for u in units]
    akv = [_dot(l_ak[u], vb[u]).astype(BF16) for u in units]
    a_hat = [_dot(tinv[u], al[u]).astype(BF16) for u in units]
    v_hat = [_dot(tinv[u], akv[u]).astype(BF16) for u in units]
    r_hat = [(rf[u] + _dot(t_rb[u], a_hat[u])).astype(BF16) for u in units]
    y_hat = [_dot(t_rb[u], v_hat[u]) + _dot(t_rk[u], vb[u]) for u in units]
    q_mat = [_dot_tn(a_hat[u], bc[u]).astype(BF16) for u in units]
    n_mat = [_dot_tn(v_hat[u], bc[u]) + _dot_tn(vb[u], kc[u]) for u in units]

    state = [s_ref[h] for h in range(RW_HEADS)]
    for j in range(nsub):
        for h, sl in enumerate(heads):
            u = j * RW_HEADS + h
            s_b = state[h].astype(BF16)
            y_ref[0, 0, row_sl[j], sl] = _dot_nt(r_hat[u], s_b) + y_hat[u]
            state[h] = state[h] * e_tot[j][:, sl] + _dot(s_b, q_mat[u]) + n_mat[u]
    for h in range(RW_HEADS):
        s_ref[h] = state[h]


def _scan_chunk_index(d, s, n_ctx_chunks, n_chunks):
    rev = jnp.where(s < n_ctx_chunks, n_ctx_chunks - 1 - s, n_chunks + n_ctx_chunks - 1 - s)
    return jnp.where(d == 0, s, rev)


def _direction_masks(c):
    i = np.arange(c)
    lower = (i[None, :] <= i[:, None]).astype(np.float32)
    tri = np.stack([lower, lower.T])
    strict = np.stack([lower - np.eye(c, dtype=np.float32), lower.T - np.eye(c, dtype=np.float32)])
    return tri, strict


def rwkv_scan(r, v, a, lw, kd, b, n_ctx):
    bsz, lt, w = r.shape
    c = RW_CHUNK
    blk = RW_NSUB * c
    nck = lt // blk
    ncc = n_ctx // blk
    tri, strict = _direction_masks(c)
    shared = pl.BlockSpec((1, blk, w), lambda d, bi, s: (bi, _scan_chunk_index(d, s, ncc, nck), 0))
    perdir = pl.BlockSpec((1, 1, blk, w), lambda d, bi, s: (d, bi, _scan_chunk_index(d, s, ncc, nck), 0))
    mask_spec = pl.BlockSpec((1, c, c), lambda d, bi, s: (d, 0, 0))
    return pl.pallas_call(
        _rwkv_kernel,
        grid=(2, bsz, nck),
        in_specs=[mask_spec, mask_spec, mask_spec, shared, shared, shared, perdir, perdir, perdir],
        out_specs=perdir,
        out_shape=jax.ShapeDtypeStruct((2, bsz, lt, w), F32),
        scratch_shapes=[pltpu.VMEM((RW_HEADS, RW_DIM, RW_DIM), F32)],
        compiler_params=_cparams(("arbitrary", "arbitrary", "arbitrary")),
        name="rwkv_scan",
    )(jnp.asarray(tri, BF16), jnp.asarray(strict, F32), jnp.asarray(tri, F32), r, v, a, lw, kd, b)


def _ssd_kernel(tri_ref, mi_ref, xq_ref, b_ref, c_ref, bt_ref, a_ref, at_ref, y_ref, s_ref):
    q = SSM_CHUNK
    rep = SSM_HEADS // SSM_GROUPS

    @pl.when(pl.program_id(2) == 0)
    def _():
        s_ref[...] = jnp.zeros_like(s_ref)

    tri = tri_ref[0]
    mask = mi_ref[0] > 0.5
    a = a_ref[0, 0]
    a_t = at_ref[0, 0]
    xq = xq_ref[0, 0]
    bm = b_ref[0].astype(BF16)
    cm = c_ref[0].astype(BF16)
    bt = bt_ref[0].astype(BF16)
    hs = range(SSM_HEADS)
    p = SSM_HEAD_DIM
    acol = _dot_exact_lhs(tri, a)
    arow = _dot_nt_exact_lhs_f32(a_t, tri)
    tot = jnp.sum(a, axis=0, keepdims=True)
    groups = [slice(g * SSM_STATE, (g + 1) * SSM_STATE) for g in range(SSM_GROUPS)]
    cms = [cm[:, gs] for gs in groups]
    cb = [_dot_nt(cms[g], bm[:, gs]) for g, gs in enumerate(groups)]
    ac = [jnp.broadcast_to(acol[:, h:h + 1], (q, q)) for h in hs]
    ar = [jnp.broadcast_to(arow[h:h + 1, :], (q, q)) for h in hs]
    gmat = [(cb[h // rep] * jnp.exp(jnp.where(mask, ac[h] - ar[h], NEG_INF))).astype(BF16) for h in hs]
    xh = [xq[:, h * p:(h + 1) * p] for h in hs]
    tot_h = [jnp.broadcast_to(tot[:, h:h + 1], (1, p)) for h in hs]
    s0 = [s_ref[h] for h in hs]
    y_in = [_dot(gmat[h], xh[h].astype(BF16)) for h in hs]
    y_st = [_dot(cms[h // rep], s0[h].astype(BF16)) for h in hs]
    xd = [(xh[h] * jnp.exp(tot_h[h] - ac[h][:, :p])).astype(BF16) for h in hs]
    upd = [_dot(bt[groups[h // rep], :], xd[h]) for h in hs]
    for h in hs:
        y_ref[0, 0, :, h * p:(h + 1) * p] = y_in[h] + y_st[h] * jnp.exp(ac[h][:, :p])
        s_ref[h] = s0[h] * jnp.exp(tot_h[h]) + upd[h]


def ssd_scan(xq, bm, cm, a, n_ctx):
    _, bsz, lt, w = xq.shape
    q = SSM_CHUNK
    nck = lt // q
    ncc = n_ctx // q
    gw = bm.shape[-1]
    tri, _ = _direction_masks(q)
    bt = jnp.swapaxes(bm, 1, 2)
    a_t = jnp.pad(jnp.swapaxes(a, 2, 3), ((0, 0), (0, 0), (0, 16 - SSM_HEADS), (0, 0)))
    a = jnp.pad(a, ((0, 0), (0, 0), (0, 0), (0, 128 - SSM_HEADS)))

    def cidx(d, s):
        return _scan_chunk_index(d, s, ncc, nck)

    mask_spec = pl.BlockSpec((1, q, q), lambda d, bi, s: (d, 0, 0))
    return pl.pallas_call(
        _ssd_kernel,
        grid=(2, bsz, nck),
        in_specs=[mask_spec, mask_spec,
                  pl.BlockSpec((1, 1, q, w), lambda d, bi, s: (d, bi, cidx(d, s), 0)),
                  pl.BlockSpec((1, q, gw), lambda d, bi, s: (bi, cidx(d, s), 0)),
                  pl.BlockSpec((1, q, gw), lambda d, bi, s: (bi, cidx(d, s), 0)),
                  pl.BlockSpec((1, gw, q), lambda d, bi, s: (bi, 0, cidx(d, s))),
                  pl.BlockSpec((1, 1, q, 128), lambda d, bi, s: (d, bi, cidx(d, s), 0)),
                  pl.BlockSpec((1, 1, 16, q), lambda d, bi, s: (d, bi, 0, cidx(d, s)))],
        out_specs=pl.BlockSpec((1, 1, q, w), lambda d, bi, s: (d, bi, cidx(d, s), 0)),
        out_shape=jax.ShapeDtypeStruct((2, bsz, lt, w), F32),
        scratch_shapes=[pltpu.VMEM((SSM_HEADS, SSM_STATE, SSM_HEAD_DIM), F32)],
        compiler_params=_cparams(("arbitrary", "arbitrary", "arbitrary")),
        name="ssd_scan",
    )(jnp.asarray(tri, BF16), jnp.asarray(tri, F32), xq, bm, cm, bt, a, a_t)


def _rms_norm(x, w, eps=NORM_EPS):
    y = x * lax.rsqrt(jnp.mean(x * x, axis=-1, keepdims=True) + eps)
    return y * w


def _head_norm(p, w, dim):
    shp = p.shape
    return _rms_norm(p.reshape(shp[:-1] + (shp[-1] // dim, dim)), w).reshape(shp)


def _axial_rope(x, length):
    d = x.shape[-1]
    nf = d // 4
    t = jnp.arange(length, dtype=jnp.int32)
    rows, cols = t // GRID_W, t % GRID_W
    inv = ROPE_BASE ** (-jnp.arange(nf, dtype=F32) / nf)
    ang_r = rows.astype(F32)[:, None] * inv
    ang_c = cols.astype(F32)[:, None] * inv
    ang = jnp.concatenate([ang_r, ang_r, ang_c, ang_c], axis=-1)
    cos = jnp.cos(ang)[None, :, None, :]
    sin = jnp.sin(ang)[None, :, None, :]
    x1, x2, x3, x4 = jnp.split(x, 4, axis=-1)
    rot = jnp.concatenate([-x2, x1, -x4, x3], axis=-1)
    return x * cos + rot * sin


def _token_shift(x, mu):
    xp = jnp.pad(x, ((0, 0), (1, 1), (0, 0)))
    nb = 0.5 * (xp[:, :-2] + xp[:, 2:])
    return x + (nb - x) * mu


def _depthwise_conv3(x, w, b):
    xp = jnp.pad(x, ((0, 0), (1, 1), (0, 0)))
    return xp[:, :-2] * w[0] + xp[:, 1:-1] * w[1] + xp[:, 2:] * w[2] + b


def _mixer_a(pc, px, q_norm_w, k_norm_w, rpb, need_ctx):
    w = BRANCH_W
    scale = NA_DIM ** -0.5

    def prep(p):
        q = _head_norm(p[..., :w], q_norm_w, NA_DIM) * scale
        k = _head_norm(p[..., w:2 * w], k_norm_w, NA_DIM)
        return q, k, p[..., 2 * w:3 * w], p[..., 3 * w:4 * w]

    qc, kc, vc, gc = prep(pc)
    qx, kx, vx, gx = prep(px)
    bf = lambda t: t.astype(BF16)
    ox = neighbourhood_attention_x(bf(qx), bf(kx), bf(vx), bf(kc), bf(vc), na_bias_table(rpb), gx)
    oc = None
    if need_ctx:
        oc = flash_attention(qc, kc, vc, jnp.zeros((), F32), NA_HEADS, 1, NA_DIM, CTX_LEN, CTX_LEN, 1)
        oc = oc * _silu(gc)
    return oc, ox


def _mixer_b(pc, px, mu, w0, w2, a0, a2, k_k, k_a, r_k, ln_w, ln_b, need_ctx):
    w = BRANCH_W
    n_ctx = pc.shape[1]

    def prep(p):
        bsz, l, _ = p.shape
        core = _token_shift(p[..., :RW_SHIFT_W], mu)
        g = p[..., RW_SHIFT_W:]
        r, k, v = core[..., :w], core[..., w:2 * w], core[..., 2 * w:3 * w]
        wd = jnp.tanh(core[..., 3 * w:3 * w + 2 * RW_LORA_W].reshape(bsz, l, 2, RW_LORA_W))
        ad = core[..., 3 * w + 2 * RW_LORA_W:].reshape(bsz, l, 2, RW_LORA_A)
        w_log = w0 + jnp.einsum('bldr,drc->bldc', wd, w2)
        log_decay = -jnp.exp(-jax.nn.softplus(-w_log) - 0.5)
        a = jax.nn.sigmoid(a0 + jnp.einsum('bldr,drc->bldc', ad, a2))
        kk = (k * k_k).reshape(bsz, l, RW_HEADS, RW_DIM)
        kk = kk / jnp.maximum(jnp.sqrt(jnp.sum(kk * kk, axis=-1, keepdims=True)), 1e-12)
        kk = kk.reshape(bsz, l, w)
        kd = k[:, :, None] * (1.0 + (a - 1.0) * k_a)
        return r, v, kk, log_decay, kd, a, g

    rc, vc, kkc, lwc, kdc, ac, gc = prep(pc)
    rx, vx, kkx, lwx, kdx, ax, gx = prep(px)
    cat = lambda c_, x_: jnp.concatenate([c_, x_], axis=1)
    r = cat(rc, rx)
    v = cat(vc, vx)
    kk = cat(kkc, kkx)
    dirs = lambda t: jnp.moveaxis(t, 2, 0)
    lw = dirs(cat(lwc, lwx))
    kd = dirs(cat(kdc, kdx))
    a = dirs(cat(ac, ax))
    y = rwkv_scan(r, v, -kk, lw, kd, kk[None] * a, n_ctx)
    y = y[0] + y[1]

    def finish(y, r, kd, v, g):
        bsz, l = y.shape[:2]
        yh = y.reshape(bsz, l, RW_HEADS, RW_DIM)
        mean = jnp.mean(yh, axis=-1, keepdims=True)
        var = jnp.mean(jnp.square(yh - mean), axis=-1, keepdims=True)
        yn = ((yh - mean) * lax.rsqrt(var + RW_GN_EPS)).reshape(bsz, l, w) * ln_w + ln_b
        rh = r.reshape(bsz, l, 1, RW_HEADS, RW_DIM)
        kdh = kd.reshape(2, bsz, l, RW_HEADS, RW_DIM)
        kdh = jnp.moveaxis(kdh, 0, 2)
        bonus = jnp.sum(rh * kdh * r_k, axis=(2, 4))[..., None] * v.reshape(bsz, l, RW_HEADS, RW_DIM)
        return (yn + bonus.reshape(bsz, l, w)) * _silu(g)

    ox = finish(y[:, n_ctx:], rx, kd[:, :, n_ctx:], vx, gx)
    oc = finish(y[:, :n_ctx], rc, kd[:, :, :n_ctx], vc, gc) if need_ctx else None
    return oc, ox


def _mixer_c(xbc_c, z_c, dt_c, xbc_x, z_x, dt_x, conv_w, conv_b, dt_bias, a_log, d_skip, norm_w, need_ctx):
    w = BRANCH_W
    n_ctx = xbc_c.shape[1]
    a_neg = -jnp.exp(a_log)
    gs = SSM_GROUPS * SSM_STATE

    def prep(xbc, dt):
        bsz, l, _ = xbc.shape
        xbc = _silu(_depthwise_conv3(xbc, conv_w, conv_b))
        dt = jax.nn.softplus(dt.reshape(bsz, l, 2, SSM_HEADS) + dt_bias)
        return xbc[..., :w], xbc[..., w:w + gs], xbc[..., w + gs:], dt

    xc, bc, cc, dtc = prep(xbc_c, dt_c)
    xx, bx, cx, dtx = prep(xbc_x, dt_x)
    cat = lambda c_, x_: jnp.concatenate([c_, x_], axis=1)
    xs = cat(xc, xx)
    dt = jnp.moveaxis(cat(dtc, dtx), 2, 0)
    bsz, lt = xs.shape[:2]
    xq = xs.reshape(1, bsz, lt, SSM_HEADS, SSM_HEAD_DIM) * dt[..., None]
    xq = xq.reshape(2, bsz, lt, w)
    a = dt * a_neg[:, None, None, :]
    y = ssd_scan(xq, cat(bc, bx), cat(cc, cx), a, n_ctx)
    y = y[0] + y[1] + (xs.reshape(bsz, lt, SSM_HEADS, SSM_HEAD_DIM) * d_skip[:, None]).reshape(bsz, lt, w)

    def finish(y, z):
        bsz, l = y.shape[:2]
        g = y * _silu(z)
        g = _rms_norm(g.reshape(bsz, l, SSM_GROUPS, w // SSM_GROUPS), norm_w.reshape(SSM_GROUPS, -1))
        return g.reshape(bsz, l, w)

    ox = finish(y[:, n_ctx:], z_x)
    oc = finish(y[:, :n_ctx], z_c) if need_ctx else None
    return oc, ox


def _mixer_d(pc, px, q_norm_w, k_norm_w, lq1, lk1, lq2, lk2, subln_w, lam_init, need_ctx):
    w = BRANCH_W
    scale = DA_DIM ** -0.5
    lam = jnp.exp(jnp.sum(lq1 * lk1)) - jnp.exp(jnp.sum(lq2 * lk2)) + lam_init

    def prep(p, rope):
        bsz, l, _ = p.shape
        q = _head_norm(p[..., :w], q_norm_w, DA_DIM)
        k = _head_norm(p[..., w:2 * w], k_norm_w, DA_DIM)
        if rope:
            q = _axial_rope(q.reshape(bsz, l, 2 * DA_HEADS, DA_DIM), l).reshape(bsz, l, w)
            k = _axial_rope(k.reshape(bsz, l, 2 * DA_HEADS, DA_DIM), l).reshape(bsz, l, w)
        return q * scale, k, p[..., 2 * w:3 * w], p[..., 3 * w:]

    qc, kc, vc, gc = prep(pc, False)
    qx, kx, vx, gx = prep(px, True)

    def finish(o, g):
        bsz, l = o.shape[:2]
        o = _rms_norm(o.reshape(bsz, l, DA_HEADS, 2 * DA_DIM), subln_w) * (1.0 - lam_init)
        return o.reshape(bsz, l, w) * _silu(g)

    k_all = jnp.concatenate([kc, kx], axis=1)
    v_all = jnp.concatenate([vc, vx], axis=1)
    ox = flash_attention(qx, k_all, v_all, lam, DA_HEADS, 2, 2 * DA_DIM, DA_TQ, DA_TK, DA_NSUB)
    oc = None
    if need_ctx:
        oc = finish(flash_attention(qc, kc, vc, lam, DA_HEADS, 2, 2 * DA_DIM, CTX_LEN, CTX_LEN, 1), gc)
    return oc, finish(ox, gx)


def _permute_w_in(w_in_l):
    old_b = 4 * BRANCH_W
    old_c = old_b + RW_SHIFT_W + BRANCH_W
    old_dt = old_c + SSM_CONV_CH
    old_z = old_dt + 2 * SSM_HEADS
    old_d = old_z + BRANCH_W
    old_end = old_d + 4 * BRANCH_W
    pad = jnp.zeros((w_in_l.shape[0], IN_W_PAD - old_end), BF16)
    parts = [w_in_l[:, :old_dt], w_in_l[:, old_z:old_end], w_in_l[:, old_dt:old_z]]
    return jnp.concatenate([p.astype(BF16) for p in parts] + [pad], axis=1)


def kernel(x, c, ctx, c_ctx, norm_w, w_ada, b_ada, w_in, na_q_norm, na_k_norm, na_rpb, rw_mu, rw_w0, rw_w2, rw_a0, rw_a2, rw_k_k, rw_k_a, rw_r_k, rw_ln_w, rw_ln_b, ssm_conv_w, ssm_conv_b, ssm_dt_bias, ssm_A_log, ssm_D, ssm_norm_w, da_q_norm, da_k_norm, da_lq1, da_lk1, da_lq2, da_lk2, da_subln, w_gate, w_up, w_out):
    bsz, seq, d = x.shape
    n_ctx = ctx.shape[1]
    depth = w_in.shape[0]
    cond = jnp.concatenate([_silu(c), _silu(c_ctx)[None], jnp.zeros((8 - bsz - 1, d), F32)], axis=0)
    x2 = x.reshape(bsz * seq, d)
    xc2 = ctx.reshape(bsz * n_ctx, d)
    for l in range(depth):
        need_ctx = l < depth - 1
        lam_init = 0.8 - 0.6 * math.exp(-0.3 * l)
        mod = small_matmul(cond, w_ada[l], b_ada[l])
        shift, scale, gate = jnp.split(mod[:bsz], 3, axis=-1)
        shift_c, scale_c, gate_c = [jnp.broadcast_to(t, (bsz, d)) for t in jnp.split(mod[bsz:bsz + 1], 3, axis=-1)]
        w_in_b = _permute_w_in(w_in[l])
        wg_b = w_gate[l].astype(BF16)
        wu_b = w_up[l].astype(BF16)
        wo_b = w_out[l].astype(BF16)

        px, hx = in_projection(x2, norm_w[l], scale, shift, w_in_b, seq, 512, 1152)
        pc, hc = in_projection(xc2, norm_w[l], scale_c, shift_c, w_in_b, n_ctx, 256, 1152)
        px = px.reshape(bsz, seq, IN_W_PAD)
        pc = pc.reshape(bsz, n_ctx, IN_W_PAD)

        def cols(p, lo, width):
            return p[..., lo:lo + width]

        oa_c, oa_x = _mixer_a(cols(pc, COL_A, 4 * BRANCH_W), cols(px, COL_A, 4 * BRANCH_W),
                              na_q_norm[l], na_k_norm[l], na_rpb[l], need_ctx)
        ob_c, ob_x = _mixer_b(cols(pc, COL_B, RW_SHIFT_W + BRANCH_W), cols(px, COL_B, RW_SHIFT_W + BRANCH_W),
                              rw_mu[l], rw_w0[l], rw_w2[l], rw_a0[l], rw_a2[l], rw_k_k[l], rw_k_a[l],
                              rw_r_k[l], rw_ln_w[l], rw_ln_b[l], need_ctx)
        om_c, om_x = _mixer_c(cols(pc, COL_XBC, SSM_CONV_CH), cols(pc, COL_Z, BRANCH_W), cols(pc, COL_DT, 2 * SSM_HEADS),
                              cols(px, COL_XBC, SSM_CONV_CH), cols(px, COL_Z, BRANCH_W), cols(px, COL_DT, 2 * SSM_HEADS),
                              ssm_conv_w[l], ssm_conv_b[l], ssm_dt_bias[l], ssm_A_log[l], ssm_D[l],
                              ssm_norm_w[l], need_ctx)
        od_c, od_x = _mixer_d(cols(pc, COL_D, 4 * BRANCH_W), cols(px, COL_D, 4 * BRANCH_W),
                              da_q_norm[l], da_k_norm[l], da_lq1[l], da_lk1[l], da_lq2[l], da_lk2[l],
                              da_subln[l], lam_init, need_ctx)

        ys_x = [t.reshape(bsz * seq, BRANCH_W) for t in (oa_x, ob_x, om_x, od_x)]
        acc_x = gate_up(hx, ys_x, wg_b, wu_b, 512, 512)
        x2 = out_projection(acc_x, wo_b, x2, gate, seq, 512, 512)
        if need_ctx:
            ys_c = [t.reshape(bsz * n_ctx, BRANCH_W) for t in (oa_c, ob_c, om_c, od_c)]
            acc_c = gate_up(hc, ys_c, wg_b, wu_b, 256, 512)
            xc2 = out_projection(acc_c, wo_b, xc2, gate_c, n_ctx, 256, 512)
    return x2.reshape(bsz, seq, d)
```

```python
import functools
import math

import numpy as np
import jax
import jax.numpy as jnp
from jax import lax
from jax.experimental import pallas as pl
from jax.experimental.pallas import tpu as pltpu

F32 = jnp.float32
BF16 = jnp.bfloat16

D_MODEL = 2048
GRID_W = 64
CTX_LEN = 256
N_BRANCH = 4
BRANCH_W = D_MODEL // N_BRANCH
NORM_EPS = 1e-6
NEG_INF = -1e30

NA_DIM = 64
NA_HEADS = BRANCH_W // NA_DIM
NA_WIN_R = 8
NA_WIN_C = 16

RW_DIM = 64
RW_HEADS = BRANCH_W // RW_DIM
RW_LORA_W = 64
RW_LORA_A = 64
RW_GN_EPS = 64e-5
RW_CHUNK = 64
RW_NSUB = 2

SSM_HEAD_DIM = 64
SSM_HEADS = BRANCH_W // SSM_HEAD_DIM
SSM_GROUPS = 2
SSM_STATE = 128
SSM_CHUNK = 128
SSM_CONV_CH = BRANCH_W + 2 * SSM_GROUPS * SSM_STATE

DA_DIM = 64
DA_HEADS = BRANCH_W // (2 * DA_DIM)
ROPE_BASE = 10000.0
DA_TQ = 256
DA_TK = 256
DA_NSUB = 2

SUBLANES = 8
LANES = 128

CB_AQ, CB_AK, CB_AV, CB_AG = 0, 1, 2, 3
CB_BR, CB_BK, CB_BV, CB_BG = 4, 5, 6, 7
CB_CX, CB_CBC, CB_CZ = 8, 9, 10
CB_DQ, CB_DK, CB_DV, CB_DG = 11, 12, 13, 14
CB_MISC = 15
MISC_DT = 2 * RW_LORA_W + 2 * RW_LORA_A
N_COL_BLOCKS = 16
IN_W_PAD = N_COL_BLOCKS * BRANCH_W
PREP_TM = 256

V7X_VMEM_BYTES = 64 * 1024 * 1024
VMEM_LIMIT = V7X_VMEM_BYTES * 7 // 8


def _cparams(n_axes):
    return pltpu.CompilerParams(dimension_semantics=("arbitrary",) * n_axes, vmem_limit_bytes=VMEM_LIMIT)


def _dot(a, b):
    return jnp.dot(a, b, preferred_element_type=F32)


def _dot_nt(a, b):
    return lax.dot_general(a, b, (((1,), (1,)), ((), ())), preferred_element_type=F32)


def _dot_tn(a, b):
    return lax.dot_general(a, b, (((0,), (0,)), ((), ())), preferred_element_type=F32)


def _split3(x):
    hi = x.astype(BF16)
    r1 = x - hi.astype(F32)
    mid = r1.astype(BF16)
    lo = (r1 - mid.astype(F32)).astype(BF16)
    return hi, mid, lo


def _dot_exact_lhs(m_bf16, x):
    hi, mid, lo = _split3(x)
    return _dot(m_bf16, hi) + _dot(m_bf16, mid) + _dot(m_bf16, lo)


def _dot_exact_rhs(x, m_bf16):
    hi, mid, lo = _split3(x)
    return _dot(hi, m_bf16) + _dot(mid, m_bf16) + _dot(lo, m_bf16)


def _dot_nt_exact_lhs_f32(x, m_bf16):
    hi, mid, lo = _split3(x)
    return _dot_nt(hi, m_bf16) + _dot_nt(mid, m_bf16) + _dot_nt(lo, m_bf16)


def _sigmoid(x):
    return 1.0 / (1.0 + jnp.exp(-x))


def _silu(x):
    return x * _sigmoid(x)


def _softplus(x):
    return jnp.maximum(x, 0.0) + jnp.log(1.0 + jnp.exp(-jnp.abs(x)))


def _group_matrix(width, group):
    g = np.arange(width) // group
    return jnp.asarray((g[:, None] == g[None, :]).astype(np.float32), BF16)


def _head_rms(x, gmat, group, w):
    ms = _dot_exact_rhs(x * x, gmat) * (1.0 / group)
    return x * lax.rsqrt(ms + NORM_EPS) * w


def _tile_rows(n_rows):
    return 768 if n_rows % 768 == 0 else 256


def _row_spec(cb, tm=PREP_TM):
    return pl.BlockSpec((1, tm, BRANCH_W), lambda b, i: (b, i, cb))


def _halo_specs(cb, tm, lt):
    per = tm // SUBLANES
    last = lt // SUBLANES - 1
    prev = pl.BlockSpec((1, SUBLANES, BRANCH_W), lambda b, i: (b, jnp.maximum(i * per - 1, 0), cb))
    nxt = pl.BlockSpec((1, SUBLANES, BRANCH_W), lambda b, i: (b, jnp.minimum((i + 1) * per, last), cb))
    return prev, nxt


def _const_spec(shape):
    return pl.BlockSpec(shape, lambda *_: (0,) * len(shape))


def _neighbours(x, prev_blk, next_blk, i, n_tiles, n_ctx_tiles):
    tm = x.shape[0]
    row = lax.broadcasted_iota(jnp.int32, (tm, 1), 0)
    seg_start = jnp.logical_or(i == 0, i == n_ctx_tiles)
    seg_end = jnp.logical_or(i == n_ctx_tiles - 1, i == n_tiles - 1)
    prev_row = jnp.where(seg_start, 0.0, prev_blk[SUBLANES - 1:SUBLANES, :])
    next_row = jnp.where(seg_end, 0.0, next_blk[0:1, :])
    x_prev = jnp.where(row == 0, prev_row, pltpu.roll(x, 1, 0))
    x_next = jnp.where(row == tm - 1, next_row, pltpu.roll(x, tm - 1, 0))
    return x_prev, x_next


def _small_mm_kernel(a_ref, w_ref, b_ref, o_ref):
    o_ref[...] = _dot(a_ref[...].astype(BF16), w_ref[...].astype(BF16)) + b_ref[...]


def small_matmul(a, w, b, tn=512):
    m, k = a.shape
    n = w.shape[1]
    return pl.pallas_call(
        _small_mm_kernel,
        grid=(n // tn,),
        in_specs=[pl.BlockSpec((m, k), lambda j: (0, 0)),
                  pl.BlockSpec((k, tn), lambda j: (0, j)),
                  pl.BlockSpec((1, tn), lambda j: (0, j))],
        out_specs=pl.BlockSpec((m, tn), lambda j: (0, j)),
        out_shape=jax.ShapeDtypeStruct((m, n), F32),
        compiler_params=_cparams(1),
        name="adaln_mm",
    )(a, w, b.reshape(1, n))


def _inproj_kernel(x_ref, nw_ref, sc_ref, sh_ref, scc_ref, shc_ref, w_ref, p_ref, h_ref, hs_ref, *,
                   tiles_per_batch, n_ctx):
    @pl.when(pl.program_id(1) == 0)
    def _():
        xf = x_ref[...]
        tm = xf.shape[0]
        ms = jnp.mean(xf * xf, axis=-1, keepdims=True)
        y = xf * lax.rsqrt(ms + NORM_EPS) * nw_ref[...]
        row = (pl.program_id(0) % tiles_per_batch) * tm + lax.broadcasted_iota(jnp.int32, (tm, 1), 0)
        is_ctx = row < n_ctx
        sc = jnp.where(is_ctx, scc_ref[...], sc_ref[0])
        sh = jnp.where(is_ctx, shc_ref[...], sh_ref[0])
        h = (y * (1.0 + sc) + sh).astype(BF16)
        hs_ref[...] = h
        h_ref[...] = h

    p_ref[...] = _dot(hs_ref[...], w_ref[...])


def in_projection(xs2, norm_w, scale, shift, scale_c, shift_c, w_bf16, lt, n_ctx, tn=1024):
    m, d = xs2.shape
    n = w_bf16.shape[1]
    tm = _tile_rows(lt)
    tpb = lt // tm
    nb = scale.shape[0]
    kern = functools.partial(_inproj_kernel, tiles_per_batch=tpb, n_ctx=n_ctx)
    return pl.pallas_call(
        kern,
        grid=(m // tm, n // tn),
        in_specs=[pl.BlockSpec((tm, d), lambda i, j: (i, 0)),
                  pl.BlockSpec((1, d), lambda i, j: (0, 0)),
                  pl.BlockSpec((1, 1, d), lambda i, j: (i // tpb, 0, 0)),
                  pl.BlockSpec((1, 1, d), lambda i, j: (i // tpb, 0, 0)),
                  pl.BlockSpec((1, d), lambda i, j: (0, 0)),
                  pl.BlockSpec((1, d), lambda i, j: (0, 0)),
                  pl.BlockSpec((d, tn), lambda i, j: (0, j))],
        out_specs=[pl.BlockSpec((tm, tn), lambda i, j: (i, j)),
                   pl.BlockSpec((tm, d), lambda i, j: (i, 0))],
        out_shape=[jax.ShapeDtypeStruct((m, n), F32),
                   jax.ShapeDtypeStruct((m, d), BF16)],
        scratch_shapes=[pltpu.VMEM((tm, d), BF16)],
        compiler_params=_cparams(2),
        name="in_proj",
    )(xs2, norm_w.reshape(1, d), scale.reshape(nb, 1, d), shift.reshape(nb, 1, d),
      scale_c.reshape(1, d), shift_c.reshape(1, d), w_bf16)


def _gate_up_kernel(h_ref, ya_ref, yb_ref, yc_ref, yd_ref, wg_ref, wu_ref, o_ref):
    h = h_ref[...]
    acc = None
    for i, y_ref in enumerate((ya_ref, yb_ref, yc_ref, yd_ref)):
        g = _dot(h, wg_ref[i])
        u = _dot(y_ref[...], wu_ref[i])
        t = _sigmoid(g) * u
        acc = t if acc is None else acc + t
    o_ref[...] = acc.astype(BF16)


def gate_up(h, ys, wg_bf16, wu_bf16, lt, tn=512):
    m, d = h.shape
    w = ys[0].shape[1]
    tm = _tile_rows(lt)
    y_spec = pl.BlockSpec((tm, w), lambda i, j: (i, 0))
    return pl.pallas_call(
        _gate_up_kernel,
        grid=(m // tm, d // tn),
        in_specs=[pl.BlockSpec((tm, d), lambda i, j: (i, 0)), y_spec, y_spec, y_spec, y_spec,
                  pl.BlockSpec((N_BRANCH, d, tn), lambda i, j: (0, 0, j)),
                  pl.BlockSpec((N_BRANCH, w, tn), lambda i, j: (0, 0, j))],
        out_specs=pl.BlockSpec((tm, tn), lambda i, j: (i, j)),
        out_shape=jax.ShapeDtypeStruct((m, d), BF16),
        compiler_params=_cparams(2),
        name="gate_up",
    )(h, *ys, wg_bf16, wu_bf16)


def _out_proj_kernel(a_ref, w_ref, x_ref, g_ref, gc_ref, o_ref, *, tiles_per_batch, n_ctx):
    tm = x_ref.shape[0]
    row = (pl.program_id(0) % tiles_per_batch) * tm + lax.broadcasted_iota(jnp.int32, (tm, 1), 0)
    gate = jnp.where(row < n_ctx, gc_ref[...], g_ref[0])
    o_ref[...] = x_ref[...] + gate * _dot(a_ref[...], w_ref[...])


def out_projection(acc, w_bf16, xs2, gate, gate_c, lt, n_ctx, tn=512):
    m, d = xs2.shape
    tm = _tile_rows(lt)
    tpb = lt // tm
    nb = gate.shape[0]
    kern = functools.partial(_out_proj_kernel, tiles_per_batch=tpb, n_ctx=n_ctx)
    return pl.pallas_call(
        kern,
        grid=(m // tm, d // tn),
        in_specs=[pl.BlockSpec((tm, d), lambda i, j: (i, 0)),
                  pl.BlockSpec((d, tn), lambda i, j: (0, j)),
                  pl.BlockSpec((tm, tn), lambda i, j: (i, j)),
                  pl.BlockSpec((1, 1, tn), lambda i, j: (i // tpb, 0, j)),
                  pl.BlockSpec((1, tn), lambda i, j: (0, j))],
        out_specs=pl.BlockSpec((tm, tn), lambda i, j: (i, j)),
        out_shape=jax.ShapeDtypeStruct((m, d), F32),
        compiler_params=_cparams(2),
        name="out_proj",
    )(acc, w_bf16, xs2, gate.reshape(nb, 1, d), gate_c.reshape(1, d))


def _rope(x, cos, sin):
    w = x.shape[1]
    lane = lax.broadcasted_iota(jnp.int32, (1, w), 1)
    first = ((lane // (DA_DIM // 4)) % 2) == 0
    rot = jnp.where(first, -pltpu.roll(x, w - DA_DIM // 4, 1), pltpu.roll(x, DA_DIM // 4, 1))
    return x * cos + rot * sin


def _prep_attn_kernel(ak_ref, av_ref, dq_ref, dk_ref, dv_ref, cos_ref, sin_ref, g64_ref,
                      wak_ref, wdq_ref, wdk_ref, kn_ref, va_ref, qt_ref, kh_ref, vt_ref):
    g64 = g64_ref[...]
    cos = cos_ref[...]
    sin = sin_ref[...]
    kn_ref[0] = _head_rms(ak_ref[0], g64, NA_DIM, wak_ref[...]).astype(BF16)
    va_ref[0] = av_ref[0].astype(BF16)
    q = _rope(_head_rms(dq_ref[0], g64, DA_DIM, wdq_ref[...]), cos, sin)
    k = _rope(_head_rms(dk_ref[0], g64, DA_DIM, wdk_ref[...]), cos, sin).astype(BF16)
    qt_ref[0] = q.T.astype(BF16)
    for hc in range(2 * DA_HEADS):
        kh_ref[0, hc] = k[:, hc * DA_DIM:(hc + 1) * DA_DIM]
    v = dv_ref[0]
    dv = 2 * DA_DIM
    for h in range(DA_HEADS):
        vt_ref[0, h, 0] = v[:, h * dv:(h + 1) * dv].T.astype(BF16)


def prep_attention(p3, cos, sin, na_k_w, da_q_w, da_k_w):
    bsz, lt, _ = p3.shape
    tm = PREP_TM
    w = BRANCH_W
    tile = lambda v, s=1.0: (jnp.tile(v, w // v.shape[0]) * s).reshape(1, w)
    tab_spec = pl.BlockSpec((tm, w), lambda b, i: (i, 0))
    vec = _const_spec((1, w))
    return pl.pallas_call(
        _prep_attn_kernel,
        grid=(bsz, lt // tm),
        in_specs=[_row_spec(CB_AK), _row_spec(CB_AV), _row_spec(CB_DQ), _row_spec(CB_DK), _row_spec(CB_DV),
                  tab_spec, tab_spec, _const_spec((w, w)), vec, vec, vec],
        out_specs=[pl.BlockSpec((1, tm, w), lambda b, i: (b, i, 0)),
                   pl.BlockSpec((1, tm, w), lambda b, i: (b, i, 0)),
                   pl.BlockSpec((1, w, tm), lambda b, i: (b, 0, i)),
                   pl.BlockSpec((1, 2 * DA_HEADS, tm, DA_DIM), lambda b, i: (b, 0, i, 0)),
                   pl.BlockSpec((1, DA_HEADS, 1, 2 * DA_DIM, tm), lambda b, i: (b, 0, i, 0, 0))],
        out_shape=[jax.ShapeDtypeStruct((bsz, lt, w), BF16),
                   jax.ShapeDtypeStruct((bsz, lt, w), BF16),
                   jax.ShapeDtypeStruct((bsz, w, lt), BF16),
                   jax.ShapeDtypeStruct((bsz, 2 * DA_HEADS, lt, DA_DIM), BF16),
                   jax.ShapeDtypeStruct((bsz, DA_HEADS, lt // tm, 2 * DA_DIM, tm), BF16)],
        compiler_params=_cparams(2),
        name="prep_attn",
    )(p3, p3, p3, p3, p3, cos, sin, _group_matrix(w, DA_DIM),
      tile(na_k_w), tile(da_q_w, DA_DIM ** -0.5 * math.log2(math.e)), tile(da_k_w))


def rope_tables(lt, n_ctx):
    nf = DA_DIM // 4
    t = jnp.arange(lt - n_ctx, dtype=jnp.int32)
    rows, cols = t // GRID_W, t % GRID_W
    inv = ROPE_BASE ** (-jnp.arange(nf, dtype=F32) / nf)
    ang_r = rows.astype(F32)[:, None] * inv
    ang_c = cols.astype(F32)[:, None] * inv
    ang = jnp.concatenate([ang_r, ang_r, ang_c, ang_c], axis=-1)
    ang = jnp.concatenate([jnp.zeros((n_ctx, DA_DIM), F32), ang], axis=0)
    reps = BRANCH_W // DA_DIM
    return jnp.tile(jnp.cos(ang), (1, reps)), jnp.tile(jnp.sin(ang), (1, reps))


def _flash_kernel(lam_ref, qt_ref, k_ref, vt_ref, g_ref, sw_ref, o_ref, *, n_latent_iters, out_scale):
    tq = qt_ref.shape[2]
    dv = vt_ref.shape[3]
    tk = vt_ref.shape[4]
    qts = [qt_ref[0, c * DA_DIM:(c + 1) * DA_DIM, :] for c in range(2)]

    def attend(carry, sts, vt):
        out = []
        for c in range(2):
            m, l, acc = carry[c]
            st = sts[c]
            m_new = jnp.maximum(m, jnp.max(st, axis=0, keepdims=True))
            alpha = jnp.exp2(m - m_new)
            pt = jnp.exp2(st - m_new)
            l = alpha * l + jnp.sum(pt, axis=0, keepdims=True)
            acc = alpha * acc + _dot(vt, pt.astype(BF16))
            out.append((m_new, l, acc))
        return tuple(out)

    def scores(chunk):
        start = pl.multiple_of(chunk * tk, tk)
        return [_dot(k_ref[0, c, pl.ds(start, tk), :], qts[c]) for c in range(2)]

    init = tuple((jnp.full((1, tq), NEG_INF, F32), jnp.zeros((1, tq), F32), jnp.zeros((dv, tq), F32))
                 for _ in range(2))
    carry = attend(init, scores(0), vt_ref[0, 0, 0])

    def body(i, carry):
        chunks = [1 + i * DA_NSUB + j for j in range(DA_NSUB)]
        sts = [scores(ch) for ch in chunks]
        for j, ch in enumerate(chunks):
            carry = attend(carry, sts[j], vt_ref[0, 0, ch])
        return carry

    n_iters = jnp.where(pl.program_id(2) == 0, 0, n_latent_iters)
    res = lax.fori_loop(0, n_iters, body, carry)
    outs = [acc / l for (_, l, acc) in res]
    o = (outs[0] - lam_ref[0] * outs[1]).T
    ms = jnp.mean(o * o, axis=-1, keepdims=True)
    o = o * lax.rsqrt(ms + NORM_EPS) * (sw_ref[...] * out_scale)
    o_ref[0] = (o * _silu(g_ref[0])).astype(BF16)


def flash_diff_attention(qt, kh, vt, p3, lam, subln_w, lam_init):
    bsz, w, lt = qt.shape
    dv = 2 * DA_DIM
    nk = lt // DA_TK
    n_latent_iters = (nk - 1) // DA_NSUB
    kern = functools.partial(_flash_kernel, n_latent_iters=n_latent_iters, out_scale=1.0 - lam_init)
    g_blocks = BRANCH_W // dv
    return pl.pallas_call(
        kern,
        grid=(bsz, DA_HEADS, lt // DA_TQ),
        in_specs=[pl.BlockSpec(memory_space=pltpu.SMEM),
                  pl.BlockSpec((1, dv, DA_TQ), lambda b, h, qi: (b, h, qi)),
                  pl.BlockSpec((1, 2, lt, DA_DIM), lambda b, h, qi: (b, h, 0, 0)),
                  pl.BlockSpec((1, 1, nk, dv, DA_TK), lambda b, h, qi: (b, h, 0, 0, 0)),
                  pl.BlockSpec((1, DA_TQ, dv), lambda b, h, qi: (b, qi, CB_DG * g_blocks + h)),
                  _const_spec((1, dv))],
        out_specs=pl.BlockSpec((1, DA_TQ, dv), lambda b, h, qi: (b, qi, h)),
        out_shape=jax.ShapeDtypeStruct((bsz, lt, BRANCH_W), BF16),
        compiler_params=_cparams(3),
        name="flash_attn",
    )(lam.reshape(1).astype(F32), qt, kh, vt, p3, subln_w.reshape(1, dv))


def _na_kernel(q_ref, g_ref, k_ref, v_ref, bias_ref, g64_ref, wq_ref, o_ref, *, n_rows, n_ctx):
    blk = pl.program_id(1)
    r = jnp.maximum(blk - n_ctx // GRID_W, 0)
    r0 = jnp.clip(r - NA_WIN_R // 2, 0, n_rows - NA_WIN_R)
    start = pl.multiple_of(n_ctx + r0 * GRID_W, GRID_W)
    n_loc = NA_WIN_R * GRID_W
    q = _head_rms(q_ref[0], g64_ref[...], NA_DIM, wq_ref[...]).astype(BF16)
    g = g_ref[0]
    hs = range(NA_HEADS)
    sls = [slice(h * NA_DIM, (h + 1) * NA_DIM) for h in hs]
    qh = [q[:, sl] for sl in sls]
    s_loc = [_dot_nt(qh[h], k_ref[0, pl.ds(start, n_loc), sls[h]]) + bias_ref[0, h] for h in hs]
    s_ctx = [_dot_nt(qh[h], k_ref[0, 0:n_ctx, sls[h]]) for h in hs]
    m = [jnp.maximum(jnp.max(s_loc[h], axis=-1, keepdims=True), jnp.max(s_ctx[h], axis=-1, keepdims=True))
         for h in hs]
    p_loc = [jnp.exp(s_loc[h] - m[h]) for h in hs]
    p_ctx = [jnp.exp(s_ctx[h] - m[h]) for h in hs]
    l = [jnp.sum(p_loc[h], axis=-1, keepdims=True) + jnp.sum(p_ctx[h], axis=-1, keepdims=True) for h in hs]
    o = [_dot(p_loc[h].astype(BF16), v_ref[0, pl.ds(start, n_loc), sls[h]])
         + _dot(p_ctx[h].astype(BF16), v_ref[0, 0:n_ctx, sls[h]]) for h in hs]
    for h in hs:
        o_ref[0, :, sls[h]] = ((o[h] / l[h]) * _silu(g[:, sls[h]])).astype(BF16)


def neighbourhood_attention(p3, kn, va, bias_tbl, na_q_w, n_ctx):
    bsz, lt, w = kn.shape
    n_rows = (lt - n_ctx) // GRID_W
    ncb = n_ctx // GRID_W
    half = NA_WIN_R // 2

    def bias_idx(b, blk):
        r = blk - ncb
        off = r - jnp.clip(r - half, 0, n_rows - NA_WIN_R)
        return (jnp.where(blk < ncb, NA_WIN_R, off), 0, 0, 0)

    full_spec = pl.BlockSpec((1, lt, w), lambda b, blk: (b, 0, 0))
    wq = (jnp.tile(na_q_w, w // NA_DIM) * NA_DIM ** -0.5).reshape(1, w)
    return pl.pallas_call(
        functools.partial(_na_kernel, n_rows=n_rows, n_ctx=n_ctx),
        grid=(bsz, lt // GRID_W),
        in_specs=[_row_spec(CB_AQ, GRID_W), _row_spec(CB_AG, GRID_W), full_spec, full_spec,
                  pl.BlockSpec((1, NA_HEADS, GRID_W, NA_WIN_R * GRID_W), bias_idx),
                  _const_spec((w, w)), _const_spec((1, w))],
        out_specs=pl.BlockSpec((1, GRID_W, w), lambda b, blk: (b, blk, 0)),
        out_shape=jax.ShapeDtypeStruct((bsz, lt, w), BF16),
        compiler_params=_cparams(2),
        name="nbr_attn",
    )(p3, p3, kn, va, bias_tbl, _group_matrix(w, NA_DIM), wq)


def na_bias_table(rpb):
    col = np.arange(GRID_W)
    c0 = np.clip(col - NA_WIN_C // 2, 0, GRID_W - NA_WIN_C)
    col_ok = (col[None, :] >= c0[:, None]) & (col[None, :] < c0[:, None] + NA_WIN_C)
    d_col = np.clip(col[None, :] - col[:, None] + (NA_WIN_C - 1), 0, 2 * NA_WIN_C - 2)
    onehot = (d_col[:, :, None] == np.arange(2 * NA_WIN_C - 1)).astype(np.float32)
    by_col = jnp.einsum('hrc,qwc->hrqw', rpb.astype(F32), jnp.asarray(onehot),
                        precision=lax.Precision.HIGHEST)
    by_col = jnp.where(jnp.asarray(col_ok)[None, None], by_col, NEG_INF)
    tbl = jnp.stack([by_col[:, NA_WIN_R - 1 - o:2 * NA_WIN_R - 1 - o] for o in range(NA_WIN_R)]
                    + [jnp.full((NA_HEADS, NA_WIN_R, GRID_W, GRID_W), NEG_INF, F32)])
    tbl = jnp.transpose(tbl, (0, 1, 3, 2, 4))
    return tbl.reshape(NA_WIN_R + 1, NA_HEADS, GRID_W, NA_WIN_R * GRID_W)


def _prep_rwkv_kernel(r_ref, rp_ref, rn_ref, k_ref, kp_ref, kn_ref, v_ref, vp_ref, vn_ref,
                      m_ref, mp_ref, mn_ref, mu_ref, w0_ref, a0_ref, w2_ref, a2_ref, kk_ref, ka_ref,
                      rk_ref, g64_ref,
                      ro_ref, vo_ref, ao_ref, lw_ref, kd_ref, bo_ref, bonus_ref, *, n_tiles, n_ctx_tiles):
    i = pl.program_id(1)
    w = BRANCH_W

    def shifted(x_ref, p_ref, n_ref, mu):
        x = x_ref[0]
        xp, xn = _neighbours(x, p_ref[0], n_ref[0], i, n_tiles, n_ctx_tiles)
        return x + (0.5 * (xp + xn) - x) * mu

    r = shifted(r_ref, rp_ref, rn_ref, mu_ref[0:1, :])
    k = shifted(k_ref, kp_ref, kn_ref, mu_ref[1:2, :])
    v = shifted(v_ref, vp_ref, vn_ref, mu_ref[2:3, :])
    misc = shifted(m_ref, mp_ref, mn_ref, mu_ref[3:4, :])
    wd = jnp.tanh(misc[:, 0:2 * RW_LORA_W]).astype(BF16)
    ad = misc[:, 2 * RW_LORA_W:MISC_DT].astype(BF16)
    w_log = w0_ref[...] + _dot(wd, w2_ref[...])
    gate = _sigmoid(a0_ref[...] + _dot(ad, a2_ref[...]))
    log_decay = -math.exp(-0.5) * _sigmoid(w_log)
    g64 = g64_ref[...]
    kk = k * kk_ref[...]
    kk = kk / jnp.maximum(jnp.sqrt(_dot_exact_rhs(kk * kk, g64)), 1e-12)
    ro_ref[0] = r
    vo_ref[0] = v.astype(BF16)
    ao_ref[0] = -kk
    coef = None
    for d in range(2):
        a_d = gate[:, d * w:(d + 1) * w]
        kd = k * (1.0 + (a_d - 1.0) * ka_ref[...])
        lw_ref[d, 0] = log_decay[:, d * w:(d + 1) * w]
        kd_ref[d, 0] = kd
        bo_ref[d, 0] = kk * a_d
        coef = kd if coef is None else coef + kd
    bonus_ref[0] = _dot_exact_rhs(r * coef * rk_ref[...], g64) * v


def prep_rwkv(p3, mu, w0, w2, a0, a2, k_k, k_a, r_k, n_ctx):
    bsz, lt, _ = p3.shape
    tm = PREP_TM
    w = BRANCH_W
    n_tiles = lt // tm
    mu4 = jnp.stack([mu[0:w], mu[w:2 * w], mu[2 * w:3 * w],
                     jnp.pad(mu[3 * w:], (0, w - (mu.shape[0] - 3 * w)))])
    zero = jnp.zeros((RW_LORA_W, w), F32)
    w2cat = jnp.concatenate([jnp.concatenate([w2[0], zero], axis=1),
                             jnp.concatenate([zero, w2[1]], axis=1)], axis=0).astype(BF16)
    a2cat = jnp.concatenate([jnp.concatenate([a2[0], zero], axis=1),
                             jnp.concatenate([zero, a2[1]], axis=1)], axis=0).astype(BF16)
    specs = []
    for cb in (CB_BR, CB_BK, CB_BV, CB_MISC):
        specs += [_row_spec(cb), *_halo_specs(cb, tm, lt)]
    vec = _const_spec((1, w))
    tok = pl.BlockSpec((1, tm, w), lambda b, i: (b, i, 0))
    tok2 = pl.BlockSpec((2, 1, tm, w), lambda b, i: (0, b, i, 0))
    kern = functools.partial(_prep_rwkv_kernel, n_tiles=n_tiles, n_ctx_tiles=n_ctx // tm)
    return pl.pallas_call(
        kern,
        grid=(bsz, n_tiles),
        in_specs=specs + [_const_spec((4, w)), _const_spec((1, 2 * w)), _const_spec((1, 2 * w)),
                          _const_spec((2 * RW_LORA_W, 2 * w)), _const_spec((2 * RW_LORA_A, 2 * w)),
                          vec, vec, vec, _const_spec((w, w))],
        out_specs=[tok, tok, tok, tok2, tok2, tok2, tok],
        out_shape=[jax.ShapeDtypeStruct((bsz, lt, w), F32),
                   jax.ShapeDtypeStruct((bsz, lt, w), BF16),
                   jax.ShapeDtypeStruct((bsz, lt, w), F32),
                   jax.ShapeDtypeStruct((2, bsz, lt, w), F32),
                   jax.ShapeDtypeStruct((2, bsz, lt, w), F32),
                   jax.ShapeDtypeStruct((2, bsz, lt, w), F32),
                   jax.ShapeDtypeStruct((bsz, lt, w), F32)],
        compiler_params=_cparams(2),
        name="prep_rwkv",
    )(*([p3] * 12), mu4, w0.reshape(1, 2 * w), a0.reshape(1, 2 * w), w2cat, a2cat,
      k_k.reshape(1, w), k_a.reshape(1, w), r_k.reshape(1, w), _group_matrix(w, RW_DIM))


def _rwkv_kernel(tri_ref, ms_ref, mi_ref, r_ref, v_ref, a_ref, lw_ref, kd_ref, b_ref, y_ref, s_ref):
    c = RW_CHUNK
    nsub = r_ref.shape[1] // c
    d = pl.program_id(0)

    @pl.when(pl.program_id(2) == 0)
    def _():
        s_ref[...] = jnp.zeros_like(s_ref)

    tri = tri_ref[0]
    m_strict = ms_ref[0] > 0.5
    m_incl = mi_ref[0] > 0.5
    rows = lax.broadcasted_iota(jnp.int32, (c, c), 0)
    cols = lax.broadcasted_iota(jnp.int32, (c, c), 1)
    eye = (rows == cols).astype(F32)
    same_half = (rows >= c // 2) == (cols >= c // 2)
    heads = [slice(h * RW_DIM, (h + 1) * RW_DIM) for h in range(RW_HEADS)]

    row_sl, e_tot = [], []
    al, rh, rf, be, ka, bc, kc, vb = [], [], [], [], [], [], [], []
    for j in range(nsub):
        off = pl.multiple_of(jnp.where(d == 0, j, nsub - 1 - j) * c, c)
        rs = pl.ds(off, c)
        row_sl.append(rs)
        lw = lw_ref[0, 0, rs, :]
        cum = _dot_exact_lhs(tri, lw)
        tot = jnp.sum(lw, axis=0, keepdims=True)
        e_m = jnp.exp(-cum)
        e_t = jnp.exp(tot - cum)
        e_tot.append(jnp.exp(tot))
        b_in = b_ref[0, 0, rs, :]
        k_in = kd_ref[0, 0, rs, :]
        alpha = (a_ref[0, rs, :] * jnp.exp(cum - lw)).astype(BF16)
        rho_f = r_ref[0, rs, :] * jnp.exp(cum)
        rho = rho_f.astype(BF16)
        beta = (b_in * e_m).astype(BF16)
        kappa = (k_in * e_m).astype(BF16)
        beta_c = (b_in * e_t).astype(BF16)
        kappa_c = (k_in * e_t).astype(BF16)
        v_b = v_ref[0, rs, :]
        for sl in heads:
            al.append(alpha[:, sl])
            rh.append(rho[:, sl])
            rf.append(rho_f[:, sl])
            be.append(beta[:, sl])
            ka.append(kappa[:, sl])
            bc.append(beta_c[:, sl])
            kc.append(kappa_c[:, sl])
            vb.append(v_b[:, sl])

    units = range(nsub * RW_HEADS)
    l_ab = [jnp.where(m_strict, _dot_nt(al[u], be[u]), 0.0) for u in units]
    l_ak = [jnp.where(m_strict, _dot_nt(al[u], ka[u]), 0.0).astype(BF16) for u in units]
    t_rb = [jnp.where(m_incl, _dot_nt(rh[u], be[u]), 0.0).astype(BF16) for u in units]
    t_rk = [jnp.where(m_incl, _dot_nt(rh[u], ka[u]), 0.0).astype(BF16) for u in units]
    l_d = [jnp.where(same_half, l_ab[u], 0.0) for u in units]
    l_o = [(l_ab[u] - l_d[u]).astype(BF16) for u in units]
    pw = [l_d[u].astype(BF16) for u in units]
    td = [eye + l_d[u] for u in units]
    for _ in range(int(math.log2(c)) - 2):
        pw = [_dot(pw[u], pw[u]).astype(BF16) for u in units]
        td = [td[u] + _dot(td[u].astype(BF16), pw[u]) for u in units]
    td_b = [td[u].astype(BF16) for u in units]
    x_o = [_dot(td_b[u], l_o[u]).astype(BF16) for u in units]
    tinv = [(td[u] + _dot(x_o[u], td_b[u])).astype(BF16) for u in units]
    akv = [_dot(l_ak[u], vb[u]).astype(BF16) for u in units]
    a_hat = [_dot(tinv[u], al[u]).astype(BF16) for u in units]
    v_hat = [_dot(tinv[u], akv[u]).astype(BF16) for u in units]
    r_hat = [(rf[u] + _dot(t_rb[u], a_hat[u])).astype(BF16) for u in units]
    y_hat = [_dot(t_rb[u], v_hat[u]) + _dot(t_rk[u], vb[u]) for u in units]
    q_mat = [_dot_tn(a_hat[u], bc[u]).astype(BF16) for u in units]
    n_mat = [_dot_tn(v_hat[u], bc[u]) + _dot_tn(vb[u], kc[u]) for u in units]

    state = [s_ref[h] for h in range(RW_HEADS)]
    for j in range(nsub):
        for h, sl in enumerate(heads):
            u = j * RW_HEADS + h
            s_b = state[h].astype(BF16)
            y_ref[0, 0, row_sl[j], sl] = _dot_nt(r_hat[u], s_b) + y_hat[u]
            state[h] = state[h] * e_tot[j][:, sl] + _dot(s_b, q_mat[u]) + n_mat[u]
    for h in range(RW_HEADS):
        s_ref[h] = state[h]


def _scan_chunk_index(d, s, n_ctx_chunks, n_chunks):
    rev = jnp.where(s < n_ctx_chunks, n_ctx_chunks - 1 - s, n_chunks + n_ctx_chunks - 1 - s)
    return jnp.where(d == 0, s, rev)


def _direction_masks(c):
    i = np.arange(c)
    lower = (i[None, :] <= i[:, None]).astype(np.float32)
    tri = np.stack([lower, lower.T])
    strict = np.stack([lower - np.eye(c, dtype=np.float32), lower.T - np.eye(c, dtype=np.float32)])
    return tri, strict


def rwkv_scan(r, v, a, lw, kd, b, n_ctx):
    bsz, lt, w = r.shape
    c = RW_CHUNK
    blk = RW_NSUB * c
    nck = lt // blk
    ncc = n_ctx // blk
    tri, strict = _direction_masks(c)
    shared = pl.BlockSpec((1, blk, w), lambda d, bi, s: (bi, _scan_chunk_index(d, s, ncc, nck), 0))
    perdir = pl.BlockSpec((1, 1, blk, w), lambda d, bi, s: (d, bi, _scan_chunk_index(d, s, ncc, nck), 0))
    mask_spec = pl.BlockSpec((1, c, c), lambda d, bi, s: (d, 0, 0))
    return pl.pallas_call(
        _rwkv_kernel,
        grid=(2, bsz, nck),
        in_specs=[mask_spec, mask_spec, mask_spec, shared, shared, shared, perdir, perdir, perdir],
        out_specs=perdir,
        out_shape=jax.ShapeDtypeStruct((2, bsz, lt, w), F32),
        scratch_shapes=[pltpu.VMEM((RW_HEADS, RW_DIM, RW_DIM), F32)],
        compiler_params=_cparams(3),
        name="rwkv_scan",
    )(jnp.asarray(tri, BF16), jnp.asarray(strict, F32), jnp.asarray(tri, F32), r, v, a, lw, kd, b)


def _finish_rwkv_kernel(y_ref, bonus_ref, g_ref, lnw_ref, lnb_ref, g64_ref, o_ref):
    y = y_ref[0, 0] + y_ref[1, 0]
    g64 = g64_ref[...]
    mean = _dot_exact_rhs(y, g64) * (1.0 / RW_DIM)
    yc = y - mean
    var = _dot_exact_rhs(yc * yc, g64) * (1.0 / RW_DIM)
    yn = yc * lax.rsqrt(var + RW_GN_EPS) * lnw_ref[...] + lnb_ref[...]
    o_ref[0] = ((yn + bonus_ref[0]) * _silu(g_ref[0])).astype(BF16)


def finish_rwkv(y, bonus, p3, ln_w, ln_b):
    _, bsz, lt, w = y.shape
    tm = PREP_TM
    tok = pl.BlockSpec((1, tm, w), lambda b, i: (b, i, 0))
    vec = _const_spec((1, w))
    return pl.pallas_call(
        _finish_rwkv_kernel,
        grid=(bsz, lt // tm),
        in_specs=[pl.BlockSpec((2, 1, tm, w), lambda b, i: (0, b, i, 0)), tok, _row_spec(CB_BG),
                  vec, vec, _const_spec((w, w))],
        out_specs=tok,
        out_shape=jax.ShapeDtypeStruct((bsz, lt, w), BF16),
        compiler_params=_cparams(2),
        name="finish_rwkv",
    )(y, bonus, p3, ln_w.reshape(1, w), ln_b.reshape(1, w), _group_matrix(w, RW_DIM))


def _prep_ssd_kernel(x_ref, xp_ref, xn_ref, bc_ref, bcp_ref, bcn_ref, m_ref, cw_ref, cb_ref, dtb_ref,
                     aneg_ref, exp_ref, dsk_ref,
                     xq_ref, bco_ref, bt_ref, a_ref, at_ref, dskip_ref, *, n_tiles, n_ctx_tiles):
    i = pl.program_id(1)
    w = BRANCH_W

    def conv(x_ref, p_ref, n_ref, half):
        x = x_ref[0]
        xp, xn = _neighbours(x, p_ref[0], n_ref[0], i, n_tiles, n_ctx_tiles)
        lo = half * w
        y = (xp * cw_ref[0:1, lo:lo + w] + x * cw_ref[1:2, lo:lo + w] + xn * cw_ref[2:3, lo:lo + w]
             + cb_ref[:, lo:lo + w])
        return _silu(y)

    xs = conv(x_ref, xp_ref, xn_ref, 0)
    bc = conv(bc_ref, bcp_ref, bcn_ref, 1)
    bco_ref[0] = bc.astype(BF16)
    bt_ref[0] = bc[:, 0:SSM_GROUPS * SSM_STATE].T.astype(BF16)
    dt = _softplus(m_ref[0][:, MISC_DT:MISC_DT + LANES] + dtb_ref[...])
    lane = lax.broadcasted_iota(jnp.int32, (1, LANES), 1)
    dt = jnp.where(lane < 2 * SSM_HEADS, dt, 0.0)
    dtx = _dot_exact_rhs(dt, exp_ref[...])
    a_all = dt * aneg_ref[...]
    first = lane < SSM_HEADS
    a_dirs = [jnp.where(first, a_all, 0.0), jnp.where(first, pltpu.roll(a_all, LANES - SSM_HEADS, 1), 0.0)]
    for d in range(2):
        xq_ref[d, 0] = xs * dtx[:, d * w:(d + 1) * w]
        a_ref[d, 0] = a_dirs[d]
        at_ref[d, 0] = a_dirs[d].T[0:2 * SUBLANES, :]
    dskip_ref[0] = xs * dsk_ref[...]


def prep_ssd(p3, conv_w, conv_b, dt_bias, a_log, d_skip, n_ctx):
    bsz, lt, _ = p3.shape
    tm = PREP_TM
    w = BRANCH_W
    n_tiles = lt // tm
    nh2 = 2 * SSM_HEADS
    pad_lanes = lambda v: jnp.pad(v.reshape(1, nh2), ((0, 0), (0, LANES - nh2)))
    expand = np.zeros((LANES, 2 * w), np.float32)
    for d in range(2):
        for h in range(SSM_HEADS):
            expand[d * SSM_HEADS + h, d * w + h * SSM_HEAD_DIM:d * w + (h + 1) * SSM_HEAD_DIM] = 1.0
    specs = [_row_spec(CB_CX), *_halo_specs(CB_CX, tm, lt), _row_spec(CB_CBC), *_halo_specs(CB_CBC, tm, lt),
             _row_spec(CB_MISC)]
    tok = pl.BlockSpec((1, tm, w), lambda b, i: (b, i, 0))
    gs = SSM_GROUPS * SSM_STATE
    kern = functools.partial(_prep_ssd_kernel, n_tiles=n_tiles, n_ctx_tiles=n_ctx // tm)
    return pl.pallas_call(
        kern,
        grid=(bsz, n_tiles),
        in_specs=specs + [_const_spec((3, 2 * w)), _const_spec((1, 2 * w)), _const_spec((1, LANES)),
                          _const_spec((1, LANES)), _const_spec((LANES, 2 * w)), _const_spec((1, w))],
        out_specs=[pl.BlockSpec((2, 1, tm, w), lambda b, i: (0, b, i, 0)), tok,
                   pl.BlockSpec((1, gs, tm), lambda b, i: (b, 0, i)),
                   pl.BlockSpec((2, 1, tm, LANES), lambda b, i: (0, b, i, 0)),
                   pl.BlockSpec((2, 1, 2 * SUBLANES, tm), lambda b, i: (0, b, 0, i)), tok],
        out_shape=[jax.ShapeDtypeStruct((2, bsz, lt, w), F32),
                   jax.ShapeDtypeStruct((bsz, lt, w), BF16),
                   jax.ShapeDtypeStruct((bsz, gs, lt), BF16),
                   jax.ShapeDtypeStruct((2, bsz, lt, LANES), F32),
                   jax.ShapeDtypeStruct((2, bsz, 2 * SUBLANES, lt), F32),
                   jax.ShapeDtypeStruct((bsz, lt, w), F32)],
        compiler_params=_cparams(2),
        name="prep_ssd",
    )(*([p3] * 7), conv_w, conv_b.reshape(1, 2 * w), pad_lanes(dt_bias), pad_lanes(-jnp.exp(a_log)),
      jnp.asarray(expand, BF16), jnp.repeat(d_skip, SSM_HEAD_DIM).reshape(1, w))


def _ssd_kernel(tri_ref, mi_ref, xq_ref, bc_ref, bt_ref, a_ref, at_ref, y_ref, s_ref):
    q = SSM_CHUNK
    rep = SSM_HEADS // SSM_GROUPS
    gw = SSM_GROUPS * SSM_STATE

    @pl.when(pl.program_id(2) == 0)
    def _():
        s_ref[...] = jnp.zeros_like(s_ref)

    tri = tri_ref[0]
    mask = mi_ref[0] > 0.5
    a = a_ref[0, 0]
    a_t = at_ref[0, 0]
    xq = xq_ref[0, 0]
    bm = bc_ref[0, :, 0:gw]
    cm = bc_ref[0, :, gw:2 * gw]
    bt = bt_ref[0]
    hs = range(SSM_HEADS)
    p = SSM_HEAD_DIM
    acol = _dot_exact_lhs(tri, a)
    arow = _dot_nt_exact_lhs_f32(a_t, tri)
    tot = jnp.sum(a, axis=0, keepdims=True)
    groups = [slice(g * SSM_STATE, (g + 1) * SSM_STATE) for g in range(SSM_GROUPS)]
    cms = [cm[:, gs] for gs in groups]
    cb = [_dot_nt(cms[g], bm[:, gs]) for g, gs in enumerate(groups)]
    ac = [jnp.broadcast_to(acol[:, h:h + 1], (q, q)) for h in hs]
    ar = [jnp.broadcast_to(arow[h:h + 1, :], (q, q)) for h in hs]
    gmat = [(cb[h // rep] * jnp.exp(jnp.where(mask, ac[h] - ar[h], NEG_INF))).astype(BF16) for h in hs]
    xh = [xq[:, h * p:(h + 1) * p] for h in hs]
    tot_h = [jnp.broadcast_to(tot[:, h:h + 1], (1, p)) for h in hs]
    s0 = [s_ref[h] for h in hs]
    y_in = [_dot(gmat[h], xh[h].astype(BF16)) for h in hs]
    y_st = [_dot(cms[h // rep], s0[h].astype(BF16)) for h in hs]
    xd = [(xh[h] * jnp.exp(tot_h[h] - ac[h][:, :p])).astype(BF16) for h in hs]
    upd = [_dot(bt[groups[h // rep], :], xd[h]) for h in hs]
    for h in hs:
        y_ref[0, 0, :, h * p:(h + 1) * p] = y_in[h] + y_st[h] * jnp.exp(ac[h][:, :p])
        s_ref[h] = s0[h] * jnp.exp(tot_h[h]) + upd[h]


def ssd_scan(xq, bc, bt, a, a_t, n_ctx):
    _, bsz, lt, w = xq.shape
    q = SSM_CHUNK
    nck = lt // q
    ncc = n_ctx // q
    gw = bt.shape[1]
    tri, _ = _direction_masks(q)

    def cidx(d, s):
        return _scan_chunk_index(d, s, ncc, nck)

    mask_spec = pl.BlockSpec((1, q, q), lambda d, bi, s: (d, 0, 0))
    return pl.pallas_call(
        _ssd_kernel,
        grid=(2, bsz, nck),
        in_specs=[mask_spec, mask_spec,
                  pl.BlockSpec((1, 1, q, w), lambda d, bi, s: (d, bi, cidx(d, s), 0)),
                  pl.BlockSpec((1, q, w), lambda d, bi, s: (bi, cidx(d, s), 0)),
                  pl.BlockSpec((1, gw, q), lambda d, bi, s: (bi, 0, cidx(d, s))),
                  pl.BlockSpec((1, 1, q, LANES), lambda d, bi, s: (d, bi, cidx(d, s), 0)),
                  pl.BlockSpec((1, 1, 2 * SUBLANES, q), lambda d, bi, s: (d, bi, 0, cidx(d, s)))],
        out_specs=pl.BlockSpec((1, 1, q, w), lambda d, bi, s: (d, bi, cidx(d, s), 0)),
        out_shape=jax.ShapeDtypeStruct((2, bsz, lt, w), F32),
        scratch_shapes=[pltpu.VMEM((SSM_HEADS, SSM_STATE, SSM_HEAD_DIM), F32)],
        compiler_params=_cparams(3),
        name="ssd_scan",
    )(jnp.asarray(tri, BF16), jnp.asarray(tri, F32), xq, bc, bt, a, a_t)


def _finish_ssd_kernel(y_ref, dskip_ref, z_ref, nw_ref, g256_ref, o_ref):
    y = y_ref[0, 0] + y_ref[1, 0] + dskip_ref[0]
    g = y * _silu(z_ref[0])
    group = BRANCH_W // SSM_GROUPS
    o_ref[0] = _head_rms(g, g256_ref[...], group, nw_ref[...]).astype(BF16)


def finish_ssd(y, dskip, p3, norm_w):
    _, bsz, lt, w = y.shape
    tm = PREP_TM
    tok = pl.BlockSpec((1, tm, w), lambda b, i: (b, i, 0))
    return pl.pallas_call(
        _finish_ssd_kernel,
        grid=(bsz, lt // tm),
        in_specs=[pl.BlockSpec((2, 1, tm, w), lambda b, i: (0, b, i, 0)), tok, _row_spec(CB_CZ),
                  _const_spec((1, w)), _const_spec((w, w))],
        out_specs=tok,
        out_shape=jax.ShapeDtypeStruct((bsz, lt, w), BF16),
        compiler_params=_cparams(2),
        name="finish_ssd",
    )(y, dskip, p3, norm_w.reshape(1, w), _group_matrix(w, w // SSM_GROUPS))


def _permute_w_in(w_in_l):
    w = BRANCH_W
    b0 = 4 * w
    lora0 = b0 + 3 * w
    bg0 = lora0 + MISC_DT
    c0 = bg0 + w
    dt0 = c0 + SSM_CONV_CH
    z0 = dt0 + 2 * SSM_HEADS
    d0 = z0 + w
    end = d0 + 4 * w
    parts = [w_in_l[:, 0:lora0], w_in_l[:, bg0:c0], w_in_l[:, c0:dt0], w_in_l[:, z0:d0], w_in_l[:, d0:end],
             w_in_l[:, lora0:bg0], w_in_l[:, dt0:z0]]
    used = sum(p.shape[1] for p in parts)
    pad = jnp.zeros((w_in_l.shape[0], IN_W_PAD - used), BF16)
    return jnp.concatenate([p.astype(BF16) for p in parts] + [pad], axis=1)


def kernel(x, c, ctx, c_ctx, norm_w, w_ada, b_ada, w_in, na_q_norm, na_k_norm, na_rpb, rw_mu, rw_w0, rw_w2, rw_a0, rw_a2, rw_k_k, rw_k_a, rw_r_k, rw_ln_w, rw_ln_b, ssm_conv_w, ssm_conv_b, ssm_dt_bias, ssm_A_log, ssm_D, ssm_norm_w, da_q_norm, da_k_norm, da_lq1, da_lk1, da_lq2, da_lk2, da_subln, w_gate, w_up, w_out):
    bsz, seq, d = x.shape
    n_ctx = ctx.shape[1]
    lt = n_ctx + seq
    depth = w_in.shape[0]
    assert n_ctx == CTX_LEN == PREP_TM == DA_TK and seq % (GRID_W * NA_WIN_R) == 0
    assert ((lt // DA_TK) - 1) % DA_NSUB == 0
    cond = jnp.concatenate([_silu(c), _silu(c_ctx)[None], jnp.zeros((SUBLANES - bsz - 1, d), F32)], axis=0)
    xs2 = jnp.concatenate([ctx, x], axis=1).reshape(bsz * lt, d)
    cos, sin = rope_tables(lt, n_ctx)
    for l in range(depth):
        lam_init = 0.8 - 0.6 * math.exp(-0.3 * l)
        mod = small_matmul(cond, w_ada[l], b_ada[l])
        shift, scale, gate = jnp.split(mod[:bsz], 3, axis=-1)
        shift_c, scale_c, gate_c = jnp.split(mod[bsz], 3, axis=-1)
        p2, h = in_projection(xs2, norm_w[l], scale, shift, scale_c, shift_c, _permute_w_in(w_in[l]), lt, n_ctx)
        p3 = p2.reshape(bsz, lt, IN_W_PAD)

        kn, va, qt, kh, vt = prep_attention(p3, cos, sin, na_k_norm[l], da_q_norm[l], da_k_norm[l])
        oa = neighbourhood_attention(p3, kn, va, na_bias_table(na_rpb[l]), na_q_norm[l], n_ctx)
        lam = jnp.exp(jnp.sum(da_lq1[l] * da_lk1[l])) - jnp.exp(jnp.sum(da_lq2[l] * da_lk2[l])) + lam_init
        od = flash_diff_attention(qt, kh, vt, p3, lam, da_subln[l], lam_init)

        r, vb, a, lw, kd, b, bonus = prep_rwkv(p3, rw_mu[l], rw_w0[l], rw_w2[l], rw_a0[l], rw_a2[l],
                                              rw_k_k[l], rw_k_a[l], rw_r_k[l].reshape(-1), n_ctx)
        ob = finish_rwkv(rwkv_scan(r, vb, a, lw, kd, b, n_ctx), bonus, p3, rw_ln_w[l], rw_ln_b[l])

        xq, bc, bt, sa, sat, dskip = prep_ssd(p3, ssm_conv_w[l], ssm_conv_b[l], ssm_dt_bias[l], ssm_A_log[l],
                                              ssm_D[l], n_ctx)
        om = finish_ssd(ssd_scan(xq, bc, bt, sa, sat, n_ctx), dskip, p3, ssm_norm_w[l])

        ys = [t.reshape(bsz * lt, BRANCH_W) for t in (oa, ob, om, od)]
        acc = gate_up(h, ys, w_gate[l].astype(BF16), w_up[l].astype(BF16), lt)
        xs2 = out_projection(acc, w_out[l].astype(BF16), xs2, gate, gate_c, lt, n_ctx)
    return xs2.reshape(bsz, lt, d)[:, n_ctx:]
```

```python
import functools
import math

import numpy as np
import jax
import jax.numpy as jnp
from jax import lax
from jax.experimental import pallas as pl
from jax.experimental.pallas import tpu as pltpu

F32 = jnp.float32
BF16 = jnp.bfloat16

D_MODEL = 2048
GRID_W = 64
CTX_LEN = 256
N_BRANCH = 4
BRANCH_W = D_MODEL // N_BRANCH
NORM_EPS = 1e-6
NEG_INF = -1e30

NA_DIM = 64
NA_HEADS = BRANCH_W // NA_DIM
NA_WIN_R = 8
NA_WIN_C = 16

RW_DIM = 64
RW_HEADS = BRANCH_W // RW_DIM
RW_LORA_W = 64
RW_LORA_A = 64
RW_GN_EPS = 64e-5
RW_CHUNK = 64
RW_NSUB = 2

SSM_HEAD_DIM = 64
SSM_HEADS = BRANCH_W // SSM_HEAD_DIM
SSM_GROUPS = 2
SSM_STATE = 128
SSM_CHUNK = 128
SSM_CONV_CH = BRANCH_W + 2 * SSM_GROUPS * SSM_STATE

DA_DIM = 64
DA_HEADS = BRANCH_W // (2 * DA_DIM)
ROPE_BASE = 10000.0
DA_TQ = 256
DA_TK = 256
DA_NSUB = 8
DA_ONES = 16

SUBLANES = 8
LANES = 128

CB_AQ, CB_AK, CB_AV, CB_AG = 0, 1, 2, 3
CB_BR, CB_BK, CB_BV, CB_BG = 4, 5, 6, 7
CB_CX, CB_CBC, CB_CZ = 8, 9, 10
CB_DQ, CB_DK, CB_DV, CB_DG = 11, 12, 13, 14
CB_MISC = 15
MISC_DT = 2 * RW_LORA_W + 2 * RW_LORA_A
N_COL_BLOCKS = 16
IN_W_PAD = N_COL_BLOCKS * BRANCH_W
PREP_TM = 256

V7X_VMEM_BYTES = 64 * 1024 * 1024
VMEM_LIMIT = V7X_VMEM_BYTES * 7 // 8


def _cparams(n_axes):
    return pltpu.CompilerParams(dimension_semantics=("arbitrary",) * n_axes, vmem_limit_bytes=VMEM_LIMIT)


def _dot(a, b):
    return jnp.dot(a, b, preferred_element_type=F32)


def _dot_nt(a, b):
    return lax.dot_general(a, b, (((1,), (1,)), ((), ())), preferred_element_type=F32)


def _dot_tn(a, b):
    return lax.dot_general(a, b, (((0,), (0,)), ((), ())), preferred_element_type=F32)


def _split3(x):
    hi = x.astype(BF16)
    r1 = x - hi.astype(F32)
    mid = r1.astype(BF16)
    lo = (r1 - mid.astype(F32)).astype(BF16)
    return hi, mid, lo


def _dot_exact_lhs(m_bf16, x):
    hi, mid, lo = _split3(x)
    return _dot(m_bf16, hi) + _dot(m_bf16, mid) + _dot(m_bf16, lo)


def _dot_exact_rhs(x, m_bf16):
    hi, mid, lo = _split3(x)
    return _dot(hi, m_bf16) + _dot(mid, m_bf16) + _dot(lo, m_bf16)


def _dot_nt_exact_lhs_f32(x, m_bf16):
    hi, mid, lo = _split3(x)
    return _dot_nt(hi, m_bf16) + _dot_nt(mid, m_bf16) + _dot_nt(lo, m_bf16)


def _sigmoid(x):
    return 1.0 / (1.0 + jnp.exp(-x))


def _silu(x):
    return x * _sigmoid(x)


def _softplus(x):
    return jnp.maximum(x, 0.0) + jnp.log(1.0 + jnp.exp(-jnp.abs(x)))


def _group_matrix(width, group):
    g = np.arange(width) // group
    return jnp.asarray((g[:, None] == g[None, :]).astype(np.float32), BF16)


def _head_rms(x, gmat, group, w):
    ms = _dot_exact_rhs(x * x, gmat) * (1.0 / group)
    return x * lax.rsqrt(ms + NORM_EPS) * w


def _tile_rows(n_rows):
    return 768 if n_rows % 768 == 0 else 256


def _row_spec(cb, tm=PREP_TM):
    return pl.BlockSpec((1, tm, BRANCH_W), lambda b, i: (b, i, cb))


def _halo_specs(cb, tm, lt):
    per = tm // SUBLANES
    last = lt // SUBLANES - 1
    prev = pl.BlockSpec((1, SUBLANES, BRANCH_W), lambda b, i: (b, jnp.maximum(i * per - 1, 0), cb))
    nxt = pl.BlockSpec((1, SUBLANES, BRANCH_W), lambda b, i: (b, jnp.minimum((i + 1) * per, last), cb))
    return prev, nxt


def _const_spec(shape):
    return pl.BlockSpec(shape, lambda *_: (0,) * len(shape))


def _neighbours(x, prev_blk, next_blk, i, n_tiles, n_ctx_tiles):
    tm = x.shape[0]
    row = lax.broadcasted_iota(jnp.int32, (tm, 1), 0)
    seg_start = jnp.logical_or(i == 0, i == n_ctx_tiles)
    seg_end = jnp.logical_or(i == n_ctx_tiles - 1, i == n_tiles - 1)
    prev_row = jnp.where(seg_start, 0.0, prev_blk[SUBLANES - 1:SUBLANES, :])
    next_row = jnp.where(seg_end, 0.0, next_blk[0:1, :])
    x_prev = jnp.where(row == 0, prev_row, pltpu.roll(x, 1, 0))
    x_next = jnp.where(row == tm - 1, next_row, pltpu.roll(x, tm - 1, 0))
    return x_prev, x_next


def _small_mm_kernel(a_ref, w_ref, b_ref, o_ref):
    o_ref[...] = _dot(a_ref[...].astype(BF16), w_ref[...].astype(BF16)) + b_ref[...]


def small_matmul(a, w, b, tn=512):
    m, k = a.shape
    n = w.shape[1]
    return pl.pallas_call(
        _small_mm_kernel,
        grid=(n // tn,),
        in_specs=[pl.BlockSpec((m, k), lambda j: (0, 0)),
                  pl.BlockSpec((k, tn), lambda j: (0, j)),
                  pl.BlockSpec((1, tn), lambda j: (0, j))],
        out_specs=pl.BlockSpec((m, tn), lambda j: (0, j)),
        out_shape=jax.ShapeDtypeStruct((m, n), F32),
        compiler_params=_cparams(1),
        name="adaln_mm",
    )(a, w, b.reshape(1, n))


def _inproj_kernel(x_ref, nw_ref, sc_ref, sh_ref, scc_ref, shc_ref, w_ref, p_ref, h_ref, hs_ref, *,
                   tiles_per_batch, n_ctx):
    @pl.when(pl.program_id(1) == 0)
    def _():
        xf = x_ref[...]
        tm = xf.shape[0]
        ms = jnp.mean(xf * xf, axis=-1, keepdims=True)
        y = xf * lax.rsqrt(ms + NORM_EPS) * nw_ref[...]
        row = (pl.program_id(0) % tiles_per_batch) * tm + lax.broadcasted_iota(jnp.int32, (tm, 1), 0)
        is_ctx = row < n_ctx
        sc = jnp.where(is_ctx, scc_ref[...], sc_ref[0])
        sh = jnp.where(is_ctx, shc_ref[...], sh_ref[0])
        h = (y * (1.0 + sc) + sh).astype(BF16)
        hs_ref[...] = h
        h_ref[...] = h

    p_ref[...] = _dot(hs_ref[...], w_ref[...])


def in_projection(xs2, norm_w, scale, shift, scale_c, shift_c, w_bf16, lt, n_ctx, tn=1024):
    m, d = xs2.shape
    n = w_bf16.shape[1]
    tm = _tile_rows(lt)
    tpb = lt // tm
    nb = scale.shape[0]
    kern = functools.partial(_inproj_kernel, tiles_per_batch=tpb, n_ctx=n_ctx)
    return pl.pallas_call(
        kern,
        grid=(m // tm, n // tn),
        in_specs=[pl.BlockSpec((tm, d), lambda i, j: (i, 0)),
                  pl.BlockSpec((1, d), lambda i, j: (0, 0)),
                  pl.BlockSpec((1, 1, d), lambda i, j: (i // tpb, 0, 0)),
                  pl.BlockSpec((1, 1, d), lambda i, j: (i // tpb, 0, 0)),
                  pl.BlockSpec((1, d), lambda i, j: (0, 0)),
                  pl.BlockSpec((1, d), lambda i, j: (0, 0)),
                  pl.BlockSpec((d, tn), lambda i, j: (0, j))],
        out_specs=[pl.BlockSpec((tm, tn), lambda i, j: (i, j)),
                   pl.BlockSpec((tm, d), lambda i, j: (i, 0))],
        out_shape=[jax.ShapeDtypeStruct((m, n), F32),
                   jax.ShapeDtypeStruct((m, d), BF16)],
        scratch_shapes=[pltpu.VMEM((tm, d), BF16)],
        compiler_params=_cparams(2),
        name="in_proj",
    )(xs2, norm_w.reshape(1, d), scale.reshape(nb, 1, d), shift.reshape(nb, 1, d),
      scale_c.reshape(1, d), shift_c.reshape(1, d), w_bf16)


def _gate_up_kernel(h_ref, ya_ref, yb_ref, yc_ref, yd_ref, wg_ref, wu_ref, o_ref):
    h = h_ref[...]
    acc = None
    for i, y_ref in enumerate((ya_ref, yb_ref, yc_ref, yd_ref)):
        g = _dot(h, wg_ref[i])
        u = _dot(y_ref[...], wu_ref[i])
        t = _sigmoid(g) * u
        acc = t if acc is None else acc + t
    o_ref[...] = acc.astype(BF16)


def gate_up(h, ys, wg_bf16, wu_bf16, lt, tn=512):
    m, d = h.shape
    w = ys[0].shape[1]
    tm = _tile_rows(lt)
    y_spec = pl.BlockSpec((tm, w), lambda i, j: (i, 0))
    return pl.pallas_call(
        _gate_up_kernel,
        grid=(m // tm, d // tn),
        in_specs=[pl.BlockSpec((tm, d), lambda i, j: (i, 0)), y_spec, y_spec, y_spec, y_spec,
                  pl.BlockSpec((N_BRANCH, d, tn), lambda i, j: (0, 0, j)),
                  pl.BlockSpec((N_BRANCH, w, tn), lambda i, j: (0, 0, j))],
        out_specs=pl.BlockSpec((tm, tn), lambda i, j: (i, j)),
        out_shape=jax.ShapeDtypeStruct((m, d), BF16),
        compiler_params=_cparams(2),
        name="gate_up",
    )(h, *ys, wg_bf16, wu_bf16)


def _out_proj_kernel(a_ref, w_ref, x_ref, g_ref, gc_ref, o_ref, *, tiles_per_batch, n_ctx):
    tm = x_ref.shape[0]
    row = (pl.program_id(0) % tiles_per_batch) * tm + lax.broadcasted_iota(jnp.int32, (tm, 1), 0)
    gate = jnp.where(row < n_ctx, gc_ref[...], g_ref[0])
    o_ref[...] = x_ref[...] + gate * _dot(a_ref[...], w_ref[...])


def out_projection(acc, w_bf16, xs2, gate, gate_c, lt, n_ctx, tn=512):
    m, d = xs2.shape
    tm = _tile_rows(lt)
    tpb = lt // tm
    nb = gate.shape[0]
    kern = functools.partial(_out_proj_kernel, tiles_per_batch=tpb, n_ctx=n_ctx)
    return pl.pallas_call(
        kern,
        grid=(m // tm, d // tn),
        in_specs=[pl.BlockSpec((tm, d), lambda i, j: (i, 0)),
                  pl.BlockSpec((d, tn), lambda i, j: (0, j)),
                  pl.BlockSpec((tm, tn), lambda i, j: (i, j)),
                  pl.BlockSpec((1, 1, tn), lambda i, j: (i // tpb, 0, j)),
                  pl.BlockSpec((1, tn), lambda i, j: (0, j))],
        out_specs=pl.BlockSpec((tm, tn), lambda i, j: (i, j)),
        out_shape=jax.ShapeDtypeStruct((m, d), F32),
        compiler_params=_cparams(2),
        name="out_proj",
    )(acc, w_bf16, xs2, gate.reshape(nb, 1, d), gate_c.reshape(1, d))


def _rope(x, cos, sin):
    w = x.shape[1]
    lane = lax.broadcasted_iota(jnp.int32, (1, w), 1)
    first = ((lane // (DA_DIM // 4)) % 2) == 0
    rot = jnp.where(first, -pltpu.roll(x, w - DA_DIM // 4, 1), pltpu.roll(x, DA_DIM // 4, 1))
    return x * cos + rot * sin


def _prep_attn_kernel(ak_ref, av_ref, dq_ref, dk_ref, dv_ref, cos_ref, sin_ref, g64_ref,
                      wak_ref, wdq_ref, wdk_ref, kn_ref, va_ref, qt_ref, kh_ref, vt_ref):
    g64 = g64_ref[...]
    cos = cos_ref[...]
    sin = sin_ref[...]
    kn_ref[0] = _head_rms(ak_ref[0], g64, NA_DIM, wak_ref[...]).astype(BF16)
    va_ref[0] = av_ref[0].astype(BF16)
    q = _rope(_head_rms(dq_ref[0], g64, DA_DIM, wdq_ref[...]), cos, sin)
    k = _rope(_head_rms(dk_ref[0], g64, DA_DIM, wdk_ref[...]), cos, sin).astype(BF16)
    qt_ref[0] = q.T.astype(BF16)
    for hc in range(2 * DA_HEADS):
        kh_ref[0, hc] = k[:, hc * DA_DIM:(hc + 1) * DA_DIM]
    v = dv_ref[0]
    dv = 2 * DA_DIM
    for h in range(DA_HEADS):
        vt_ref[0, h, 0, 0:dv, :] = v[:, h * dv:(h + 1) * dv].T.astype(BF16)
        vt_ref[0, h, 0, dv:dv + DA_ONES, :] = jnp.ones((DA_ONES, v.shape[0]), BF16)


def prep_attention(p3, cos, sin, na_k_w, da_q_w, da_k_w):
    bsz, lt, _ = p3.shape
    tm = PREP_TM
    w = BRANCH_W
    tile = lambda v, s=1.0: (jnp.tile(v, w // v.shape[0]) * s).reshape(1, w)
    tab_spec = pl.BlockSpec((tm, w), lambda b, i: (i, 0))
    vec = _const_spec((1, w))
    return pl.pallas_call(
        _prep_attn_kernel,
        grid=(bsz, lt // tm),
        in_specs=[_row_spec(CB_AK), _row_spec(CB_AV), _row_spec(CB_DQ), _row_spec(CB_DK), _row_spec(CB_DV),
                  tab_spec, tab_spec, _const_spec((w, w)), vec, vec, vec],
        out_specs=[pl.BlockSpec((1, tm, w), lambda b, i: (b, i, 0)),
                   pl.BlockSpec((1, tm, w), lambda b, i: (b, i, 0)),
                   pl.BlockSpec((1, w, tm), lambda b, i: (b, 0, i)),
                   pl.BlockSpec((1, 2 * DA_HEADS, tm, DA_DIM), lambda b, i: (b, 0, i, 0)),
                   pl.BlockSpec((1, DA_HEADS, 1, 2 * DA_DIM + DA_ONES, tm), lambda b, i: (b, 0, i, 0, 0))],
        out_shape=[jax.ShapeDtypeStruct((bsz, lt, w), BF16),
                   jax.ShapeDtypeStruct((bsz, lt, w), BF16),
                   jax.ShapeDtypeStruct((bsz, w, lt), BF16),
                   jax.ShapeDtypeStruct((bsz, 2 * DA_HEADS, lt, DA_DIM), BF16),
                   jax.ShapeDtypeStruct((bsz, DA_HEADS, lt // tm, 2 * DA_DIM + DA_ONES, tm), BF16)],
        compiler_params=_cparams(2),
        name="prep_attn",
    )(p3, p3, p3, p3, p3, cos, sin, _group_matrix(w, DA_DIM),
      tile(na_k_w), tile(da_q_w, DA_DIM ** -0.5 * math.log2(math.e)), tile(da_k_w))


def rope_tables(lt, n_ctx):
    nf = DA_DIM // 4
    t = jnp.arange(lt - n_ctx, dtype=jnp.int32)
    rows, cols = t // GRID_W, t % GRID_W
    inv = ROPE_BASE ** (-jnp.arange(nf, dtype=F32) / nf)
    ang_r = rows.astype(F32)[:, None] * inv
    ang_c = cols.astype(F32)[:, None] * inv
    ang = jnp.concatenate([ang_r, ang_r, ang_c, ang_c], axis=-1)
    ang = jnp.concatenate([jnp.zeros((n_ctx, DA_DIM), F32), ang], axis=0)
    reps = BRANCH_W // DA_DIM
    return jnp.tile(jnp.cos(ang), (1, reps)), jnp.tile(jnp.sin(ang), (1, reps))


def _flash_kernel(lam_ref, qt_ref, k_ref, vt_ref, g_ref, sw_ref, o_ref, *, n_latent_iters, out_scale):
    tq = qt_ref.shape[2]
    dve = vt_ref.shape[3]
    dv = dve - DA_ONES
    tk = vt_ref.shape[4]
    qts = [qt_ref[0, c * DA_DIM:(c + 1) * DA_DIM, :] for c in range(2)]

    def attend(carry, first_chunk, n_chunks):
        chunks = [first_chunk + g for g in range(n_chunks)]
        sts = [[_dot(k_ref[0, c, pl.ds(pl.multiple_of(ch * tk, tk), tk), :], qts[c]) for c in range(2)]
               for ch in chunks]
        carry = list(carry)
        for g, ch in enumerate(chunks):
            vt = vt_ref[0, 0, ch]
            for c in range(2):
                m, acc = carry[c]
                m_new = jnp.maximum(m, jnp.max(sts[g][c], axis=0, keepdims=True))
                pt = jnp.exp2(sts[g][c] - m_new).astype(BF16)
                carry[c] = (m_new, jnp.exp2(m - m_new) * acc + _dot(vt, pt))
        return tuple(carry)

    init = tuple((jnp.full((1, tq), NEG_INF, F32), jnp.zeros((dve, tq), F32)) for _ in range(2))
    carry = attend(init, 0, 1)
    n_iters = jnp.where(pl.program_id(2) == 0, 0, n_latent_iters)
    res = lax.fori_loop(0, n_iters, lambda i, cr: attend(cr, 1 + i * DA_NSUB, DA_NSUB), carry)
    outs = [acc[0:dv] / acc[dv:dv + 1] for (_, acc) in res]
    o = (outs[0] - lam_ref[0] * outs[1]).T
    ms = jnp.mean(o * o, axis=-1, keepdims=True)
    o = o * lax.rsqrt(ms + NORM_EPS) * (sw_ref[...] * out_scale)
    o_ref[0] = (o * _silu(g_ref[0])).astype(BF16)


def flash_diff_attention(qt, kh, vt, p3, lam, subln_w, lam_init):
    bsz, w, lt = qt.shape
    dv = 2 * DA_DIM
    nk = lt // DA_TK
    n_latent_iters = (nk - 1) // DA_NSUB
    kern = functools.partial(_flash_kernel, n_latent_iters=n_latent_iters, out_scale=1.0 - lam_init)
    g_blocks = BRANCH_W // dv
    return pl.pallas_call(
        kern,
        grid=(bsz, DA_HEADS, lt // DA_TQ),
        in_specs=[pl.BlockSpec(memory_space=pltpu.SMEM),
                  pl.BlockSpec((1, dv, DA_TQ), lambda b, h, qi: (b, h, qi)),
                  pl.BlockSpec((1, 2, lt, DA_DIM), lambda b, h, qi: (b, h, 0, 0)),
                  pl.BlockSpec((1, 1, nk, dv + DA_ONES, DA_TK), lambda b, h, qi: (b, h, 0, 0, 0)),
                  pl.BlockSpec((1, DA_TQ, dv), lambda b, h, qi: (b, qi, CB_DG * g_blocks + h)),
                  _const_spec((1, dv))],
        out_specs=pl.BlockSpec((1, DA_TQ, dv), lambda b, h, qi: (b, qi, h)),
        out_shape=jax.ShapeDtypeStruct((bsz, lt, BRANCH_W), BF16),
        compiler_params=_cparams(3),
        name="flash_attn",
    )(lam.reshape(1).astype(F32), qt, kh, vt, p3, subln_w.reshape(1, dv))


def _na_kernel(q_ref, g_ref, k_ref, v_ref, bias_ref, g64_ref, wq_ref, o_ref, *, n_rows, n_ctx):
    blk = pl.program_id(1)
    r = jnp.maximum(blk - n_ctx // GRID_W, 0)
    r0 = jnp.clip(r - NA_WIN_R // 2, 0, n_rows - NA_WIN_R)
    start = pl.multiple_of(n_ctx + r0 * GRID_W, GRID_W)
    n_loc = NA_WIN_R * GRID_W
    q = _head_rms(q_ref[0], g64_ref[...], NA_DIM, wq_ref[...]).astype(BF16)
    g = g_ref[0]
    hs = range(NA_HEADS)
    sls = [slice(h * NA_DIM, (h + 1) * NA_DIM) for h in hs]
    qh = [q[:, sl] for sl in sls]
    s_loc = [_dot_nt(qh[h], k_ref[0, pl.ds(start, n_loc), sls[h]]) + bias_ref[0, h] for h in hs]
    s_ctx = [_dot_nt(qh[h], k_ref[0, 0:n_ctx, sls[h]]) for h in hs]
    m = [jnp.maximum(jnp.max(s_loc[h], axis=-1, keepdims=True), jnp.max(s_ctx[h], axis=-1, keepdims=True))
         for h in hs]
    p_loc = [jnp.exp(s_loc[h] - m[h]) for h in hs]
    p_ctx = [jnp.exp(s_ctx[h] - m[h]) for h in hs]
    l = [jnp.sum(p_loc[h], axis=-1, keepdims=True) + jnp.sum(p_ctx[h], axis=-1, keepdims=True) for h in hs]
    o = [_dot(p_loc[h].astype(BF16), v_ref[0, pl.ds(start, n_loc), sls[h]])
         + _dot(p_ctx[h].astype(BF16), v_ref[0, 0:n_ctx, sls[h]]) for h in hs]
    for h in hs:
        o_ref[0, :, sls[h]] = ((o[h] / l[h]) * _silu(g[:, sls[h]])).astype(BF16)


def neighbourhood_attention(p3, kn, va, bias_tbl, na_q_w, n_ctx):
    bsz, lt, w = kn.shape
    n_rows = (lt - n_ctx) // GRID_W
    ncb = n_ctx // GRID_W
    half = NA_WIN_R // 2

    def bias_idx(b, blk):
        r = blk - ncb
        off = r - jnp.clip(r - half, 0, n_rows - NA_WIN_R)
        return (jnp.where(blk < ncb, NA_WIN_R, off), 0, 0, 0)

    full_spec = pl.BlockSpec((1, lt, w), lambda b, blk: (b, 0, 0))
    wq = (jnp.tile(na_q_w, w // NA_DIM) * NA_DIM ** -0.5).reshape(1, w)
    return pl.pallas_call(
        functools.partial(_na_kernel, n_rows=n_rows, n_ctx=n_ctx),
        grid=(bsz, lt // GRID_W),
        in_specs=[_row_spec(CB_AQ, GRID_W), _row_spec(CB_AG, GRID_W), full_spec, full_spec,
                  pl.BlockSpec((1, NA_HEADS, GRID_W, NA_WIN_R * GRID_W), bias_idx),
                  _const_spec((w, w)), _const_spec((1, w))],
        out_specs=pl.BlockSpec((1, GRID_W, w), lambda b, blk: (b, blk, 0)),
        out_shape=jax.ShapeDtypeStruct((bsz, lt, w), BF16),
        compiler_params=_cparams(2),
        name="nbr_attn",
    )(p3, p3, kn, va, bias_tbl, _group_matrix(w, NA_DIM), wq)


def na_bias_table(rpb):
    col = np.arange(GRID_W)
    c0 = np.clip(col - NA_WIN_C // 2, 0, GRID_W - NA_WIN_C)
    col_ok = (col[None, :] >= c0[:, None]) & (col[None, :] < c0[:, None] + NA_WIN_C)
    d_col = np.clip(col[None, :] - col[:, None] + (NA_WIN_C - 1), 0, 2 * NA_WIN_C - 2)
    onehot = (d_col[:, :, None] == np.arange(2 * NA_WIN_C - 1)).astype(np.float32)
    by_col = jnp.einsum('hrc,qwc->hrqw', rpb.astype(F32), jnp.asarray(onehot),
                        precision=lax.Precision.HIGHEST)
    by_col = jnp.where(jnp.asarray(col_ok)[None, None], by_col, NEG_INF)
    tbl = jnp.stack([by_col[:, NA_WIN_R - 1 - o:2 * NA_WIN_R - 1 - o] for o in range(NA_WIN_R)]
                    + [jnp.full((NA_HEADS, NA_WIN_R, GRID_W, GRID_W), NEG_INF, F32)])
    tbl = jnp.transpose(tbl, (0, 1, 3, 2, 4))
    return tbl.reshape(NA_WIN_R + 1, NA_HEADS, GRID_W, NA_WIN_R * GRID_W)


def _prep_rwkv_kernel(r_ref, rp_ref, rn_ref, k_ref, kp_ref, kn_ref, v_ref, vp_ref, vn_ref,
                      m_ref, mp_ref, mn_ref, mu_ref, w0_ref, a0_ref, w2_ref, a2_ref, kk_ref, ka_ref,
                      rk_ref, g64_ref,
                      ro_ref, vo_ref, ao_ref, lw_ref, kd_ref, bo_ref, bonus_ref, *, n_tiles, n_ctx_tiles):
    i = pl.program_id(1)
    w = BRANCH_W

    def shifted(x_ref, p_ref, n_ref, mu):
        x = x_ref[0]
        xp, xn = _neighbours(x, p_ref[0], n_ref[0], i, n_tiles, n_ctx_tiles)
        return x + (0.5 * (xp + xn) - x) * mu

    r = shifted(r_ref, rp_ref, rn_ref, mu_ref[0:1, :])
    k = shifted(k_ref, kp_ref, kn_ref, mu_ref[1:2, :])
    v = shifted(v_ref, vp_ref, vn_ref, mu_ref[2:3, :])
    misc = shifted(m_ref, mp_ref, mn_ref, mu_ref[3:4, :])
    wd = jnp.tanh(misc[:, 0:2 * RW_LORA_W]).astype(BF16)
    ad = misc[:, 2 * RW_LORA_W:MISC_DT].astype(BF16)
    w_log = w0_ref[...] + _dot(wd, w2_ref[...])
    gate = _sigmoid(a0_ref[...] + _dot(ad, a2_ref[...]))
    log_decay = -math.exp(-0.5) * _sigmoid(w_log)
    g64 = g64_ref[...]
    kk = k * kk_ref[...]
    kk = kk / jnp.maximum(jnp.sqrt(_dot_exact_rhs(kk * kk, g64)), 1e-12)
    ro_ref[0] = r
    vo_ref[0] = v.astype(BF16)
    ao_ref[0] = -kk
    coef = None
    for d in range(2):
        a_d = gate[:, d * w:(d + 1) * w]
        kd = k * (1.0 + (a_d - 1.0) * ka_ref[...])
        lw_ref[d, 0] = log_decay[:, d * w:(d + 1) * w]
        kd_ref[d, 0] = kd
        bo_ref[d, 0] = kk * a_d
        coef = kd if coef is None else coef + kd
    bonus_ref[0] = _dot_exact_rhs(r * coef * rk_ref[...], g64) * v


def prep_rwkv(p3, mu, w0, w2, a0, a2, k_k, k_a, r_k, n_ctx):
    bsz, lt, _ = p3.shape
    tm = PREP_TM
    w = BRANCH_W
    n_tiles = lt // tm
    mu4 = jnp.stack([mu[0:w], mu[w:2 * w], mu[2 * w:3 * w],
                     jnp.pad(mu[3 * w:], (0, w - (mu.shape[0] - 3 * w)))])
    zero = jnp.zeros((RW_LORA_W, w), F32)
    w2cat = jnp.concatenate([jnp.concatenate([w2[0], zero], axis=1),
                             jnp.concatenate([zero, w2[1]], axis=1)], axis=0).astype(BF16)
    a2cat = jnp.concatenate([jnp.concatenate([a2[0], zero], axis=1),
                             jnp.concatenate([zero, a2[1]], axis=1)], axis=0).astype(BF16)
    specs = []
    for cb in (CB_BR, CB_BK, CB_BV, CB_MISC):
        specs += [_row_spec(cb), *_halo_specs(cb, tm, lt)]
    vec = _const_spec((1, w))
    tok = pl.BlockSpec((1, tm, w), lambda b, i: (b, i, 0))
    tok2 = pl.BlockSpec((2, 1, tm, w), lambda b, i: (0, b, i, 0))
    kern = functools.partial(_prep_rwkv_kernel, n_tiles=n_tiles, n_ctx_tiles=n_ctx // tm)
    return pl.pallas_call(
        kern,
        grid=(bsz, n_tiles),
        in_specs=specs + [_const_spec((4, w)), _const_spec((1, 2 * w)), _const_spec((1, 2 * w)),
                          _const_spec((2 * RW_LORA_W, 2 * w)), _const_spec((2 * RW_LORA_A, 2 * w)),
                          vec, vec, vec, _const_spec((w, w))],
        out_specs=[tok, tok, tok, tok2, tok2, tok2, tok],
        out_shape=[jax.ShapeDtypeStruct((bsz, lt, w), F32),
                   jax.ShapeDtypeStruct((bsz, lt, w), BF16),
                   jax.ShapeDtypeStruct((bsz, lt, w), F32),
                   jax.ShapeDtypeStruct((2, bsz, lt, w), F32),
                   jax.ShapeDtypeStruct((2, bsz, lt, w), F32),
                   jax.ShapeDtypeStruct((2, bsz, lt, w), F32),
                   jax.ShapeDtypeStruct((bsz, lt, w), F32)],
        compiler_params=_cparams(2),
        name="prep_rwkv",
    )(*([p3] * 12), mu4, w0.reshape(1, 2 * w), a0.reshape(1, 2 * w), w2cat, a2cat,
      k_k.reshape(1, w), k_a.reshape(1, w), r_k.reshape(1, w), _group_matrix(w, RW_DIM))


def _rwkv_kernel(tri_ref, ms_ref, mi_ref, r_ref, v_ref, a_ref, lw_ref, kd_ref, b_ref, y_ref, s_ref):
    c = RW_CHUNK
    nsub = r_ref.shape[1] // c
    d = pl.program_id(0)

    @pl.when(pl.program_id(2) == 0)
    def _():
        s_ref[...] = jnp.zeros_like(s_ref)

    tri = tri_ref[0]
    m_strict = ms_ref[0] > 0.5
    m_incl = mi_ref[0] > 0.5
    rows = lax.broadcasted_iota(jnp.int32, (c, c), 0)
    cols = lax.broadcasted_iota(jnp.int32, (c, c), 1)
    eye = (rows == cols).astype(F32)
    same_half = (rows >= c // 2) == (cols >= c // 2)
    heads = [slice(h * RW_DIM, (h + 1) * RW_DIM) for h in range(RW_HEADS)]

    row_sl, e_tot = [], []
    al, rh, rf, be, ka, bc, kc, vb = [], [], [], [], [], [], [], []
    for j in range(nsub):
        off = pl.multiple_of(jnp.where(d == 0, j, nsub - 1 - j) * c, c)
        rs = pl.ds(off, c)
        row_sl.append(rs)
        lw = lw_ref[0, 0, rs, :]
        cum = _dot_exact_lhs(tri, lw)
        tot = jnp.sum(lw, axis=0, keepdims=True)
        e_m = jnp.exp(-cum)
        e_t = jnp.exp(tot - cum)
        e_tot.append(jnp.exp(tot))
        b_in = b_ref[0, 0, rs, :]
        k_in = kd_ref[0, 0, rs, :]
        alpha = (a_ref[0, rs, :] * jnp.exp(cum - lw)).astype(BF16)
        rho_f = r_ref[0, rs, :] * jnp.exp(cum)
        rho = rho_f.astype(BF16)
        beta = (b_in * e_m).astype(BF16)
        kappa = (k_in * e_m).astype(BF16)
        beta_c = (b_in * e_t).astype(BF16)
        kappa_c = (k_in * e_t).astype(BF16)
        v_b = v_ref[0, rs, :]
        for sl in heads:
            al.append(alpha[:, sl])
            rh.append(rho[:, sl])
            rf.append(rho_f[:, sl])
            be.append(beta[:, sl])
            ka.append(kappa[:, sl])
            bc.append(beta_c[:, sl])
            kc.append(kappa_c[:, sl])
            vb.append(v_b[:, sl])

    units = range(nsub * RW_HEADS)
    l_ab = [jnp.where(m_strict, _dot_nt(al[u], be[u]), 0.0) for u in units]
    l_ak = [jnp.where(m_strict, _dot_nt(al[u], ka[u]), 0.0).astype(BF16) for u in units]
    t_rb = [jnp.where(m_incl, _dot_nt(rh[u], be[u]), 0.0).astype(BF16) for u in units]
    t_rk = [jnp.where(m_incl, _dot_nt(rh[u], ka[u]), 0.0).astype(BF16) for u in units]
    l_d = [jnp.where(same_half, l_ab[u], 0.0) for u in units]
    l_o = [(l_ab[u] - l_d[u]).astype(BF16) for u in units]
    pw = [l_d[u].astype(BF16) for u in units]
    td = [eye + l_d[u] for u in units]
    for _ in range(int(math.log2(c)) - 2):
        pw = [_dot(pw[u], pw[u]).astype(BF16) for u in units]
        td = [td[u] + _dot(td[u].astype(BF16), pw[u]) for u in units]
    td_b = [td[u].astype(BF16) for u in units]
    x_o = [_dot(td_b[u], l_o[u]).astype(BF16) for u in units]
    tinv = [(td[u] + _dot(x_o[u], td_b[u])).astype(BF16) for u in units]
    akv = [_dot(l_ak[u], vb[u]).astype(BF16) for u in units]
    a_hat = [_dot(tinv[u], al[u]).astype(BF16) for u in units]
    v_hat = [_dot(tinv[u], akv[u]).astype(BF16) for u in units]
    r_hat = [(rf[u] + _dot(t_rb[u], a_hat[u])).astype(BF16) for u in units]
    y_hat = [_dot(t_rb[u], v_hat[u]) + _dot(t_rk[u], vb[u]) for u in units]
    q_mat = [_dot_tn(a_hat[u], bc[u]).astype(BF16) for u in units]
    n_mat = [_dot_tn(v_hat[u], bc[u]) + _dot_tn(vb[u], kc[u]) for u in units]

    state = [s_ref[h] for h in range(RW_HEADS)]
    for j in range(nsub):
        for h, sl in enumerate(heads):
            u = j * RW_HEADS + h
            s_b = state[h].astype(BF16)
            y_ref[0, 0, row_sl[j], sl] = _dot_nt(r_hat[u], s_b) + y_hat[u]
            state[h] = state[h] * e_tot[j][:, sl] + _dot(s_b, q_mat[u]) + n_mat[u]
    for h in range(RW_HEADS):
        s_ref[h] = state[h]


def _scan_chunk_index(d, s, n_ctx_chunks, n_chunks):
    rev = jnp.where(s < n_ctx_chunks, n_ctx_chunks - 1 - s, n_chunks + n_ctx_chunks - 1 - s)
    return jnp.where(d == 0, s, rev)


def _direction_masks(c):
    i = np.arange(c)
    lower = (i[None, :] <= i[:, None]).astype(np.float32)
    tri = np.stack([lower, lower.T])
    strict = np.stack([lower - np.eye(c, dtype=np.float32), lower.T - np.eye(c, dtype=np.float32)])
    return tri, strict


def rwkv_scan(r, v, a, lw, kd, b, n_ctx):
    bsz, lt, w = r.shape
    c = RW_CHUNK
    blk = RW_NSUB * c
    nck = lt // blk
    ncc = n_ctx // blk
    tri, strict = _direction_masks(c)
    shared = pl.BlockSpec((1, blk, w), lambda d, bi, s: (bi, _scan_chunk_index(d, s, ncc, nck), 0))
    perdir = pl.BlockSpec((1, 1, blk, w), lambda d, bi, s: (d, bi, _scan_chunk_index(d, s, ncc, nck), 0))
    mask_spec = pl.BlockSpec((1, c, c), lambda d, bi, s: (d, 0, 0))
    return pl.pallas_call(
        _rwkv_kernel,
        grid=(2, bsz, nck),
        in_specs=[mask_spec, mask_spec, mask_spec, shared, shared, shared, perdir, perdir, perdir],
        out_specs=perdir,
        out_shape=jax.ShapeDtypeStruct((2, bsz, lt, w), F32),
        scratch_shapes=[pltpu.VMEM((RW_HEADS, RW_DIM, RW_DIM), F32)],
        compiler_params=_cparams(3),
        name="rwkv_scan",
    )(jnp.asarray(tri, BF16), jnp.asarray(strict, F32), jnp.asarray(tri, F32), r, v, a, lw, kd, b)


def _finish_rwkv_kernel(y_ref, bonus_ref, g_ref, lnw_ref, lnb_ref, g64_ref, o_ref):
    y = y_ref[0, 0] + y_ref[1, 0]
    g64 = g64_ref[...]
    mean = _dot_exact_rhs(y, g64) * (1.0 / RW_DIM)
    yc = y - mean
    var = _dot_exact_rhs(yc * yc, g64) * (1.0 / RW_DIM)
    yn = yc * lax.rsqrt(var + RW_GN_EPS) * lnw_ref[...] + lnb_ref[...]
    o_ref[0] = ((yn + bonus_ref[0]) * _silu(g_ref[0])).astype(BF16)


def finish_rwkv(y, bonus, p3, ln_w, ln_b):
    _, bsz, lt, w = y.shape
    tm = PREP_TM
    tok = pl.BlockSpec((1, tm, w), lambda b, i: (b, i, 0))
    vec = _const_spec((1, w))
    return pl.pallas_call(
        _finish_rwkv_kernel,
        grid=(bsz, lt // tm),
        in_specs=[pl.BlockSpec((2, 1, tm, w), lambda b, i: (0, b, i, 0)), tok, _row_spec(CB_BG),
                  vec, vec, _const_spec((w, w))],
        out_specs=tok,
        out_shape=jax.ShapeDtypeStruct((bsz, lt, w), BF16),
        compiler_params=_cparams(2),
        name="finish_rwkv",
    )(y, bonus, p3, ln_w.reshape(1, w), ln_b.reshape(1, w), _group_matrix(w, RW_DIM))


def _prep_ssd_kernel(x_ref, xp_ref, xn_ref, bc_ref, bcp_ref, bcn_ref, m_ref, cw_ref, cb_ref, dtb_ref,
                     aneg_ref, exp_ref, dsk_ref,
                     xq_ref, bco_ref, bt_ref, a_ref, at_ref, dskip_ref, *, n_tiles, n_ctx_tiles):
    i = pl.program_id(1)
    w = BRANCH_W

    def conv(x_ref, p_ref, n_ref, half):
        x = x_ref[0]
        xp, xn = _neighbours(x, p_ref[0], n_ref[0], i, n_tiles, n_ctx_tiles)
        lo = half * w
        y = (xp * cw_ref[0:1, lo:lo + w] + x * cw_ref[1:2, lo:lo + w] + xn * cw_ref[2:3, lo:lo + w]
             + cb_ref[:, lo:lo + w])
        return _silu(y)

    xs = conv(x_ref, xp_ref, xn_ref, 0)
    bc = conv(bc_ref, bcp_ref, bcn_ref, 1)
    bco_ref[0] = bc.astype(BF16)
    bt_ref[0] = bc[:, 0:SSM_GROUPS * SSM_STATE].T.astype(BF16)
    dt = _softplus(m_ref[0][:, MISC_DT:MISC_DT + LANES] + dtb_ref[...])
    lane = lax.broadcasted_iota(jnp.int32, (1, LANES), 1)
    dt = jnp.where(lane < 2 * SSM_HEADS, dt, 0.0)
    dtx = _dot_exact_rhs(dt, exp_ref[...])
    a_all = dt * aneg_ref[...]
    first = lane < SSM_HEADS
    a_dirs = [jnp.where(first, a_all, 0.0), jnp.where(first, pltpu.roll(a_all, LANES - SSM_HEADS, 1), 0.0)]
    for d in range(2):
        xq_ref[d, 0] = xs * dtx[:, d * w:(d + 1) * w]
        a_ref[d, 0] = a_dirs[d]
        at_ref[d, 0] = a_dirs[d].T[0:2 * SUBLANES, :]
    dskip_ref[0] = xs * dsk_ref[...]


def prep_ssd(p3, conv_w, conv_b, dt_bias, a_log, d_skip, n_ctx):
    bsz, lt, _ = p3.shape
    tm = PREP_TM
    w = BRANCH_W
    n_tiles = lt // tm
    nh2 = 2 * SSM_HEADS
    pad_lanes = lambda v: jnp.pad(v.reshape(1, nh2), ((0, 0), (0, LANES - nh2)))
    expand = np.zeros((LANES, 2 * w), np.float32)
    for d in range(2):
        for h in range(SSM_HEADS):
            expand[d * SSM_HEADS + h, d * w + h * SSM_HEAD_DIM:d * w + (h + 1) * SSM_HEAD_DIM] = 1.0
    specs = [_row_spec(CB_CX), *_halo_specs(CB_CX, tm, lt), _row_spec(CB_CBC), *_halo_specs(CB_CBC, tm, lt),
             _row_spec(CB_MISC)]
    tok = pl.BlockSpec((1, tm, w), lambda b, i: (b, i, 0))
    gs = SSM_GROUPS * SSM_STATE
    kern = functools.partial(_prep_ssd_kernel, n_tiles=n_tiles, n_ctx_tiles=n_ctx // tm)
    return pl.pallas_call(
        kern,
        grid=(bsz, n_tiles),
        in_specs=specs + [_const_spec((3, 2 * w)), _const_spec((1, 2 * w)), _const_spec((1, LANES)),
                          _const_spec((1, LANES)), _const_spec((LANES, 2 * w)), _const_spec((1, w))],
        out_specs=[pl.BlockSpec((2, 1, tm, w), lambda b, i: (0, b, i, 0)), tok,
                   pl.BlockSpec((1, gs, tm), lambda b, i: (b, 0, i)),
                   pl.BlockSpec((2, 1, tm, LANES), lambda b, i: (0, b, i, 0)),
                   pl.BlockSpec((2, 1, 2 * SUBLANES, tm), lambda b, i: (0, b, 0, i)), tok],
        out_shape=[jax.ShapeDtypeStruct((2, bsz, lt, w), F32),
                   jax.ShapeDtypeStruct((bsz, lt, w), BF16),
                   jax.ShapeDtypeStruct((bsz, gs, lt), BF16),
                   jax.ShapeDtypeStruct((2, bsz, lt, LANES), F32),
                   jax.ShapeDtypeStruct((2, bsz, 2 * SUBLANES, lt), F32),
                   jax.ShapeDtypeStruct((bsz, lt, w), F32)],
        compiler_params=_cparams(2),
        name="prep_ssd",
    )(*([p3] * 7), conv_w, conv_b.reshape(1, 2 * w), pad_lanes(dt_bias), pad_lanes(-jnp.exp(a_log)),
      jnp.asarray(expand, BF16), jnp.repeat(d_skip, SSM_HEAD_DIM).reshape(1, w))


def _ssd_kernel(tri_ref, mi_ref, xq_ref, bc_ref, bt_ref, a_ref, at_ref, y_ref, s_ref):
    q = SSM_CHUNK
    rep = SSM_HEADS // SSM_GROUPS
    gw = SSM_GROUPS * SSM_STATE

    @pl.when(pl.program_id(2) == 0)
    def _():
        s_ref[...] = jnp.zeros_like(s_ref)

    tri = tri_ref[0]
    mask = mi_ref[0] > 0.5
    a = a_ref[0, 0]
    a_t = at_ref[0, 0]
    xq = xq_ref[0, 0]
    bm = bc_ref[0, :, 0:gw]
    cm = bc_ref[0, :, gw:2 * gw]
    bt = bt_ref[0]
    hs = range(SSM_HEADS)
    p = SSM_HEAD_DIM
    acol = _dot_exact_lhs(tri, a)
    arow = _dot_nt_exact_lhs_f32(a_t, tri)
    tot = jnp.sum(a, axis=0, keepdims=True)
    groups = [slice(g * SSM_STATE, (g + 1) * SSM_STATE) for g in range(SSM_GROUPS)]
    cms = [cm[:, gs] for gs in groups]
    cb = [_dot_nt(cms[g], bm[:, gs]) for g, gs in enumerate(groups)]
    ac = [jnp.broadcast_to(acol[:, h:h + 1], (q, q)) for h in hs]
    ar = [jnp.broadcast_to(arow[h:h + 1, :], (q, q)) for h in hs]
    gmat = [(cb[h // rep] * jnp.exp(jnp.where(mask, ac[h] - ar[h], NEG_INF))).astype(BF16) for h in hs]
    xh = [xq[:, h * p:(h + 1) * p] for h in hs]
    tot_h = [jnp.broadcast_to(tot[:, h:h + 1], (1, p)) for h in hs]
    s0 = [s_ref[h] for h in hs]
    y_in = [_dot(gmat[h], xh[h].astype(BF16)) for h in hs]
    y_st = [_dot(cms[h // rep], s0[h].astype(BF16)) for h in hs]
    xd = [(xh[h] * jnp.exp(tot_h[h] - ac[h][:, :p])).astype(BF16) for h in hs]
    upd = [_dot(bt[groups[h // rep], :], xd[h]) for h in hs]
    for h in hs:
        y_ref[0, 0, :, h * p:(h + 1) * p] = y_in[h] + y_st[h] * jnp.exp(ac[h][:, :p])
        s_ref[h] = s0[h] * jnp.exp(tot_h[h]) + upd[h]


def ssd_scan(xq, bc, bt, a, a_t, n_ctx):
    _, bsz, lt, w = xq.shape
    q = SSM_CHUNK
    nck = lt // q
    ncc = n_ctx // q
    gw = bt.shape[1]
    tri, _ = _direction_masks(q)

    def cidx(d, s):
        return _scan_chunk_index(d, s, ncc, nck)

    mask_spec = pl.BlockSpec((1, q, q), lambda d, bi, s: (d, 0, 0))
    return pl.pallas_call(
        _ssd_kernel,
        grid=(2, bsz, nck),
        in_specs=[mask_spec, mask_spec,
                  pl.BlockSpec((1, 1, q, w), lambda d, bi, s: (d, bi, cidx(d, s), 0)),
                  pl.BlockSpec((1, q, w), lambda d, bi, s: (bi, cidx(d, s), 0)),
                  pl.BlockSpec((1, gw, q), lambda d, bi, s: (bi, 0, cidx(d, s))),
                  pl.BlockSpec((1, 1, q, LANES), lambda d, bi, s: (d, bi, cidx(d, s), 0)),
                  pl.BlockSpec((1, 1, 2 * SUBLANES, q), lambda d, bi, s: (d, bi, 0, cidx(d, s)))],
        out_specs=pl.BlockSpec((1, 1, q, w), lambda d, bi, s: (d, bi, cidx(d, s), 0)),
        out_shape=jax.ShapeDtypeStruct((2, bsz, lt, w), F32),
        scratch_shapes=[pltpu.VMEM((SSM_HEADS, SSM_STATE, SSM_HEAD_DIM), F32)],
        compiler_params=_cparams(3),
        name="ssd_scan",
    )(jnp.asarray(tri, BF16), jnp.asarray(tri, F32), xq, bc, bt, a, a_t)


def _finish_ssd_kernel(y_ref, dskip_ref, z_ref, nw_ref, g256_ref, o_ref):
    y = y_ref[0, 0] + y_ref[1, 0] + dskip_ref[0]
    g = y * _silu(z_ref[0])
    group = BRANCH_W // SSM_GROUPS
    o_ref[0] = _head_rms(g, g256_ref[...], group, nw_ref[...]).astype(BF16)


def finish_ssd(y, dskip, p3, norm_w):
    _, bsz, lt, w = y.shape
    tm = PREP_TM
    tok = pl.BlockSpec((1, tm, w), lambda b, i: (b, i, 0))
    return pl.pallas_call(
        _finish_ssd_kernel,
        grid=(bsz, lt // tm),
        in_specs=[pl.BlockSpec((2, 1, tm, w), lambda b, i: (0, b, i, 0)), tok, _row_spec(CB_CZ),
                  _const_spec((1, w)), _const_spec((w, w))],
        out_specs=tok,
        out_shape=jax.ShapeDtypeStruct((bsz, lt, w), BF16),
        compiler_params=_cparams(2),
        name="finish_ssd",
    )(y, dskip, p3, norm_w.reshape(1, w), _group_matrix(w, w // SSM_GROUPS))


def _permute_w_in(w_in_l):
    w = BRANCH_W
    b0 = 4 * w
    lora0 = b0 + 3 * w
    bg0 = lora0 + MISC_DT
    c0 = bg0 + w
    dt0 = c0 + SSM_CONV_CH
    z0 = dt0 + 2 * SSM_HEADS
    d0 = z0 + w
    end = d0 + 4 * w
    parts = [w_in_l[:, 0:lora0], w_in_l[:, bg0:c0], w_in_l[:, c0:dt0], w_in_l[:, z0:d0], w_in_l[:, d0:end],
             w_in_l[:, lora0:bg0], w_in_l[:, dt0:z0]]
    used = sum(p.shape[1] for p in parts)
    pad = jnp.zeros((w_in_l.shape[0], IN_W_PAD - used), BF16)
    return jnp.concatenate([p.astype(BF16) for p in parts] + [pad], axis=1)


def kernel(x, c, ctx, c_ctx, norm_w, w_ada, b_ada, w_in, na_q_norm, na_k_norm, na_rpb, rw_mu, rw_w0, rw_w2, rw_a0, rw_a2, rw_k_k, rw_k_a, rw_r_k, rw_ln_w, rw_ln_b, ssm_conv_w, ssm_conv_b, ssm_dt_bias, ssm_A_log, ssm_D, ssm_norm_w, da_q_norm, da_k_norm, da_lq1, da_lk1, da_lq2, da_lk2, da_subln, w_gate, w_up, w_out):
    bsz, seq, d = x.shape
    n_ctx = ctx.shape[1]
    lt = n_ctx + seq
    depth = w_in.shape[0]
    assert n_ctx == CTX_LEN == PREP_TM == DA_TK and seq % (GRID_W * NA_WIN_R) == 0
    assert ((lt // DA_TK) - 1) % DA_NSUB == 0
    cond = jnp.concatenate([_silu(c), _silu(c_ctx)[None], jnp.zeros((SUBLANES - bsz - 1, d), F32)], axis=0)
    xs2 = jnp.concatenate([ctx, x], axis=1).reshape(bsz * lt, d)
    cos, sin = rope_tables(lt, n_ctx)
    for l in range(depth):
        lam_init = 0.8 - 0.6 * math.exp(-0.3 * l)
        mod = small_matmul(cond, w_ada[l], b_ada[l])
        shift, scale, gate = jnp.split(mod[:bsz], 3, axis=-1)
        shift_c, scale_c, gate_c = jnp.split(mod[bsz], 3, axis=-1)
        p2, h = in_projection(xs2, norm_w[l], scale, shift, scale_c, shift_c, _permute_w_in(w_in[l]), lt, n_ctx)
        p3 = p2.reshape(bsz, lt, IN_W_PAD)

        kn, va, qt, kh, vt = prep_attention(p3, cos, sin, na_k_norm[l], da_q_norm[l], da_k_norm[l])
        oa = neighbourhood_attention(p3, kn, va, na_bias_table(na_rpb[l]), na_q_norm[l], n_ctx)
        lam = jnp.exp(jnp.sum(da_lq1[l] * da_lk1[l])) - jnp.exp(jnp.sum(da_lq2[l] * da_lk2[l])) + lam_init
        od = flash_diff_attention(qt, kh, vt, p3, lam, da_subln[l], lam_init)

        r, vb, a, lw, kd, b, bonus = prep_rwkv(p3, rw_mu[l], rw_w0[l], rw_w2[l], rw_a0[l], rw_a2[l],
                                              rw_k_k[l], rw_k_a[l], rw_r_k[l].reshape(-1), n_ctx)
        ob = finish_rwkv(rwkv_scan(r, vb, a, lw, kd, b, n_ctx), bonus, p3, rw_ln_w[l], rw_ln_b[l])

        xq, bc, bt, sa, sat, dskip = prep_ssd(p3, ssm_conv_w[l], ssm_conv_b[l], ssm_dt_bias[l], ssm_A_log[l],
                                              ssm_D[l], n_ctx)
        om = finish_ssd(ssd_scan(xq, bc, bt, sa, sat, n_ctx), dskip, p3, ssm_norm_w[l])

        ys = [t.reshape(bsz * lt, BRANCH_W) for t in (oa, ob, om, od)]
        acc = gate_up(h, ys, w_gate[l].astype(BF16), w_up[l].astype(BF16), lt)
        xs2 = out_projection(acc, w_out[l].astype(BF16), xs2, gate, gate_c, lt, n_ctx)
    return xs2.reshape(bsz, lt, d)[:, n_ctx:]
```

```python
import functools
import math

import numpy as np
import jax
import jax.numpy as jnp
from jax import lax
from jax.experimental import pallas as pl
from jax.experimental.pallas import tpu as pltpu

F32 = jnp.float32
BF16 = jnp.bfloat16

D_MODEL = 2048
GRID_W = 64
CTX_LEN = 256
N_BRANCH = 4
BRANCH_W = D_MODEL // N_BRANCH
NORM_EPS = 1e-6
NEG_INF = -1e30

NA_DIM = 64
NA_HEADS = BRANCH_W // NA_DIM
NA_WIN_R = 8
NA_WIN_C = 16

RW_DIM = 64
RW_HEADS = BRANCH_W // RW_DIM
RW_LORA_W = 64
RW_LORA_A = 64
RW_GN_EPS = 64e-5
RW_CHUNK = 64
RW_NSUB = 2

SSM_HEAD_DIM = 64
SSM_HEADS = BRANCH_W // SSM_HEAD_DIM
SSM_GROUPS = 2
SSM_STATE = 128
SSM_CHUNK = 128
SSM_CONV_CH = BRANCH_W + 2 * SSM_GROUPS * SSM_STATE

DA_DIM = 64
DA_HEADS = BRANCH_W // (2 * DA_DIM)
ROPE_BASE = 10000.0
DA_TQ = 256
DA_TK = 256
DA_NSUB = 32
DA_ONES = 16

SUBLANES = 8
LANES = 128

CB_AQ, CB_AK, CB_AV, CB_AG = 0, 1, 2, 3
CB_BR, CB_BK, CB_BV, CB_BG = 4, 5, 6, 7
CB_CX, CB_CBC, CB_CZ = 8, 9, 10
CB_DQ, CB_DK, CB_DV, CB_DG = 11, 12, 13, 14
CB_MISC = 15
MISC_DT = 2 * RW_LORA_W + 2 * RW_LORA_A
N_COL_BLOCKS = 16
IN_W_PAD = N_COL_BLOCKS * BRANCH_W
PREP_TM = 256

V7X_VMEM_BYTES = 64 * 1024 * 1024
VMEM_LIMIT = V7X_VMEM_BYTES * 7 // 8


def _cparams(n_axes):
    return pltpu.CompilerParams(dimension_semantics=("arbitrary",) * n_axes, vmem_limit_bytes=VMEM_LIMIT)


def _dot(a, b):
    return jnp.dot(a, b, preferred_element_type=F32)


def _dot_nt(a, b):
    return lax.dot_general(a, b, (((1,), (1,)), ((), ())), preferred_element_type=F32)


def _dot_tn(a, b):
    return lax.dot_general(a, b, (((0,), (0,)), ((), ())), preferred_element_type=F32)


def _split3(x):
    hi = x.astype(BF16)
    r1 = x - hi.astype(F32)
    mid = r1.astype(BF16)
    lo = (r1 - mid.astype(F32)).astype(BF16)
    return hi, mid, lo


def _dot_exact_lhs(m_bf16, x):
    hi, mid, lo = _split3(x)
    return _dot(m_bf16, hi) + _dot(m_bf16, mid) + _dot(m_bf16, lo)


def _dot_exact_rhs(x, m_bf16):
    hi, mid, lo = _split3(x)
    return _dot(hi, m_bf16) + _dot(mid, m_bf16) + _dot(lo, m_bf16)


def _dot_nt_exact_lhs_f32(x, m_bf16):
    hi, mid, lo = _split3(x)
    return _dot_nt(hi, m_bf16) + _dot_nt(mid, m_bf16) + _dot_nt(lo, m_bf16)


def _sigmoid(x):
    return 1.0 / (1.0 + jnp.exp(-x))


def _silu(x):
    return x * _sigmoid(x)


def _softplus(x):
    return jnp.maximum(x, 0.0) + jnp.log(1.0 + jnp.exp(-jnp.abs(x)))


def _group_matrix(width, group):
    g = np.arange(width) // group
    return jnp.asarray((g[:, None] == g[None, :]).astype(np.float32), BF16)


def _head_rms(x, gmat, group, w):
    ms = _dot_exact_rhs(x * x, gmat) * (1.0 / group)
    return x * lax.rsqrt(ms + NORM_EPS) * w


def _tile_rows(n_rows):
    return 768 if n_rows % 768 == 0 else 256


def _row_spec(cb, tm=PREP_TM):
    return pl.BlockSpec((1, tm, BRANCH_W), lambda b, i: (b, i, cb))


def _halo_specs(cb, tm, lt):
    per = tm // SUBLANES
    last = lt // SUBLANES - 1
    prev = pl.BlockSpec((1, SUBLANES, BRANCH_W), lambda b, i: (b, jnp.maximum(i * per - 1, 0), cb))
    nxt = pl.BlockSpec((1, SUBLANES, BRANCH_W), lambda b, i: (b, jnp.minimum((i + 1) * per, last), cb))
    return prev, nxt


def _const_spec(shape):
    return pl.BlockSpec(shape, lambda *_: (0,) * len(shape))


def _neighbours(x, prev_blk, next_blk, i, n_tiles, n_ctx_tiles):
    tm = x.shape[0]
    row = lax.broadcasted_iota(jnp.int32, (tm, 1), 0)
    seg_start = jnp.logical_or(i == 0, i == n_ctx_tiles)
    seg_end = jnp.logical_or(i == n_ctx_tiles - 1, i == n_tiles - 1)
    prev_row = jnp.where(seg_start, 0.0, prev_blk[SUBLANES - 1:SUBLANES, :])
    next_row = jnp.where(seg_end, 0.0, next_blk[0:1, :])
    x_prev = jnp.where(row == 0, prev_row, pltpu.roll(x, 1, 0))
    x_next = jnp.where(row == tm - 1, next_row, pltpu.roll(x, tm - 1, 0))
    return x_prev, x_next


def _small_mm_kernel(a_ref, w_ref, b_ref, o_ref):
    o_ref[...] = _dot(a_ref[...].astype(BF16), w_ref[...].astype(BF16)) + b_ref[...]


def small_matmul(a, w, b, tn=512):
    m, k = a.shape
    n = w.shape[1]
    return pl.pallas_call(
        _small_mm_kernel,
        grid=(n // tn,),
        in_specs=[pl.BlockSpec((m, k), lambda j: (0, 0)),
                  pl.BlockSpec((k, tn), lambda j: (0, j)),
                  pl.BlockSpec((1, tn), lambda j: (0, j))],
        out_specs=pl.BlockSpec((m, tn), lambda j: (0, j)),
        out_shape=jax.ShapeDtypeStruct((m, n), F32),
        compiler_params=_cparams(1),
        name="adaln_mm",
    )(a, w, b.reshape(1, n))


def _inproj_kernel(x_ref, nw_ref, sc_ref, sh_ref, scc_ref, shc_ref, w_ref, p_ref, h_ref, hs_ref, *,
                   tiles_per_batch, n_ctx):
    @pl.when(pl.program_id(1) == 0)
    def _():
        xf = x_ref[...]
        tm = xf.shape[0]
        ms = jnp.mean(xf * xf, axis=-1, keepdims=True)
        y = xf * lax.rsqrt(ms + NORM_EPS) * nw_ref[...]
        row = (pl.program_id(0) % tiles_per_batch) * tm + lax.broadcasted_iota(jnp.int32, (tm, 1), 0)
        is_ctx = row < n_ctx
        sc = jnp.where(is_ctx, scc_ref[...], sc_ref[0])
        sh = jnp.where(is_ctx, shc_ref[...], sh_ref[0])
        h = (y * (1.0 + sc) + sh).astype(BF16)
        hs_ref[...] = h
        h_ref[...] = h

    p_ref[...] = _dot(hs_ref[...], w_ref[...])


def in_projection(xs2, norm_w, scale, shift, scale_c, shift_c, w_bf16, lt, n_ctx, tn=1024):
    m, d = xs2.shape
    n = w_bf16.shape[1]
    tm = _tile_rows(lt)
    tpb = lt // tm
    nb = scale.shape[0]
    kern = functools.partial(_inproj_kernel, tiles_per_batch=tpb, n_ctx=n_ctx)
    return pl.pallas_call(
        kern,
        grid=(m // tm, n // tn),
        in_specs=[pl.BlockSpec((tm, d), lambda i, j: (i, 0)),
                  pl.BlockSpec((1, d), lambda i, j: (0, 0)),
                  pl.BlockSpec((1, 1, d), lambda i, j: (i // tpb, 0, 0)),
                  pl.BlockSpec((1, 1, d), lambda i, j: (i // tpb, 0, 0)),
                  pl.BlockSpec((1, d), lambda i, j: (0, 0)),
                  pl.BlockSpec((1, d), lambda i, j: (0, 0)),
                  pl.BlockSpec((d, tn), lambda i, j: (0, j))],
        out_specs=[pl.BlockSpec((tm, tn), lambda i, j: (i, j)),
                   pl.BlockSpec((tm, d), lambda i, j: (i, 0))],
        out_shape=[jax.ShapeDtypeStruct((m, n), F32),
                   jax.ShapeDtypeStruct((m, d), BF16)],
        scratch_shapes=[pltpu.VMEM((tm, d), BF16)],
        compiler_params=_cparams(2),
        name="in_proj",
    )(xs2, norm_w.reshape(1, d), scale.reshape(nb, 1, d), shift.reshape(nb, 1, d),
      scale_c.reshape(1, d), shift_c.reshape(1, d), w_bf16)


def _gate_up_kernel(h_ref, ya_ref, yb_ref, yc_ref, yd_ref, wg_ref, wu_ref, o_ref):
    h = h_ref[...]
    acc = None
    for i, y_ref in enumerate((ya_ref, yb_ref, yc_ref, yd_ref)):
        g = _dot(h, wg_ref[i])
        u = _dot(y_ref[...], wu_ref[i])
        t = _sigmoid(g) * u
        acc = t if acc is None else acc + t
    o_ref[...] = acc.astype(BF16)


def gate_up(h, ys, wg_bf16, wu_bf16, lt, tn=512):
    m, d = h.shape
    w = ys[0].shape[1]
    tm = _tile_rows(lt)
    y_spec = pl.BlockSpec((tm, w), lambda i, j: (i, 0))
    return pl.pallas_call(
        _gate_up_kernel,
        grid=(m // tm, d // tn),
        in_specs=[pl.BlockSpec((tm, d), lambda i, j: (i, 0)), y_spec, y_spec, y_spec, y_spec,
                  pl.BlockSpec((N_BRANCH, d, tn), lambda i, j: (0, 0, j)),
                  pl.BlockSpec((N_BRANCH, w, tn), lambda i, j: (0, 0, j))],
        out_specs=pl.BlockSpec((tm, tn), lambda i, j: (i, j)),
        out_shape=jax.ShapeDtypeStruct((m, d), BF16),
        compiler_params=_cparams(2),
        name="gate_up",
    )(h, *ys, wg_bf16, wu_bf16)


def _out_proj_kernel(a_ref, w_ref, x_ref, g_ref, gc_ref, o_ref, *, tiles_per_batch, n_ctx):
    tm = x_ref.shape[0]
    row = (pl.program_id(0) % tiles_per_batch) * tm + lax.broadcasted_iota(jnp.int32, (tm, 1), 0)
    gate = jnp.where(row < n_ctx, gc_ref[...], g_ref[0])
    o_ref[...] = x_ref[...] + gate * _dot(a_ref[...], w_ref[...])


def out_projection(acc, w_bf16, xs2, gate, gate_c, lt, n_ctx, tn=512):
    m, d = xs2.shape
    tm = _tile_rows(lt)
    tpb = lt // tm
    nb = gate.shape[0]
    kern = functools.partial(_out_proj_kernel, tiles_per_batch=tpb, n_ctx=n_ctx)
    return pl.pallas_call(
        kern,
        grid=(m // tm, d // tn),
        in_specs=[pl.BlockSpec((tm, d), lambda i, j: (i, 0)),
                  pl.BlockSpec((d, tn), lambda i, j: (0, j)),
                  pl.BlockSpec((tm, tn), lambda i, j: (i, j)),
                  pl.BlockSpec((1, 1, tn), lambda i, j: (i // tpb, 0, j)),
                  pl.BlockSpec((1, tn), lambda i, j: (0, j))],
        out_specs=pl.BlockSpec((tm, tn), lambda i, j: (i, j)),
        out_shape=jax.ShapeDtypeStruct((m, d), F32),
        compiler_params=_cparams(2),
        name="out_proj",
    )(acc, w_bf16, xs2, gate.reshape(nb, 1, d), gate_c.reshape(1, d))


def _rope(x, cos, sin):
    w = x.shape[1]
    lane = lax.broadcasted_iota(jnp.int32, (1, w), 1)
    first = ((lane // (DA_DIM // 4)) % 2) == 0
    rot = jnp.where(first, -pltpu.roll(x, w - DA_DIM // 4, 1), pltpu.roll(x, DA_DIM // 4, 1))
    return x * cos + rot * sin


def _prep_attn_kernel(ak_ref, av_ref, dq_ref, dk_ref, dv_ref, cos_ref, sin_ref, g64_ref,
                      wak_ref, wdq_ref, wdk_ref, kn_ref, va_ref, qt_ref, kh_ref, vt_ref):
    g64 = g64_ref[...]
    cos = cos_ref[...]
    sin = sin_ref[...]
    kn_ref[0] = _head_rms(ak_ref[0], g64, NA_DIM, wak_ref[...]).astype(BF16)
    va_ref[0] = av_ref[0].astype(BF16)
    q = _rope(_head_rms(dq_ref[0], g64, DA_DIM, wdq_ref[...]), cos, sin)
    k = _rope(_head_rms(dk_ref[0], g64, DA_DIM, wdk_ref[...]), cos, sin).astype(BF16)
    qt_ref[0] = q.T.astype(BF16)
    for hc in range(2 * DA_HEADS):
        kh_ref[0, hc] = k[:, hc * DA_DIM:(hc + 1) * DA_DIM]
    v = dv_ref[0]
    dv = 2 * DA_DIM
    for h in range(DA_HEADS):
        vt_ref[0, h, 0, 0:dv, :] = v[:, h * dv:(h + 1) * dv].T.astype(BF16)
        vt_ref[0, h, 0, dv:dv + DA_ONES, :] = jnp.ones((DA_ONES, v.shape[0]), BF16)


def prep_attention(p3, cos, sin, na_k_w, da_q_w, da_k_w):
    bsz, lt, _ = p3.shape
    tm = PREP_TM
    w = BRANCH_W
    tile = lambda v, s=1.0: (jnp.tile(v, w // v.shape[0]) * s).reshape(1, w)
    tab_spec = pl.BlockSpec((tm, w), lambda b, i: (i, 0))
    vec = _const_spec((1, w))
    return pl.pallas_call(
        _prep_attn_kernel,
        grid=(bsz, lt // tm),
        in_specs=[_row_spec(CB_AK), _row_spec(CB_AV), _row_spec(CB_DQ), _row_spec(CB_DK), _row_spec(CB_DV),
                  tab_spec, tab_spec, _const_spec((w, w)), vec, vec, vec],
        out_specs=[pl.BlockSpec((1, tm, w), lambda b, i: (b, i, 0)),
                   pl.BlockSpec((1, tm, w), lambda b, i: (b, i, 0)),
                   pl.BlockSpec((1, w, tm), lambda b, i: (b, 0, i)),
                   pl.BlockSpec((1, 2 * DA_HEADS, tm, DA_DIM), lambda b, i: (b, 0, i, 0)),
                   pl.BlockSpec((1, DA_HEADS, 1, 2 * DA_DIM + DA_ONES, tm), lambda b, i: (b, 0, i, 0, 0))],
        out_shape=[jax.ShapeDtypeStruct((bsz, lt, w), BF16),
                   jax.ShapeDtypeStruct((bsz, lt, w), BF16),
                   jax.ShapeDtypeStruct((bsz, w, lt), BF16),
                   jax.ShapeDtypeStruct((bsz, 2 * DA_HEADS, lt, DA_DIM), BF16),
                   jax.ShapeDtypeStruct((bsz, DA_HEADS, lt // tm, 2 * DA_DIM + DA_ONES, tm), BF16)],
        compiler_params=_cparams(2),
        name="prep_attn",
    )(p3, p3, p3, p3, p3, cos, sin, _group_matrix(w, DA_DIM),
      tile(na_k_w), tile(da_q_w, DA_DIM ** -0.5 * math.log2(math.e)), tile(da_k_w))


def rope_tables(lt, n_ctx):
    nf = DA_DIM // 4
    t = jnp.arange(lt - n_ctx, dtype=jnp.int32)
    rows, cols = t // GRID_W, t % GRID_W
    inv = ROPE_BASE ** (-jnp.arange(nf, dtype=F32) / nf)
    ang_r = rows.astype(F32)[:, None] * inv
    ang_c = cols.astype(F32)[:, None] * inv
    ang = jnp.concatenate([ang_r, ang_r, ang_c, ang_c], axis=-1)
    ang = jnp.concatenate([jnp.zeros((n_ctx, DA_DIM), F32), ang], axis=0)
    reps = BRANCH_W // DA_DIM
    return jnp.tile(jnp.cos(ang), (1, reps)), jnp.tile(jnp.sin(ang), (1, reps))


def _flash_kernel(lam_ref, qt_ref, k_ref, vt_ref, g_ref, sw_ref, o_ref, *, n_latent_iters, out_scale):
    tq = qt_ref.shape[2]
    dve = vt_ref.shape[3]
    dv = dve - DA_ONES
    tk = vt_ref.shape[4]
    qts = [qt_ref[0, c * DA_DIM:(c + 1) * DA_DIM, :] for c in range(2)]

    def attend(carry, first_chunk, n_chunks):
        chunks = [first_chunk + g for g in range(n_chunks)]
        sts = [[_dot(k_ref[0, c, pl.ds(pl.multiple_of(ch * tk, tk), tk), :], qts[c]) for c in range(2)]
               for ch in chunks]
        carry = list(carry)
        for g, ch in enumerate(chunks):
            vt = vt_ref[0, 0, ch]
            for c in range(2):
                m, acc = carry[c]
                st = sts[g][c].astype(BF16)
                m_new = jnp.maximum(m, jnp.max(st, axis=0, keepdims=True).astype(F32))
                pt = jnp.exp2(st - m_new.astype(BF16))
                carry[c] = (m_new, jnp.exp2(m - m_new) * acc + _dot(vt, pt))
        return tuple(carry)

    init = tuple((jnp.full((1, tq), NEG_INF, F32), jnp.zeros((dve, tq), F32)) for _ in range(2))
    carry = attend(init, 0, 1)
    n_iters = jnp.where(pl.program_id(2) == 0, 0, n_latent_iters)
    res = lax.fori_loop(0, n_iters, lambda i, cr: attend(cr, 1 + i * DA_NSUB, DA_NSUB), carry)
    outs = [acc[0:dv] * (1.0 / acc[dv:dv + 1]) for (_, acc) in res]
    ot = outs[0] - lam_ref[0] * outs[1]
    ot = ot * lax.rsqrt(jnp.mean(ot * ot, axis=0, keepdims=True) + NORM_EPS)
    o = ot.T * (sw_ref[...] * out_scale)
    o_ref[0] = (o * _silu(g_ref[0])).astype(BF16)


def flash_diff_attention(qt, kh, vt, p3, lam, subln_w, lam_init):
    bsz, w, lt = qt.shape
    dv = 2 * DA_DIM
    nk = lt // DA_TK
    n_latent_iters = (nk - 1) // DA_NSUB
    kern = functools.partial(_flash_kernel, n_latent_iters=n_latent_iters, out_scale=1.0 - lam_init)
    g_blocks = BRANCH_W // dv
    return pl.pallas_call(
        kern,
        grid=(bsz, DA_HEADS, lt // DA_TQ),
        in_specs=[pl.BlockSpec(memory_space=pltpu.SMEM),
                  pl.BlockSpec((1, dv, DA_TQ), lambda b, h, qi: (b, h, qi)),
                  pl.BlockSpec((1, 2, lt, DA_DIM), lambda b, h, qi: (b, h, 0, 0)),
                  pl.BlockSpec((1, 1, nk, dv + DA_ONES, DA_TK), lambda b, h, qi: (b, h, 0, 0, 0)),
                  pl.BlockSpec((1, DA_TQ, dv), lambda b, h, qi: (b, qi, CB_DG * g_blocks + h)),
                  _const_spec((1, dv))],
        out_specs=pl.BlockSpec((1, DA_TQ, dv), lambda b, h, qi: (b, qi, h)),
        out_shape=jax.ShapeDtypeStruct((bsz, lt, BRANCH_W), BF16),
        compiler_params=_cparams(3),
        name="flash_attn",
    )(lam.reshape(1).astype(F32), qt, kh, vt, p3, subln_w.reshape(1, dv))


def _na_kernel(q_ref, g_ref, k_ref, v_ref, bias_ref, g64_ref, wq_ref, o_ref, *, n_rows, n_ctx):
    blk = pl.program_id(1)
    r = jnp.maximum(blk - n_ctx // GRID_W, 0)
    r0 = jnp.clip(r - NA_WIN_R // 2, 0, n_rows - NA_WIN_R)
    start = pl.multiple_of(n_ctx + r0 * GRID_W, GRID_W)
    n_loc = NA_WIN_R * GRID_W
    q = _head_rms(q_ref[0], g64_ref[...], NA_DIM, wq_ref[...]).astype(BF16)
    g = g_ref[0]
    hs = range(NA_HEADS)
    sls = [slice(h * NA_DIM, (h + 1) * NA_DIM) for h in hs]
    qh = [q[:, sl] for sl in sls]
    s_loc = [_dot_nt(qh[h], k_ref[0, pl.ds(start, n_loc), sls[h]]) + bias_ref[0, h] for h in hs]
    s_ctx = [_dot_nt(qh[h], k_ref[0, 0:n_ctx, sls[h]]) for h in hs]
    m = [jnp.maximum(jnp.max(s_loc[h], axis=-1, keepdims=True), jnp.max(s_ctx[h], axis=-1, keepdims=True))
         for h in hs]
    p_loc = [jnp.exp(s_loc[h] - m[h]) for h in hs]
    p_ctx = [jnp.exp(s_ctx[h] - m[h]) for h in hs]
    l = [jnp.sum(p_loc[h], axis=-1, keepdims=True) + jnp.sum(p_ctx[h], axis=-1, keepdims=True) for h in hs]
    o = [_dot(p_loc[h].astype(BF16), v_ref[0, pl.ds(start, n_loc), sls[h]])
         + _dot(p_ctx[h].astype(BF16), v_ref[0, 0:n_ctx, sls[h]]) for h in hs]
    for h in hs:
        o_ref[0, :, sls[h]] = ((o[h] / l[h]) * _silu(g[:, sls[h]])).astype(BF16)


def neighbourhood_attention(p3, kn, va, bias_tbl, na_q_w, n_ctx):
    bsz, lt, w = kn.shape
    n_rows = (lt - n_ctx) // GRID_W
    ncb = n_ctx // GRID_W
    half = NA_WIN_R // 2

    def bias_idx(b, blk):
        r = blk - ncb
        off = r - jnp.clip(r - half, 0, n_rows - NA_WIN_R)
        return (jnp.where(blk < ncb, NA_WIN_R, off), 0, 0, 0)

    full_spec = pl.BlockSpec((1, lt, w), lambda b, blk: (b, 0, 0))
    wq = (jnp.tile(na_q_w, w // NA_DIM) * NA_DIM ** -0.5).reshape(1, w)
    return pl.pallas_call(
        functools.partial(_na_kernel, n_rows=n_rows, n_ctx=n_ctx),
        grid=(bsz, lt // GRID_W),
        in_specs=[_row_spec(CB_AQ, GRID_W), _row_spec(CB_AG, GRID_W), full_spec, full_spec,
                  pl.BlockSpec((1, NA_HEADS, GRID_W, NA_WIN_R * GRID_W), bias_idx),
                  _const_spec((w, w)), _const_spec((1, w))],
        out_specs=pl.BlockSpec((1, GRID_W, w), lambda b, blk: (b, blk, 0)),
        out_shape=jax.ShapeDtypeStruct((bsz, lt, w), BF16),
        compiler_params=_cparams(2),
        name="nbr_attn",
    )(p3, p3, kn, va, bias_tbl, _group_matrix(w, NA_DIM), wq)


def na_bias_table(rpb):
    col = np.arange(GRID_W)
    c0 = np.clip(col - NA_WIN_C // 2, 0, GRID_W - NA_WIN_C)
    col_ok = (col[None, :] >= c0[:, None]) & (col[None, :] < c0[:, None] + NA_WIN_C)
    d_col = np.clip(col[None, :] - col[:, None] + (NA_WIN_C - 1), 0, 2 * NA_WIN_C - 2)
    onehot = (d_col[:, :, None] == np.arange(2 * NA_WIN_C - 1)).astype(np.float32)
    by_col = jnp.einsum('hrc,qwc->hrqw', rpb.astype(F32), jnp.asarray(onehot),
                        precision=lax.Precision.HIGHEST)
    by_col = jnp.where(jnp.asarray(col_ok)[None, None], by_col, NEG_INF)
    tbl = jnp.stack([by_col[:, NA_WIN_R - 1 - o:2 * NA_WIN_R - 1 - o] for o in range(NA_WIN_R)]
                    + [jnp.full((NA_HEADS, NA_WIN_R, GRID_W, GRID_W), NEG_INF, F32)])
    tbl = jnp.transpose(tbl, (0, 1, 3, 2, 4))
    return tbl.reshape(NA_WIN_R + 1, NA_HEADS, GRID_W, NA_WIN_R * GRID_W)


def _prep_rwkv_kernel(r_ref, rp_ref, rn_ref, k_ref, kp_ref, kn_ref, v_ref, vp_ref, vn_ref,
                      m_ref, mp_ref, mn_ref, mu_ref, w0_ref, a0_ref, w2_ref, a2_ref, kk_ref, ka_ref,
                      rk_ref, g64_ref,
                      ro_ref, vo_ref, ao_ref, lw_ref, kd_ref, bo_ref, bonus_ref, *, n_tiles, n_ctx_tiles):
    i = pl.program_id(1)
    w = BRANCH_W

    def shifted(x_ref, p_ref, n_ref, mu):
        x = x_ref[0]
        xp, xn = _neighbours(x, p_ref[0], n_ref[0], i, n_tiles, n_ctx_tiles)
        return x + (0.5 * (xp + xn) - x) * mu

    r = shifted(r_ref, rp_ref, rn_ref, mu_ref[0:1, :])
    k = shifted(k_ref, kp_ref, kn_ref, mu_ref[1:2, :])
    v = shifted(v_ref, vp_ref, vn_ref, mu_ref[2:3, :])
    misc = shifted(m_ref, mp_ref, mn_ref, mu_ref[3:4, :])
    wd = jnp.tanh(misc[:, 0:2 * RW_LORA_W]).astype(BF16)
    ad = misc[:, 2 * RW_LORA_W:MISC_DT].astype(BF16)
    w_log = w0_ref[...] + _dot(wd, w2_ref[...])
    gate = _sigmoid(a0_ref[...] + _dot(ad, a2_ref[...]))
    log_decay = -math.exp(-0.5) * _sigmoid(w_log)
    g64 = g64_ref[...]
    kk = k * kk_ref[...]
    kk = kk / jnp.maximum(jnp.sqrt(_dot_exact_rhs(kk * kk, g64)), 1e-12)
    ro_ref[0] = r
    vo_ref[0] = v.astype(BF16)
    ao_ref[0] = -kk
    coef = None
    for d in range(2):
        a_d = gate[:, d * w:(d + 1) * w]
        kd = k * (1.0 + (a_d - 1.0) * ka_ref[...])
        lw_ref[d, 0] = log_decay[:, d * w:(d + 1) * w]
        kd_ref[d, 0] = kd
        bo_ref[d, 0] = kk * a_d
        coef = kd if coef is None else coef + kd
    bonus_ref[0] = _dot_exact_rhs(r * coef * rk_ref[...], g64) * v


def prep_rwkv(p3, mu, w0, w2, a0, a2, k_k, k_a, r_k, n_ctx):
    bsz, lt, _ = p3.shape
    tm = PREP_TM
    w = BRANCH_W
    n_tiles = lt // tm
    mu4 = jnp.stack([mu[0:w], mu[w:2 * w], mu[2 * w:3 * w],
                     jnp.pad(mu[3 * w:], (0, w - (mu.shape[0] - 3 * w)))])
    zero = jnp.zeros((RW_LORA_W, w), F32)
    w2cat = jnp.concatenate([jnp.concatenate([w2[0], zero], axis=1),
                             jnp.concatenate([zero, w2[1]], axis=1)], axis=0).astype(BF16)
    a2cat = jnp.concatenate([jnp.concatenate([a2[0], zero], axis=1),
                             jnp.concatenate([zero, a2[1]], axis=1)], axis=0).astype(BF16)
    specs = []
    for cb in (CB_BR, CB_BK, CB_BV, CB_MISC):
        specs += [_row_spec(cb), *_halo_specs(cb, tm, lt)]
    vec = _const_spec((1, w))
    tok = pl.BlockSpec((1, tm, w), lambda b, i: (b, i, 0))
    tok2 = pl.BlockSpec((2, 1, tm, w), lambda b, i: (0, b, i, 0))
    kern = functools.partial(_prep_rwkv_kernel, n_tiles=n_tiles, n_ctx_tiles=n_ctx // tm)
    return pl.pallas_call(
        kern,
        grid=(bsz, n_tiles),
        in_specs=specs + [_const_spec((4, w)), _const_spec((1, 2 * w)), _const_spec((1, 2 * w)),
                          _const_spec((2 * RW_LORA_W, 2 * w)), _const_spec((2 * RW_LORA_A, 2 * w)),
                          vec, vec, vec, _const_spec((w, w))],
        out_specs=[tok, tok, tok, tok2, tok2, tok2, tok],
        out_shape=[jax.ShapeDtypeStruct((bsz, lt, w), F32),
                   jax.ShapeDtypeStruct((bsz, lt, w), BF16),
                   jax.ShapeDtypeStruct((bsz, lt, w), F32),
                   jax.ShapeDtypeStruct((2, bsz, lt, w), F32),
                   jax.ShapeDtypeStruct((2, bsz, lt, w), F32),
                   jax.ShapeDtypeStruct((2, bsz, lt, w), F32),
                   jax.ShapeDtypeStruct((bsz, lt, w), F32)],
        compiler_params=_cparams(2),
        name="prep_rwkv",
    )(*([p3] * 12), mu4, w0.reshape(1, 2 * w), a0.reshape(1, 2 * w), w2cat, a2cat,
      k_k.reshape(1, w), k_a.reshape(1, w), r_k.reshape(1, w), _group_matrix(w, RW_DIM))


def _rwkv_kernel(tri_ref, mq_ref, r_ref, v_ref, a_ref, lw_ref, kd_ref, b_ref, y_ref, s_ref):
    c = RW_CHUNK
    nsub = r_ref.shape[1] // c
    d = pl.program_id(0)

    @pl.when(pl.program_id(2) == 0)
    def _():
        s_ref[...] = jnp.zeros_like(s_ref)

    tri = tri_ref[0]
    m_quad = mq_ref[0] > 0.5
    rows = lax.broadcasted_iota(jnp.int32, (c, c), 0)
    cols = lax.broadcasted_iota(jnp.int32, (c, c), 1)
    eye = (rows == cols).astype(F32)
    same_half = (rows >= c // 2) == (cols >= c // 2)
    heads = [slice(h * RW_DIM, (h + 1) * RW_DIM) for h in range(RW_HEADS)]

    row_sl, e_tot = [], []
    al, rh, rf, be, ka, bc, kc, vb = [], [], [], [], [], [], [], []
    for j in range(nsub):
        off = pl.multiple_of(jnp.where(d == 0, j, nsub - 1 - j) * c, c)
        rs = pl.ds(off, c)
        row_sl.append(rs)
        lw = lw_ref[0, 0, rs, :]
        cum = _dot_exact_lhs(tri, lw)
        tot = jnp.sum(lw, axis=0, keepdims=True)
        e_m = jnp.exp(-cum)
        e_t = jnp.exp(tot - cum)
        e_tot.append(jnp.exp(tot))
        b_in = b_ref[0, 0, rs, :]
        k_in = kd_ref[0, 0, rs, :]
        alpha = (a_ref[0, rs, :] * jnp.exp(cum - lw)).astype(BF16)
        rho_f = r_ref[0, rs, :] * jnp.exp(cum)
        rho = rho_f.astype(BF16)
        beta = (b_in * e_m).astype(BF16)
        kappa = (k_in * e_m).astype(BF16)
        beta_c = (b_in * e_t).astype(BF16)
        kappa_c = (k_in * e_t).astype(BF16)
        v_b = v_ref[0, rs, :]
        for sl in heads:
            al.append(alpha[:, sl])
            rh.append(rho[:, sl])
            rf.append(rho_f[:, sl])
            be.append(beta[:, sl])
            ka.append(kappa[:, sl])
            bc.append(beta_c[:, sl])
            kc.append(kappa_c[:, sl])
            vb.append(v_b[:, sl])

    units = range(nsub * RW_HEADS)
    cat = lambda x, y: jnp.concatenate([x, y], axis=0)
    zeros = jnp.zeros((c, RW_DIM), BF16)
    prod = [jnp.where(m_quad, _dot_nt(cat(al[u], rh[u]), cat(be[u], ka[u])), 0.0) for u in units]
    l_ab = [prod[u][0:c, 0:c] for u in units]
    top = [prod[u][0:c].astype(BF16) for u in units]
    bot = [prod[u][c:2 * c].astype(BF16) for u in units]
    l_d = [jnp.where(same_half, l_ab[u], 0.0) for u in units]
    l_o = [(l_ab[u] - l_d[u]).astype(BF16) for u in units]
    pw = [l_d[u].astype(BF16) for u in units]
    td = [eye + l_d[u] for u in units]
    for _ in range(int(math.log2(c)) - 2):
        pw = [_dot(pw[u], pw[u]).astype(BF16) for u in units]
        td = [td[u] + _dot(td[u].astype(BF16), pw[u]) for u in units]
    td_b = [td[u].astype(BF16) for u in units]
    x_o = [_dot(td_b[u], l_o[u]).astype(BF16) for u in units]
    tinv = [(td[u] + _dot(x_o[u], td_b[u])).astype(BF16) for u in units]
    akv = [_dot(top[u], cat(zeros, vb[u])).astype(BF16) for u in units]
    a_hat = [_dot(tinv[u], al[u]).astype(BF16) for u in units]
    v_hat = [_dot(tinv[u], akv[u]).astype(BF16) for u in units]
    r_hat = [(rf[u] + _dot(bot[u], cat(a_hat[u], zeros))).astype(BF16) for u in units]
    vv = [cat(v_hat[u], vb[u]) for u in units]
    y_hat = [_dot(bot[u], vv[u]) for u in units]
    q_mat = [_dot_tn(a_hat[u], bc[u]).astype(BF16) for u in units]
    n_mat = [_dot_tn(vv[u], cat(bc[u], kc[u])) for u in units]

    state = [s_ref[h] for h in range(RW_HEADS)]
    for j in range(nsub):
        for h, sl in enumerate(heads):
            u = j * RW_HEADS + h
            s_b = state[h].astype(BF16)
            y_ref[0, 0, row_sl[j], sl] = _dot_nt(r_hat[u], s_b) + y_hat[u]
            state[h] = state[h] * e_tot[j][:, sl] + _dot(s_b, q_mat[u]) + n_mat[u]
    for h in range(RW_HEADS):
        s_ref[h] = state[h]


def _scan_chunk_index(d, s, n_ctx_chunks, n_chunks):
    rev = jnp.where(s < n_ctx_chunks, n_ctx_chunks - 1 - s, n_chunks + n_ctx_chunks - 1 - s)
    return jnp.where(d == 0, s, rev)


def _direction_masks(c):
    i = np.arange(c)
    lower = (i[None, :] <= i[:, None]).astype(np.float32)
    tri = np.stack([lower, lower.T])
    strict = np.stack([lower - np.eye(c, dtype=np.float32), lower.T - np.eye(c, dtype=np.float32)])
    return tri, strict


def rwkv_scan(r, v, a, lw, kd, b, n_ctx):
    bsz, lt, w = r.shape
    c = RW_CHUNK
    blk = RW_NSUB * c
    nck = lt // blk
    ncc = n_ctx // blk
    tri, strict = _direction_masks(c)
    shared = pl.BlockSpec((1, blk, w), lambda d, bi, s: (bi, _scan_chunk_index(d, s, ncc, nck), 0))
    perdir = pl.BlockSpec((1, 1, blk, w), lambda d, bi, s: (d, bi, _scan_chunk_index(d, s, ncc, nck), 0))
    mask_spec = pl.BlockSpec((1, c, c), lambda d, bi, s: (d, 0, 0))
    quad_spec = pl.BlockSpec((1, 2 * c, 2 * c), lambda d, bi, s: (d, 0, 0))
    quad = np.concatenate([np.tile(strict, (1, 1, 2)), np.tile(tri, (1, 1, 2))], axis=1)
    return pl.pallas_call(
        _rwkv_kernel,
        grid=(2, bsz, nck),
        in_specs=[mask_spec, quad_spec, shared, shared, shared, perdir, perdir, perdir],
        out_specs=perdir,
        out_shape=jax.ShapeDtypeStruct((2, bsz, lt, w), F32),
        scratch_shapes=[pltpu.VMEM((RW_HEADS, RW_DIM, RW_DIM), F32)],
        compiler_params=_cparams(3),
        name="rwkv_scan",
    )(jnp.asarray(tri, BF16), jnp.asarray(quad, F32), r, v, a, lw, kd, b)


def _finish_rwkv_kernel(y_ref, bonus_ref, g_ref, lnw_ref, lnb_ref, g64_ref, o_ref):
    y = y_ref[0, 0] + y_ref[1, 0]
    g64 = g64_ref[...]
    mean = _dot_exact_rhs(y, g64) * (1.0 / RW_DIM)
    yc = y - mean
    var = _dot_exact_rhs(yc * yc, g64) * (1.0 / RW_DIM)
    yn = yc * lax.rsqrt(var + RW_GN_EPS) * lnw_ref[...] + lnb_ref[...]
    o_ref[0] = ((yn + bonus_ref[0]) * _silu(g_ref[0])).astype(BF16)


def finish_rwkv(y, bonus, p3, ln_w, ln_b):
    _, bsz, lt, w = y.shape
    tm = PREP_TM
    tok = pl.BlockSpec((1, tm, w), lambda b, i: (b, i, 0))
    vec = _const_spec((1, w))
    return pl.pallas_call(
        _finish_rwkv_kernel,
        grid=(bsz, lt // tm),
        in_specs=[pl.BlockSpec((2, 1, tm, w), lambda b, i: (0, b, i, 0)), tok, _row_spec(CB_BG),
                  vec, vec, _const_spec((w, w))],
        out_specs=tok,
        out_shape=jax.ShapeDtypeStruct((bsz, lt, w), BF16),
        compiler_params=_cparams(2),
        name="finish_rwkv",
    )(y, bonus, p3, ln_w.reshape(1, w), ln_b.reshape(1, w), _group_matrix(w, RW_DIM))


def _prep_ssd_kernel(x_ref, xp_ref, xn_ref, bc_ref, bcp_ref, bcn_ref, m_ref, cw_ref, cb_ref, dtb_ref,
                     aneg_ref, exp_ref, dsk_ref,
                     xq_ref, bco_ref, bt_ref, a_ref, at_ref, dskip_ref, *, n_tiles, n_ctx_tiles):
    i = pl.program_id(1)
    w = BRANCH_W

    def conv(x_ref, p_ref, n_ref, half):
        x = x_ref[0]
        xp, xn = _neighbours(x, p_ref[0], n_ref[0], i, n_tiles, n_ctx_tiles)
        lo = half * w
        y = (xp * cw_ref[0:1, lo:lo + w] + x * cw_ref[1:2, lo:lo + w] + xn * cw_ref[2:3, lo:lo + w]
             + cb_ref[:, lo:lo + w])
        return _silu(y)

    xs = conv(x_ref, xp_ref, xn_ref, 0)
    bc = conv(bc_ref, bcp_ref, bcn_ref, 1)
    bco_ref[0] = bc.astype(BF16)
    bt_ref[0] = bc[:, 0:SSM_GROUPS * SSM_STATE].T.astype(BF16)
    dt = _softplus(m_ref[0][:, MISC_DT:MISC_DT + LANES] + dtb_ref[...])
    lane = lax.broadcasted_iota(jnp.int32, (1, LANES), 1)
    dt = jnp.where(lane < 2 * SSM_HEADS, dt, 0.0)
    dtx = _dot_exact_rhs(dt, exp_ref[...])
    a_all = dt * aneg_ref[...]
    first = lane < SSM_HEADS
    a_dirs = [jnp.where(first, a_all, 0.0), jnp.where(first, pltpu.roll(a_all, LANES - SSM_HEADS, 1), 0.0)]
    for d in range(2):
        xq_ref[d, 0] = xs * dtx[:, d * w:(d + 1) * w]
        a_ref[d, 0] = a_dirs[d]
        at_ref[d, 0] = a_dirs[d].T[0:2 * SUBLANES, :]
    dskip_ref[0] = xs * dsk_ref[...]


def prep_ssd(p3, conv_w, conv_b, dt_bias, a_log, d_skip, n_ctx):
    bsz, lt, _ = p3.shape
    tm = PREP_TM
    w = BRANCH_W
    n_tiles = lt // tm
    nh2 = 2 * SSM_HEADS
    pad_lanes = lambda v: jnp.pad(v.reshape(1, nh2), ((0, 0), (0, LANES - nh2)))
    expand = np.zeros((LANES, 2 * w), np.float32)
    for d in range(2):
        for h in range(SSM_HEADS):
            expand[d * SSM_HEADS + h, d * w + h * SSM_HEAD_DIM:d * w + (h + 1) * SSM_HEAD_DIM] = 1.0
    specs = [_row_spec(CB_CX), *_halo_specs(CB_CX, tm, lt), _row_spec(CB_CBC), *_halo_specs(CB_CBC, tm, lt),
             _row_spec(CB_MISC)]
    tok = pl.BlockSpec((1, tm, w), lambda b, i: (b, i, 0))
    gs = SSM_GROUPS * SSM_STATE
    kern = functools.partial(_prep_ssd_kernel, n_tiles=n_tiles, n_ctx_tiles=n_ctx // tm)
    return pl.pallas_call(
        kern,
        grid=(bsz, n_tiles),
        in_specs=specs + [_const_spec((3, 2 * w)), _const_spec((1, 2 * w)), _const_spec((1, LANES)),
                          _const_spec((1, LANES)), _const_spec((LANES, 2 * w)), _const_spec((1, w))],
        out_specs=[pl.BlockSpec((2, 1, tm, w), lambda b, i: (0, b, i, 0)), tok,
                   pl.BlockSpec((1, gs, tm), lambda b, i: (b, 0, i)),
                   pl.BlockSpec((2, 1, tm, LANES), lambda b, i: (0, b, i, 0)),
                   pl.BlockSpec((2, 1, 2 * SUBLANES, tm), lambda b, i: (0, b, 0, i)), tok],
        out_shape=[jax.ShapeDtypeStruct((2, bsz, lt, w), F32),
                   jax.ShapeDtypeStruct((bsz, lt, w), BF16),
                   jax.ShapeDtypeStruct((bsz, gs, lt), BF16),
                   jax.ShapeDtypeStruct((2, bsz, lt, LANES), F32),
                   jax.ShapeDtypeStruct((2, bsz, 2 * SUBLANES, lt), F32),
                   jax.ShapeDtypeStruct((bsz, lt, w), F32)],
        compiler_params=_cparams(2),
        name="prep_ssd",
    )(*([p3] * 7), conv_w, conv_b.reshape(1, 2 * w), pad_lanes(dt_bias), pad_lanes(-jnp.exp(a_log)),
      jnp.asarray(expand, BF16), jnp.repeat(d_skip, SSM_HEAD_DIM).reshape(1, w))


def _ssd_kernel(tri_ref, mi_ref, xq_ref, bc_ref, bt_ref, a_ref, at_ref, y_ref, s_ref):
    q = SSM_CHUNK
    rep = SSM_HEADS // SSM_GROUPS
    gw = SSM_GROUPS * SSM_STATE

    @pl.when(pl.program_id(2) == 0)
    def _():
        s_ref[...] = jnp.zeros_like(s_ref)

    tri = tri_ref[0]
    mask = mi_ref[0] > 0.5
    a = a_ref[0, 0]
    a_t = at_ref[0, 0]
    xq = xq_ref[0, 0]
    bm = bc_ref[0, :, 0:gw]
    cm = bc_ref[0, :, gw:2 * gw]
    bt = bt_ref[0]
    hs = range(SSM_HEADS)
    p = SSM_HEAD_DIM
    acol = _dot_exact_lhs(tri, a)
    arow = _dot_nt_exact_lhs_f32(a_t, tri)
    tot = jnp.sum(a, axis=0, keepdims=True)
    groups = [slice(g * SSM_STATE, (g + 1) * SSM_STATE) for g in range(SSM_GROUPS)]
    cms = [cm[:, gs] for gs in groups]
    cb = [_dot_nt(cms[g], bm[:, gs]) for g, gs in enumerate(groups)]
    ac = [jnp.broadcast_to(acol[:, h:h + 1], (q, q)) for h in hs]
    ar = [jnp.broadcast_to(arow[h:h + 1, :], (q, q)) for h in hs]
    gmat = [(cb[h // rep] * jnp.exp(jnp.where(mask, ac[h] - ar[h], NEG_INF))).astype(BF16) for h in hs]
    xh = [xq[:, h * p:(h + 1) * p] for h in hs]
    tot_h = [jnp.broadcast_to(tot[:, h:h + 1], (1, p)) for h in hs]
    s0 = [s_ref[h] for h in hs]
    y_in = [_dot(gmat[h], xh[h].astype(BF16)) for h in hs]
    y_st = [_dot(cms[h // rep], s0[h].astype(BF16)) for h in hs]
    xd = [(xh[h] * jnp.exp(tot_h[h] - ac[h][:, :p])).astype(BF16) for h in hs]
    upd = [_dot(bt[groups[h // rep], :], xd[h]) for h in hs]
    for h in hs:
        y_ref[0, 0, :, h * p:(h + 1) * p] = y_in[h] + y_st[h] * jnp.exp(ac[h][:, :p])
        s_ref[h] = s0[h] * jnp.exp(tot_h[h]) + upd[h]


def ssd_scan(xq, bc, bt, a, a_t, n_ctx):
    _, bsz, lt, w = xq.shape
    q = SSM_CHUNK
    nck = lt // q
    ncc = n_ctx // q
    gw = bt.shape[1]
    tri, _ = _direction_masks(q)

    def cidx(d, s):
        return _scan_chunk_index(d, s, ncc, nck)

    mask_spec = pl.BlockSpec((1, q, q), lambda d, bi, s: (d, 0, 0))
    return pl.pallas_call(
        _ssd_kernel,
        grid=(2, bsz, nck),
        in_specs=[mask_spec, mask_spec,
                  pl.BlockSpec((1, 1, q, w), lambda d, bi, s: (d, bi, cidx(d, s), 0)),
                  pl.BlockSpec((1, q, w), lambda d, bi, s: (bi, cidx(d, s), 0)),
                  pl.BlockSpec((1, gw, q), lambda d, bi, s: (bi, 0, cidx(d, s))),
                  pl.BlockSpec((1, 1, q, LANES), lambda d, bi, s: (d, bi, cidx(d, s), 0)),
                  pl.BlockSpec((1, 1, 2 * SUBLANES, q), lambda d, bi, s: (d, bi, 0, cidx(d, s)))],
        out_specs=pl.BlockSpec((1, 1, q, w), lambda d, bi, s: (d, bi, cidx(d, s), 0)),
        out_shape=jax.ShapeDtypeStruct((2, bsz, lt, w), F32),
        scratch_shapes=[pltpu.VMEM((SSM_HEADS, SSM_STATE, SSM_HEAD_DIM), F32)],
        compiler_params=_cparams(3),
        name="ssd_scan",
    )(jnp.asarray(tri, BF16), jnp.asarray(tri, F32), xq, bc, bt, a, a_t)


def _finish_ssd_kernel(y_ref, dskip_ref, z_ref, nw_ref, g256_ref, o_ref):
    y = y_ref[0, 0] + y_ref[1, 0] + dskip_ref[0]
    g = y * _silu(z_ref[0])
    group = BRANCH_W // SSM_GROUPS
    o_ref[0] = _head_rms(g, g256_ref[...], group, nw_ref[...]).astype(BF16)


def finish_ssd(y, dskip, p3, norm_w):
    _, bsz, lt, w = y.shape
    tm = PREP_TM
    tok = pl.BlockSpec((1, tm, w), lambda b, i: (b, i, 0))
    return pl.pallas_call(
        _finish_ssd_kernel,
        grid=(bsz, lt // tm),
        in_specs=[pl.BlockSpec((2, 1, tm, w), lambda b, i: (0, b, i, 0)), tok, _row_spec(CB_CZ),
                  _const_spec((1, w)), _const_spec((w, w))],
        out_specs=tok,
        out_shape=jax.ShapeDtypeStruct((bsz, lt, w), BF16),
        compiler_params=_cparams(2),
        name="finish_ssd",
    )(y, dskip, p3, norm_w.reshape(1, w), _group_matrix(w, w // SSM_GROUPS))


def _permute_w_in(w_in_l):
    w = BRANCH_W
    b0 = 4 * w
    lora0 = b0 + 3 * w
    bg0 = lora0 + MISC_DT
    c0 = bg0 + w
    dt0 = c0 + SSM_CONV_CH
    z0 = dt0 + 2 * SSM_HEADS
    d0 = z0 + w
    end = d0 + 4 * w
    parts = [w_in_l[:, 0:lora0], w_in_l[:, bg0:c0], w_in_l[:, c0:dt0], w_in_l[:, z0:d0], w_in_l[:, d0:end],
             w_in_l[:, lora0:bg0], w_in_l[:, dt0:z0]]
    used = sum(p.shape[1] for p in parts)
    pad = jnp.zeros((w_in_l.shape[0], IN_W_PAD - used), BF16)
    return jnp.concatenate([p.astype(BF16) for p in parts] + [pad], axis=1)


def kernel(x, c, ctx, c_ctx, norm_w, w_ada, b_ada, w_in, na_q_norm, na_k_norm, na_rpb, rw_mu, rw_w0, rw_w2, rw_a0, rw_a2, rw_k_k, rw_k_a, rw_r_k, rw_ln_w, rw_ln_b, ssm_conv_w, ssm_conv_b, ssm_dt_bias, ssm_A_log, ssm_D, ssm_norm_w, da_q_norm, da_k_norm, da_lq1, da_lk1, da_lq2, da_lk2, da_subln, w_gate, w_up, w_out):
    bsz, seq, d = x.shape
    n_ctx = ctx.shape[1]
    lt = n_ctx + seq
    depth = w_in.shape[0]
    assert n_ctx == CTX_LEN == PREP_TM == DA_TK and seq % (GRID_W * NA_WIN_R) == 0
    assert ((lt // DA_TK) - 1) % DA_NSUB == 0
    cond = jnp.concatenate([_silu(c), _silu(c_ctx)[None], jnp.zeros((SUBLANES - bsz - 1, d), F32)], axis=0)
    xs2 = jnp.concatenate([ctx, x], axis=1).reshape(bsz * lt, d)
    cos, sin = rope_tables(lt, n_ctx)
    for l in range(depth):
        lam_init = 0.8 - 0.6 * math.exp(-0.3 * l)
        mod = small_matmul(cond, w_ada[l], b_ada[l])
        shift, scale, gate = jnp.split(mod[:bsz], 3, axis=-1)
        shift_c, scale_c, gate_c = jnp.split(mod[bsz], 3, axis=-1)
        p2, h = in_projection(xs2, norm_w[l], scale, shift, scale_c, shift_c, _permute_w_in(w_in[l]), lt, n_ctx)
        p3 = p2.reshape(bsz, lt, IN_W_PAD)

        kn, va, qt, kh, vt = prep_attention(p3, cos, sin, na_k_norm[l], da_q_norm[l], da_k_norm[l])
        oa = neighbourhood_attention(p3, kn, va, na_bias_table(na_rpb[l]), na_q_norm[l], n_ctx)
        lam = jnp.exp(jnp.sum(da_lq1[l] * da_lk1[l])) - jnp.exp(jnp.sum(da_lq2[l] * da_lk2[l])) + lam_init
        od = flash_diff_attention(qt, kh, vt, p3, lam, da_subln[l], lam_init)

        r, vb, a, lw, kd, b, bonus = prep_rwkv(p3, rw_mu[l], rw_w0[l], rw_w2[l], rw_a0[l], rw_a2[l],
                                              rw_k_k[l], rw_k_a[l], rw_r_k[l].reshape(-1), n_ctx)
        ob = finish_rwkv(rwkv_scan(r, vb, a, lw, kd, b, n_ctx), bonus, p3, rw_ln_w[l], rw_ln_b[l])

        xq, bc, bt, sa, sat, dskip = prep_ssd(p3, ssm_conv_w[l], ssm_conv_b[l], ssm_dt_bias[l], ssm_A_log[l],
                                              ssm_D[l], n_ctx)
        om = finish_ssd(ssd_scan(xq, bc, bt, sa, sat, n_ctx), dskip, p3, ssm_norm_w[l])

        ys = [t.reshape(bsz * lt, BRANCH_W) for t in (oa, ob, om, od)]
        acc = gate_up(h, ys, w_gate[l].astype(BF16), w_up[l].astype(BF16), lt)
        xs2 = out_projection(acc, w_out[l].astype(BF16), xs2, gate, gate_c, lt, n_ctx)
    return xs2.reshape(bsz, lt, d)[:, n_ctx:]
```

```python
import functools
import math

import numpy as np
import jax
import jax.numpy as jnp
from jax import lax
from jax.experimental import pallas as pl
from jax.experimental.pallas import tpu as pltpu

F32 = jnp.float32
BF16 = jnp.bfloat16

D_MODEL = 2048
GRID_W = 64
CTX_LEN = 256
N_BRANCH = 4
BRANCH_W = D_MODEL // N_BRANCH
NORM_EPS = 1e-6
NEG_INF = -1e30

NA_DIM = 64
NA_HEADS = BRANCH_W // NA_DIM
NA_WIN_R = 8
NA_WIN_C = 16
NA_NBLK = 2

RW_DIM = 64
RW_HEADS = BRANCH_W // RW_DIM
RW_LORA_W = 64
RW_LORA_A = 64
RW_GN_EPS = 64e-5
RW_CHUNK = 64
RW_NSUB = 4

SSM_HEAD_DIM = 64
SSM_HEADS = BRANCH_W // SSM_HEAD_DIM
SSM_GROUPS = 2
SSM_STATE = 128
SSM_CHUNK = 128
SSM_NSUB = 2
SSM_CONV_CH = BRANCH_W + 2 * SSM_GROUPS * SSM_STATE

DA_DIM = 64
DA_HEADS = BRANCH_W // (2 * DA_DIM)
ROPE_BASE = 10000.0
DA_TQ = 256
DA_TK = 256
DA_NSUB = 32
DA_ONES = 16

SUBLANES = 8
LANES = 128

CB_AQ, CB_AK, CB_AV, CB_AG = 0, 1, 2, 3
CB_BR, CB_BK, CB_BV, CB_BG = 4, 5, 6, 7
CB_CX, CB_CBC, CB_CZ = 8, 9, 10
CB_DQ, CB_DK, CB_DV, CB_DG = 11, 12, 13, 14
CB_MISC = 15
MISC_DT = 2 * RW_LORA_W + 2 * RW_LORA_A
N_COL_BLOCKS = 16
IN_W_PAD = N_COL_BLOCKS * BRANCH_W
PREP_TM = 256
NORM_ROWS = 32

V7X_VMEM_BYTES = 64 * 1024 * 1024
VMEM_LIMIT = V7X_VMEM_BYTES * 7 // 8


def _cparams(n_axes):
    return pltpu.CompilerParams(dimension_semantics=("arbitrary",) * n_axes, vmem_limit_bytes=VMEM_LIMIT)


def _dot(a, b):
    return jnp.dot(a, b, preferred_element_type=F32)


def _dot_nt(a, b):
    return lax.dot_general(a, b, (((1,), (1,)), ((), ())), preferred_element_type=F32)


def _dot_tn(a, b):
    return lax.dot_general(a, b, (((0,), (0,)), ((), ())), preferred_element_type=F32)


def _split3(x):
    hi = x.astype(BF16)
    r1 = x - hi.astype(F32)
    mid = r1.astype(BF16)
    lo = (r1 - mid.astype(F32)).astype(BF16)
    return hi, mid, lo


def _dot_exact_lhs(m_bf16, x):
    hi, mid, lo = _split3(x)
    return _dot(m_bf16, hi) + _dot(m_bf16, mid) + _dot(m_bf16, lo)


def _dot_exact_rhs(x, m_bf16):
    hi, mid, lo = _split3(x)
    return _dot(hi, m_bf16) + _dot(mid, m_bf16) + _dot(lo, m_bf16)


def _dot_nt_exact_lhs_f32(x, m_bf16):
    hi, mid, lo = _split3(x)
    return _dot_nt(hi, m_bf16) + _dot_nt(mid, m_bf16) + _dot_nt(lo, m_bf16)


def _sigmoid(x):
    return 1.0 / (1.0 + jnp.exp(-x))


def _silu(x):
    return x * _sigmoid(x)


def _softplus(x):
    return jnp.maximum(x, 0.0) + jnp.log(1.0 + jnp.exp(-jnp.abs(x)))


def _group_matrix(width, group):
    g = np.arange(width) // group
    return jnp.asarray((g[:, None] == g[None, :]).astype(np.float32), BF16)


def _head_rms(x, gmat, group, w):
    ms = _dot_exact_rhs(x * x, gmat) * (1.0 / group)
    return x * lax.rsqrt(ms + NORM_EPS) * w


def _tile_rows(n_rows):
    return 768 if n_rows % 768 == 0 else 256


def _row_spec(cb, tm=PREP_TM):
    return pl.BlockSpec((1, tm, BRANCH_W), lambda b, i: (b, i, cb))


def _halo_specs(cb, tm, lt):
    per = tm // SUBLANES
    last = lt // SUBLANES - 1
    prev = pl.BlockSpec((1, SUBLANES, BRANCH_W), lambda b, i: (b, jnp.maximum(i * per - 1, 0), cb))
    nxt = pl.BlockSpec((1, SUBLANES, BRANCH_W), lambda b, i: (b, jnp.minimum((i + 1) * per, last), cb))
    return prev, nxt


def _const_spec(shape):
    return pl.BlockSpec(shape, lambda *_: (0,) * len(shape))


def _neighbours(x, prev_blk, next_blk, i, n_tiles, n_ctx_tiles):
    tm = x.shape[0]
    row = lax.broadcasted_iota(jnp.int32, (tm, 1), 0)
    seg_start = jnp.logical_or(i == 0, i == n_ctx_tiles)
    seg_end = jnp.logical_or(i == n_ctx_tiles - 1, i == n_tiles - 1)
    prev_row = jnp.where(seg_start, 0.0, prev_blk[SUBLANES - 1:SUBLANES, :])
    next_row = jnp.where(seg_end, 0.0, next_blk[0:1, :])
    x_prev = jnp.where(row == 0, prev_row, pltpu.roll(x, 1, 0))
    x_next = jnp.where(row == tm - 1, next_row, pltpu.roll(x, tm - 1, 0))
    return x_prev, x_next


def _small_mm_kernel(a_ref, w_ref, b_ref, o_ref):
    o_ref[...] = _dot(a_ref[...].astype(BF16), w_ref[...].astype(BF16)) + b_ref[...]


def small_matmul(a, w, b, tn=512):
    m, k = a.shape
    n = w.shape[1]
    return pl.pallas_call(
        _small_mm_kernel,
        grid=(n // tn,),
        in_specs=[pl.BlockSpec((m, k), lambda j: (0, 0)),
                  pl.BlockSpec((k, tn), lambda j: (0, j)),
                  pl.BlockSpec((1, tn), lambda j: (0, j))],
        out_specs=pl.BlockSpec((m, tn), lambda j: (0, j)),
        out_shape=jax.ShapeDtypeStruct((m, n), F32),
        compiler_params=_cparams(1),
        name="adaln_mm",
    )(a, w, b.reshape(1, n))


def _inproj_kernel(x_ref, nw_ref, sc_ref, sh_ref, scc_ref, shc_ref, w_ref, p_ref, h_ref, hs_ref, *,
                   tiles_per_batch, n_ctx):
    @pl.when(pl.program_id(1) == 0)
    def _():
        tm = x_ref.shape[0]
        rc = NORM_ROWS
        base = (pl.program_id(0) % tiles_per_batch) * tm

        def norm_rows(k, carry):
            r0 = pl.multiple_of(k * rc, rc)
            xf = x_ref[pl.ds(r0, rc), :]
            ms = jnp.mean(xf * xf, axis=-1, keepdims=True)
            y = xf * lax.rsqrt(ms + NORM_EPS) * nw_ref[...]
            is_ctx = base + r0 < n_ctx
            sc = 1.0 + jnp.where(is_ctx, scc_ref[...], sc_ref[0])
            sh = jnp.where(is_ctx, shc_ref[...], sh_ref[0])
            h = (y * sc + sh).astype(BF16)
            hs_ref[pl.ds(r0, rc), :] = h
            h_ref[pl.ds(r0, rc), :] = h
            return carry

        lax.fori_loop(0, tm // rc, norm_rows, 0, unroll=4)

    p_ref[...] = _dot(hs_ref[...], w_ref[...])


def in_projection(xs2, norm_w, scale, shift, scale_c, shift_c, w_bf16, lt, n_ctx, tn=1024):
    m, d = xs2.shape
    n = w_bf16.shape[1]
    tm = _tile_rows(lt)
    tpb = lt // tm
    nb = scale.shape[0]
    kern = functools.partial(_inproj_kernel, tiles_per_batch=tpb, n_ctx=n_ctx)
    return pl.pallas_call(
        kern,
        grid=(m // tm, n // tn),
        in_specs=[pl.BlockSpec((tm, d), lambda i, j: (i, 0)),
                  pl.BlockSpec((1, d), lambda i, j: (0, 0)),
                  pl.BlockSpec((1, 1, d), lambda i, j: (i // tpb, 0, 0)),
                  pl.BlockSpec((1, 1, d), lambda i, j: (i // tpb, 0, 0)),
                  pl.BlockSpec((1, d), lambda i, j: (0, 0)),
                  pl.BlockSpec((1, d), lambda i, j: (0, 0)),
                  pl.BlockSpec((d, tn), lambda i, j: (0, j))],
        out_specs=[pl.BlockSpec((tm, tn), lambda i, j: (i, j)),
                   pl.BlockSpec((tm, d), lambda i, j: (i, 0))],
        out_shape=[jax.ShapeDtypeStruct((m, n), F32),
                   jax.ShapeDtypeStruct((m, d), BF16)],
        scratch_shapes=[pltpu.VMEM((tm, d), BF16)],
        compiler_params=_cparams(2),
        name="in_proj",
    )(xs2, norm_w.reshape(1, d), scale.reshape(nb, 1, d), shift.reshape(nb, 1, d),
      scale_c.reshape(1, d), shift_c.reshape(1, d), w_bf16)


def _gate_up_kernel(h_ref, ya_ref, yb_ref, yc_ref, yd_ref, wg_ref, wu_ref, o_ref):
    h = h_ref[...]
    acc = None
    for i, y_ref in enumerate((ya_ref, yb_ref, yc_ref, yd_ref)):
        g = _dot(h, wg_ref[i])
        u = _dot(y_ref[...], wu_ref[i])
        t = _sigmoid(g) * u
        acc = t if acc is None else acc + t
    o_ref[...] = acc.astype(BF16)


def gate_up(h, ys, wg_bf16, wu_bf16, lt, tn=512):
    m, d = h.shape
    w = ys[0].shape[1]
    tm = _tile_rows(lt)
    y_spec = pl.BlockSpec((tm, w), lambda i, j: (i, 0))
    return pl.pallas_call(
        _gate_up_kernel,
        grid=(m // tm, d // tn),
        in_specs=[pl.BlockSpec((tm, d), lambda i, j: (i, 0)), y_spec, y_spec, y_spec, y_spec,
                  pl.BlockSpec((N_BRANCH, d, tn), lambda i, j: (0, 0, j)),
                  pl.BlockSpec((N_BRANCH, w, tn), lambda i, j: (0, 0, j))],
        out_specs=pl.BlockSpec((tm, tn), lambda i, j: (i, j)),
        out_shape=jax.ShapeDtypeStruct((m, d), BF16),
        compiler_params=_cparams(2),
        name="gate_up",
    )(h, *ys, wg_bf16, wu_bf16)


def _out_proj_kernel(a_ref, w_ref, x_ref, g_ref, gc_ref, o_ref, *, tiles_per_batch, n_ctx):
    tm = x_ref.shape[0]
    row = (pl.program_id(0) % tiles_per_batch) * tm + lax.broadcasted_iota(jnp.int32, (tm, 1), 0)
    gate = jnp.where(row < n_ctx, gc_ref[...], g_ref[0])
    o_ref[...] = x_ref[...] + gate * _dot(a_ref[...], w_ref[...])


def out_projection(acc, w_bf16, xs2, gate, gate_c, lt, n_ctx, tn=512):
    m, d = xs2.shape
    tm = _tile_rows(lt)
    tpb = lt // tm
    nb = gate.shape[0]
    kern = functools.partial(_out_proj_kernel, tiles_per_batch=tpb, n_ctx=n_ctx)
    return pl.pallas_call(
        kern,
        grid=(m // tm, d // tn),
        in_specs=[pl.BlockSpec((tm, d), lambda i, j: (i, 0)),
                  pl.BlockSpec((d, tn), lambda i, j: (0, j)),
                  pl.BlockSpec((tm, tn), lambda i, j: (i, j)),
                  pl.BlockSpec((1, 1, tn), lambda i, j: (i // tpb, 0, j)),
                  pl.BlockSpec((1, tn), lambda i, j: (0, j))],
        out_specs=pl.BlockSpec((tm, tn), lambda i, j: (i, j)),
        out_shape=jax.ShapeDtypeStruct((m, d), F32),
        compiler_params=_cparams(2),
        name="out_proj",
    )(acc, w_bf16, xs2, gate.reshape(nb, 1, d), gate_c.reshape(1, d))


def _rope(x, cos, sin):
    w = x.shape[1]
    lane = lax.broadcasted_iota(jnp.int32, (1, w), 1)
    first = ((lane // (DA_DIM // 4)) % 2) == 0
    rot = jnp.where(first, -pltpu.roll(x, w - DA_DIM // 4, 1), pltpu.roll(x, DA_DIM // 4, 1))
    return x * cos + rot * sin


def _prep_attn_kernel(ak_ref, av_ref, dq_ref, dk_ref, dv_ref, cos_ref, sin_ref, g64_ref,
                      wak_ref, wdq_ref, wdk_ref, kn_ref, va_ref, qt_ref, kh_ref, vt_ref):
    g64 = g64_ref[...]
    cos = cos_ref[...]
    sin = sin_ref[...]
    kn_ref[0] = _head_rms(ak_ref[0], g64, NA_DIM, wak_ref[...]).astype(BF16)
    va_ref[0] = av_ref[0].astype(BF16)
    q = _rope(_head_rms(dq_ref[0], g64, DA_DIM, wdq_ref[...]), cos, sin)
    k = _rope(_head_rms(dk_ref[0], g64, DA_DIM, wdk_ref[...]), cos, sin).astype(BF16)
    qt_ref[0] = q.T.astype(BF16)
    for hc in range(2 * DA_HEADS):
        kh_ref[0, hc] = k[:, hc * DA_DIM:(hc + 1) * DA_DIM]
    v = dv_ref[0]
    dv = 2 * DA_DIM
    for h in range(DA_HEADS):
        vt_ref[0, h, 0, 0:dv, :] = v[:, h * dv:(h + 1) * dv].T.astype(BF16)
        vt_ref[0, h, 0, dv:dv + DA_ONES, :] = jnp.ones((DA_ONES, v.shape[0]), BF16)


def prep_attention(p3, cos, sin, na_k_w, da_q_w, da_k_w):
    bsz, lt, _ = p3.shape
    tm = PREP_TM
    w = BRANCH_W
    tile = lambda v, s=1.0: (jnp.tile(v, w // v.shape[0]) * s).reshape(1, w)
    tab_spec = pl.BlockSpec((tm, w), lambda b, i: (i, 0))
    vec = _const_spec((1, w))
    return pl.pallas_call(
        _prep_attn_kernel,
        grid=(bsz, lt // tm),
        in_specs=[_row_spec(CB_AK), _row_spec(CB_AV), _row_spec(CB_DQ), _row_spec(CB_DK), _row_spec(CB_DV),
                  tab_spec, tab_spec, _const_spec((w, w)), vec, vec, vec],
        out_specs=[pl.BlockSpec((1, tm, w), lambda b, i: (b, i, 0)),
                   pl.BlockSpec((1, tm, w), lambda b, i: (b, i, 0)),
                   pl.BlockSpec((1, w, tm), lambda b, i: (b, 0, i)),
                   pl.BlockSpec((1, 2 * DA_HEADS, tm, DA_DIM), lambda b, i: (b, 0, i, 0)),
                   pl.BlockSpec((1, DA_HEADS, 1, 2 * DA_DIM + DA_ONES, tm), lambda b, i: (b, 0, i, 0, 0))],
        out_shape=[jax.ShapeDtypeStruct((bsz, lt, w), BF16),
                   jax.ShapeDtypeStruct((bsz, lt, w), BF16),
                   jax.ShapeDtypeStruct((bsz, w, lt), BF16),
                   jax.ShapeDtypeStruct((bsz, 2 * DA_HEADS, lt, DA_DIM), BF16),
                   jax.ShapeDtypeStruct((bsz, DA_HEADS, lt // tm, 2 * DA_DIM + DA_ONES, tm), BF16)],
        compiler_params=_cparams(2),
        name="prep_attn",
    )(p3, p3, p3, p3, p3, cos, sin, _group_matrix(w, DA_DIM),
      tile(na_k_w), tile(da_q_w, DA_DIM ** -0.5 * math.log2(math.e)), tile(da_k_w))


def rope_tables(lt, n_ctx):
    nf = DA_DIM // 4
    t = jnp.arange(lt - n_ctx, dtype=jnp.int32)
    rows, cols = t // GRID_W, t % GRID_W
    inv = ROPE_BASE ** (-jnp.arange(nf, dtype=F32) / nf)
    ang_r = rows.astype(F32)[:, None] * inv
    ang_c = cols.astype(F32)[:, None] * inv
    ang = jnp.concatenate([ang_r, ang_r, ang_c, ang_c], axis=-1)
    ang = jnp.concatenate([jnp.zeros((n_ctx, DA_DIM), F32), ang], axis=0)
    reps = BRANCH_W // DA_DIM
    return jnp.tile(jnp.cos(ang), (1, reps)), jnp.tile(jnp.sin(ang), (1, reps))


def _flash_kernel(lam_ref, qt_ref, k_ref, vt_ref, g_ref, sw_ref, o_ref, *, n_latent_iters, out_scale):
    tq = qt_ref.shape[2]
    dve = vt_ref.shape[3]
    dv = dve - DA_ONES
    tk = vt_ref.shape[4]
    qts = [qt_ref[0, c * DA_DIM:(c + 1) * DA_DIM, :] for c in range(2)]

    def attend(carry, first_chunk, n_chunks):
        chunks = [first_chunk + g for g in range(n_chunks)]
        sts = [[_dot(k_ref[0, c, pl.ds(pl.multiple_of(ch * tk, tk), tk), :], qts[c]) for c in range(2)]
               for ch in chunks]
        carry = list(carry)
        for g, ch in enumerate(chunks):
            vt = vt_ref[0, 0, ch]
            for c in range(2):
                m, acc = carry[c]
                st = sts[g][c].astype(BF16)
                m_new = jnp.maximum(m, jnp.max(st, axis=0, keepdims=True).astype(F32))
                pt = jnp.exp2(st - m_new.astype(BF16))
                carry[c] = (m_new, jnp.exp2(m - m_new) * acc + _dot(vt, pt))
        return tuple(carry)

    init = tuple((jnp.full((1, tq), NEG_INF, F32), jnp.zeros((dve, tq), F32)) for _ in range(2))
    carry = attend(init, 0, 1)
    n_iters = jnp.where(pl.program_id(2) == 0, 0, n_latent_iters)
    res = lax.fori_loop(0, n_iters, lambda i, cr: attend(cr, 1 + i * DA_NSUB, DA_NSUB), carry)
    outs = [acc[0:dv] * (1.0 / acc[dv:dv + 1]) for (_, acc) in res]
    ot = outs[0] - lam_ref[0] * outs[1]
    ot = ot * lax.rsqrt(jnp.mean(ot * ot, axis=0, keepdims=True) + NORM_EPS)
    o = ot.T * (sw_ref[...] * out_scale)
    o_ref[0] = (o * _silu(g_ref[0])).astype(BF16)


def flash_diff_attention(qt, kh, vt, p3, lam, subln_w, lam_init):
    bsz, w, lt = qt.shape
    dv = 2 * DA_DIM
    nk = lt // DA_TK
    n_latent_iters = (nk - 1) // DA_NSUB
    kern = functools.partial(_flash_kernel, n_latent_iters=n_latent_iters, out_scale=1.0 - lam_init)
    g_blocks = BRANCH_W // dv
    return pl.pallas_call(
        kern,
        grid=(bsz, DA_HEADS, lt // DA_TQ),
        in_specs=[pl.BlockSpec(memory_space=pltpu.SMEM),
                  pl.BlockSpec((1, dv, DA_TQ), lambda b, h, qi: (b, h, qi)),
                  pl.BlockSpec((1, 2, lt, DA_DIM), lambda b, h, qi: (b, h, 0, 0)),
                  pl.BlockSpec((1, 1, nk, dv + DA_ONES, DA_TK), lambda b, h, qi: (b, h, 0, 0, 0)),
                  pl.BlockSpec((1, DA_TQ, dv), lambda b, h, qi: (b, qi, CB_DG * g_blocks + h)),
                  _const_spec((1, dv))],
        out_specs=pl.BlockSpec((1, DA_TQ, dv), lambda b, h, qi: (b, qi, h)),
        out_shape=jax.ShapeDtypeStruct((bsz, lt, BRANCH_W), BF16),
        compiler_params=_cparams(3),
        name="flash_attn",
    )(lam.reshape(1).astype(F32), qt, kh, vt, p3, subln_w.reshape(1, dv))


def _na_kernel(q_ref, g_ref, k_ref, v_ref, *rest, n_rows, n_ctx):
    bias_refs = rest[:NA_NBLK]
    g64_ref, wq_ref, o_ref = rest[NA_NBLK:]
    n_loc = NA_WIN_R * GRID_W
    q = _head_rms(q_ref[0], g64_ref[...], NA_DIM, wq_ref[...]).astype(BF16)
    g = g_ref[0]
    sls = [slice(h * NA_DIM, (h + 1) * NA_DIM) for h in range(NA_HEADS)]
    rows, starts, units = [], [], []
    for j in range(NA_NBLK):
        blk = pl.program_id(1) * NA_NBLK + j
        r = jnp.maximum(blk - n_ctx // GRID_W, 0)
        r0 = jnp.clip(r - NA_WIN_R // 2, 0, n_rows - NA_WIN_R)
        starts.append(pl.multiple_of(n_ctx + r0 * GRID_W, GRID_W))
        rows.append(slice(j * GRID_W, (j + 1) * GRID_W))
        units += [(j, h) for h in range(NA_HEADS)]
    qh = [q[rows[j], sls[h]] for j, h in units]
    s_loc = [_dot_nt(qh[u], k_ref[0, pl.ds(starts[j], n_loc), sls[h]]) + bias_refs[j][0, h]
             for u, (j, h) in enumerate(units)]
    s_ctx = [_dot_nt(qh[u], k_ref[0, 0:n_ctx, sls[h]]) for u, (j, h) in enumerate(units)]
    us = range(len(units))
    m = [jnp.maximum(jnp.max(s_loc[u], axis=-1, keepdims=True), jnp.max(s_ctx[u], axis=-1, keepdims=True))
         for u in us]
    p_loc = [jnp.exp(s_loc[u] - m[u]) for u in us]
    p_ctx = [jnp.exp(s_ctx[u] - m[u]) for u in us]
    l = [jnp.sum(p_loc[u], axis=-1, keepdims=True) + jnp.sum(p_ctx[u], axis=-1, keepdims=True) for u in us]
    o = [_dot(p_loc[u].astype(BF16), v_ref[0, pl.ds(starts[j], n_loc), sls[h]])
         + _dot(p_ctx[u].astype(BF16), v_ref[0, 0:n_ctx, sls[h]]) for u, (j, h) in enumerate(units)]
    for u, (j, h) in enumerate(units):
        o_ref[0, rows[j], sls[h]] = ((o[u] / l[u]) * _silu(g[rows[j], sls[h]])).astype(BF16)


def neighbourhood_attention(p3, kn, va, bias_tbl, na_q_w, n_ctx):
    bsz, lt, w = kn.shape
    n_rows = (lt - n_ctx) // GRID_W
    ncb = n_ctx // GRID_W
    half = NA_WIN_R // 2

    def bias_spec(j):
        def idx(b, s):
            blk = s * NA_NBLK + j
            r = blk - ncb
            off = r - jnp.clip(r - half, 0, n_rows - NA_WIN_R)
            return (jnp.where(blk < ncb, NA_WIN_R, off), 0, 0, 0)
        return pl.BlockSpec((1, NA_HEADS, GRID_W, NA_WIN_R * GRID_W), idx)

    tq = NA_NBLK * GRID_W
    full_spec = pl.BlockSpec((1, lt, w), lambda b, s: (b, 0, 0))
    wq = (jnp.tile(na_q_w, w // NA_DIM) * NA_DIM ** -0.5).reshape(1, w)
    return pl.pallas_call(
        functools.partial(_na_kernel, n_rows=n_rows, n_ctx=n_ctx),
        grid=(bsz, lt // tq),
        in_specs=[_row_spec(CB_AQ, tq), _row_spec(CB_AG, tq), full_spec, full_spec,
                  *[bias_spec(j) for j in range(NA_NBLK)],
                  _const_spec((w, w)), _const_spec((1, w))],
        out_specs=pl.BlockSpec((1, tq, w), lambda b, s: (b, s, 0)),
        out_shape=jax.ShapeDtypeStruct((bsz, lt, w), BF16),
        compiler_params=_cparams(2),
        name="nbr_attn",
    )(p3, p3, kn, va, *([bias_tbl] * NA_NBLK), _group_matrix(w, NA_DIM), wq)


def na_bias_table(rpb):
    col = np.arange(GRID_W)
    c0 = np.clip(col - NA_WIN_C // 2, 0, GRID_W - NA_WIN_C)
    col_ok = (col[None, :] >= c0[:, None]) & (col[None, :] < c0[:, None] + NA_WIN_C)
    d_col = np.clip(col[None, :] - col[:, None] + (NA_WIN_C - 1), 0, 2 * NA_WIN_C - 2)
    onehot = (d_col[:, :, None] == np.arange(2 * NA_WIN_C - 1)).astype(np.float32)
    by_col = jnp.einsum('hrc,qwc->hrqw', rpb.astype(F32), jnp.asarray(onehot),
                        precision=lax.Precision.HIGHEST)
    by_col = jnp.where(jnp.asarray(col_ok)[None, None], by_col, NEG_INF)
    tbl = jnp.stack([by_col[:, NA_WIN_R - 1 - o:2 * NA_WIN_R - 1 - o] for o in range(NA_WIN_R)]
                    + [jnp.full((NA_HEADS, NA_WIN_R, GRID_W, GRID_W), NEG_INF, F32)])
    tbl = jnp.transpose(tbl, (0, 1, 3, 2, 4))
    return tbl.reshape(NA_WIN_R + 1, NA_HEADS, GRID_W, NA_WIN_R * GRID_W)


def _prep_rwkv_kernel(r_ref, rp_ref, rn_ref, k_ref, kp_ref, kn_ref, v_ref, vp_ref, vn_ref,
                      m_ref, mp_ref, mn_ref, mu_ref, w0_ref, a0_ref, w2_ref, a2_ref, kk_ref, ka_ref,
                      rk_ref, g64_ref,
                      ro_ref, vo_ref, ao_ref, lw_ref, kd_ref, bo_ref, bonus_ref, *, n_tiles, n_ctx_tiles):
    i = pl.program_id(1)
    w = BRANCH_W

    def shifted(x_ref, p_ref, n_ref, mu):
        x = x_ref[0]
        xp, xn = _neighbours(x, p_ref[0], n_ref[0], i, n_tiles, n_ctx_tiles)
        return x + (0.5 * (xp + xn) - x) * mu

    r = shifted(r_ref, rp_ref, rn_ref, mu_ref[0:1, :])
    k = shifted(k_ref, kp_ref, kn_ref, mu_ref[1:2, :])
    v = shifted(v_ref, vp_ref, vn_ref, mu_ref[2:3, :])
    misc = shifted(m_ref, mp_ref, mn_ref, mu_ref[3:4, :])
    wd = jnp.tanh(misc[:, 0:2 * RW_LORA_W]).astype(BF16)
    ad = misc[:, 2 * RW_LORA_W:MISC_DT].astype(BF16)
    w_log = w0_ref[...] + _dot(wd, w2_ref[...])
    gate = _sigmoid(a0_ref[...] + _dot(ad, a2_ref[...]))
    log_decay = -math.exp(-0.5) * _sigmoid(w_log)
    g64 = g64_ref[...]
    kk = k * kk_ref[...]
    kk = kk / jnp.maximum(jnp.sqrt(_dot_exact_rhs(kk * kk, g64)), 1e-12)
    ro_ref[0] = r
    vo_ref[0] = v.astype(BF16)
    ao_ref[0] = -kk
    coef = None
    for d in range(2):
        a_d = gate[:, d * w:(d + 1) * w]
        kd = k * (1.0 + (a_d - 1.0) * ka_ref[...])
        lw_ref[d, 0] = log_decay[:, d * w:(d + 1) * w]
        kd_ref[d, 0] = kd
        bo_ref[d, 0] = kk * a_d
        coef = kd if coef is None else coef + kd
    bonus_ref[0] = _dot_exact_rhs(r * coef * rk_ref[...], g64) * v


def prep_rwkv(p3, mu, w0, w2, a0, a2, k_k, k_a, r_k, n_ctx):
    bsz, lt, _ = p3.shape
    tm = PREP_TM
    w = BRANCH_W
    n_tiles = lt // tm
    mu4 = jnp.stack([mu[0:w], mu[w:2 * w], mu[2 * w:3 * w],
                     jnp.pad(mu[3 * w:], (0, w - (mu.shape[0] - 3 * w)))])
    zero = jnp.zeros((RW_LORA_W, w), F32)
    w2cat = jnp.concatenate([jnp.concatenate([w2[0], zero], axis=1),
                             jnp.concatenate([zero, w2[1]], axis=1)], axis=0).astype(BF16)
    a2cat = jnp.concatenate([jnp.concatenate([a2[0], zero], axis=1),
                             jnp.concatenate([zero, a2[1]], axis=1)], axis=0).astype(BF16)
    specs = []
    for cb in (CB_BR, CB_BK, CB_BV, CB_MISC):
        specs += [_row_spec(cb), *_halo_specs(cb, tm, lt)]
    vec = _const_spec((1, w))
    tok = pl.BlockSpec((1, tm, w), lambda b, i: (b, i, 0))
    tok2 = pl.BlockSpec((2, 1, tm, w), lambda b, i: (0, b, i, 0))
    kern = functools.partial(_prep_rwkv_kernel, n_tiles=n_tiles, n_ctx_tiles=n_ctx // tm)
    return pl.pallas_call(
        kern,
        grid=(bsz, n_tiles),
        in_specs=specs + [_const_spec((4, w)), _const_spec((1, 2 * w)), _const_spec((1, 2 * w)),
                          _const_spec((2 * RW_LORA_W, 2 * w)), _const_spec((2 * RW_LORA_A, 2 * w)),
                          vec, vec, vec, _const_spec((w, w))],
        out_specs=[tok, tok, tok, tok2, tok2, tok2, tok],
        out_shape=[jax.ShapeDtypeStruct((bsz, lt, w), F32),
                   jax.ShapeDtypeStruct((bsz, lt, w), BF16),
                   jax.ShapeDtypeStruct((bsz, lt, w), F32),
                   jax.ShapeDtypeStruct((2, bsz, lt, w), F32),
                   jax.ShapeDtypeStruct((2, bsz, lt, w), F32),
                   jax.ShapeDtypeStruct((2, bsz, lt, w), F32),
                   jax.ShapeDtypeStruct((bsz, lt, w), F32)],
        compiler_params=_cparams(2),
        name="prep_rwkv",
    )(*([p3] * 12), mu4, w0.reshape(1, 2 * w), a0.reshape(1, 2 * w), w2cat, a2cat,
      k_k.reshape(1, w), k_a.reshape(1, w), r_k.reshape(1, w), _group_matrix(w, RW_DIM))


def _rwkv_kernel(tri_ref, mq_ref, r_ref, v_ref, a_ref, lw_ref, kd_ref, b_ref, y_ref, s_ref):
    c = RW_CHUNK
    nsub = r_ref.shape[1] // c
    d = pl.program_id(0)

    @pl.when(pl.program_id(2) == 0)
    def _():
        s_ref[...] = jnp.zeros_like(s_ref)

    tri = tri_ref[0]
    m_quad = mq_ref[0] > 0.5
    rows = lax.broadcasted_iota(jnp.int32, (c, c), 0)
    cols = lax.broadcasted_iota(jnp.int32, (c, c), 1)
    eye = (rows == cols).astype(F32)
    same_half = (rows >= c // 2) == (cols >= c // 2)
    heads = [slice(h * RW_DIM, (h + 1) * RW_DIM) for h in range(RW_HEADS)]

    row_sl, e_tot = [], []
    al, rh, rf, be, ka, bc, kc, vb = [], [], [], [], [], [], [], []
    for j in range(nsub):
        off = pl.multiple_of(jnp.where(d == 0, j, nsub - 1 - j) * c, c)
        rs = pl.ds(off, c)
        row_sl.append(rs)
        lw = lw_ref[0, 0, rs, :]
        cum = _dot_exact_lhs(tri, lw)
        tot = jnp.sum(lw, axis=0, keepdims=True)
        e_m = jnp.exp(-cum)
        e_t = jnp.exp(tot - cum)
        e_tot.append(jnp.exp(tot))
        b_in = b_ref[0, 0, rs, :]
        k_in = kd_ref[0, 0, rs, :]
        alpha = (a_ref[0, rs, :] * jnp.exp(cum - lw)).astype(BF16)
        rho_f = r_ref[0, rs, :] * jnp.exp(cum)
        rho = rho_f.astype(BF16)
        beta = (b_in * e_m).astype(BF16)
        kappa = (k_in * e_m).astype(BF16)
        beta_c = (b_in * e_t).astype(BF16)
        kappa_c = (k_in * e_t).astype(BF16)
        v_b = v_ref[0, rs, :]
        for sl in heads:
            al.append(alpha[:, sl])
            rh.append(rho[:, sl])
            rf.append(rho_f[:, sl])
            be.append(beta[:, sl])
            ka.append(kappa[:, sl])
            bc.append(beta_c[:, sl])
            kc.append(kappa_c[:, sl])
            vb.append(v_b[:, sl])

    units = range(nsub * RW_HEADS)
    cat = lambda x, y: jnp.concatenate([x, y], axis=0)
    zeros = jnp.zeros((c, RW_DIM), BF16)
    prod = [jnp.where(m_quad, _dot_nt(cat(al[u], rh[u]), cat(be[u], ka[u])), 0.0) for u in units]
    l_ab = [prod[u][0:c, 0:c] for u in units]
    top = [prod[u][0:c].astype(BF16) for u in units]
    bot = [prod[u][c:2 * c].astype(BF16) for u in units]
    l_d = [jnp.where(same_half, l_ab[u], 0.0) for u in units]
    l_o = [(l_ab[u] - l_d[u]).astype(BF16) for u in units]
    pw = [l_d[u].astype(BF16) for u in units]
    td = [eye + l_d[u] for u in units]
    for _ in range(int(math.log2(c)) - 2):
        pw = [_dot(pw[u], pw[u]).astype(BF16) for u in units]
        td = [td[u] + _dot(td[u].astype(BF16), pw[u]) for u in units]
    td_b = [td[u].astype(BF16) for u in units]
    x_o = [_dot(td_b[u], l_o[u]).astype(BF16) for u in units]
    tinv = [(td[u] + _dot(x_o[u], td_b[u])).astype(BF16) for u in units]
    akv = [_dot(top[u], cat(zeros, vb[u])).astype(BF16) for u in units]
    a_hat = [_dot(tinv[u], al[u]).astype(BF16) for u in units]
    v_hat = [_dot(tinv[u], akv[u]).astype(BF16) for u in units]
    r_hat = [(rf[u] + _dot(bot[u], cat(a_hat[u], zeros))).astype(BF16) for u in units]
    vv = [cat(v_hat[u], vb[u]) for u in units]
    y_hat = [_dot(bot[u], vv[u]) for u in units]
    q_mat = [_dot_tn(a_hat[u], bc[u]).astype(BF16) for u in units]
    n_mat = [_dot_tn(vv[u], cat(bc[u], kc[u])) for u in units]

    state = [s_ref[h] for h in range(RW_HEADS)]
    for j in range(nsub):
        for h, sl in enumerate(heads):
            u = j * RW_HEADS + h
            s_b = state[h].astype(BF16)
            y_ref[0, 0, row_sl[j], sl] = _dot_nt(r_hat[u], s_b) + y_hat[u]
            state[h] = state[h] * e_tot[j][:, sl] + _dot(s_b, q_mat[u]) + n_mat[u]
    for h in range(RW_HEADS):
        s_ref[h] = state[h]


def _scan_chunk_index(d, s, n_ctx_chunks, n_chunks):
    rev = jnp.where(s < n_ctx_chunks, n_ctx_chunks - 1 - s, n_chunks + n_ctx_chunks - 1 - s)
    return jnp.where(d == 0, s, rev)


def _direction_masks(c):
    i = np.arange(c)
    lower = (i[None, :] <= i[:, None]).astype(np.float32)
    tri = np.stack([lower, lower.T])
    strict = np.stack([lower - np.eye(c, dtype=np.float32), lower.T - np.eye(c, dtype=np.float32)])
    return tri, strict


def rwkv_scan(r, v, a, lw, kd, b, n_ctx):
    bsz, lt, w = r.shape
    c = RW_CHUNK
    blk = RW_NSUB * c
    nck = lt // blk
    ncc = n_ctx // blk
    tri, strict = _direction_masks(c)
    shared = pl.BlockSpec((1, blk, w), lambda d, bi, s: (bi, _scan_chunk_index(d, s, ncc, nck), 0))
    perdir = pl.BlockSpec((1, 1, blk, w), lambda d, bi, s: (d, bi, _scan_chunk_index(d, s, ncc, nck), 0))
    mask_spec = pl.BlockSpec((1, c, c), lambda d, bi, s: (d, 0, 0))
    quad_spec = pl.BlockSpec((1, 2 * c, 2 * c), lambda d, bi, s: (d, 0, 0))
    quad = np.concatenate([np.tile(strict, (1, 1, 2)), np.tile(tri, (1, 1, 2))], axis=1)
    return pl.pallas_call(
        _rwkv_kernel,
        grid=(2, bsz, nck),
        in_specs=[mask_spec, quad_spec, shared, shared, shared, perdir, perdir, perdir],
        out_specs=perdir,
        out_shape=jax.ShapeDtypeStruct((2, bsz, lt, w), F32),
        scratch_shapes=[pltpu.VMEM((RW_HEADS, RW_DIM, RW_DIM), F32)],
        compiler_params=_cparams(3),
        name="rwkv_scan",
    )(jnp.asarray(tri, BF16), jnp.asarray(quad, F32), r, v, a, lw, kd, b)


def _finish_rwkv_kernel(y_ref, bonus_ref, g_ref, lnw_ref, lnb_ref, g64_ref, o_ref):
    y = y_ref[0, 0] + y_ref[1, 0]
    g64 = g64_ref[...]
    mean = _dot_exact_rhs(y, g64) * (1.0 / RW_DIM)
    yc = y - mean
    var = _dot_exact_rhs(yc * yc, g64) * (1.0 / RW_DIM)
    yn = yc * lax.rsqrt(var + RW_GN_EPS) * lnw_ref[...] + lnb_ref[...]
    o_ref[0] = ((yn + bonus_ref[0]) * _silu(g_ref[0])).astype(BF16)


def finish_rwkv(y, bonus, p3, ln_w, ln_b):
    _, bsz, lt, w = y.shape
    tm = _tile_rows(lt)
    tok = pl.BlockSpec((1, tm, w), lambda b, i: (b, i, 0))
    vec = _const_spec((1, w))
    return pl.pallas_call(
        _finish_rwkv_kernel,
        grid=(bsz, lt // tm),
        in_specs=[pl.BlockSpec((2, 1, tm, w), lambda b, i: (0, b, i, 0)), tok, _row_spec(CB_BG, tm),
                  vec, vec, _const_spec((w, w))],
        out_specs=tok,
        out_shape=jax.ShapeDtypeStruct((bsz, lt, w), BF16),
        compiler_params=_cparams(2),
        name="finish_rwkv",
    )(y, bonus, p3, ln_w.reshape(1, w), ln_b.reshape(1, w), _group_matrix(w, RW_DIM))


def _prep_ssd_kernel(x_ref, xp_ref, xn_ref, bc_ref, bcp_ref, bcn_ref, m_ref, cw_ref, cb_ref, dtb_ref,
                     aneg_ref, exp_ref, dsk_ref,
                     xq_ref, bco_ref, bt_ref, a_ref, at_ref, dskip_ref, *, n_tiles, n_ctx_tiles):
    i = pl.program_id(1)
    w = BRANCH_W

    def conv(x_ref, p_ref, n_ref, half):
        x = x_ref[0]
        xp, xn = _neighbours(x, p_ref[0], n_ref[0], i, n_tiles, n_ctx_tiles)
        lo = half * w
        y = (xp * cw_ref[0:1, lo:lo + w] + x * cw_ref[1:2, lo:lo + w] + xn * cw_ref[2:3, lo:lo + w]
             + cb_ref[:, lo:lo + w])
        return _silu(y)

    xs = conv(x_ref, xp_ref, xn_ref, 0)
    bc = conv(bc_ref, bcp_ref, bcn_ref, 1)
    bco_ref[0] = bc.astype(BF16)
    bm_f = bc[:, 0:SSM_GROUPS * SSM_STATE]
    q = SSM_CHUNK
    for j in range(bc.shape[0] // q):
        bt_ref[0, j] = bm_f[j * q:(j + 1) * q, :].T.astype(BF16)
    dt = _softplus(m_ref[0][:, MISC_DT:MISC_DT + LANES] + dtb_ref[...])
    lane = lax.broadcasted_iota(jnp.int32, (1, LANES), 1)
    dt = jnp.where(lane < 2 * SSM_HEADS, dt, 0.0)
    dtx = _dot_exact_rhs(dt, exp_ref[...])
    a_all = dt * aneg_ref[...]
    first = lane < SSM_HEADS
    a_dirs = [jnp.where(first, a_all, 0.0), jnp.where(first, pltpu.roll(a_all, LANES - SSM_HEADS, 1), 0.0)]
    for d in range(2):
        xq_ref[d, 0] = xs * dtx[:, d * w:(d + 1) * w]
        a_ref[d, 0] = a_dirs[d]
        for j in range(bc.shape[0] // q):
            at_ref[d, 0, j] = a_dirs[d][j * q:(j + 1) * q, :].T[0:2 * SUBLANES, :]
    dskip_ref[0] = xs * dsk_ref[...]


def prep_ssd(p3, conv_w, conv_b, dt_bias, a_log, d_skip, n_ctx):
    bsz, lt, _ = p3.shape
    tm = PREP_TM
    w = BRANCH_W
    n_tiles = lt // tm
    nh2 = 2 * SSM_HEADS
    pad_lanes = lambda v: jnp.pad(v.reshape(1, nh2), ((0, 0), (0, LANES - nh2)))
    expand = np.zeros((LANES, 2 * w), np.float32)
    for d in range(2):
        for h in range(SSM_HEADS):
            expand[d * SSM_HEADS + h, d * w + h * SSM_HEAD_DIM:d * w + (h + 1) * SSM_HEAD_DIM] = 1.0
    specs = [_row_spec(CB_CX), *_halo_specs(CB_CX, tm, lt), _row_spec(CB_CBC), *_halo_specs(CB_CBC, tm, lt),
             _row_spec(CB_MISC)]
    tok = pl.BlockSpec((1, tm, w), lambda b, i: (b, i, 0))
    gs = SSM_GROUPS * SSM_STATE
    kern = functools.partial(_prep_ssd_kernel, n_tiles=n_tiles, n_ctx_tiles=n_ctx // tm)
    return pl.pallas_call(
        kern,
        grid=(bsz, n_tiles),
        in_specs=specs + [_const_spec((3, 2 * w)), _const_spec((1, 2 * w)), _const_spec((1, LANES)),
                          _const_spec((1, LANES)), _const_spec((LANES, 2 * w)), _const_spec((1, w))],
        out_specs=[pl.BlockSpec((2, 1, tm, w), lambda b, i: (0, b, i, 0)), tok,
                   pl.BlockSpec((1, tm // SSM_CHUNK, gs, SSM_CHUNK), lambda b, i: (b, i, 0, 0)),
                   pl.BlockSpec((2, 1, tm, LANES), lambda b, i: (0, b, i, 0)),
                   pl.BlockSpec((2, 1, tm // SSM_CHUNK, 2 * SUBLANES, SSM_CHUNK), lambda b, i: (0, b, i, 0, 0)), tok],
        out_shape=[jax.ShapeDtypeStruct((2, bsz, lt, w), F32),
                   jax.ShapeDtypeStruct((bsz, lt, w), BF16),
                   jax.ShapeDtypeStruct((bsz, lt // SSM_CHUNK, gs, SSM_CHUNK), BF16),
                   jax.ShapeDtypeStruct((2, bsz, lt, LANES), F32),
                   jax.ShapeDtypeStruct((2, bsz, lt // SSM_CHUNK, 2 * SUBLANES, SSM_CHUNK), F32),
                   jax.ShapeDtypeStruct((bsz, lt, w), F32)],
        compiler_params=_cparams(2),
        name="prep_ssd",
    )(*([p3] * 7), conv_w, conv_b.reshape(1, 2 * w), pad_lanes(dt_bias), pad_lanes(-jnp.exp(a_log)),
      jnp.asarray(expand, BF16), jnp.repeat(d_skip, SSM_HEAD_DIM).reshape(1, w))


def _ssd_kernel(tri_ref, mi_ref, xq_ref, bc_ref, bt_ref, a_ref, at_ref, y_ref, s_ref):
    q = SSM_CHUNK
    nsub = bt_ref.shape[1]
    rep = SSM_HEADS // SSM_GROUPS
    gw = SSM_GROUPS * SSM_STATE
    p = SSM_HEAD_DIM
    d = pl.program_id(0)

    @pl.when(pl.program_id(2) == 0)
    def _():
        s_ref[...] = jnp.zeros_like(s_ref)

    tri = tri_ref[0]
    mask = mi_ref[0] > 0.5
    hs = range(SSM_HEADS)
    groups = [slice(g * SSM_STATE, (g + 1) * SSM_STATE) for g in range(SSM_GROUPS)]

    rows, cms, y_in, e_col, e_tot, upd = [], [], [], [], [], []
    for j in range(nsub):
        ch = jnp.where(d == 0, j, nsub - 1 - j)
        rs = pl.ds(pl.multiple_of(ch * q, q), q)
        rows.append(rs)
        a = a_ref[0, 0, rs, :]
        a_t = at_ref[0, 0, ch]
        xq = xq_ref[0, 0, rs, :]
        bm = bc_ref[0, rs, 0:gw]
        cm = bc_ref[0, rs, gw:2 * gw]
        bt = bt_ref[0, ch]
        acol = _dot_exact_lhs(tri, a)
        arow = _dot_nt_exact_lhs_f32(a_t, tri)
        tot = jnp.sum(a, axis=0, keepdims=True)
        cm_g = [cm[:, gs] for gs in groups]
        cb = [_dot_nt(cm_g[g], bm[:, gs]) for g, gs in enumerate(groups)]
        ac = [jnp.broadcast_to(acol[:, h:h + 1], (q, q)) for h in hs]
        ar = [jnp.broadcast_to(arow[h:h + 1, :], (q, q)) for h in hs]
        gmat = [(cb[h // rep] * jnp.exp(jnp.where(mask, ac[h] - ar[h], NEG_INF))).astype(BF16) for h in hs]
        xh = [xq[:, h * p:(h + 1) * p] for h in hs]
        tot_h = [jnp.broadcast_to(tot[:, h:h + 1], (1, p)) for h in hs]
        xd = [(xh[h] * jnp.exp(tot_h[h] - ac[h][:, :p])).astype(BF16) for h in hs]
        cms.append(cm_g)
        y_in.append([_dot(gmat[h], xh[h].astype(BF16)) for h in hs])
        e_col.append([jnp.exp(ac[h][:, :p]) for h in hs])
        e_tot.append([jnp.exp(tot_h[h]) for h in hs])
        upd.append([_dot(bt[groups[h // rep], :], xd[h]) for h in hs])

    state = [s_ref[h] for h in hs]
    for j in range(nsub):
        y_st = [_dot(cms[j][h // rep], state[h].astype(BF16)) for h in hs]
        for h in hs:
            y_ref[0, 0, rows[j], h * p:(h + 1) * p] = y_in[j][h] + y_st[h] * e_col[j][h]
            state[h] = state[h] * e_tot[j][h] + upd[j][h]
    for h in hs:
        s_ref[h] = state[h]


def ssd_scan(xq, bc, bt, a, a_t, n_ctx):
    _, bsz, lt, w = xq.shape
    q = SSM_CHUNK
    blk = SSM_NSUB * q
    nck = lt // blk
    ncc = n_ctx // blk
    gw = bt.shape[2]
    tri, _ = _direction_masks(q)

    def cidx(d, s):
        return _scan_chunk_index(d, s, ncc, nck)

    mask_spec = pl.BlockSpec((1, q, q), lambda d, bi, s: (d, 0, 0))
    return pl.pallas_call(
        _ssd_kernel,
        grid=(2, bsz, nck),
        in_specs=[mask_spec, mask_spec,
                  pl.BlockSpec((1, 1, blk, w), lambda d, bi, s: (d, bi, cidx(d, s), 0)),
                  pl.BlockSpec((1, blk, w), lambda d, bi, s: (bi, cidx(d, s), 0)),
                  pl.BlockSpec((1, SSM_NSUB, gw, q), lambda d, bi, s: (bi, cidx(d, s), 0, 0)),
                  pl.BlockSpec((1, 1, blk, LANES), lambda d, bi, s: (d, bi, cidx(d, s), 0)),
                  pl.BlockSpec((1, 1, SSM_NSUB, 2 * SUBLANES, q), lambda d, bi, s: (d, bi, cidx(d, s), 0, 0))],
        out_specs=pl.BlockSpec((1, 1, blk, w), lambda d, bi, s: (d, bi, cidx(d, s), 0)),
        out_shape=jax.ShapeDtypeStruct((2, bsz, lt, w), F32),
        scratch_shapes=[pltpu.VMEM((SSM_HEADS, SSM_STATE, SSM_HEAD_DIM), F32)],
        compiler_params=_cparams(3),
        name="ssd_scan",
    )(jnp.asarray(tri, BF16), jnp.asarray(tri, F32), xq, bc, bt, a, a_t)


def _finish_ssd_kernel(y_ref, dskip_ref, z_ref, nw_ref, g256_ref, o_ref):
    y = y_ref[0, 0] + y_ref[1, 0] + dskip_ref[0]
    g = y * _silu(z_ref[0])
    group = BRANCH_W // SSM_GROUPS
    o_ref[0] = _head_rms(g, g256_ref[...], group, nw_ref[...]).astype(BF16)


def finish_ssd(y, dskip, p3, norm_w):
    _, bsz, lt, w = y.shape
    tm = _tile_rows(lt)
    tok = pl.BlockSpec((1, tm, w), lambda b, i: (b, i, 0))
    return pl.pallas_call(
        _finish_ssd_kernel,
        grid=(bsz, lt // tm),
        in_specs=[pl.BlockSpec((2, 1, tm, w), lambda b, i: (0, b, i, 0)), tok, _row_spec(CB_CZ, tm),
                  _const_spec((1, w)), _const_spec((w, w))],
        out_specs=tok,
        out_shape=jax.ShapeDtypeStruct((bsz, lt, w), BF16),
        compiler_params=_cparams(2),
        name="finish_ssd",
    )(y, dskip, p3, norm_w.reshape(1, w), _group_matrix(w, w // SSM_GROUPS))


def _permute_w_in(w_in_l):
    w = BRANCH_W
    b0 = 4 * w
    lora0 = b0 + 3 * w
    bg0 = lora0 + MISC_DT
    c0 = bg0 + w
    dt0 = c0 + SSM_CONV_CH
    z0 = dt0 + 2 * SSM_HEADS
    d0 = z0 + w
    end = d0 + 4 * w
    parts = [w_in_l[:, 0:lora0], w_in_l[:, bg0:c0], w_in_l[:, c0:dt0], w_in_l[:, z0:d0], w_in_l[:, d0:end],
             w_in_l[:, lora0:bg0], w_in_l[:, dt0:z0]]
    used = sum(p.shape[1] for p in parts)
    pad = jnp.zeros((w_in_l.shape[0], IN_W_PAD - used), BF16)
    return jnp.concatenate([p.astype(BF16) for p in parts] + [pad], axis=1)


def kernel(x, c, ctx, c_ctx, norm_w, w_ada, b_ada, w_in, na_q_norm, na_k_norm, na_rpb, rw_mu, rw_w0, rw_w2, rw_a0, rw_a2, rw_k_k, rw_k_a, rw_r_k, rw_ln_w, rw_ln_b, ssm_conv_w, ssm_conv_b, ssm_dt_bias, ssm_A_log, ssm_D, ssm_norm_w, da_q_norm, da_k_norm, da_lq1, da_lk1, da_lq2, da_lk2, da_subln, w_gate, w_up, w_out):
    bsz, seq, d = x.shape
    n_ctx = ctx.shape[1]
    lt = n_ctx + seq
    depth = w_in.shape[0]
    assert n_ctx == CTX_LEN == PREP_TM == DA_TK and seq % (GRID_W * NA_WIN_R) == 0
    assert ((lt // DA_TK) - 1) % DA_NSUB == 0 and n_ctx % NORM_ROWS == 0
    cond = jnp.concatenate([_silu(c), _silu(c_ctx)[None], jnp.zeros((SUBLANES - bsz - 1, d), F32)], axis=0)
    xs2 = jnp.concatenate([ctx, x], axis=1).reshape(bsz * lt, d)
    cos, sin = rope_tables(lt, n_ctx)
    for l in range(depth):
        lam_init = 0.8 - 0.6 * math.exp(-0.3 * l)
        mod = small_matmul(cond, w_ada[l], b_ada[l])
        shift, scale, gate = jnp.split(mod[:bsz], 3, axis=-1)
        shift_c, scale_c, gate_c = jnp.split(mod[bsz], 3, axis=-1)
        p2, h = in_projection(xs2, norm_w[l], scale, shift, scale_c, shift_c, _permute_w_in(w_in[l]), lt, n_ctx)
        p3 = p2.reshape(bsz, lt, IN_W_PAD)

        kn, va, qt, kh, vt = prep_attention(p3, cos, sin, na_k_norm[l], da_q_norm[l], da_k_norm[l])
        oa = neighbourhood_attention(p3, kn, va, na_bias_table(na_rpb[l]), na_q_norm[l], n_ctx)
        lam = jnp.exp(jnp.sum(da_lq1[l] * da_lk1[l])) - jnp.exp(jnp.sum(da_lq2[l] * da_lk2[l])) + lam_init
        od = flash_diff_attention(qt, kh, vt, p3, lam, da_subln[l], lam_init)

        r, vb, a, lw, kd, b, bonus = prep_rwkv(p3, rw_mu[l], rw_w0[l], rw_w2[l], rw_a0[l], rw_a2[l],
                                              rw_k_k[l], rw_k_a[l], rw_r_k[l].reshape(-1), n_ctx)
        ob = finish_rwkv(rwkv_scan(r, vb, a, lw, kd, b, n_ctx), bonus, p3, rw_ln_w[l], rw_ln_b[l])

        xq, bc, bt, sa, sat, dskip = prep_ssd(p3, ssm_conv_w[l], ssm_conv_b[l], ssm_dt_bias[l], ssm_A_log[l],
                                              ssm_D[l], n_ctx)
        om = finish_ssd(ssd_scan(xq, bc, bt, sa, sat, n_ctx), dskip, p3, ssm_norm_w[l])

        ys = [t.reshape(bsz * lt, BRANCH_W) for t in (oa, ob, om, od)]
        acc = gate_up(h, ys, w_gate[l].astype(BF16), w_up[l].astype(BF16), lt)
        xs2 = out_projection(acc, w_out[l].astype(BF16), xs2, gate, gate_c, lt, n_ctx)
    return xs2.reshape(bsz, lt, d)[:, n_ctx:]
```

```python
import functools
import math

import numpy as np
import jax
import jax.numpy as jnp
from jax import lax
from jax.experimental import pallas as pl
from jax.experimental.pallas import tpu as pltpu

F32 = jnp.float32
BF16 = jnp.bfloat16

D_MODEL = 2048
GRID_W = 64
CTX_LEN = 256
N_BRANCH = 4
BRANCH_W = D_MODEL // N_BRANCH
NORM_EPS = 1e-6
NEG_INF = -1e30

NA_DIM = 64
NA_HEADS = BRANCH_W // NA_DIM
NA_WIN_R = 8
NA_WIN_C = 16
NA_NBLK = 2

RW_DIM = 64
RW_HEADS = BRANCH_W // RW_DIM
RW_LORA_W = 64
RW_LORA_A = 64
RW_GN_EPS = 64e-5
RW_CHUNK = 64
RW_NSUB = 4

SSM_HEAD_DIM = 64
SSM_HEADS = BRANCH_W // SSM_HEAD_DIM
SSM_GROUPS = 2
SSM_STATE = 128
SSM_CHUNK = 128
SSM_NSUB = 2
SSM_CONV_CH = BRANCH_W + 2 * SSM_GROUPS * SSM_STATE

DA_DIM = 64
DA_HEADS = BRANCH_W // (2 * DA_DIM)
ROPE_BASE = 10000.0
DA_TQ = 256
DA_TK = 256
DA_NSUB = 32
DA_ONES = 16

SUBLANES = 8
LANES = 128

CB_AQ, CB_AK, CB_AV, CB_AG = 0, 1, 2, 3
CB_BR, CB_BK, CB_BV, CB_BG = 4, 5, 6, 7
CB_CX, CB_CBC, CB_CZ = 8, 9, 10
CB_DQ, CB_DK, CB_DV, CB_DG = 11, 12, 13, 14
CB_MISC = 15
MISC_DT = 2 * RW_LORA_W + 2 * RW_LORA_A
N_COL_BLOCKS = 16
IN_W_PAD = N_COL_BLOCKS * BRANCH_W
PREP_TM = 256
NORM_ROWS = 32

V7X_VMEM_BYTES = 64 * 1024 * 1024
VMEM_LIMIT = V7X_VMEM_BYTES * 7 // 8


def _cparams(n_axes):
    return pltpu.CompilerParams(dimension_semantics=("arbitrary",) * n_axes, vmem_limit_bytes=VMEM_LIMIT)


def _dot(a, b):
    return jnp.dot(a, b, preferred_element_type=F32)


def _dot_nt(a, b):
    return lax.dot_general(a, b, (((1,), (1,)), ((), ())), preferred_element_type=F32)


def _dot_tn(a, b):
    return lax.dot_general(a, b, (((0,), (0,)), ((), ())), preferred_element_type=F32)


def _split3(x):
    hi = x.astype(BF16)
    r1 = x - hi.astype(F32)
    mid = r1.astype(BF16)
    lo = (r1 - mid.astype(F32)).astype(BF16)
    return hi, mid, lo


def _dot_exact_lhs(m_bf16, x):
    hi, mid, lo = _split3(x)
    return _dot(m_bf16, hi) + _dot(m_bf16, mid) + _dot(m_bf16, lo)


def _dot_exact_rhs(x, m_bf16):
    hi, mid, lo = _split3(x)
    return _dot(hi, m_bf16) + _dot(mid, m_bf16) + _dot(lo, m_bf16)


def _dot_nt_exact_lhs_f32(x, m_bf16):
    hi, mid, lo = _split3(x)
    return _dot_nt(hi, m_bf16) + _dot_nt(mid, m_bf16) + _dot_nt(lo, m_bf16)


def _sigmoid(x):
    return 1.0 / (1.0 + jnp.exp(-x))


def _silu(x):
    return x * _sigmoid(x)


def _softplus(x):
    return jnp.maximum(x, 0.0) + jnp.log(1.0 + jnp.exp(-jnp.abs(x)))


def _group_matrix(width, group):
    g = np.arange(width) // group
    return jnp.asarray((g[:, None] == g[None, :]).astype(np.float32), BF16)


def _head_rms(x, gmat, group, w):
    ms = _dot_exact_rhs(x * x, gmat) * (1.0 / group)
    return x * lax.rsqrt(ms + NORM_EPS) * w


def _tile_rows(n_rows):
    return 768 if n_rows % 768 == 0 else 256


def _row_spec(cb, tm=PREP_TM):
    return pl.BlockSpec((1, tm, BRANCH_W), lambda b, i: (b, i, cb))


def _halo_specs(cb, tm, lt):
    per = tm // SUBLANES
    last = lt // SUBLANES - 1
    prev = pl.BlockSpec((1, SUBLANES, BRANCH_W), lambda b, i: (b, jnp.maximum(i * per - 1, 0), cb))
    nxt = pl.BlockSpec((1, SUBLANES, BRANCH_W), lambda b, i: (b, jnp.minimum((i + 1) * per, last), cb))
    return prev, nxt


def _const_spec(shape):
    return pl.BlockSpec(shape, lambda *_: (0,) * len(shape))


def _neighbours(x, prev_blk, next_blk, i, n_tiles, n_ctx_tiles):
    tm = x.shape[0]
    row = lax.broadcasted_iota(jnp.int32, (tm, 1), 0)
    seg_start = jnp.logical_or(i == 0, i == n_ctx_tiles)
    seg_end = jnp.logical_or(i == n_ctx_tiles - 1, i == n_tiles - 1)
    prev_row = jnp.where(seg_start, 0.0, prev_blk[SUBLANES - 1:SUBLANES, :])
    next_row = jnp.where(seg_end, 0.0, next_blk[0:1, :])
    x_prev = jnp.where(row == 0, prev_row, pltpu.roll(x, 1, 0))
    x_next = jnp.where(row == tm - 1, next_row, pltpu.roll(x, tm - 1, 0))
    return x_prev, x_next


def _small_mm_kernel(a_ref, w_ref, b_ref, o_ref):
    o_ref[...] = _dot(a_ref[...].astype(BF16), w_ref[...].astype(BF16)) + b_ref[...]


def small_matmul(a, w, b, tn=512):
    m, k = a.shape
    n = w.shape[1]
    return pl.pallas_call(
        _small_mm_kernel,
        grid=(n // tn,),
        in_specs=[pl.BlockSpec((m, k), lambda j: (0, 0)),
                  pl.BlockSpec((k, tn), lambda j: (0, j)),
                  pl.BlockSpec((1, tn), lambda j: (0, j))],
        out_specs=pl.BlockSpec((m, tn), lambda j: (0, j)),
        out_shape=jax.ShapeDtypeStruct((m, n), F32),
        compiler_params=_cparams(1),
        name="adaln_mm",
    )(a, w, b.reshape(1, n))


def _inproj_kernel(x_ref, nw_ref, sc_ref, sh_ref, scc_ref, shc_ref, w_ref, p_ref, h_ref, hs_ref, *,
                   tiles_per_batch, n_ctx):
    @pl.when(pl.program_id(1) == 0)
    def _():
        tm = x_ref.shape[0]
        rc = NORM_ROWS
        base = (pl.program_id(0) % tiles_per_batch) * tm

        def norm_rows(k, carry):
            r0 = pl.multiple_of(k * rc, rc)
            xf = x_ref[pl.ds(r0, rc), :]
            ms = jnp.mean(xf * xf, axis=-1, keepdims=True)
            y = xf * lax.rsqrt(ms + NORM_EPS) * nw_ref[...]
            is_ctx = base + r0 < n_ctx
            sc = 1.0 + jnp.where(is_ctx, scc_ref[...], sc_ref[0])
            sh = jnp.where(is_ctx, shc_ref[...], sh_ref[0])
            h = (y * sc + sh).astype(BF16)
            hs_ref[pl.ds(r0, rc), :] = h
            h_ref[pl.ds(r0, rc), :] = h
            return carry

        lax.fori_loop(0, tm // rc, norm_rows, 0, unroll=4)

    p_ref[...] = _dot(hs_ref[...], w_ref[...])


def in_projection(xs2, norm_w, scale, shift, scale_c, shift_c, w_bf16, lt, n_ctx, tn=1024):
    m, d = xs2.shape
    n = w_bf16.shape[1]
    tm = _tile_rows(lt)
    tpb = lt // tm
    nb = scale.shape[0]
    kern = functools.partial(_inproj_kernel, tiles_per_batch=tpb, n_ctx=n_ctx)
    return pl.pallas_call(
        kern,
        grid=(m // tm, n // tn),
        in_specs=[pl.BlockSpec((tm, d), lambda i, j: (i, 0)),
                  pl.BlockSpec((1, d), lambda i, j: (0, 0)),
                  pl.BlockSpec((1, 1, d), lambda i, j: (i // tpb, 0, 0)),
                  pl.BlockSpec((1, 1, d), lambda i, j: (i // tpb, 0, 0)),
                  pl.BlockSpec((1, d), lambda i, j: (0, 0)),
                  pl.BlockSpec((1, d), lambda i, j: (0, 0)),
                  pl.BlockSpec((d, tn), lambda i, j: (0, j))],
        out_specs=[pl.BlockSpec((tm, tn), lambda i, j: (i, j)),
                   pl.BlockSpec((tm, d), lambda i, j: (i, 0))],
        out_shape=[jax.ShapeDtypeStruct((m, n), F32),
                   jax.ShapeDtypeStruct((m, d), BF16)],
        scratch_shapes=[pltpu.VMEM((tm, d), BF16)],
        compiler_params=_cparams(2),
        name="in_proj",
    )(xs2, norm_w.reshape(1, d), scale.reshape(nb, 1, d), shift.reshape(nb, 1, d),
      scale_c.reshape(1, d), shift_c.reshape(1, d), w_bf16)


def _gate_up_kernel(h_ref, ya_ref, yb_ref, yc_ref, yd_ref, wg_ref, wu_ref, o_ref):
    h = h_ref[...]
    acc = None
    for i, y_ref in enumerate((ya_ref, yb_ref, yc_ref, yd_ref)):
        g = _dot(h, wg_ref[i])
        u = _dot(y_ref[...], wu_ref[i])
        t = _sigmoid(g) * u
        acc = t if acc is None else acc + t
    o_ref[...] = acc.astype(BF16)


def gate_up(h, ys, wg_bf16, wu_bf16, lt, tn=512):
    m, d = h.shape
    w = ys[0].shape[1]
    tm = _tile_rows(lt)
    y_spec = pl.BlockSpec((tm, w), lambda i, j: (i, 0))
    return pl.pallas_call(
        _gate_up_kernel,
        grid=(m // tm, d // tn),
        in_specs=[pl.BlockSpec((tm, d), lambda i, j: (i, 0)), y_spec, y_spec, y_spec, y_spec,
                  pl.BlockSpec((N_BRANCH, d, tn), lambda i, j: (0, 0, j)),
                  pl.BlockSpec((N_BRANCH, w, tn), lambda i, j: (0, 0, j))],
        out_specs=pl.BlockSpec((tm, tn), lambda i, j: (i, j)),
        out_shape=jax.ShapeDtypeStruct((m, d), BF16),
        compiler_params=_cparams(2),
        name="gate_up",
    )(h, *ys, wg_bf16, wu_bf16)


def _out_proj_kernel(a_ref, w_ref, x_ref, g_ref, gc_ref, o_ref, *, tiles_per_batch, n_ctx):
    tm = x_ref.shape[0]
    row = (pl.program_id(0) % tiles_per_batch) * tm + lax.broadcasted_iota(jnp.int32, (tm, 1), 0)
    gate = jnp.where(row < n_ctx, gc_ref[...], g_ref[0])
    o_ref[...] = x_ref[...] + gate * _dot(a_ref[...], w_ref[...])


def out_projection(acc, w_bf16, xs2, gate, gate_c, lt, n_ctx, tn=512):
    m, d = xs2.shape
    tm = _tile_rows(lt)
    tpb = lt // tm
    nb = gate.shape[0]
    kern = functools.partial(_out_proj_kernel, tiles_per_batch=tpb, n_ctx=n_ctx)
    return pl.pallas_call(
        kern,
        grid=(m // tm, d // tn),
        in_specs=[pl.BlockSpec((tm, d), lambda i, j: (i, 0)),
                  pl.BlockSpec((d, tn), lambda i, j: (0, j)),
                  pl.BlockSpec((tm, tn), lambda i, j: (i, j)),
                  pl.BlockSpec((1, 1, tn), lambda i, j: (i // tpb, 0, j)),
                  pl.BlockSpec((1, tn), lambda i, j: (0, j))],
        out_specs=pl.BlockSpec((tm, tn), lambda i, j: (i, j)),
        out_shape=jax.ShapeDtypeStruct((m, d), F32),
        compiler_params=_cparams(2),
        name="out_proj",
    )(acc, w_bf16, xs2, gate.reshape(nb, 1, d), gate_c.reshape(1, d))


def _rope(x, cos, sin):
    w = x.shape[1]
    lane = lax.broadcasted_iota(jnp.int32, (1, w), 1)
    first = ((lane // (DA_DIM // 4)) % 2) == 0
    rot = jnp.where(first, -pltpu.roll(x, w - DA_DIM // 4, 1), pltpu.roll(x, DA_DIM // 4, 1))
    return x * cos + rot * sin


def _prep_attn_kernel(ak_ref, av_ref, dq_ref, dk_ref, dv_ref, cos_ref, sin_ref, g64_ref,
                      wak_ref, wdq_ref, wdk_ref, kn_ref, va_ref, qt_ref, kh_ref, vt_ref):
    g64 = g64_ref[...]
    cos = cos_ref[...]
    sin = sin_ref[...]
    kn_ref[0] = _head_rms(ak_ref[0], g64, NA_DIM, wak_ref[...]).astype(BF16)
    va_ref[0] = av_ref[0].astype(BF16)
    q = _rope(_head_rms(dq_ref[0], g64, DA_DIM, wdq_ref[...]), cos, sin)
    k = _rope(_head_rms(dk_ref[0], g64, DA_DIM, wdk_ref[...]), cos, sin).astype(BF16)
    qt_ref[0] = q.T.astype(BF16)
    for hc in range(2 * DA_HEADS):
        kh_ref[0, hc] = k[:, hc * DA_DIM:(hc + 1) * DA_DIM]
    v = dv_ref[0]
    dv = 2 * DA_DIM
    for h in range(DA_HEADS):
        vt_ref[0, h, 0, 0:dv, :] = v[:, h * dv:(h + 1) * dv].T.astype(BF16)
        vt_ref[0, h, 0, dv:dv + DA_ONES, :] = jnp.ones((DA_ONES, v.shape[0]), BF16)


def prep_attention(p3, cos, sin, na_k_w, da_q_w, da_k_w):
    bsz, lt, _ = p3.shape
    tm = PREP_TM
    w = BRANCH_W
    tile = lambda v, s=1.0: (jnp.tile(v, w // v.shape[0]) * s).reshape(1, w)
    tab_spec = pl.BlockSpec((tm, w), lambda b, i: (i, 0))
    vec = _const_spec((1, w))
    return pl.pallas_call(
        _prep_attn_kernel,
        grid=(bsz, lt // tm),
        in_specs=[_row_spec(CB_AK), _row_spec(CB_AV), _row_spec(CB_DQ), _row_spec(CB_DK), _row_spec(CB_DV),
                  tab_spec, tab_spec, _const_spec((w, w)), vec, vec, vec],
        out_specs=[pl.BlockSpec((1, tm, w), lambda b, i: (b, i, 0)),
                   pl.BlockSpec((1, tm, w), lambda b, i: (b, i, 0)),
                   pl.BlockSpec((1, w, tm), lambda b, i: (b, 0, i)),
                   pl.BlockSpec((1, 2 * DA_HEADS, tm, DA_DIM), lambda b, i: (b, 0, i, 0)),
                   pl.BlockSpec((1, DA_HEADS, 1, 2 * DA_DIM + DA_ONES, tm), lambda b, i: (b, 0, i, 0, 0))],
        out_shape=[jax.ShapeDtypeStruct((bsz, lt, w), BF16),
                   jax.ShapeDtypeStruct((bsz, lt, w), BF16),
                   jax.ShapeDtypeStruct((bsz, w, lt), BF16),
                   jax.ShapeDtypeStruct((bsz, 2 * DA_HEADS, lt, DA_DIM), BF16),
                   jax.ShapeDtypeStruct((bsz, DA_HEADS, lt // tm, 2 * DA_DIM + DA_ONES, tm), BF16)],
        compiler_params=_cparams(2),
        name="prep_attn",
    )(p3, p3, p3, p3, p3, cos, sin, _group_matrix(w, DA_DIM),
      tile(na_k_w), tile(da_q_w, DA_DIM ** -0.5 * math.log2(math.e)), tile(da_k_w))


def rope_tables(lt, n_ctx):
    nf = DA_DIM // 4
    t = jnp.arange(lt - n_ctx, dtype=jnp.int32)
    rows, cols = t // GRID_W, t % GRID_W
    inv = ROPE_BASE ** (-jnp.arange(nf, dtype=F32) / nf)
    ang_r = rows.astype(F32)[:, None] * inv
    ang_c = cols.astype(F32)[:, None] * inv
    ang = jnp.concatenate([ang_r, ang_r, ang_c, ang_c], axis=-1)
    ang = jnp.concatenate([jnp.zeros((n_ctx, DA_DIM), F32), ang], axis=0)
    reps = BRANCH_W // DA_DIM
    return jnp.tile(jnp.cos(ang), (1, reps)), jnp.tile(jnp.sin(ang), (1, reps))


def _flash_kernel(lam_ref, qt_ref, k_ref, vt_ref, g_ref, sw_ref, o_ref, *, n_latent_iters, out_scale):
    tq = qt_ref.shape[2]
    dve = vt_ref.shape[3]
    dv = dve - DA_ONES
    tk = vt_ref.shape[4]
    qts = [qt_ref[0, c * DA_DIM:(c + 1) * DA_DIM, :] for c in range(2)]
    gate = _silu(g_ref[0])

    def attend(carry, first_chunk, n_chunks):
        chunks = [first_chunk + g for g in range(n_chunks)]
        sts = [[_dot(k_ref[0, c, pl.ds(pl.multiple_of(ch * tk, tk), tk), :], qts[c]) for c in range(2)]
               for ch in chunks]
        carry = list(carry)
        for g, ch in enumerate(chunks):
            vt = vt_ref[0, 0, ch]
            for c in range(2):
                m, acc = carry[c]
                st = sts[g][c].astype(BF16)
                m_new = jnp.maximum(m, jnp.max(st, axis=0, keepdims=True).astype(F32))
                pt = jnp.exp2(st - m_new.astype(BF16))
                carry[c] = (m_new, jnp.exp2(m - m_new) * acc + _dot(vt, pt))
        return tuple(carry)

    init = tuple((jnp.full((1, tq), NEG_INF, F32), jnp.zeros((dve, tq), F32)) for _ in range(2))
    carry = attend(init, 0, 1)
    n_iters = jnp.where(pl.program_id(2) == 0, 0, n_latent_iters)
    res = lax.fori_loop(0, n_iters, lambda i, cr: attend(cr, 1 + i * DA_NSUB, DA_NSUB), carry)
    outs = [acc[0:dv] * (1.0 / acc[dv:dv + 1]) for (_, acc) in res]
    ot = outs[0] - lam_ref[0] * outs[1]
    ot = ot * lax.rsqrt(jnp.mean(ot * ot, axis=0, keepdims=True) + NORM_EPS)
    o = ot.T * (sw_ref[...] * out_scale)
    o_ref[0] = (o * gate).astype(BF16)


def flash_diff_attention(qt, kh, vt, p3, lam, subln_w, lam_init):
    bsz, w, lt = qt.shape
    dv = 2 * DA_DIM
    nk = lt // DA_TK
    n_latent_iters = (nk - 1) // DA_NSUB
    kern = functools.partial(_flash_kernel, n_latent_iters=n_latent_iters, out_scale=1.0 - lam_init)
    g_blocks = BRANCH_W // dv
    return pl.pallas_call(
        kern,
        grid=(bsz, DA_HEADS, lt // DA_TQ),
        in_specs=[pl.BlockSpec(memory_space=pltpu.SMEM),
                  pl.BlockSpec((1, dv, DA_TQ), lambda b, h, qi: (b, h, qi)),
                  pl.BlockSpec((1, 2, lt, DA_DIM), lambda b, h, qi: (b, h, 0, 0)),
                  pl.BlockSpec((1, 1, nk, dv + DA_ONES, DA_TK), lambda b, h, qi: (b, h, 0, 0, 0)),
                  pl.BlockSpec((1, DA_TQ, dv), lambda b, h, qi: (b, qi, CB_DG * g_blocks + h)),
                  _const_spec((1, dv))],
        out_specs=pl.BlockSpec((1, DA_TQ, dv), lambda b, h, qi: (b, qi, h)),
        out_shape=jax.ShapeDtypeStruct((bsz, lt, BRANCH_W), BF16),
        compiler_params=_cparams(3),
        name="flash_attn",
    )(lam.reshape(1).astype(F32), qt, kh, vt, p3, subln_w.reshape(1, dv))


def _na_kernel(q_ref, g_ref, k_ref, v_ref, *rest, n_rows, n_ctx):
    bias_refs = rest[:NA_NBLK]
    g64_ref, wq_ref, o_ref = rest[NA_NBLK:]
    n_loc = NA_WIN_R * GRID_W
    q = _head_rms(q_ref[0], g64_ref[...], NA_DIM, wq_ref[...]).astype(BF16)
    g = g_ref[0]
    pair = [slice(hp * LANES, (hp + 1) * LANES) for hp in range(NA_HEADS // 2)]
    low_half = lax.broadcasted_iota(jnp.int32, (1, LANES), 1) < NA_DIM
    rows, starts, units = [], [], []
    for j in range(NA_NBLK):
        blk = pl.program_id(1) * NA_NBLK + j
        r = jnp.maximum(blk - n_ctx // GRID_W, 0)
        r0 = jnp.clip(r - NA_WIN_R // 2, 0, n_rows - NA_WIN_R)
        starts.append(pl.multiple_of(n_ctx + r0 * GRID_W, GRID_W))
        rows.append(slice(j * GRID_W, (j + 1) * GRID_W))
        units += [(j, hp) for hp in range(NA_HEADS // 2)]
    zero = jnp.zeros((), BF16)
    cat = lambda x, y: jnp.concatenate([x, y], axis=0)
    q2 = [cat(jnp.where(low_half, q[rows[j], pair[hp]], zero), jnp.where(low_half, zero, q[rows[j], pair[hp]]))
          for j, hp in units]
    s_loc = [_dot_nt(q2[u], k_ref[0, pl.ds(starts[j], n_loc), pair[hp]])
             + cat(bias_refs[j][0, 2 * hp], bias_refs[j][0, 2 * hp + 1]) for u, (j, hp) in enumerate(units)]
    s_ctx = [_dot_nt(q2[u], k_ref[0, 0:n_ctx, pair[hp]]) for u, (j, hp) in enumerate(units)]
    us = range(len(units))
    m = [jnp.maximum(jnp.max(s_loc[u], axis=-1, keepdims=True), jnp.max(s_ctx[u], axis=-1, keepdims=True))
         for u in us]
    p_loc = [jnp.exp(s_loc[u] - m[u]) for u in us]
    p_ctx = [jnp.exp(s_ctx[u] - m[u]) for u in us]
    l = [jnp.sum(p_loc[u], axis=-1, keepdims=True) + jnp.sum(p_ctx[u], axis=-1, keepdims=True) for u in us]
    o = [(_dot(p_loc[u].astype(BF16), v_ref[0, pl.ds(starts[j], n_loc), pair[hp]])
          + _dot(p_ctx[u].astype(BF16), v_ref[0, 0:n_ctx, pair[hp]])) / l[u]
         for u, (j, hp) in enumerate(units)]
    for u, (j, hp) in enumerate(units):
        o_pair = jnp.where(low_half, o[u][0:GRID_W], o[u][GRID_W:2 * GRID_W])
        o_ref[0, rows[j], pair[hp]] = (o_pair * _silu(g[rows[j], pair[hp]])).astype(BF16)


def neighbourhood_attention(p3, kn, va, bias_tbl, na_q_w, n_ctx):
    bsz, lt, w = kn.shape
    n_rows = (lt - n_ctx) // GRID_W
    ncb = n_ctx // GRID_W
    half = NA_WIN_R // 2

    def bias_spec(j):
        def idx(b, s):
            blk = s * NA_NBLK + j
            r = blk - ncb
            off = r - jnp.clip(r - half, 0, n_rows - NA_WIN_R)
            return (jnp.where(blk < ncb, NA_WIN_R, off), 0, 0, 0)
        return pl.BlockSpec((1, NA_HEADS, GRID_W, NA_WIN_R * GRID_W), idx)

    tq = NA_NBLK * GRID_W
    full_spec = pl.BlockSpec((1, lt, w), lambda b, s: (b, 0, 0))
    wq = (jnp.tile(na_q_w, w // NA_DIM) * NA_DIM ** -0.5).reshape(1, w)
    return pl.pallas_call(
        functools.partial(_na_kernel, n_rows=n_rows, n_ctx=n_ctx),
        grid=(bsz, lt // tq),
        in_specs=[_row_spec(CB_AQ, tq), _row_spec(CB_AG, tq), full_spec, full_spec,
                  *[bias_spec(j) for j in range(NA_NBLK)],
                  _const_spec((w, w)), _const_spec((1, w))],
        out_specs=pl.BlockSpec((1, tq, w), lambda b, s: (b, s, 0)),
        out_shape=jax.ShapeDtypeStruct((bsz, lt, w), BF16),
        compiler_params=_cparams(2),
        name="nbr_attn",
    )(p3, p3, kn, va, *([bias_tbl] * NA_NBLK), _group_matrix(w, NA_DIM), wq)


def na_bias_table(rpb):
    col = np.arange(GRID_W)
    c0 = np.clip(col - NA_WIN_C // 2, 0, GRID_W - NA_WIN_C)
    col_ok = (col[None, :] >= c0[:, None]) & (col[None, :] < c0[:, None] + NA_WIN_C)
    d_col = np.clip(col[None, :] - col[:, None] + (NA_WIN_C - 1), 0, 2 * NA_WIN_C - 2)
    onehot = (d_col[:, :, None] == np.arange(2 * NA_WIN_C - 1)).astype(np.float32)
    by_col = jnp.einsum('hrc,qwc->hrqw', rpb.astype(F32), jnp.asarray(onehot),
                        precision=lax.Precision.HIGHEST)
    by_col = jnp.where(jnp.asarray(col_ok)[None, None], by_col, NEG_INF)
    tbl = jnp.stack([by_col[:, NA_WIN_R - 1 - o:2 * NA_WIN_R - 1 - o] for o in range(NA_WIN_R)]
                    + [jnp.full((NA_HEADS, NA_WIN_R, GRID_W, GRID_W), NEG_INF, F32)])
    tbl = jnp.transpose(tbl, (0, 1, 3, 2, 4))
    return tbl.reshape(NA_WIN_R + 1, NA_HEADS, GRID_W, NA_WIN_R * GRID_W)


def _prep_rwkv_kernel(r_ref, rp_ref, rn_ref, k_ref, kp_ref, kn_ref, v_ref, vp_ref, vn_ref,
                      m_ref, mp_ref, mn_ref, mu_ref, w0_ref, a0_ref, w2_ref, a2_ref, kk_ref, ka_ref,
                      rk_ref, g64_ref,
                      ro_ref, vo_ref, ao_ref, lw_ref, kd_ref, bo_ref, bonus_ref, *, n_tiles, n_ctx_tiles):
    i = pl.program_id(1)
    w = BRANCH_W

    def shifted(x_ref, p_ref, n_ref, mu):
        x = x_ref[0]
        xp, xn = _neighbours(x, p_ref[0], n_ref[0], i, n_tiles, n_ctx_tiles)
        return x + (0.5 * (xp + xn) - x) * mu

    r = shifted(r_ref, rp_ref, rn_ref, mu_ref[0:1, :])
    k = shifted(k_ref, kp_ref, kn_ref, mu_ref[1:2, :])
    v = shifted(v_ref, vp_ref, vn_ref, mu_ref[2:3, :])
    misc = shifted(m_ref, mp_ref, mn_ref, mu_ref[3:4, :])
    wd = jnp.tanh(misc[:, 0:2 * RW_LORA_W]).astype(BF16)
    ad = misc[:, 2 * RW_LORA_W:MISC_DT].astype(BF16)
    w_log = w0_ref[...] + _dot(wd, w2_ref[...])
    gate = _sigmoid(a0_ref[...] + _dot(ad, a2_ref[...]))
    log_decay = -math.exp(-0.5) * _sigmoid(w_log)
    g64 = g64_ref[...]
    kk = k * kk_ref[...]
    kk = kk / jnp.maximum(jnp.sqrt(_dot_exact_rhs(kk * kk, g64)), 1e-12)
    ro_ref[0] = r
    vo_ref[0] = v.astype(BF16)
    ao_ref[0] = -kk
    coef = None
    for d in range(2):
        a_d = gate[:, d * w:(d + 1) * w]
        kd = k * (1.0 + (a_d - 1.0) * ka_ref[...])
        lw_ref[d, 0] = log_decay[:, d * w:(d + 1) * w]
        kd_ref[d, 0] = kd
        bo_ref[d, 0] = kk * a_d
        coef = kd if coef is None else coef + kd
    bonus_ref[0] = _dot_exact_rhs(r * coef * rk_ref[...], g64) * v


def prep_rwkv(p3, mu, w0, w2, a0, a2, k_k, k_a, r_k, n_ctx):
    bsz, lt, _ = p3.shape
    tm = PREP_TM
    w = BRANCH_W
    n_tiles = lt // tm
    mu4 = jnp.stack([mu[0:w], mu[w:2 * w], mu[2 * w:3 * w],
                     jnp.pad(mu[3 * w:], (0, w - (mu.shape[0] - 3 * w)))])
    zero = jnp.zeros((RW_LORA_W, w), F32)
    w2cat = jnp.concatenate([jnp.concatenate([w2[0], zero], axis=1),
                             jnp.concatenate([zero, w2[1]], axis=1)], axis=0).astype(BF16)
    a2cat = jnp.concatenate([jnp.concatenate([a2[0], zero], axis=1),
                             jnp.concatenate([zero, a2[1]], axis=1)], axis=0).astype(BF16)
    specs = []
    for cb in (CB_BR, CB_BK, CB_BV, CB_MISC):
        specs += [_row_spec(cb), *_halo_specs(cb, tm, lt)]
    vec = _const_spec((1, w))
    tok = pl.BlockSpec((1, tm, w), lambda b, i: (b, i, 0))
    tok2 = pl.BlockSpec((2, 1, tm, w), lambda b, i: (0, b, i, 0))
    kern = functools.partial(_prep_rwkv_kernel, n_tiles=n_tiles, n_ctx_tiles=n_ctx // tm)
    return pl.pallas_call(
        kern,
        grid=(bsz, n_tiles),
        in_specs=specs + [_const_spec((4, w)), _const_spec((1, 2 * w)), _const_spec((1, 2 * w)),
                          _const_spec((2 * RW_LORA_W, 2 * w)), _const_spec((2 * RW_LORA_A, 2 * w)),
                          vec, vec, vec, _const_spec((w, w))],
        out_specs=[tok, tok, tok, tok2, tok2, tok2, tok],
        out_shape=[jax.ShapeDtypeStruct((bsz, lt, w), F32),
                   jax.ShapeDtypeStruct((bsz, lt, w), BF16),
                   jax.ShapeDtypeStruct((bsz, lt, w), F32),
                   jax.ShapeDtypeStruct((2, bsz, lt, w), F32),
                   jax.ShapeDtypeStruct((2, bsz, lt, w), F32),
                   jax.ShapeDtypeStruct((2, bsz, lt, w), F32),
                   jax.ShapeDtypeStruct((bsz, lt, w), F32)],
        compiler_params=_cparams(2),
        name="prep_rwkv",
    )(*([p3] * 12), mu4, w0.reshape(1, 2 * w), a0.reshape(1, 2 * w), w2cat, a2cat,
      k_k.reshape(1, w), k_a.reshape(1, w), r_k.reshape(1, w), _group_matrix(w, RW_DIM))


def _rwkv_kernel(tri_ref, mq_ref, r_ref, v_ref, a_ref, lw_ref, kd_ref, b_ref, y_ref, s_ref):
    c = RW_CHUNK
    nsub = r_ref.shape[1] // c
    d = pl.program_id(0)

    @pl.when(pl.program_id(2) == 0)
    def _():
        s_ref[...] = jnp.zeros_like(s_ref)

    tri = tri_ref[0]
    m_quad = mq_ref[0] > 0.5
    rows = lax.broadcasted_iota(jnp.int32, (c, c), 0)
    cols = lax.broadcasted_iota(jnp.int32, (c, c), 1)
    eye = (rows == cols).astype(F32)
    same_half = (rows >= c // 2) == (cols >= c // 2)
    heads = [slice(h * RW_DIM, (h + 1) * RW_DIM) for h in range(RW_HEADS)]

    row_sl, e_tot = [], []
    al, rh, rf, be, ka, bc, kc, vb = [], [], [], [], [], [], [], []
    for j in range(nsub):
        off = pl.multiple_of(jnp.where(d == 0, j, nsub - 1 - j) * c, c)
        rs = pl.ds(off, c)
        row_sl.append(rs)
        lw = lw_ref[0, 0, rs, :]
        cum = _dot_exact_lhs(tri, lw)
        tot = jnp.sum(lw, axis=0, keepdims=True)
        e_m = jnp.exp(-cum)
        e_t = jnp.exp(tot - cum)
        e_tot.append(jnp.exp(tot))
        b_in = b_ref[0, 0, rs, :]
        k_in = kd_ref[0, 0, rs, :]
        alpha = (a_ref[0, rs, :] * jnp.exp(cum - lw)).astype(BF16)
        rho_f = r_ref[0, rs, :] * jnp.exp(cum)
        rho = rho_f.astype(BF16)
        beta = (b_in * e_m).astype(BF16)
        kappa = (k_in * e_m).astype(BF16)
        beta_c = (b_in * e_t).astype(BF16)
        kappa_c = (k_in * e_t).astype(BF16)
        v_b = v_ref[0, rs, :]
        for sl in heads:
            al.append(alpha[:, sl])
            rh.append(rho[:, sl])
            rf.append(rho_f[:, sl])
            be.append(beta[:, sl])
            ka.append(kappa[:, sl])
            bc.append(beta_c[:, sl])
            kc.append(kappa_c[:, sl])
            vb.append(v_b[:, sl])

    units = range(nsub * RW_HEADS)
    cat = lambda x, y: jnp.concatenate([x, y], axis=0)
    zeros = jnp.zeros((c, RW_DIM), BF16)
    prod = [jnp.where(m_quad, _dot_nt(cat(al[u], rh[u]), cat(be[u], ka[u])), 0.0) for u in units]
    l_ab = [prod[u][0:c, 0:c] for u in units]
    top = [prod[u][0:c].astype(BF16) for u in units]
    bot = [prod[u][c:2 * c].astype(BF16) for u in units]
    l_d = [jnp.where(same_half, l_ab[u], 0.0) for u in units]
    l_o = [(l_ab[u] - l_d[u]).astype(BF16) for u in units]
    pw = [l_d[u].astype(BF16) for u in units]
    td = [eye + l_d[u] for u in units]
    for _ in range(int(math.log2(c)) - 2):
        pw = [_dot(pw[u], pw[u]).astype(BF16) for u in units]
        td = [td[u] + _dot(td[u].astype(BF16), pw[u]) for u in units]
    td_b = [td[u].astype(BF16) for u in units]
    x_o = [_dot(td_b[u], l_o[u]).astype(BF16) for u in units]
    tinv = [(td[u] + _dot(x_o[u], td_b[u])).astype(BF16) for u in units]
    akv = [_dot(top[u], cat(zeros, vb[u])).astype(BF16) for u in units]
    a_hat = [_dot(tinv[u], al[u]).astype(BF16) for u in units]
    v_hat = [_dot(tinv[u], akv[u]).astype(BF16) for u in units]
    r_hat = [(rf[u] + _dot(bot[u], cat(a_hat[u], zeros))).astype(BF16) for u in units]
    vv = [cat(v_hat[u], vb[u]) for u in units]
    y_hat = [_dot(bot[u], vv[u]) for u in units]
    q_mat = [_dot_tn(a_hat[u], bc[u]).astype(BF16) for u in units]
    n_mat = [_dot_tn(vv[u], cat(bc[u], kc[u])) for u in units]

    state = [s_ref[h] for h in range(RW_HEADS)]
    for j in range(nsub):
        for h, sl in enumerate(heads):
            u = j * RW_HEADS + h
            s_b = state[h].astype(BF16)
            y_ref[0, 0, row_sl[j], sl] = _dot_nt(r_hat[u], s_b) + y_hat[u]
            state[h] = state[h] * e_tot[j][:, sl] + _dot(s_b, q_mat[u]) + n_mat[u]
    for h in range(RW_HEADS):
        s_ref[h] = state[h]


def _scan_chunk_index(d, s, n_ctx_chunks, n_chunks):
    rev = jnp.where(s < n_ctx_chunks, n_ctx_chunks - 1 - s, n_chunks + n_ctx_chunks - 1 - s)
    return jnp.where(d == 0, s, rev)


def _direction_masks(c):
    i = np.arange(c)
    lower = (i[None, :] <= i[:, None]).astype(np.float32)
    tri = np.stack([lower, lower.T])
    strict = np.stack([lower - np.eye(c, dtype=np.float32), lower.T - np.eye(c, dtype=np.float32)])
    return tri, strict


def rwkv_scan(r, v, a, lw, kd, b, n_ctx):
    bsz, lt, w = r.shape
    c = RW_CHUNK
    blk = RW_NSUB * c
    nck = lt // blk
    ncc = n_ctx // blk
    tri, strict = _direction_masks(c)
    shared = pl.BlockSpec((1, blk, w), lambda d, bi, s: (bi, _scan_chunk_index(d, s, ncc, nck), 0))
    perdir = pl.BlockSpec((1, 1, blk, w), lambda d, bi, s: (d, bi, _scan_chunk_index(d, s, ncc, nck), 0))
    mask_spec = pl.BlockSpec((1, c, c), lambda d, bi, s: (d, 0, 0))
    quad_spec = pl.BlockSpec((1, 2 * c, 2 * c), lambda d, bi, s: (d, 0, 0))
    quad = np.concatenate([np.tile(strict, (1, 1, 2)), np.tile(tri, (1, 1, 2))], axis=1)
    return pl.pallas_call(
        _rwkv_kernel,
        grid=(2, bsz, nck),
        in_specs=[mask_spec, quad_spec, shared, shared, shared, perdir, perdir, perdir],
        out_specs=perdir,
        out_shape=jax.ShapeDtypeStruct((2, bsz, lt, w), F32),
        scratch_shapes=[pltpu.VMEM((RW_HEADS, RW_DIM, RW_DIM), F32)],
        compiler_params=_cparams(3),
        name="rwkv_scan",
    )(jnp.asarray(tri, BF16), jnp.asarray(quad, F32), r, v, a, lw, kd, b)


def _finish_rwkv_kernel(y_ref, bonus_ref, g_ref, lnw_ref, lnb_ref, g64_ref, o_ref):
    y = y_ref[0, 0] + y_ref[1, 0]
    g64 = g64_ref[...]
    mean = _dot_exact_rhs(y, g64) * (1.0 / RW_DIM)
    yc = y - mean
    var = _dot_exact_rhs(yc * yc, g64) * (1.0 / RW_DIM)
    yn = yc * lax.rsqrt(var + RW_GN_EPS) * lnw_ref[...] + lnb_ref[...]
    o_ref[0] = ((yn + bonus_ref[0]) * _silu(g_ref[0])).astype(BF16)


def finish_rwkv(y, bonus, p3, ln_w, ln_b):
    _, bsz, lt, w = y.shape
    tm = _tile_rows(lt)
    tok = pl.BlockSpec((1, tm, w), lambda b, i: (b, i, 0))
    vec = _const_spec((1, w))
    return pl.pallas_call(
        _finish_rwkv_kernel,
        grid=(bsz, lt // tm),
        in_specs=[pl.BlockSpec((2, 1, tm, w), lambda b, i: (0, b, i, 0)), tok, _row_spec(CB_BG, tm),
                  vec, vec, _const_spec((w, w))],
        out_specs=tok,
        out_shape=jax.ShapeDtypeStruct((bsz, lt, w), BF16),
        compiler_params=_cparams(2),
        name="finish_rwkv",
    )(y, bonus, p3, ln_w.reshape(1, w), ln_b.reshape(1, w), _group_matrix(w, RW_DIM))


def _prep_ssd_kernel(x_ref, xp_ref, xn_ref, bc_ref, bcp_ref, bcn_ref, m_ref, cw_ref, cb_ref, dtb_ref,
                     aneg_ref, exp_ref, dsk_ref,
                     xq_ref, bco_ref, bt_ref, a_ref, at_ref, dskip_ref, *, n_tiles, n_ctx_tiles):
    i = pl.program_id(1)
    w = BRANCH_W

    def conv(x_ref, p_ref, n_ref, half):
        x = x_ref[0]
        xp, xn = _neighbours(x, p_ref[0], n_ref[0], i, n_tiles, n_ctx_tiles)
        lo = half * w
        y = (xp * cw_ref[0:1, lo:lo + w] + x * cw_ref[1:2, lo:lo + w] + xn * cw_ref[2:3, lo:lo + w]
             + cb_ref[:, lo:lo + w])
        return _silu(y)

    xs = conv(x_ref, xp_ref, xn_ref, 0)
    bc = conv(bc_ref, bcp_ref, bcn_ref, 1)
    bco_ref[0] = bc.astype(BF16)
    bm_f = bc[:, 0:SSM_GROUPS * SSM_STATE]
    q = SSM_CHUNK
    for j in range(bc.shape[0] // q):
        bt_ref[0, j] = bm_f[j * q:(j + 1) * q, :].T.astype(BF16)
    dt = _softplus(m_ref[0][:, MISC_DT:MISC_DT + LANES] + dtb_ref[...])
    lane = lax.broadcasted_iota(jnp.int32, (1, LANES), 1)
    dt = jnp.where(lane < 2 * SSM_HEADS, dt, 0.0)
    dtx = _dot_exact_rhs(dt, exp_ref[...])
    a_all = dt * aneg_ref[...]
    first = lane < SSM_HEADS
    a_dirs = [jnp.where(first, a_all, 0.0), jnp.where(first, pltpu.roll(a_all, LANES - SSM_HEADS, 1), 0.0)]
    for d in range(2):
        xq_ref[d, 0] = xs * dtx[:, d * w:(d + 1) * w]
        a_ref[d, 0] = a_dirs[d]
        for j in range(bc.shape[0] // q):
            at_ref[d, 0, j] = a_dirs[d][j * q:(j + 1) * q, :].T[0:2 * SUBLANES, :]
    dskip_ref[0] = xs * dsk_ref[...]


def prep_ssd(p3, conv_w, conv_b, dt_bias, a_log, d_skip, n_ctx):
    bsz, lt, _ = p3.shape
    tm = PREP_TM
    w = BRANCH_W
    n_tiles = lt // tm
    nh2 = 2 * SSM_HEADS
    pad_lanes = lambda v: jnp.pad(v.reshape(1, nh2), ((0, 0), (0, LANES - nh2)))
    expand = np.zeros((LANES, 2 * w), np.float32)
    for d in range(2):
        for h in range(SSM_HEADS):
            expand[d * SSM_HEADS + h, d * w + h * SSM_HEAD_DIM:d * w + (h + 1) * SSM_HEAD_DIM] = 1.0
    specs = [_row_spec(CB_CX), *_halo_specs(CB_CX, tm, lt), _row_spec(CB_CBC), *_halo_specs(CB_CBC, tm, lt),
             _row_spec(CB_MISC)]
    tok = pl.BlockSpec((1, tm, w), lambda b, i: (b, i, 0))
    gs = SSM_GROUPS * SSM_STATE
    kern = functools.partial(_prep_ssd_kernel, n_tiles=n_tiles, n_ctx_tiles=n_ctx // tm)
    return pl.pallas_call(
        kern,
        grid=(bsz, n_tiles),
        in_specs=specs + [_const_spec((3, 2 * w)), _const_spec((1, 2 * w)), _const_spec((1, LANES)),
                          _const_spec((1, LANES)), _const_spec((LANES, 2 * w)), _const_spec((1, w))],
        out_specs=[pl.BlockSpec((2, 1, tm, w), lambda b, i: (0, b, i, 0)), tok,
                   pl.BlockSpec((1, tm // SSM_CHUNK, gs, SSM_CHUNK), lambda b, i: (b, i, 0, 0)),
                   pl.BlockSpec((2, 1, tm, LANES), lambda b, i: (0, b, i, 0)),
                   pl.BlockSpec((2, 1, tm // SSM_CHUNK, 2 * SUBLANES, SSM_CHUNK), lambda b, i: (0, b, i, 0, 0)), tok],
        out_shape=[jax.ShapeDtypeStruct((2, bsz, lt, w), F32),
                   jax.ShapeDtypeStruct((bsz, lt, w), BF16),
                   jax.ShapeDtypeStruct((bsz, lt // SSM_CHUNK, gs, SSM_CHUNK), BF16),
                   jax.ShapeDtypeStruct((2, bsz, lt, LANES), F32),
                   jax.ShapeDtypeStruct((2, bsz, lt // SSM_CHUNK, 2 * SUBLANES, SSM_CHUNK), F32),
                   jax.ShapeDtypeStruct((bsz, lt, w), F32)],
        compiler_params=_cparams(2),
        name="prep_ssd",
    )(*([p3] * 7), conv_w, conv_b.reshape(1, 2 * w), pad_lanes(dt_bias), pad_lanes(-jnp.exp(a_log)),
      jnp.asarray(expand, BF16), jnp.repeat(d_skip, SSM_HEAD_DIM).reshape(1, w))


def _ssd_kernel(tri_ref, mi_ref, xq_ref, bc_ref, bt_ref, a_ref, at_ref, y_ref, s_ref):
    q = SSM_CHUNK
    nsub = bt_ref.shape[1]
    rep = SSM_HEADS // SSM_GROUPS
    gw = SSM_GROUPS * SSM_STATE
    p = SSM_HEAD_DIM
    d = pl.program_id(0)

    @pl.when(pl.program_id(2) == 0)
    def _():
        s_ref[...] = jnp.zeros_like(s_ref)

    tri = tri_ref[0]
    mask = mi_ref[0] > 0.5
    hs = range(SSM_HEADS)
    groups = [slice(g * SSM_STATE, (g + 1) * SSM_STATE) for g in range(SSM_GROUPS)]

    rows, cms, y_in, e_col, e_tot, upd = [], [], [], [], [], []
    for j in range(nsub):
        ch = jnp.where(d == 0, j, nsub - 1 - j)
        rs = pl.ds(pl.multiple_of(ch * q, q), q)
        rows.append(rs)
        a = a_ref[0, 0, rs, :]
        a_t = at_ref[0, 0, ch]
        xq = xq_ref[0, 0, rs, :]
        bm = bc_ref[0, rs, 0:gw]
        cm = bc_ref[0, rs, gw:2 * gw]
        bt = bt_ref[0, ch]
        acol = _dot_exact_lhs(tri, a)
        arow = _dot_nt_exact_lhs_f32(a_t, tri)
        tot = jnp.sum(a, axis=0, keepdims=True)
        cm_g = [cm[:, gs] for gs in groups]
        cb = [_dot_nt(cm_g[g], bm[:, gs]) for g, gs in enumerate(groups)]
        ac = [jnp.broadcast_to(acol[:, h:h + 1], (q, q)) for h in hs]
        ar = [jnp.broadcast_to(arow[h:h + 1, :], (q, q)) for h in hs]
        gmat = [(cb[h // rep] * jnp.exp(jnp.where(mask, ac[h] - ar[h], NEG_INF))).astype(BF16) for h in hs]
        xh = [xq[:, h * p:(h + 1) * p] for h in hs]
        tot_h = [jnp.broadcast_to(tot[:, h:h + 1], (1, p)) for h in hs]
        xd = [(xh[h] * jnp.exp(tot_h[h] - ac[h][:, :p])).astype(BF16) for h in hs]
        cms.append(cm_g)
        y_in.append([_dot(gmat[h], xh[h].astype(BF16)) for h in hs])
        e_col.append([jnp.exp(ac[h][:, :p]) for h in hs])
        e_tot.append([jnp.exp(tot_h[h]) for h in hs])
        upd.append([_dot(bt[groups[h // rep], :], xd[h]) for h in hs])

    state = [s_ref[h] for h in hs]
    for j in range(nsub):
        y_st = [_dot(cms[j][h // rep], state[h].astype(BF16)) for h in hs]
        for h in hs:
            y_ref[0, 0, rows[j], h * p:(h + 1) * p] = y_in[j][h] + y_st[h] * e_col[j][h]
            state[h] = state[h] * e_tot[j][h] + upd[j][h]
    for h in hs:
        s_ref[h] = state[h]


def ssd_scan(xq, bc, bt, a, a_t, n_ctx):
    _, bsz, lt, w = xq.shape
    q = SSM_CHUNK
    blk = SSM_NSUB * q
    nck = lt // blk
    ncc = n_ctx // blk
    gw = bt.shape[2]
    tri, _ = _direction_masks(q)

    def cidx(d, s):
        return _scan_chunk_index(d, s, ncc, nck)

    mask_spec = pl.BlockSpec((1, q, q), lambda d, bi, s: (d, 0, 0))
    return pl.pallas_call(
        _ssd_kernel,
        grid=(2, bsz, nck),
        in_specs=[mask_spec, mask_spec,
                  pl.BlockSpec((1, 1, blk, w), lambda d, bi, s: (d, bi, cidx(d, s), 0)),
                  pl.BlockSpec((1, blk, w), lambda d, bi, s: (bi, cidx(d, s), 0)),
                  pl.BlockSpec((1, SSM_NSUB, gw, q), lambda d, bi, s: (bi, cidx(d, s), 0, 0)),
                  pl.BlockSpec((1, 1, blk, LANES), lambda d, bi, s: (d, bi, cidx(d, s), 0)),
                  pl.BlockSpec((1, 1, SSM_NSUB, 2 * SUBLANES, q), lambda d, bi, s: (d, bi, cidx(d, s), 0, 0))],
        out_specs=pl.BlockSpec((1, 1, blk, w), lambda d, bi, s: (d, bi, cidx(d, s), 0)),
        out_shape=jax.ShapeDtypeStruct((2, bsz, lt, w), F32),
        scratch_shapes=[pltpu.VMEM((SSM_HEADS, SSM_STATE, SSM_HEAD_DIM), F32)],
        compiler_params=_cparams(3),
        name="ssd_scan",
    )(jnp.asarray(tri, BF16), jnp.asarray(tri, F32), xq, bc, bt, a, a_t)


def _finish_ssd_kernel(y_ref, dskip_ref, z_ref, nw_ref, g256_ref, o_ref):
    y = y_ref[0, 0] + y_ref[1, 0] + dskip_ref[0]
    g = y * _silu(z_ref[0])
    group = BRANCH_W // SSM_GROUPS
    o_ref[0] = _head_rms(g, g256_ref[...], group, nw_ref[...]).astype(BF16)


def finish_ssd(y, dskip, p3, norm_w):
    _, bsz, lt, w = y.shape
    tm = _tile_rows(lt)
    tok = pl.BlockSpec((1, tm, w), lambda b, i: (b, i, 0))
    return pl.pallas_call(
        _finish_ssd_kernel,
        grid=(bsz, lt // tm),
        in_specs=[pl.BlockSpec((2, 1, tm, w), lambda b, i: (0, b, i, 0)), tok, _row_spec(CB_CZ, tm),
                  _const_spec((1, w)), _const_spec((w, w))],
        out_specs=tok,
        out_shape=jax.ShapeDtypeStruct((bsz, lt, w), BF16),
        compiler_params=_cparams(2),
        name="finish_ssd",
    )(y, dskip, p3, norm_w.reshape(1, w), _group_matrix(w, w // SSM_GROUPS))


def _permute_w_in(w_in_l):
    w = BRANCH_W
    b0 = 4 * w
    lora0 = b0 + 3 * w
    bg0 = lora0 + MISC_DT
    c0 = bg0 + w
    dt0 = c0 + SSM_CONV_CH
    z0 = dt0 + 2 * SSM_HEADS
    d0 = z0 + w
    end = d0 + 4 * w
    parts = [w_in_l[:, 0:lora0], w_in_l[:, bg0:c0], w_in_l[:, c0:dt0], w_in_l[:, z0:d0], w_in_l[:, d0:end],
             w_in_l[:, lora0:bg0], w_in_l[:, dt0:z0]]
    used = sum(p.shape[1] for p in parts)
    pad = jnp.zeros((w_in_l.shape[0], IN_W_PAD - used), BF16)
    return jnp.concatenate([p.astype(BF16) for p in parts] + [pad], axis=1)


def kernel(x, c, ctx, c_ctx, norm_w, w_ada, b_ada, w_in, na_q_norm, na_k_norm, na_rpb, rw_mu, rw_w0, rw_w2, rw_a0, rw_a2, rw_k_k, rw_k_a, rw_r_k, rw_ln_w, rw_ln_b, ssm_conv_w, ssm_conv_b, ssm_dt_bias, ssm_A_log, ssm_D, ssm_norm_w, da_q_norm, da_k_norm, da_lq1, da_lk1, da_lq2, da_lk2, da_subln, w_gate, w_up, w_out):
    bsz, seq, d = x.shape
    n_ctx = ctx.shape[1]
    lt = n_ctx + seq
    depth = w_in.shape[0]
    assert n_ctx == CTX_LEN == PREP_TM == DA_TK and seq % (GRID_W * NA_WIN_R) == 0
    assert ((lt // DA_TK) - 1) % DA_NSUB == 0 and n_ctx % NORM_ROWS == 0
    cond = jnp.concatenate([_silu(c), _silu(c_ctx)[None], jnp.zeros((SUBLANES - bsz - 1, d), F32)], axis=0)
    xs2 = jnp.concatenate([ctx, x], axis=1).reshape(bsz * lt, d)
    cos, sin = rope_tables(lt, n_ctx)
    for l in range(depth):
        lam_init = 0.8 - 0.6 * math.exp(-0.3 * l)
        mod = small_matmul(cond, w_ada[l], b_ada[l])
        shift, scale, gate = jnp.split(mod[:bsz], 3, axis=-1)
        shift_c, scale_c, gate_c = jnp.split(mod[bsz], 3, axis=-1)
        p2, h = in_projection(xs2, norm_w[l], scale, shift, scale_c, shift_c, _permute_w_in(w_in[l]), lt, n_ctx)
        p3 = p2.reshape(bsz, lt, IN_W_PAD)

        kn, va, qt, kh, vt = prep_attention(p3, cos, sin, na_k_norm[l], da_q_norm[l], da_k_norm[l])
        oa = neighbourhood_attention(p3, kn, va, na_bias_table(na_rpb[l]), na_q_norm[l], n_ctx)
        lam = jnp.exp(jnp.sum(da_lq1[l] * da_lk1[l])) - jnp.exp(jnp.sum(da_lq2[l] * da_lk2[l])) + lam_init
        od = flash_diff_attention(qt, kh, vt, p3, lam, da_subln[l], lam_init)

        r, vb, a, lw, kd, b, bonus = prep_rwkv(p3, rw_mu[l], rw_w0[l], rw_w2[l], rw_a0[l], rw_a2[l],
                                              rw_k_k[l], rw_k_a[l], rw_r_k[l].reshape(-1), n_ctx)
        ob = finish_rwkv(rwkv_scan(r, vb, a, lw, kd, b, n_ctx), bonus, p3, rw_ln_w[l], rw_ln_b[l])

        xq, bc, bt, sa, sat, dskip = prep_ssd(p3, ssm_conv_w[l], ssm_conv_b[l], ssm_dt_bias[l], ssm_A_log[l],
                                              ssm_D[l], n_ctx)
        om = finish_ssd(ssd_scan(xq, bc, bt, sa, sat, n_ctx), dskip, p3, ssm_norm_w[l])

        ys = [t.reshape(bsz * lt, BRANCH_W) for t in (oa, ob, om, od)]
        acc = gate_up(h, ys, w_gate[l].astype(BF16), w_up[l].astype(BF16), lt)
        xs2 = out_projection(acc, w_out[l].astype(BF16), xs2, gate, gate_c, lt, n_ctx)
    return xs2.reshape(bsz, lt, d)[:, n_ctx:]
```

```python
import functools
import math

import numpy as np
import jax
import jax.numpy as jnp
from jax import lax
from jax.experimental import pallas as pl
from jax.experimental.pallas import tpu as pltpu

F32 = jnp.float32
BF16 = jnp.bfloat16

D_MODEL = 2048
GRID_W = 64
CTX_LEN = 256
N_BRANCH = 4
BRANCH_W = D_MODEL // N_BRANCH
NORM_EPS = 1e-6
NEG_INF = -1e30

NA_DIM = 64
NA_HEADS = BRANCH_W // NA_DIM
NA_WIN_R = 8
NA_WIN_C = 16
NA_NBLK = 2

RW_DIM = 64
RW_HEADS = BRANCH_W // RW_DIM
RW_LORA_W = 64
RW_LORA_A = 64
RW_GN_EPS = 64e-5
RW_CHUNK = 64
RW_NSUB = 4

SSM_HEAD_DIM = 64
SSM_HEADS = BRANCH_W // SSM_HEAD_DIM
SSM_GROUPS = 2
SSM_STATE = 128
SSM_CHUNK = 128
SSM_NSUB = 2
SSM_CONV_CH = BRANCH_W + 2 * SSM_GROUPS * SSM_STATE

DA_DIM = 64
DA_HEADS = BRANCH_W // (2 * DA_DIM)
ROPE_BASE = 10000.0
DA_TQ = 256
DA_TK = 256
DA_NSUB = 32
DA_ONES = 16

SUBLANES = 8
LANES = 128

CB_AQ, CB_AK, CB_AV, CB_AG = 0, 1, 2, 3
CB_BR, CB_BK, CB_BV, CB_BG = 4, 5, 6, 7
CB_CX, CB_CBC, CB_CZ = 8, 9, 10
CB_DQ, CB_DK, CB_DV, CB_DG = 11, 12, 13, 14
CB_MISC = 15
MISC_DT = 2 * RW_LORA_W + 2 * RW_LORA_A
N_COL_BLOCKS = 16
IN_W_PAD = N_COL_BLOCKS * BRANCH_W
PREP_TM = 256
NORM_ROWS = 32

V7X_VMEM_BYTES = 64 * 1024 * 1024
VMEM_LIMIT = V7X_VMEM_BYTES * 7 // 8


def _cparams(n_axes):
    return pltpu.CompilerParams(dimension_semantics=("arbitrary",) * n_axes, vmem_limit_bytes=VMEM_LIMIT)


def _dot(a, b):
    return jnp.dot(a, b, preferred_element_type=F32)


def _dot_nt(a, b):
    return lax.dot_general(a, b, (((1,), (1,)), ((), ())), preferred_element_type=F32)


def _dot_tn(a, b):
    return lax.dot_general(a, b, (((0,), (0,)), ((), ())), preferred_element_type=F32)


def _split3(x):
    hi = x.astype(BF16)
    r1 = x - hi.astype(F32)
    mid = r1.astype(BF16)
    lo = (r1 - mid.astype(F32)).astype(BF16)
    return hi, mid, lo


def _dot_exact_lhs(m_bf16, x):
    hi, mid, lo = _split3(x)
    return _dot(m_bf16, hi) + _dot(m_bf16, mid) + _dot(m_bf16, lo)


def _dot_exact_rhs(x, m_bf16):
    hi, mid, lo = _split3(x)
    return _dot(hi, m_bf16) + _dot(mid, m_bf16) + _dot(lo, m_bf16)


def _group_sum(x, gmat_bf16):
    hi = x.astype(BF16)
    lo = (x - hi.astype(F32)).astype(BF16)
    return _dot(hi, gmat_bf16) + _dot(lo, gmat_bf16)


def _dot_nt_exact_lhs_f32(x, m_bf16):
    hi, mid, lo = _split3(x)
    return _dot_nt(hi, m_bf16) + _dot_nt(mid, m_bf16) + _dot_nt(lo, m_bf16)


def _sigmoid(x):
    return 1.0 / (1.0 + jnp.exp(-x))


def _silu(x):
    return x * _sigmoid(x)


def _softplus(x):
    return jnp.maximum(x, 0.0) + jnp.log(1.0 + jnp.exp(-jnp.abs(x)))


def _group_matrix(width, group):
    g = np.arange(width) // group
    return jnp.asarray((g[:, None] == g[None, :]).astype(np.float32), BF16)


def _head_rms(x, gmat, group, w):
    ms = _group_sum(x * x, gmat) * (1.0 / group)
    return x * lax.rsqrt(ms + NORM_EPS) * w


def _tile_rows(n_rows):
    return 768 if n_rows % 768 == 0 else 256


def _row_spec(cb, tm=PREP_TM):
    return pl.BlockSpec((1, tm, BRANCH_W), lambda b, i: (b, i, cb))


def _halo_specs(cb, tm, lt):
    per = tm // SUBLANES
    last = lt // SUBLANES - 1
    prev = pl.BlockSpec((1, SUBLANES, BRANCH_W), lambda b, i: (b, jnp.maximum(i * per - 1, 0), cb))
    nxt = pl.BlockSpec((1, SUBLANES, BRANCH_W), lambda b, i: (b, jnp.minimum((i + 1) * per, last), cb))
    return prev, nxt


def _const_spec(shape):
    return pl.BlockSpec(shape, lambda *_: (0,) * len(shape))


def _neighbours(x, prev_blk, next_blk, i, n_tiles, n_ctx_tiles):
    tm = x.shape[0]
    row = lax.broadcasted_iota(jnp.int32, (tm, 1), 0)
    seg_start = jnp.logical_or(i == 0, i == n_ctx_tiles)
    seg_end = jnp.logical_or(i == n_ctx_tiles - 1, i == n_tiles - 1)
    prev_row = jnp.where(seg_start, 0.0, prev_blk[SUBLANES - 1:SUBLANES, :])
    next_row = jnp.where(seg_end, 0.0, next_blk[0:1, :])
    x_prev = jnp.where(row == 0, prev_row, pltpu.roll(x, 1, 0))
    x_next = jnp.where(row == tm - 1, next_row, pltpu.roll(x, tm - 1, 0))
    return x_prev, x_next


def _small_mm_kernel(a_ref, w_ref, b_ref, o_ref):
    o_ref[...] = _dot(a_ref[...].astype(BF16), w_ref[...].astype(BF16)) + b_ref[...]


def small_matmul(a, w, b, tn=512):
    m, k = a.shape
    n = w.shape[1]
    return pl.pallas_call(
        _small_mm_kernel,
        grid=(n // tn,),
        in_specs=[pl.BlockSpec((m, k), lambda j: (0, 0)),
                  pl.BlockSpec((k, tn), lambda j: (0, j)),
                  pl.BlockSpec((1, tn), lambda j: (0, j))],
        out_specs=pl.BlockSpec((m, tn), lambda j: (0, j)),
        out_shape=jax.ShapeDtypeStruct((m, n), F32),
        compiler_params=_cparams(1),
        name="adaln_mm",
    )(a, w, b.reshape(1, n))


def _inproj_kernel(x_ref, nw_ref, sc_ref, sh_ref, scc_ref, shc_ref, w_ref, p_ref, h_ref, hs_ref, *,
                   tiles_per_batch, n_ctx):
    @pl.when(pl.program_id(1) == 0)
    def _():
        tm = x_ref.shape[0]
        rc = NORM_ROWS
        base = (pl.program_id(0) % tiles_per_batch) * tm

        def norm_rows(k, carry):
            r0 = pl.multiple_of(k * rc, rc)
            xf = x_ref[pl.ds(r0, rc), :]
            ms = jnp.mean(xf * xf, axis=-1, keepdims=True)
            y = xf * lax.rsqrt(ms + NORM_EPS) * nw_ref[...]
            is_ctx = base + r0 < n_ctx
            sc = 1.0 + jnp.where(is_ctx, scc_ref[...], sc_ref[0])
            sh = jnp.where(is_ctx, shc_ref[...], sh_ref[0])
            h = (y * sc + sh).astype(BF16)
            hs_ref[pl.ds(r0, rc), :] = h
            h_ref[pl.ds(r0, rc), :] = h
            return carry

        lax.fori_loop(0, tm // rc, norm_rows, 0, unroll=4)

    p_ref[...] = _dot(hs_ref[...], w_ref[...])


def in_projection(xs2, norm_w, scale, shift, scale_c, shift_c, w_bf16, lt, n_ctx, tn=1024):
    m, d = xs2.shape
    n = w_bf16.shape[1]
    tm = _tile_rows(lt)
    tpb = lt // tm
    nb = scale.shape[0]
    kern = functools.partial(_inproj_kernel, tiles_per_batch=tpb, n_ctx=n_ctx)
    return pl.pallas_call(
        kern,
        grid=(m // tm, n // tn),
        in_specs=[pl.BlockSpec((tm, d), lambda i, j: (i, 0)),
                  pl.BlockSpec((1, d), lambda i, j: (0, 0)),
                  pl.BlockSpec((1, 1, d), lambda i, j: (i // tpb, 0, 0)),
                  pl.BlockSpec((1, 1, d), lambda i, j: (i // tpb, 0, 0)),
                  pl.BlockSpec((1, d), lambda i, j: (0, 0)),
                  pl.BlockSpec((1, d), lambda i, j: (0, 0)),
                  pl.BlockSpec((d, tn), lambda i, j: (0, j))],
        out_specs=[pl.BlockSpec((tm, tn), lambda i, j: (i, j)),
                   pl.BlockSpec((tm, d), lambda i, j: (i, 0))],
        out_shape=[jax.ShapeDtypeStruct((m, n), F32),
                   jax.ShapeDtypeStruct((m, d), BF16)],
        scratch_shapes=[pltpu.VMEM((tm, d), BF16)],
        compiler_params=_cparams(2),
        name="in_proj",
    )(xs2, norm_w.reshape(1, d), scale.reshape(nb, 1, d), shift.reshape(nb, 1, d),
      scale_c.reshape(1, d), shift_c.reshape(1, d), w_bf16)


def _gate_up_kernel(h_ref, ya_ref, yb_ref, yc_ref, yd_ref, wg_ref, wu_ref, o_ref):
    h = h_ref[...]
    acc = None
    for i, y_ref in enumerate((ya_ref, yb_ref, yc_ref, yd_ref)):
        g = _dot(h, wg_ref[i])
        u = _dot(y_ref[...], wu_ref[i])
        t = _sigmoid(g) * u
        acc = t if acc is None else acc + t
    o_ref[...] = acc.astype(BF16)


def gate_up(h, ys, wg_bf16, wu_bf16, lt, tn=512):
    m, d = h.shape
    w = ys[0].shape[1]
    tm = _tile_rows(lt)
    y_spec = pl.BlockSpec((tm, w), lambda i, j: (i, 0))
    return pl.pallas_call(
        _gate_up_kernel,
        grid=(m // tm, d // tn),
        in_specs=[pl.BlockSpec((tm, d), lambda i, j: (i, 0)), y_spec, y_spec, y_spec, y_spec,
                  pl.BlockSpec((N_BRANCH, d, tn), lambda i, j: (0, 0, j)),
                  pl.BlockSpec((N_BRANCH, w, tn), lambda i, j: (0, 0, j))],
        out_specs=pl.BlockSpec((tm, tn), lambda i, j: (i, j)),
        out_shape=jax.ShapeDtypeStruct((m, d), BF16),
        compiler_params=_cparams(2),
        name="gate_up",
    )(h, *ys, wg_bf16, wu_bf16)


def _out_proj_kernel(a_ref, w_ref, x_ref, g_ref, gc_ref, o_ref, *, tiles_per_batch, n_ctx):
    tm = x_ref.shape[0]
    row = (pl.program_id(0) % tiles_per_batch) * tm + lax.broadcasted_iota(jnp.int32, (tm, 1), 0)
    gate = jnp.where(row < n_ctx, gc_ref[...], g_ref[0])
    o_ref[...] = x_ref[...] + gate * _dot(a_ref[...], w_ref[...])


def out_projection(acc, w_bf16, xs2, gate, gate_c, lt, n_ctx, tn=512):
    m, d = xs2.shape
    tm = _tile_rows(lt)
    tpb = lt // tm
    nb = gate.shape[0]
    kern = functools.partial(_out_proj_kernel, tiles_per_batch=tpb, n_ctx=n_ctx)
    return pl.pallas_call(
        kern,
        grid=(m // tm, d // tn),
        in_specs=[pl.BlockSpec((tm, d), lambda i, j: (i, 0)),
                  pl.BlockSpec((d, tn), lambda i, j: (0, j)),
                  pl.BlockSpec((tm, tn), lambda i, j: (i, j)),
                  pl.BlockSpec((1, 1, tn), lambda i, j: (i // tpb, 0, j)),
                  pl.BlockSpec((1, tn), lambda i, j: (0, j))],
        out_specs=pl.BlockSpec((tm, tn), lambda i, j: (i, j)),
        out_shape=jax.ShapeDtypeStruct((m, d), F32),
        compiler_params=_cparams(2),
        name="out_proj",
    )(acc, w_bf16, xs2, gate.reshape(nb, 1, d), gate_c.reshape(1, d))


def _rope(x, cos, sin):
    w = x.shape[1]
    lane = lax.broadcasted_iota(jnp.int32, (1, w), 1)
    first = ((lane // (DA_DIM // 4)) % 2) == 0
    rot = jnp.where(first, -pltpu.roll(x, w - DA_DIM // 4, 1), pltpu.roll(x, DA_DIM // 4, 1))
    return x * cos + rot * sin


def _prep_attn_kernel(ak_ref, av_ref, dq_ref, dk_ref, dv_ref, cos_ref, sin_ref, g64_ref,
                      wak_ref, wdq_ref, wdk_ref, kn_ref, va_ref, qt_ref, kh_ref, vt_ref):
    g64 = g64_ref[...]
    cos = cos_ref[...]
    sin = sin_ref[...]
    kn_ref[0] = _head_rms(ak_ref[0], g64, NA_DIM, wak_ref[...]).astype(BF16)
    va_ref[0] = av_ref[0].astype(BF16)
    q = _rope(_head_rms(dq_ref[0], g64, DA_DIM, wdq_ref[...]), cos, sin)
    k = _rope(_head_rms(dk_ref[0], g64, DA_DIM, wdk_ref[...]), cos, sin).astype(BF16)
    qt_ref[0] = q.T.astype(BF16)
    for hc in range(2 * DA_HEADS):
        kh_ref[0, hc] = k[:, hc * DA_DIM:(hc + 1) * DA_DIM]
    v = dv_ref[0]
    dv = 2 * DA_DIM
    for h in range(DA_HEADS):
        vt_ref[0, h, 0, 0:dv, :] = v[:, h * dv:(h + 1) * dv].T.astype(BF16)
        vt_ref[0, h, 0, dv:dv + DA_ONES, :] = jnp.ones((DA_ONES, v.shape[0]), BF16)


def prep_attention(p3, cos, sin, na_k_w, da_q_w, da_k_w):
    bsz, lt, _ = p3.shape
    tm = PREP_TM
    w = BRANCH_W
    tile = lambda v, s=1.0: (jnp.tile(v, w // v.shape[0]) * s).reshape(1, w)
    tab_spec = pl.BlockSpec((tm, w), lambda b, i: (i, 0))
    vec = _const_spec((1, w))
    return pl.pallas_call(
        _prep_attn_kernel,
        grid=(bsz, lt // tm),
        in_specs=[_row_spec(CB_AK), _row_spec(CB_AV), _row_spec(CB_DQ), _row_spec(CB_DK), _row_spec(CB_DV),
                  tab_spec, tab_spec, _const_spec((w, w)), vec, vec, vec],
        out_specs=[pl.BlockSpec((1, tm, w), lambda b, i: (b, i, 0)),
                   pl.BlockSpec((1, tm, w), lambda b, i: (b, i, 0)),
                   pl.BlockSpec((1, w, tm), lambda b, i: (b, 0, i)),
                   pl.BlockSpec((1, 2 * DA_HEADS, tm, DA_DIM), lambda b, i: (b, 0, i, 0)),
                   pl.BlockSpec((1, DA_HEADS, 1, 2 * DA_DIM + DA_ONES, tm), lambda b, i: (b, 0, i, 0, 0))],
        out_shape=[jax.ShapeDtypeStruct((bsz, lt, w), BF16),
                   jax.ShapeDtypeStruct((bsz, lt, w), BF16),
                   jax.ShapeDtypeStruct((bsz, w, lt), BF16),
                   jax.ShapeDtypeStruct((bsz, 2 * DA_HEADS, lt, DA_DIM), BF16),
                   jax.ShapeDtypeStruct((bsz, DA_HEADS, lt // tm, 2 * DA_DIM + DA_ONES, tm), BF16)],
        compiler_params=_cparams(2),
        name="prep_attn",
    )(p3, p3, p3, p3, p3, cos, sin, _group_matrix(w, DA_DIM),
      tile(na_k_w), tile(da_q_w, DA_DIM ** -0.5 * math.log2(math.e)), tile(da_k_w))


def rope_tables(lt, n_ctx):
    nf = DA_DIM // 4
    t = jnp.arange(lt - n_ctx, dtype=jnp.int32)
    rows, cols = t // GRID_W, t % GRID_W
    inv = ROPE_BASE ** (-jnp.arange(nf, dtype=F32) / nf)
    ang_r = rows.astype(F32)[:, None] * inv
    ang_c = cols.astype(F32)[:, None] * inv
    ang = jnp.concatenate([ang_r, ang_r, ang_c, ang_c], axis=-1)
    ang = jnp.concatenate([jnp.zeros((n_ctx, DA_DIM), F32), ang], axis=0)
    reps = BRANCH_W // DA_DIM
    return jnp.tile(jnp.cos(ang), (1, reps)), jnp.tile(jnp.sin(ang), (1, reps))


def _flash_kernel(lam_ref, qt_ref, k_ref, vt_ref, g_ref, sw_ref, o_ref, *, n_latent_iters, out_scale):
    tq = qt_ref.shape[2]
    dve = vt_ref.shape[3]
    dv = dve - DA_ONES
    tk = vt_ref.shape[4]
    qts = [qt_ref[0, c * DA_DIM:(c + 1) * DA_DIM, :] for c in range(2)]
    gate = _silu(g_ref[0])

    def attend(carry, first_chunk, n_chunks):
        chunks = [first_chunk + g for g in range(n_chunks)]
        sts = [[_dot(k_ref[0, c, pl.ds(pl.multiple_of(ch * tk, tk), tk), :], qts[c]) for c in range(2)]
               for ch in chunks]
        carry = list(carry)
        for g, ch in enumerate(chunks):
            vt = vt_ref[0, 0, ch]
            for c in range(2):
                m, acc = carry[c]
                st = sts[g][c].astype(BF16)
                m_new = jnp.maximum(m, jnp.max(st, axis=0, keepdims=True).astype(F32))
                pt = jnp.exp2(st - m_new.astype(BF16))
                carry[c] = (m_new, jnp.exp2(m - m_new) * acc + _dot(vt, pt))
        return tuple(carry)

    init = tuple((jnp.full((1, tq), NEG_INF, F32), jnp.zeros((dve, tq), F32)) for _ in range(2))
    carry = attend(init, 0, 1)
    n_iters = jnp.where(pl.program_id(2) == 0, 0, n_latent_iters)
    res = lax.fori_loop(0, n_iters, lambda i, cr: attend(cr, 1 + i * DA_NSUB, DA_NSUB), carry)
    outs = [acc[0:dv] * (1.0 / acc[dv:dv + 1]) for (_, acc) in res]
    ot = outs[0] - lam_ref[0] * outs[1]
    ot = ot * lax.rsqrt(jnp.mean(ot * ot, axis=0, keepdims=True) + NORM_EPS)
    o = ot.T * (sw_ref[...] * out_scale)
    o_ref[0] = (o * gate).astype(BF16)


def flash_diff_attention(qt, kh, vt, p3, lam, subln_w, lam_init):
    bsz, w, lt = qt.shape
    dv = 2 * DA_DIM
    nk = lt // DA_TK
    n_latent_iters = (nk - 1) // DA_NSUB
    kern = functools.partial(_flash_kernel, n_latent_iters=n_latent_iters, out_scale=1.0 - lam_init)
    g_blocks = BRANCH_W // dv
    return pl.pallas_call(
        kern,
        grid=(bsz, DA_HEADS, lt // DA_TQ),
        in_specs=[pl.BlockSpec(memory_space=pltpu.SMEM),
                  pl.BlockSpec((1, dv, DA_TQ), lambda b, h, qi: (b, h, qi)),
                  pl.BlockSpec((1, 2, lt, DA_DIM), lambda b, h, qi: (b, h, 0, 0)),
                  pl.BlockSpec((1, 1, nk, dv + DA_ONES, DA_TK), lambda b, h, qi: (b, h, 0, 0, 0)),
                  pl.BlockSpec((1, DA_TQ, dv), lambda b, h, qi: (b, qi, CB_DG * g_blocks + h)),
                  _const_spec((1, dv))],
        out_specs=pl.BlockSpec((1, DA_TQ, dv), lambda b, h, qi: (b, qi, h)),
        out_shape=jax.ShapeDtypeStruct((bsz, lt, BRANCH_W), BF16),
        compiler_params=_cparams(3),
        name="flash_attn",
    )(lam.reshape(1).astype(F32), qt, kh, vt, p3, subln_w.reshape(1, dv))


def _na_kernel(q_ref, g_ref, k_ref, v_ref, *rest, n_rows, n_ctx):
    bias_refs = rest[:NA_NBLK]
    g64_ref, wq_ref, o_ref = rest[NA_NBLK:]
    n_loc = NA_WIN_R * GRID_W
    q = _head_rms(q_ref[0], g64_ref[...], NA_DIM, wq_ref[...]).astype(BF16)
    g = g_ref[0]
    pair = [slice(hp * LANES, (hp + 1) * LANES) for hp in range(NA_HEADS // 2)]
    low_half = lax.broadcasted_iota(jnp.int32, (1, LANES), 1) < NA_DIM
    rows, starts, units = [], [], []
    for j in range(NA_NBLK):
        blk = pl.program_id(1) * NA_NBLK + j
        r = jnp.maximum(blk - n_ctx // GRID_W, 0)
        r0 = jnp.clip(r - NA_WIN_R // 2, 0, n_rows - NA_WIN_R)
        starts.append(pl.multiple_of(n_ctx + r0 * GRID_W, GRID_W))
        rows.append(slice(j * GRID_W, (j + 1) * GRID_W))
        units += [(j, hp) for hp in range(NA_HEADS // 2)]
    zero = jnp.zeros((), BF16)
    cat = lambda x, y: jnp.concatenate([x, y], axis=0)
    q2 = [cat(jnp.where(low_half, q[rows[j], pair[hp]], zero), jnp.where(low_half, zero, q[rows[j], pair[hp]]))
          for j, hp in units]
    s_loc = [_dot_nt(q2[u], k_ref[0, pl.ds(starts[j], n_loc), pair[hp]])
             + cat(bias_refs[j][0, 2 * hp], bias_refs[j][0, 2 * hp + 1]) for u, (j, hp) in enumerate(units)]
    s_ctx = [_dot_nt(q2[u], k_ref[0, 0:n_ctx, pair[hp]]) for u, (j, hp) in enumerate(units)]
    us = range(len(units))
    m = [jnp.maximum(jnp.max(s_loc[u], axis=-1, keepdims=True), jnp.max(s_ctx[u], axis=-1, keepdims=True))
         for u in us]
    p_loc = [jnp.exp(s_loc[u] - m[u]) for u in us]
    p_ctx = [jnp.exp(s_ctx[u] - m[u]) for u in us]
    l = [jnp.sum(p_loc[u], axis=-1, keepdims=True) + jnp.sum(p_ctx[u], axis=-1, keepdims=True) for u in us]
    o = [(_dot(p_loc[u].astype(BF16), v_ref[0, pl.ds(starts[j], n_loc), pair[hp]])
          + _dot(p_ctx[u].astype(BF16), v_ref[0, 0:n_ctx, pair[hp]])) / l[u]
         for u, (j, hp) in enumerate(units)]
    for u, (j, hp) in enumerate(units):
        o_pair = jnp.where(low_half, o[u][0:GRID_W], o[u][GRID_W:2 * GRID_W])
        o_ref[0, rows[j], pair[hp]] = (o_pair * _silu(g[rows[j], pair[hp]])).astype(BF16)


def neighbourhood_attention(p3, kn, va, bias_tbl, na_q_w, n_ctx):
    bsz, lt, w = kn.shape
    n_rows = (lt - n_ctx) // GRID_W
    ncb = n_ctx // GRID_W
    half = NA_WIN_R // 2

    def bias_spec(j):
        def idx(b, s):
            blk = s * NA_NBLK + j
            r = blk - ncb
            off = r - jnp.clip(r - half, 0, n_rows - NA_WIN_R)
            return (jnp.where(blk < ncb, NA_WIN_R, off), 0, 0, 0)
        return pl.BlockSpec((1, NA_HEADS, GRID_W, NA_WIN_R * GRID_W), idx)

    tq = NA_NBLK * GRID_W
    full_spec = pl.BlockSpec((1, lt, w), lambda b, s: (b, 0, 0))
    wq = (jnp.tile(na_q_w, w // NA_DIM) * NA_DIM ** -0.5).reshape(1, w)
    return pl.pallas_call(
        functools.partial(_na_kernel, n_rows=n_rows, n_ctx=n_ctx),
        grid=(bsz, lt // tq),
        in_specs=[_row_spec(CB_AQ, tq), _row_spec(CB_AG, tq), full_spec, full_spec,
                  *[bias_spec(j) for j in range(NA_NBLK)],
                  _const_spec((w, w)), _const_spec((1, w))],
        out_specs=pl.BlockSpec((1, tq, w), lambda b, s: (b, s, 0)),
        out_shape=jax.ShapeDtypeStruct((bsz, lt, w), BF16),
        compiler_params=_cparams(2),
        name="nbr_attn",
    )(p3, p3, kn, va, *([bias_tbl] * NA_NBLK), _group_matrix(w, NA_DIM), wq)


def na_bias_table(rpb):
    col = np.arange(GRID_W)
    c0 = np.clip(col - NA_WIN_C // 2, 0, GRID_W - NA_WIN_C)
    col_ok = (col[None, :] >= c0[:, None]) & (col[None, :] < c0[:, None] + NA_WIN_C)
    d_col = np.clip(col[None, :] - col[:, None] + (NA_WIN_C - 1), 0, 2 * NA_WIN_C - 2)
    onehot = (d_col[:, :, None] == np.arange(2 * NA_WIN_C - 1)).astype(np.float32)
    by_col = jnp.einsum('hrc,qwc->hrqw', rpb.astype(F32), jnp.asarray(onehot),
                        precision=lax.Precision.HIGHEST)
    by_col = jnp.where(jnp.asarray(col_ok)[None, None], by_col, NEG_INF)
    tbl = jnp.stack([by_col[:, NA_WIN_R - 1 - o:2 * NA_WIN_R - 1 - o] for o in range(NA_WIN_R)]
                    + [jnp.full((NA_HEADS, NA_WIN_R, GRID_W, GRID_W), NEG_INF, F32)])
    tbl = jnp.transpose(tbl, (0, 1, 3, 2, 4))
    return tbl.reshape(NA_WIN_R + 1, NA_HEADS, GRID_W, NA_WIN_R * GRID_W)


def _prep_rwkv_kernel(r_ref, rp_ref, rn_ref, k_ref, kp_ref, kn_ref, v_ref, vp_ref, vn_ref,
                      m_ref, mp_ref, mn_ref, mu_ref, w0_ref, a0_ref, w2_ref, a2_ref, kk_ref, ka_ref,
                      rk_ref, g64_ref,
                      ro_ref, vo_ref, ao_ref, lw_ref, kd_ref, bo_ref, bonus_ref, *, n_tiles, n_ctx_tiles):
    i = pl.program_id(1)
    w = BRANCH_W

    def shifted(x_ref, p_ref, n_ref, mu):
        x = x_ref[0]
        xp, xn = _neighbours(x, p_ref[0], n_ref[0], i, n_tiles, n_ctx_tiles)
        return x + (0.5 * (xp + xn) - x) * mu

    r = shifted(r_ref, rp_ref, rn_ref, mu_ref[0:1, :])
    k = shifted(k_ref, kp_ref, kn_ref, mu_ref[1:2, :])
    v = shifted(v_ref, vp_ref, vn_ref, mu_ref[2:3, :])
    misc = shifted(m_ref, mp_ref, mn_ref, mu_ref[3:4, :])
    wd = jnp.tanh(misc[:, 0:2 * RW_LORA_W]).astype(BF16)
    ad = misc[:, 2 * RW_LORA_W:MISC_DT].astype(BF16)
    w_log = w0_ref[...] + _dot(wd, w2_ref[...])
    gate = _sigmoid(a0_ref[...] + _dot(ad, a2_ref[...]))
    log_decay = -math.exp(-0.5) * _sigmoid(w_log)
    g64 = g64_ref[...]
    kk = k * kk_ref[...]
    kk = kk / jnp.maximum(jnp.sqrt(_group_sum(kk * kk, g64)), 1e-12)
    ro_ref[0] = r
    vo_ref[0] = v.astype(BF16)
    ao_ref[0] = -kk
    coef = None
    for d in range(2):
        a_d = gate[:, d * w:(d + 1) * w]
        kd = k * (1.0 + (a_d - 1.0) * ka_ref[...])
        lw_ref[d, 0] = log_decay[:, d * w:(d + 1) * w]
        kd_ref[d, 0] = kd
        bo_ref[d, 0] = kk * a_d
        coef = kd if coef is None else coef + kd
    bonus_ref[0] = _group_sum(r * coef * rk_ref[...], g64) * v


def prep_rwkv(p3, mu, w0, w2, a0, a2, k_k, k_a, r_k, n_ctx):
    bsz, lt, _ = p3.shape
    tm = PREP_TM
    w = BRANCH_W
    n_tiles = lt // tm
    mu4 = jnp.stack([mu[0:w], mu[w:2 * w], mu[2 * w:3 * w],
                     jnp.pad(mu[3 * w:], (0, w - (mu.shape[0] - 3 * w)))])
    zero = jnp.zeros((RW_LORA_W, w), F32)
    w2cat = jnp.concatenate([jnp.concatenate([w2[0], zero], axis=1),
                             jnp.concatenate([zero, w2[1]], axis=1)], axis=0).astype(BF16)
    a2cat = jnp.concatenate([jnp.concatenate([a2[0], zero], axis=1),
                             jnp.concatenate([zero, a2[1]], axis=1)], axis=0).astype(BF16)
    specs = []
    for cb in (CB_BR, CB_BK, CB_BV, CB_MISC):
        specs += [_row_spec(cb), *_halo_specs(cb, tm, lt)]
    vec = _const_spec((1, w))
    tok = pl.BlockSpec((1, tm, w), lambda b, i: (b, i, 0))
    tok2 = pl.BlockSpec((2, 1, tm, w), lambda b, i: (0, b, i, 0))
    kern = functools.partial(_prep_rwkv_kernel, n_tiles=n_tiles, n_ctx_tiles=n_ctx // tm)
    return pl.pallas_call(
        kern,
        grid=(bsz, n_tiles),
        in_specs=specs + [_const_spec((4, w)), _const_spec((1, 2 * w)), _const_spec((1, 2 * w)),
                          _const_spec((2 * RW_LORA_W, 2 * w)), _const_spec((2 * RW_LORA_A, 2 * w)),
                          vec, vec, vec, _const_spec((w, w))],
        out_specs=[tok, tok, tok, tok2, tok2, tok2, tok],
        out_shape=[jax.ShapeDtypeStruct((bsz, lt, w), F32),
                   jax.ShapeDtypeStruct((bsz, lt, w), BF16),
                   jax.ShapeDtypeStruct((bsz, lt, w), F32),
                   jax.ShapeDtypeStruct((2, bsz, lt, w), F32),
                   jax.ShapeDtypeStruct((2, bsz, lt, w), F32),
                   jax.ShapeDtypeStruct((2, bsz, lt, w), F32),
                   jax.ShapeDtypeStruct((bsz, lt, w), F32)],
        compiler_params=_cparams(2),
        name="prep_rwkv",
    )(*([p3] * 12), mu4, w0.reshape(1, 2 * w), a0.reshape(1, 2 * w), w2cat, a2cat,
      k_k.reshape(1, w), k_a.reshape(1, w), r_k.reshape(1, w), _group_matrix(w, RW_DIM))


def _rwkv_kernel(tri_ref, mq_ref, r_ref, v_ref, a_ref, lw_ref, kd_ref, b_ref, y_ref, s_ref):
    c = RW_CHUNK
    nsub = r_ref.shape[1] // c
    d = pl.program_id(0)

    @pl.when(pl.program_id(2) == 0)
    def _():
        s_ref[...] = jnp.zeros_like(s_ref)

    tri = tri_ref[0]
    m_quad = mq_ref[0] > 0.5
    rows = lax.broadcasted_iota(jnp.int32, (c, c), 0)
    cols = lax.broadcasted_iota(jnp.int32, (c, c), 1)
    eye = (rows == cols).astype(F32)
    same_half = (rows >= c // 2) == (cols >= c // 2)
    heads = [slice(h * RW_DIM, (h + 1) * RW_DIM) for h in range(RW_HEADS)]

    row_sl, e_tot = [], []
    al, rh, rf, be, ka, bc, kc, vb = [], [], [], [], [], [], [], []
    for j in range(nsub):
        off = pl.multiple_of(jnp.where(d == 0, j, nsub - 1 - j) * c, c)
        rs = pl.ds(off, c)
        row_sl.append(rs)
        lw = lw_ref[0, 0, rs, :]
        cum = _dot_exact_lhs(tri, lw)
        tot = jnp.sum(lw, axis=0, keepdims=True)
        e_m = jnp.exp(-cum)
        e_t = jnp.exp(tot - cum)
        e_tot.append(jnp.exp(tot))
        b_in = b_ref[0, 0, rs, :]
        k_in = kd_ref[0, 0, rs, :]
        alpha = (a_ref[0, rs, :] * jnp.exp(cum - lw)).astype(BF16)
        rho_f = r_ref[0, rs, :] * jnp.exp(cum)
        rho = rho_f.astype(BF16)
        beta = (b_in * e_m).astype(BF16)
        kappa = (k_in * e_m).astype(BF16)
        beta_c = (b_in * e_t).astype(BF16)
        kappa_c = (k_in * e_t).astype(BF16)
        v_b = v_ref[0, rs, :]
        for sl in heads:
            al.append(alpha[:, sl])
            rh.append(rho[:, sl])
            rf.append(rho_f[:, sl])
            be.append(beta[:, sl])
            ka.append(kappa[:, sl])
            bc.append(beta_c[:, sl])
            kc.append(kappa_c[:, sl])
            vb.append(v_b[:, sl])

    units = range(nsub * RW_HEADS)
    cat = lambda x, y: jnp.concatenate([x, y], axis=0)
    zeros = jnp.zeros((c, RW_DIM), BF16)
    prod = [jnp.where(m_quad, _dot_nt(cat(al[u], rh[u]), cat(be[u], ka[u])), 0.0) for u in units]
    l_ab = [prod[u][0:c, 0:c] for u in units]
    top = [prod[u][0:c].astype(BF16) for u in units]
    bot = [prod[u][c:2 * c].astype(BF16) for u in units]
    l_d = [jnp.where(same_half, l_ab[u], 0.0) for u in units]
    l_o = [(l_ab[u] - l_d[u]).astype(BF16) for u in units]
    ld_b = [l_d[u].astype(BF16) for u in units]
    pw = [_dot(ld_b[u], ld_b[u]).astype(BF16) for u in units]
    td = [eye + l_d[u] for u in units]
    for _ in range(int(math.log2(c)) - 3):
        both = [_dot(cat(td[u].astype(BF16), pw[u]), pw[u]) for u in units]
        td = [td[u] + both[u][0:c] for u in units]
        pw = [both[u][c:2 * c].astype(BF16) for u in units]
    td = [td[u] + _dot(td[u].astype(BF16), pw[u]) for u in units]
    td_b = [td[u].astype(BF16) for u in units]
    x_o = [_dot(td_b[u], l_o[u]).astype(BF16) for u in units]
    tinv = [(td[u] + _dot(x_o[u], td_b[u])).astype(BF16) for u in units]
    akv = [_dot(top[u], cat(zeros, vb[u])).astype(BF16) for u in units]
    lcat = lambda x, y: jnp.concatenate([x, y], axis=1)
    av = [_dot(tinv[u], lcat(al[u], akv[u])).astype(BF16) for u in units]
    ry = [_dot(bot[u], cat(av[u], lcat(zeros, vb[u]))) for u in units]
    r_hat = [(rf[u] + ry[u][:, 0:RW_DIM]).astype(BF16) for u in units]
    y_hat = [ry[u][:, RW_DIM:2 * RW_DIM] for u in units]
    qn = [_dot_tn(av[u], bc[u]) for u in units]
    q_mat = [qn[u][0:RW_DIM].astype(BF16) for u in units]
    n_mat = [qn[u][RW_DIM:2 * RW_DIM] + _dot_tn(vb[u], kc[u]) for u in units]

    state = [s_ref[h] for h in range(RW_HEADS)]
    for j in range(nsub):
        for h, sl in enumerate(heads):
            u = j * RW_HEADS + h
            s_b = state[h].astype(BF16)
            y_ref[0, 0, row_sl[j], sl] = _dot_nt(r_hat[u], s_b) + y_hat[u]
            state[h] = state[h] * e_tot[j][:, sl] + _dot(s_b, q_mat[u]) + n_mat[u]
    for h in range(RW_HEADS):
        s_ref[h] = state[h]


def _scan_chunk_index(d, s, n_ctx_chunks, n_chunks):
    rev = jnp.where(s < n_ctx_chunks, n_ctx_chunks - 1 - s, n_chunks + n_ctx_chunks - 1 - s)
    return jnp.where(d == 0, s, rev)


def _direction_masks(c):
    i = np.arange(c)
    lower = (i[None, :] <= i[:, None]).astype(np.float32)
    tri = np.stack([lower, lower.T])
    strict = np.stack([lower - np.eye(c, dtype=np.float32), lower.T - np.eye(c, dtype=np.float32)])
    return tri, strict


def rwkv_scan(r, v, a, lw, kd, b, n_ctx):
    bsz, lt, w = r.shape
    c = RW_CHUNK
    blk = RW_NSUB * c
    nck = lt // blk
    ncc = n_ctx // blk
    tri, strict = _direction_masks(c)
    shared = pl.BlockSpec((1, blk, w), lambda d, bi, s: (bi, _scan_chunk_index(d, s, ncc, nck), 0))
    perdir = pl.BlockSpec((1, 1, blk, w), lambda d, bi, s: (d, bi, _scan_chunk_index(d, s, ncc, nck), 0))
    mask_spec = pl.BlockSpec((1, c, c), lambda d, bi, s: (d, 0, 0))
    quad_spec = pl.BlockSpec((1, 2 * c, 2 * c), lambda d, bi, s: (d, 0, 0))
    quad = np.concatenate([np.tile(strict, (1, 1, 2)), np.tile(tri, (1, 1, 2))], axis=1)
    return pl.pallas_call(
        _rwkv_kernel,
        grid=(2, bsz, nck),
        in_specs=[mask_spec, quad_spec, shared, shared, shared, perdir, perdir, perdir],
        out_specs=perdir,
        out_shape=jax.ShapeDtypeStruct((2, bsz, lt, w), F32),
        scratch_shapes=[pltpu.VMEM((RW_HEADS, RW_DIM, RW_DIM), F32)],
        compiler_params=_cparams(3),
        name="rwkv_scan",
    )(jnp.asarray(tri, BF16), jnp.asarray(quad, F32), r, v, a, lw, kd, b)


def _finish_rwkv_kernel(y_ref, bonus_ref, g_ref, lnw_ref, lnb_ref, g64_ref, o_ref):
    y = y_ref[0, 0] + y_ref[1, 0]
    g64 = g64_ref[...]
    mean = _dot_exact_rhs(y, g64) * (1.0 / RW_DIM)
    yc = y - mean
    var = _group_sum(yc * yc, g64) * (1.0 / RW_DIM)
    yn = yc * lax.rsqrt(var + RW_GN_EPS) * lnw_ref[...] + lnb_ref[...]
    o_ref[0] = ((yn + bonus_ref[0]) * _silu(g_ref[0])).astype(BF16)


def finish_rwkv(y, bonus, p3, ln_w, ln_b):
    _, bsz, lt, w = y.shape
    tm = _tile_rows(lt)
    tok = pl.BlockSpec((1, tm, w), lambda b, i: (b, i, 0))
    vec = _const_spec((1, w))
    return pl.pallas_call(
        _finish_rwkv_kernel,
        grid=(bsz, lt // tm),
        in_specs=[pl.BlockSpec((2, 1, tm, w), lambda b, i: (0, b, i, 0)), tok, _row_spec(CB_BG, tm),
                  vec, vec, _const_spec((w, w))],
        out_specs=tok,
        out_shape=jax.ShapeDtypeStruct((bsz, lt, w), BF16),
        compiler_params=_cparams(2),
        name="finish_rwkv",
    )(y, bonus, p3, ln_w.reshape(1, w), ln_b.reshape(1, w), _group_matrix(w, RW_DIM))


def _prep_ssd_kernel(x_ref, xp_ref, xn_ref, bc_ref, bcp_ref, bcn_ref, m_ref, cw_ref, cb_ref, dtb_ref,
                     aneg_ref, exp_ref, dsk_ref,
                     xq_ref, bco_ref, bt_ref, a_ref, at_ref, dskip_ref, *, n_tiles, n_ctx_tiles):
    i = pl.program_id(1)
    w = BRANCH_W

    def conv(x_ref, p_ref, n_ref, half):
        x = x_ref[0]
        xp, xn = _neighbours(x, p_ref[0], n_ref[0], i, n_tiles, n_ctx_tiles)
        lo = half * w
        y = (xp * cw_ref[0:1, lo:lo + w] + x * cw_ref[1:2, lo:lo + w] + xn * cw_ref[2:3, lo:lo + w]
             + cb_ref[:, lo:lo + w])
        return _silu(y)

    xs = conv(x_ref, xp_ref, xn_ref, 0)
    bc = conv(bc_ref, bcp_ref, bcn_ref, 1)
    bco_ref[0] = bc.astype(BF16)
    bm_f = bc[:, 0:SSM_GROUPS * SSM_STATE]
    q = SSM_CHUNK
    for j in range(bc.shape[0] // q):
        bt_ref[0, j] = bm_f[j * q:(j + 1) * q, :].T.astype(BF16)
    dt = _softplus(m_ref[0][:, MISC_DT:MISC_DT + LANES] + dtb_ref[...])
    lane = lax.broadcasted_iota(jnp.int32, (1, LANES), 1)
    dt = jnp.where(lane < 2 * SSM_HEADS, dt, 0.0)
    dtx = _dot_exact_rhs(dt, exp_ref[...])
    a_all = dt * aneg_ref[...]
    first = lane < SSM_HEADS
    a_dirs = [jnp.where(first, a_all, 0.0), jnp.where(first, pltpu.roll(a_all, LANES - SSM_HEADS, 1), 0.0)]
    for d in range(2):
        xq_ref[d, 0] = xs * dtx[:, d * w:(d + 1) * w]
        a_ref[d, 0] = a_dirs[d]
        for j in range(bc.shape[0] // q):
            at_ref[d, 0, j] = a_dirs[d][j * q:(j + 1) * q, :].T[0:2 * SUBLANES, :]
    dskip_ref[0] = xs * dsk_ref[...]


def prep_ssd(p3, conv_w, conv_b, dt_bias, a_log, d_skip, n_ctx):
    bsz, lt, _ = p3.shape
    tm = PREP_TM
    w = BRANCH_W
    n_tiles = lt // tm
    nh2 = 2 * SSM_HEADS
    pad_lanes = lambda v: jnp.pad(v.reshape(1, nh2), ((0, 0), (0, LANES - nh2)))
    expand = np.zeros((LANES, 2 * w), np.float32)
    for d in range(2):
        for h in range(SSM_HEADS):
            expand[d * SSM_HEADS + h, d * w + h * SSM_HEAD_DIM:d * w + (h + 1) * SSM_HEAD_DIM] = 1.0
    specs = [_row_spec(CB_CX), *_halo_specs(CB_CX, tm, lt), _row_spec(CB_CBC), *_halo_specs(CB_CBC, tm, lt),
             _row_spec(CB_MISC)]
    tok = pl.BlockSpec((1, tm, w), lambda b, i: (b, i, 0))
    gs = SSM_GROUPS * SSM_STATE
    kern = functools.partial(_prep_ssd_kernel, n_tiles=n_tiles, n_ctx_tiles=n_ctx // tm)
    return pl.pallas_call(
        kern,
        grid=(bsz, n_tiles),
        in_specs=specs + [_const_spec((3, 2 * w)), _const_spec((1, 2 * w)), _const_spec((1, LANES)),
                          _const_spec((1, LANES)), _const_spec((LANES, 2 * w)), _const_spec((1, w))],
        out_specs=[pl.BlockSpec((2, 1, tm, w), lambda b, i: (0, b, i, 0)), tok,
                   pl.BlockSpec((1, tm // SSM_CHUNK, gs, SSM_CHUNK), lambda b, i: (b, i, 0, 0)),
                   pl.BlockSpec((2, 1, tm, LANES), lambda b, i: (0, b, i, 0)),
                   pl.BlockSpec((2, 1, tm // SSM_CHUNK, 2 * SUBLANES, SSM_CHUNK), lambda b, i: (0, b, i, 0, 0)), tok],
        out_shape=[jax.ShapeDtypeStruct((2, bsz, lt, w), F32),
                   jax.ShapeDtypeStruct((bsz, lt, w), BF16),
                   jax.ShapeDtypeStruct((bsz, lt // SSM_CHUNK, gs, SSM_CHUNK), BF16),
                   jax.ShapeDtypeStruct((2, bsz, lt, LANES), F32),
                   jax.ShapeDtypeStruct((2, bsz, lt // SSM_CHUNK, 2 * SUBLANES, SSM_CHUNK), F32),
                   jax.ShapeDtypeStruct((bsz, lt, w), F32)],
        compiler_params=_cparams(2),
        name="prep_ssd",
    )(*([p3] * 7), conv_w, conv_b.reshape(1, 2 * w), pad_lanes(dt_bias), pad_lanes(-jnp.exp(a_log)),
      jnp.asarray(expand, BF16), jnp.repeat(d_skip, SSM_HEAD_DIM).reshape(1, w))


def _ssd_kernel(tri_ref, mi_ref, xq_ref, bc_ref, bt_ref, a_ref, at_ref, y_ref, s_ref):
    q = SSM_CHUNK
    nsub = bt_ref.shape[1]
    rep = SSM_HEADS // SSM_GROUPS
    gw = SSM_GROUPS * SSM_STATE
    p = SSM_HEAD_DIM
    d = pl.program_id(0)

    @pl.when(pl.program_id(2) == 0)
    def _():
        s_ref[...] = jnp.zeros_like(s_ref)

    tri = tri_ref[0]
    mask = mi_ref[0] > 0.5
    hs = range(SSM_HEADS)
    groups = [slice(g * SSM_STATE, (g + 1) * SSM_STATE) for g in range(SSM_GROUPS)]

    rows, cms, y_in, e_col, e_tot, upd = [], [], [], [], [], []
    for j in range(nsub):
        ch = jnp.where(d == 0, j, nsub - 1 - j)
        rs = pl.ds(pl.multiple_of(ch * q, q), q)
        rows.append(rs)
        a = a_ref[0, 0, rs, :]
        a_t = at_ref[0, 0, ch]
        xq = xq_ref[0, 0, rs, :]
        bm = bc_ref[0, rs, 0:gw]
        cm = bc_ref[0, rs, gw:2 * gw]
        bt = bt_ref[0, ch]
        acol = _dot_exact_lhs(tri, a)
        arow = _dot_nt_exact_lhs_f32(a_t, tri)
        tot = jnp.sum(a, axis=0, keepdims=True)
        cm_g = [cm[:, gs] for gs in groups]
        cb = [_dot_nt(cm_g[g], bm[:, gs]) for g, gs in enumerate(groups)]
        ac = [jnp.broadcast_to(acol[:, h:h + 1], (q, q)) for h in hs]
        ar = [jnp.broadcast_to(arow[h:h + 1, :], (q, q)) for h in hs]
        gmat = [(cb[h // rep] * jnp.exp(jnp.where(mask, ac[h] - ar[h], NEG_INF))).astype(BF16) for h in hs]
        xh = [xq[:, h * p:(h + 1) * p] for h in hs]
        tot_h = [jnp.broadcast_to(tot[:, h:h + 1], (1, p)) for h in hs]
        xd = [(xh[h] * jnp.exp(tot_h[h] - ac[h][:, :p])).astype(BF16) for h in hs]
        cms.append(cm_g)
        y_in.append([_dot(gmat[h], xh[h].astype(BF16)) for h in hs])
        e_col.append([jnp.exp(ac[h][:, :p]) for h in hs])
        e_tot.append([jnp.exp(tot_h[h]) for h in hs])
        upd.append([_dot(bt[groups[h // rep], :], xd[h]) for h in hs])

    state = [s_ref[h] for h in hs]
    for j in range(nsub):
        y_st = [_dot(cms[j][h // rep], state[h].astype(BF16)) for h in hs]
        for h in hs:
            y_ref[0, 0, rows[j], h * p:(h + 1) * p] = y_in[j][h] + y_st[h] * e_col[j][h]
            state[h] = state[h] * e_tot[j][h] + upd[j][h]
    for h in hs:
        s_ref[h] = state[h]


def ssd_scan(xq, bc, bt, a, a_t, n_ctx):
    _, bsz, lt, w = xq.shape
    q = SSM_CHUNK
    blk = SSM_NSUB * q
    nck = lt // blk
    ncc = n_ctx // blk
    gw = bt.shape[2]
    tri, _ = _direction_masks(q)

    def cidx(d, s):
        return _scan_chunk_index(d, s, ncc, nck)

    mask_spec = pl.BlockSpec((1, q, q), lambda d, bi, s: (d, 0, 0))
    return pl.pallas_call(
        _ssd_kernel,
        grid=(2, bsz, nck),
        in_specs=[mask_spec, mask_spec,
                  pl.BlockSpec((1, 1, blk, w), lambda d, bi, s: (d, bi, cidx(d, s), 0)),
                  pl.BlockSpec((1, blk, w), lambda d, bi, s: (bi, cidx(d, s), 0)),
                  pl.BlockSpec((1, SSM_NSUB, gw, q), lambda d, bi, s: (bi, cidx(d, s), 0, 0)),
                  pl.BlockSpec((1, 1, blk, LANES), lambda d, bi, s: (d, bi, cidx(d, s), 0)),
                  pl.BlockSpec((1, 1, SSM_NSUB, 2 * SUBLANES, q), lambda d, bi, s: (d, bi, cidx(d, s), 0, 0))],
        out_specs=pl.BlockSpec((1, 1, blk, w), lambda d, bi, s: (d, bi, cidx(d, s), 0)),
        out_shape=jax.ShapeDtypeStruct((2, bsz, lt, w), F32),
        scratch_shapes=[pltpu.VMEM((SSM_HEADS, SSM_STATE, SSM_HEAD_DIM), F32)],
        compiler_params=_cparams(3),
        name="ssd_scan",
    )(jnp.asarray(tri, BF16), jnp.asarray(tri, F32), xq, bc, bt, a, a_t)


def _finish_ssd_kernel(y_ref, dskip_ref, z_ref, nw_ref, g256_ref, o_ref):
    y = y_ref[0, 0] + y_ref[1, 0] + dskip_ref[0]
    g = y * _silu(z_ref[0])
    group = BRANCH_W // SSM_GROUPS
    o_ref[0] = _head_rms(g, g256_ref[...], group, nw_ref[...]).astype(BF16)


def finish_ssd(y, dskip, p3, norm_w):
    _, bsz, lt, w = y.shape
    tm = _tile_rows(lt)
    tok = pl.BlockSpec((1, tm, w), lambda b, i: (b, i, 0))
    return pl.pallas_call(
        _finish_ssd_kernel,
        grid=(bsz, lt // tm),
        in_specs=[pl.BlockSpec((2, 1, tm, w), lambda b, i: (0, b, i, 0)), tok, _row_spec(CB_CZ, tm),
                  _const_spec((1, w)), _const_spec((w, w))],
        out_specs=tok,
        out_shape=jax.ShapeDtypeStruct((bsz, lt, w), BF16),
        compiler_params=_cparams(2),
        name="finish_ssd",
    )(y, dskip, p3, norm_w.reshape(1, w), _group_matrix(w, w // SSM_GROUPS))


def _permute_w_in(w_in_l):
    w = BRANCH_W
    b0 = 4 * w
    lora0 = b0 + 3 * w
    bg0 = lora0 + MISC_DT
    c0 = bg0 + w
    dt0 = c0 + SSM_CONV_CH
    z0 = dt0 + 2 * SSM_HEADS
    d0 = z0 + w
    end = d0 + 4 * w
    parts = [w_in_l[:, 0:lora0], w_in_l[:, bg0:c0], w_in_l[:, c0:dt0], w_in_l[:, z0:d0], w_in_l[:, d0:end],
             w_in_l[:, lora0:bg0], w_in_l[:, dt0:z0]]
    used = sum(p.shape[1] for p in parts)
    pad = jnp.zeros((w_in_l.shape[0], IN_W_PAD - used), BF16)
    return jnp.concatenate([p.astype(BF16) for p in parts] + [pad], axis=1)


def kernel(x, c, ctx, c_ctx, norm_w, w_ada, b_ada, w_in, na_q_norm, na_k_norm, na_rpb, rw_mu, rw_w0, rw_w2, rw_a0, rw_a2, rw_k_k, rw_k_a, rw_r_k, rw_ln_w, rw_ln_b, ssm_conv_w, ssm_conv_b, ssm_dt_bias, ssm_A_log, ssm_D, ssm_norm_w, da_q_norm, da_k_norm, da_lq1, da_lk1, da_lq2, da_lk2, da_subln, w_gate, w_up, w_out):
    bsz, seq, d = x.shape
    n_ctx = ctx.shape[1]
    lt = n_ctx + seq
    depth = w_in.shape[0]
    assert n_ctx == CTX_LEN == PREP_TM == DA_TK and seq % (GRID_W * NA_WIN_R) == 0
    assert ((lt // DA_TK) - 1) % DA_NSUB == 0 and n_ctx % NORM_ROWS == 0
    cond = jnp.concatenate([_silu(c), _silu(c_ctx)[None], jnp.zeros((SUBLANES - bsz - 1, d), F32)], axis=0)
    xs2 = jnp.concatenate([ctx, x], axis=1).reshape(bsz * lt, d)
    cos, sin = rope_tables(lt, n_ctx)
    for l in range(depth):
        lam_init = 0.8 - 0.6 * math.exp(-0.3 * l)
        mod = small_matmul(cond, w_ada[l], b_ada[l])
        shift, scale, gate = jnp.split(mod[:bsz], 3, axis=-1)
        shift_c, scale_c, gate_c = jnp.split(mod[bsz], 3, axis=-1)
        p2, h = in_projection(xs2, norm_w[l], scale, shift, scale_c, shift_c, _permute_w_in(w_in[l]), lt, n_ctx)
        p3 = p2.reshape(bsz, lt, IN_W_PAD)

        kn, va, qt, kh, vt = prep_attention(p3, cos, sin, na_k_norm[l], da_q_norm[l], da_k_norm[l])
        oa = neighbourhood_attention(p3, kn, va, na_bias_table(na_rpb[l]), na_q_norm[l], n_ctx)
        lam = jnp.exp(jnp.sum(da_lq1[l] * da_lk1[l])) - jnp.exp(jnp.sum(da_lq2[l] * da_lk2[l])) + lam_init
        od = flash_diff_attention(qt, kh, vt, p3, lam, da_subln[l], lam_init)

        r, vb, a, lw, kd, b, bonus = prep_rwkv(p3, rw_mu[l], rw_w0[l], rw_w2[l], rw_a0[l], rw_a2[l],
                                              rw_k_k[l], rw_k_a[l], rw_r_k[l].reshape(-1), n_ctx)
        ob = finish_rwkv(rwkv_scan(r, vb, a, lw, kd, b, n_ctx), bonus, p3, rw_ln_w[l], rw_ln_b[l])

        xq, bc, bt, sa, sat, dskip = prep_ssd(p3, ssm_conv_w[l], ssm_conv_b[l], ssm_dt_bias[l], ssm_A_log[l],
                                              ssm_D[l], n_ctx)
        om = finish_ssd(ssd_scan(xq, bc, bt, sa, sat, n_ctx), dskip, p3, ssm_norm_w[l])

        ys = [t.reshape(bsz * lt, BRANCH_W) for t in (oa, ob, om, od)]
        acc = gate_up(h, ys, w_gate[l].astype(BF16), w_up[l].astype(BF16), lt)
        xs2 = out_projection(acc, w_out[l].astype(BF16), xs2, gate, gate_c, lt, n_ctx)
    return xs2.reshape(bsz, lt, d)[:, n_ctx:]
```

```python
import functools
import math

import numpy as np
import jax
import jax.numpy as jnp
from jax import lax
from jax.experimental import pallas as pl
from jax.experimental.pallas import tpu as pltpu

F32 = jnp.float32
BF16 = jnp.bfloat16

D_MODEL = 2048
GRID_W = 64
CTX_LEN = 256
N_BRANCH = 4
BRANCH_W = D_MODEL // N_BRANCH
NORM_EPS = 1e-6
NEG_INF = -1e30

NA_DIM = 64
NA_HEADS = BRANCH_W // NA_DIM
NA_WIN_R = 8
NA_WIN_C = 16
NA_NBLK = 2

RW_DIM = 64
RW_HEADS = BRANCH_W // RW_DIM
RW_LORA_W = 64
RW_LORA_A = 64
RW_GN_EPS = 64e-5
RW_CHUNK = 64
RW_NSUB = 4

SSM_HEAD_DIM = 64
SSM_HEADS = BRANCH_W // SSM_HEAD_DIM
SSM_GROUPS = 2
SSM_STATE = 128
SSM_CHUNK = 128
SSM_NSUB = 2
SSM_CONV_CH = BRANCH_W + 2 * SSM_GROUPS * SSM_STATE

DA_DIM = 64
DA_HEADS = BRANCH_W // (2 * DA_DIM)
ROPE_BASE = 10000.0
DA_TQ = 256
DA_TK = 256
DA_NSUB = 32
DA_ONES = 16

SUBLANES = 8
LANES = 128

CB_AQ, CB_AK, CB_AV, CB_AG = 0, 1, 2, 3
CB_BR, CB_BK, CB_BV, CB_BG = 4, 5, 6, 7
CB_CX, CB_CBC, CB_CZ = 8, 9, 10
CB_DQ, CB_DK, CB_DV, CB_DG = 11, 12, 13, 14
CB_MISC = 15
MISC_DT = 2 * RW_LORA_W + 2 * RW_LORA_A
N_COL_BLOCKS = 16
IN_W_PAD = N_COL_BLOCKS * BRANCH_W
PREP_TM = 256
NORM_ROWS = 32

V7X_VMEM_BYTES = 64 * 1024 * 1024
VMEM_LIMIT = V7X_VMEM_BYTES * 7 // 8


def _cparams(n_axes):
    return pltpu.CompilerParams(dimension_semantics=("arbitrary",) * n_axes, vmem_limit_bytes=VMEM_LIMIT)


def _dot(a, b):
    return jnp.dot(a, b, preferred_element_type=F32)


def _dot_nt(a, b):
    return lax.dot_general(a, b, (((1,), (1,)), ((), ())), preferred_element_type=F32)


def _dot_tn(a, b):
    return lax.dot_general(a, b, (((0,), (0,)), ((), ())), preferred_element_type=F32)


def _split3(x):
    hi = x.astype(BF16)
    r1 = x - hi.astype(F32)
    mid = r1.astype(BF16)
    lo = (r1 - mid.astype(F32)).astype(BF16)
    return hi, mid, lo


def _dot_exact_lhs(m_bf16, x):
    hi, mid, lo = _split3(x)
    return _dot(m_bf16, hi) + _dot(m_bf16, mid) + _dot(m_bf16, lo)


def _dot_exact_rhs(x, m_bf16):
    hi, mid, lo = _split3(x)
    return _dot(hi, m_bf16) + _dot(mid, m_bf16) + _dot(lo, m_bf16)


def _group_sum(x, gmat_bf16):
    hi = x.astype(BF16)
    lo = (x - hi.astype(F32)).astype(BF16)
    return _dot(hi, gmat_bf16) + _dot(lo, gmat_bf16)


def _dot_nt_exact_lhs_f32(x, m_bf16):
    hi, mid, lo = _split3(x)
    return _dot_nt(hi, m_bf16) + _dot_nt(mid, m_bf16) + _dot_nt(lo, m_bf16)


def _sigmoid(x):
    return 1.0 / (1.0 + jnp.exp(-x))


def _silu(x):
    return x * _sigmoid(x)


def _softplus(x):
    return jnp.maximum(x, 0.0) + jnp.log(1.0 + jnp.exp(-jnp.abs(x)))


def _group_matrix(width, group):
    g = np.arange(width) // group
    return jnp.asarray((g[:, None] == g[None, :]).astype(np.float32), BF16)


def _head_rms(x, gmat, group, w):
    ms = _group_sum(x * x, gmat) * (1.0 / group)
    return x * lax.rsqrt(ms + NORM_EPS) * w


def _tile_rows(n_rows):
    return 768 if n_rows % 768 == 0 else 256


def _row_spec(cb, tm=PREP_TM):
    return pl.BlockSpec((1, tm, BRANCH_W), lambda b, i: (b, i, cb))


def _halo_specs(cb, tm, lt):
    per = tm // SUBLANES
    last = lt // SUBLANES - 1
    prev = pl.BlockSpec((1, SUBLANES, BRANCH_W), lambda b, i: (b, jnp.maximum(i * per - 1, 0), cb))
    nxt = pl.BlockSpec((1, SUBLANES, BRANCH_W), lambda b, i: (b, jnp.minimum((i + 1) * per, last), cb))
    return prev, nxt


def _const_spec(shape):
    return pl.BlockSpec(shape, lambda *_: (0,) * len(shape))


def _neighbours(x, prev_blk, next_blk, i, n_tiles, n_ctx_tiles):
    tm = x.shape[0]
    row = lax.broadcasted_iota(jnp.int32, (tm, 1), 0)
    seg_start = jnp.logical_or(i == 0, i == n_ctx_tiles)
    seg_end = jnp.logical_or(i == n_ctx_tiles - 1, i == n_tiles - 1)
    prev_row = jnp.where(seg_start, 0.0, prev_blk[SUBLANES - 1:SUBLANES, :])
    next_row = jnp.where(seg_end, 0.0, next_blk[0:1, :])
    x_prev = jnp.where(row == 0, prev_row, pltpu.roll(x, 1, 0))
    x_next = jnp.where(row == tm - 1, next_row, pltpu.roll(x, tm - 1, 0))
    return x_prev, x_next


def _small_mm_kernel(a_ref, w_ref, b_ref, o_ref):
    o_ref[...] = _dot(a_ref[...].astype(BF16), w_ref[...].astype(BF16)) + b_ref[...]


def small_matmul(a, w, b, tn=512):
    m, k = a.shape
    n = w.shape[1]
    return pl.pallas_call(
        _small_mm_kernel,
        grid=(n // tn,),
        in_specs=[pl.BlockSpec((m, k), lambda j: (0, 0)),
                  pl.BlockSpec((k, tn), lambda j: (0, j)),
                  pl.BlockSpec((1, tn), lambda j: (0, j))],
        out_specs=pl.BlockSpec((m, tn), lambda j: (0, j)),
        out_shape=jax.ShapeDtypeStruct((m, n), F32),
        compiler_params=_cparams(1),
        name="adaln_mm",
    )(a, w, b.reshape(1, n))


def _inproj_kernel(x_ref, nw_ref, sc_ref, sh_ref, scc_ref, shc_ref, w_ref, p_ref, h_ref, hs_ref, *,
                   tiles_per_batch, n_ctx):
    @pl.when(pl.program_id(1) == 0)
    def _():
        tm = x_ref.shape[0]
        rc = NORM_ROWS
        base = (pl.program_id(0) % tiles_per_batch) * tm

        def norm_rows(k, carry):
            r0 = pl.multiple_of(k * rc, rc)
            xf = x_ref[pl.ds(r0, rc), :]
            ms = jnp.mean(xf * xf, axis=-1, keepdims=True)
            y = xf * lax.rsqrt(ms + NORM_EPS) * nw_ref[...]
            is_ctx = base + r0 < n_ctx
            sc = 1.0 + jnp.where(is_ctx, scc_ref[...], sc_ref[0])
            sh = jnp.where(is_ctx, shc_ref[...], sh_ref[0])
            h = (y * sc + sh).astype(BF16)
            hs_ref[pl.ds(r0, rc), :] = h
            h_ref[pl.ds(r0, rc), :] = h
            return carry

        lax.fori_loop(0, tm // rc, norm_rows, 0, unroll=4)

    p_ref[...] = _dot(hs_ref[...], w_ref[...])


def in_projection(xs2, norm_w, scale, shift, scale_c, shift_c, w_bf16, lt, n_ctx, tn=2048):
    m, d = xs2.shape
    n = w_bf16.shape[1]
    tm = _tile_rows(lt)
    tpb = lt // tm
    nb = scale.shape[0]
    kern = functools.partial(_inproj_kernel, tiles_per_batch=tpb, n_ctx=n_ctx)
    return pl.pallas_call(
        kern,
        grid=(m // tm, n // tn),
        in_specs=[pl.BlockSpec((tm, d), lambda i, j: (i, 0)),
                  pl.BlockSpec((1, d), lambda i, j: (0, 0)),
                  pl.BlockSpec((1, 1, d), lambda i, j: (i // tpb, 0, 0)),
                  pl.BlockSpec((1, 1, d), lambda i, j: (i // tpb, 0, 0)),
                  pl.BlockSpec((1, d), lambda i, j: (0, 0)),
                  pl.BlockSpec((1, d), lambda i, j: (0, 0)),
                  pl.BlockSpec((d, tn), lambda i, j: (0, j))],
        out_specs=[pl.BlockSpec((tm, tn), lambda i, j: (i, j)),
                   pl.BlockSpec((tm, d), lambda i, j: (i, 0))],
        out_shape=[jax.ShapeDtypeStruct((m, n), F32),
                   jax.ShapeDtypeStruct((m, d), BF16)],
        scratch_shapes=[pltpu.VMEM((tm, d), BF16)],
        compiler_params=_cparams(2),
        name="in_proj",
    )(xs2, norm_w.reshape(1, d), scale.reshape(nb, 1, d), shift.reshape(nb, 1, d),
      scale_c.reshape(1, d), shift_c.reshape(1, d), w_bf16)


def _gate_up_kernel(h_ref, ya_ref, yb_ref, yc_ref, yd_ref, wg_ref, wu_ref, o_ref):
    h = h_ref[...]
    acc = None
    for i, y_ref in enumerate((ya_ref, yb_ref, yc_ref, yd_ref)):
        g = _dot(h, wg_ref[i])
        u = _dot(y_ref[...], wu_ref[i])
        t = _sigmoid(g) * u
        acc = t if acc is None else acc + t
    o_ref[...] = acc.astype(BF16)


def gate_up(h, ys, wg_bf16, wu_bf16, lt, tn=512):
    m, d = h.shape
    w = ys[0].shape[1]
    tm = _tile_rows(lt)
    y_spec = pl.BlockSpec((tm, w), lambda i, j: (i, 0))
    return pl.pallas_call(
        _gate_up_kernel,
        grid=(m // tm, d // tn),
        in_specs=[pl.BlockSpec((tm, d), lambda i, j: (i, 0)), y_spec, y_spec, y_spec, y_spec,
                  pl.BlockSpec((N_BRANCH, d, tn), lambda i, j: (0, 0, j)),
                  pl.BlockSpec((N_BRANCH, w, tn), lambda i, j: (0, 0, j))],
        out_specs=pl.BlockSpec((tm, tn), lambda i, j: (i, j)),
        out_shape=jax.ShapeDtypeStruct((m, d), BF16),
        compiler_params=_cparams(2),
        name="gate_up",
    )(h, *ys, wg_bf16, wu_bf16)


def _out_proj_kernel(a_ref, w_ref, x_ref, g_ref, gc_ref, o_ref, *, tiles_per_batch, n_ctx):
    tm = x_ref.shape[0]
    row = (pl.program_id(0) % tiles_per_batch) * tm + lax.broadcasted_iota(jnp.int32, (tm, 1), 0)
    gate = jnp.where(row < n_ctx, gc_ref[...], g_ref[0])
    o_ref[...] = x_ref[...] + gate * _dot(a_ref[...], w_ref[...])


def out_projection(acc, w_bf16, xs2, gate, gate_c, lt, n_ctx, tn=1024):
    m, d = xs2.shape
    tm = _tile_rows(lt)
    tpb = lt // tm
    nb = gate.shape[0]
    kern = functools.partial(_out_proj_kernel, tiles_per_batch=tpb, n_ctx=n_ctx)
    return pl.pallas_call(
        kern,
        grid=(m // tm, d // tn),
        in_specs=[pl.BlockSpec((tm, d), lambda i, j: (i, 0)),
                  pl.BlockSpec((d, tn), lambda i, j: (0, j)),
                  pl.BlockSpec((tm, tn), lambda i, j: (i, j)),
                  pl.BlockSpec((1, 1, tn), lambda i, j: (i // tpb, 0, j)),
                  pl.BlockSpec((1, tn), lambda i, j: (0, j))],
        out_specs=pl.BlockSpec((tm, tn), lambda i, j: (i, j)),
        out_shape=jax.ShapeDtypeStruct((m, d), F32),
        compiler_params=_cparams(2),
        name="out_proj",
    )(acc, w_bf16, xs2, gate.reshape(nb, 1, d), gate_c.reshape(1, d))


def _rope(x, cos, sin):
    w = x.shape[1]
    lane = lax.broadcasted_iota(jnp.int32, (1, w), 1)
    first = ((lane // (DA_DIM // 4)) % 2) == 0
    rot = jnp.where(first, -pltpu.roll(x, w - DA_DIM // 4, 1), pltpu.roll(x, DA_DIM // 4, 1))
    return x * cos + rot * sin


def _prep_attn_kernel(ak_ref, av_ref, dq_ref, dk_ref, dv_ref, cos_ref, sin_ref, g64_ref,
                      wak_ref, wdq_ref, wdk_ref, kn_ref, va_ref, qt_ref, kh_ref, vt_ref):
    g64 = g64_ref[...]
    cos = cos_ref[...]
    sin = sin_ref[...]
    kn_ref[0] = _head_rms(ak_ref[0], g64, NA_DIM, wak_ref[...]).astype(BF16)
    va_ref[0] = av_ref[0].astype(BF16)
    q = _rope(_head_rms(dq_ref[0], g64, DA_DIM, wdq_ref[...]), cos, sin)
    k = _rope(_head_rms(dk_ref[0], g64, DA_DIM, wdk_ref[...]), cos, sin).astype(BF16)
    qt_ref[0] = q.T.astype(BF16)
    for hc in range(2 * DA_HEADS):
        kh_ref[0, hc] = k[:, hc * DA_DIM:(hc + 1) * DA_DIM]
    v = dv_ref[0]
    dv = 2 * DA_DIM
    for h in range(DA_HEADS):
        vt_ref[0, h, 0, 0:dv, :] = v[:, h * dv:(h + 1) * dv].T.astype(BF16)
        vt_ref[0, h, 0, dv:dv + DA_ONES, :] = jnp.ones((DA_ONES, v.shape[0]), BF16)


def prep_attention(p3, cos, sin, na_k_w, da_q_w, da_k_w):
    bsz, lt, _ = p3.shape
    tm = PREP_TM
    w = BRANCH_W
    tile = lambda v, s=1.0: (jnp.tile(v, w // v.shape[0]) * s).reshape(1, w)
    tab_spec = pl.BlockSpec((tm, w), lambda b, i: (i, 0))
    vec = _const_spec((1, w))
    return pl.pallas_call(
        _prep_attn_kernel,
        grid=(bsz, lt // tm),
        in_specs=[_row_spec(CB_AK), _row_spec(CB_AV), _row_spec(CB_DQ), _row_spec(CB_DK), _row_spec(CB_DV),
                  tab_spec, tab_spec, _const_spec((w, w)), vec, vec, vec],
        out_specs=[pl.BlockSpec((1, tm, w), lambda b, i: (b, i, 0)),
                   pl.BlockSpec((1, tm, w), lambda b, i: (b, i, 0)),
                   pl.BlockSpec((1, w, tm), lambda b, i: (b, 0, i)),
                   pl.BlockSpec((1, 2 * DA_HEADS, tm, DA_DIM), lambda b, i: (b, 0, i, 0)),
                   pl.BlockSpec((1, DA_HEADS, 1, 2 * DA_DIM + DA_ONES, tm), lambda b, i: (b, 0, i, 0, 0))],
        out_shape=[jax.ShapeDtypeStruct((bsz, lt, w), BF16),
                   jax.ShapeDtypeStruct((bsz, lt, w), BF16),
                   jax.ShapeDtypeStruct((bsz, w, lt), BF16),
                   jax.ShapeDtypeStruct((bsz, 2 * DA_HEADS, lt, DA_DIM), BF16),
                   jax.ShapeDtypeStruct((bsz, DA_HEADS, lt // tm, 2 * DA_DIM + DA_ONES, tm), BF16)],
        compiler_params=_cparams(2),
        name="prep_attn",
    )(p3, p3, p3, p3, p3, cos, sin, _group_matrix(w, DA_DIM),
      tile(na_k_w), tile(da_q_w, DA_DIM ** -0.5 * math.log2(math.e)), tile(da_k_w))


def rope_tables(lt, n_ctx):
    nf = DA_DIM // 4
    t = jnp.arange(lt - n_ctx, dtype=jnp.int32)
    rows, cols = t // GRID_W, t % GRID_W
    inv = ROPE_BASE ** (-jnp.arange(nf, dtype=F32) / nf)
    ang_r = rows.astype(F32)[:, None] * inv
    ang_c = cols.astype(F32)[:, None] * inv
    ang = jnp.concatenate([ang_r, ang_r, ang_c, ang_c], axis=-1)
    ang = jnp.concatenate([jnp.zeros((n_ctx, DA_DIM), F32), ang], axis=0)
    reps = BRANCH_W // DA_DIM
    return jnp.tile(jnp.cos(ang), (1, reps)), jnp.tile(jnp.sin(ang), (1, reps))


def _flash_kernel(lam_ref, qt_ref, k_ref, vt_ref, g_ref, sw_ref, o_ref, *, n_latent_iters, out_scale):
    tq = qt_ref.shape[2]
    dve = vt_ref.shape[3]
    dv = dve - DA_ONES
    tk = vt_ref.shape[4]
    qts = [qt_ref[0, c * DA_DIM:(c + 1) * DA_DIM, :] for c in range(2)]
    gate = _silu(g_ref[0])

    def attend(carry, first_chunk, n_chunks):
        chunks = [first_chunk + g for g in range(n_chunks)]
        sts = [[_dot(k_ref[0, c, pl.ds(pl.multiple_of(ch * tk, tk), tk), :], qts[c]) for c in range(2)]
               for ch in chunks]
        carry = list(carry)
        for g, ch in enumerate(chunks):
            vt = vt_ref[0, 0, ch]
            for c in range(2):
                m, acc = carry[c]
                st = sts[g][c].astype(BF16)
                m_new = jnp.maximum(m, jnp.max(st, axis=0, keepdims=True).astype(F32))
                pt = jnp.exp2(st - m_new.astype(BF16))
                carry[c] = (m_new, jnp.exp2(m - m_new) * acc + _dot(vt, pt))
        return tuple(carry)

    init = tuple((jnp.full((1, tq), NEG_INF, F32), jnp.zeros((dve, tq), F32)) for _ in range(2))
    carry = attend(init, 0, 1)
    n_iters = jnp.where(pl.program_id(2) == 0, 0, n_latent_iters)
    res = lax.fori_loop(0, n_iters, lambda i, cr: attend(cr, 1 + i * DA_NSUB, DA_NSUB), carry)
    outs = [acc[0:dv] * (1.0 / acc[dv:dv + 1]) for (_, acc) in res]
    ot = outs[0] - lam_ref[0] * outs[1]
    ot = ot * lax.rsqrt(jnp.mean(ot * ot, axis=0, keepdims=True) + NORM_EPS)
    o = ot.T * (sw_ref[...] * out_scale)
    o_ref[0] = (o * gate).astype(BF16)


def flash_diff_attention(qt, kh, vt, p3, lam, subln_w, lam_init):
    bsz, w, lt = qt.shape
    dv = 2 * DA_DIM
    nk = lt // DA_TK
    n_latent_iters = (nk - 1) // DA_NSUB
    kern = functools.partial(_flash_kernel, n_latent_iters=n_latent_iters, out_scale=1.0 - lam_init)
    g_blocks = BRANCH_W // dv
    return pl.pallas_call(
        kern,
        grid=(bsz, DA_HEADS, lt // DA_TQ),
        in_specs=[pl.BlockSpec(memory_space=pltpu.SMEM),
                  pl.BlockSpec((1, dv, DA_TQ), lambda b, h, qi: (b, h, qi)),
                  pl.BlockSpec((1, 2, lt, DA_DIM), lambda b, h, qi: (b, h, 0, 0)),
                  pl.BlockSpec((1, 1, nk, dv + DA_ONES, DA_TK), lambda b, h, qi: (b, h, 0, 0, 0)),
                  pl.BlockSpec((1, DA_TQ, dv), lambda b, h, qi: (b, qi, CB_DG * g_blocks + h)),
                  _const_spec((1, dv))],
        out_specs=pl.BlockSpec((1, DA_TQ, dv), lambda b, h, qi: (b, qi, h)),
        out_shape=jax.ShapeDtypeStruct((bsz, lt, BRANCH_W), BF16),
        compiler_params=_cparams(3),
        name="flash_attn",
    )(lam.reshape(1).astype(F32), qt, kh, vt, p3, subln_w.reshape(1, dv))


def _na_kernel(q_ref, g_ref, k_ref, v_ref, *rest, n_rows, n_ctx):
    bias_refs = rest[:NA_NBLK]
    g64_ref, wq_ref, o_ref = rest[NA_NBLK:]
    n_loc = NA_WIN_R * GRID_W
    q = _head_rms(q_ref[0], g64_ref[...], NA_DIM, wq_ref[...]).astype(BF16)
    g = g_ref[0]
    pair = [slice(hp * LANES, (hp + 1) * LANES) for hp in range(NA_HEADS // 2)]
    low_half = lax.broadcasted_iota(jnp.int32, (1, LANES), 1) < NA_DIM
    rows, starts, units = [], [], []
    for j in range(NA_NBLK):
        blk = pl.program_id(1) * NA_NBLK + j
        r = jnp.maximum(blk - n_ctx // GRID_W, 0)
        r0 = jnp.clip(r - NA_WIN_R // 2, 0, n_rows - NA_WIN_R)
        starts.append(pl.multiple_of(n_ctx + r0 * GRID_W, GRID_W))
        rows.append(slice(j * GRID_W, (j + 1) * GRID_W))
        units += [(j, hp) for hp in range(NA_HEADS // 2)]
    zero = jnp.zeros((), BF16)
    cat = lambda x, y: jnp.concatenate([x, y], axis=0)
    q2 = [cat(jnp.where(low_half, q[rows[j], pair[hp]], zero), jnp.where(low_half, zero, q[rows[j], pair[hp]]))
          for j, hp in units]
    s_loc = [_dot_nt(q2[u], k_ref[0, pl.ds(starts[j], n_loc), pair[hp]])
             + cat(bias_refs[j][0, 2 * hp], bias_refs[j][0, 2 * hp + 1]) for u, (j, hp) in enumerate(units)]
    s_ctx = [_dot_nt(q2[u], k_ref[0, 0:n_ctx, pair[hp]]) for u, (j, hp) in enumerate(units)]
    us = range(len(units))
    m = [jnp.maximum(jnp.max(s_loc[u], axis=-1, keepdims=True), jnp.max(s_ctx[u], axis=-1, keepdims=True))
         for u in us]
    p_loc = [jnp.exp(s_loc[u] - m[u]) for u in us]
    p_ctx = [jnp.exp(s_ctx[u] - m[u]) for u in us]
    l = [jnp.sum(p_loc[u], axis=-1, keepdims=True) + jnp.sum(p_ctx[u], axis=-1, keepdims=True) for u in us]
    o = [(_dot(p_loc[u].astype(BF16), v_ref[0, pl.ds(starts[j], n_loc), pair[hp]])
          + _dot(p_ctx[u].astype(BF16), v_ref[0, 0:n_ctx, pair[hp]])) / l[u]
         for u, (j, hp) in enumerate(units)]
    for u, (j, hp) in enumerate(units):
        o_pair = jnp.where(low_half, o[u][0:GRID_W], o[u][GRID_W:2 * GRID_W])
        o_ref[0, rows[j], pair[hp]] = (o_pair * _silu(g[rows[j], pair[hp]])).astype(BF16)


def neighbourhood_attention(p3, kn, va, bias_tbl, na_q_w, n_ctx):
    bsz, lt, w = kn.shape
    n_rows = (lt - n_ctx) // GRID_W
    ncb = n_ctx // GRID_W
    half = NA_WIN_R // 2

    def bias_spec(j):
        def idx(b, s):
            blk = s * NA_NBLK + j
            r = blk - ncb
            off = r - jnp.clip(r - half, 0, n_rows - NA_WIN_R)
            return (jnp.where(blk < ncb, NA_WIN_R, off), 0, 0, 0)
        return pl.BlockSpec((1, NA_HEADS, GRID_W, NA_WIN_R * GRID_W), idx)

    tq = NA_NBLK * GRID_W
    full_spec = pl.BlockSpec((1, lt, w), lambda b, s: (b, 0, 0))
    wq = (jnp.tile(na_q_w, w // NA_DIM) * NA_DIM ** -0.5).reshape(1, w)
    return pl.pallas_call(
        functools.partial(_na_kernel, n_rows=n_rows, n_ctx=n_ctx),
        grid=(bsz, lt // tq),
        in_specs=[_row_spec(CB_AQ, tq), _row_spec(CB_AG, tq), full_spec, full_spec,
                  *[bias_spec(j) for j in range(NA_NBLK)],
                  _const_spec((w, w)), _const_spec((1, w))],
        out_specs=pl.BlockSpec((1, tq, w), lambda b, s: (b, s, 0)),
        out_shape=jax.ShapeDtypeStruct((bsz, lt, w), BF16),
        compiler_params=_cparams(2),
        name="nbr_attn",
    )(p3, p3, kn, va, *([bias_tbl] * NA_NBLK), _group_matrix(w, NA_DIM), wq)


def na_bias_table(rpb):
    col = np.arange(GRID_W)
    c0 = np.clip(col - NA_WIN_C // 2, 0, GRID_W - NA_WIN_C)
    col_ok = (col[None, :] >= c0[:, None]) & (col[None, :] < c0[:, None] + NA_WIN_C)
    d_col = np.clip(col[None, :] - col[:, None] + (NA_WIN_C - 1), 0, 2 * NA_WIN_C - 2)
    onehot = (d_col[:, :, None] == np.arange(2 * NA_WIN_C - 1)).astype(np.float32)
    by_col = jnp.einsum('hrc,qwc->hrqw', rpb.astype(F32), jnp.asarray(onehot),
                        precision=lax.Precision.HIGHEST)
    by_col = jnp.where(jnp.asarray(col_ok)[None, None], by_col, NEG_INF)
    tbl = jnp.stack([by_col[:, NA_WIN_R - 1 - o:2 * NA_WIN_R - 1 - o] for o in range(NA_WIN_R)]
                    + [jnp.full((NA_HEADS, NA_WIN_R, GRID_W, GRID_W), NEG_INF, F32)])
    tbl = jnp.transpose(tbl, (0, 1, 3, 2, 4))
    return tbl.reshape(NA_WIN_R + 1, NA_HEADS, GRID_W, NA_WIN_R * GRID_W)


def _prep_rwkv_kernel(r_ref, rp_ref, rn_ref, k_ref, kp_ref, kn_ref, v_ref, vp_ref, vn_ref,
                      m_ref, mp_ref, mn_ref, mu_ref, w0_ref, a0_ref, w2_ref, a2_ref, kk_ref, ka_ref,
                      rk_ref, g64_ref,
                      ro_ref, vo_ref, ao_ref, lw_ref, kd_ref, bo_ref, bonus_ref, *, n_tiles, n_ctx_tiles):
    i = pl.program_id(1)
    w = BRANCH_W

    def shifted(x_ref, p_ref, n_ref, mu):
        x = x_ref[0]
        xp, xn = _neighbours(x, p_ref[0], n_ref[0], i, n_tiles, n_ctx_tiles)
        return x + (0.5 * (xp + xn) - x) * mu

    r = shifted(r_ref, rp_ref, rn_ref, mu_ref[0:1, :])
    k = shifted(k_ref, kp_ref, kn_ref, mu_ref[1:2, :])
    v = shifted(v_ref, vp_ref, vn_ref, mu_ref[2:3, :])
    misc = shifted(m_ref, mp_ref, mn_ref, mu_ref[3:4, :])
    wd = jnp.tanh(misc[:, 0:2 * RW_LORA_W]).astype(BF16)
    ad = misc[:, 2 * RW_LORA_W:MISC_DT].astype(BF16)
    w_log = w0_ref[...] + _dot(wd, w2_ref[...])
    gate = _sigmoid(a0_ref[...] + _dot(ad, a2_ref[...]))
    log_decay = -math.exp(-0.5) * _sigmoid(w_log)
    g64 = g64_ref[...]
    kk = k * kk_ref[...]
    kk = kk / jnp.maximum(jnp.sqrt(_group_sum(kk * kk, g64)), 1e-12)
    ro_ref[0] = r
    vo_ref[0] = v.astype(BF16)
    ao_ref[0] = -kk
    coef = None
    for d in range(2):
        a_d = gate[:, d * w:(d + 1) * w]
        kd = k * (1.0 + (a_d - 1.0) * ka_ref[...])
        lw_ref[d, 0] = log_decay[:, d * w:(d + 1) * w]
        kd_ref[d, 0] = kd
        bo_ref[d, 0] = kk * a_d
        coef = kd if coef is None else coef + kd
    bonus_ref[0] = _group_sum(r * coef * rk_ref[...], g64) * v


def prep_rwkv(p3, mu, w0, w2, a0, a2, k_k, k_a, r_k, n_ctx):
    bsz, lt, _ = p3.shape
    tm = PREP_TM
    w = BRANCH_W
    n_tiles = lt // tm
    mu4 = jnp.stack([mu[0:w], mu[w:2 * w], mu[2 * w:3 * w],
                     jnp.pad(mu[3 * w:], (0, w - (mu.shape[0] - 3 * w)))])
    zero = jnp.zeros((RW_LORA_W, w), F32)
    w2cat = jnp.concatenate([jnp.concatenate([w2[0], zero], axis=1),
                             jnp.concatenate([zero, w2[1]], axis=1)], axis=0).astype(BF16)
    a2cat = jnp.concatenate([jnp.concatenate([a2[0], zero], axis=1),
                             jnp.concatenate([zero, a2[1]], axis=1)], axis=0).astype(BF16)
    specs = []
    for cb in (CB_BR, CB_BK, CB_BV, CB_MISC):
        specs += [_row_spec(cb), *_halo_specs(cb, tm, lt)]
    vec = _const_spec((1, w))
    tok = pl.BlockSpec((1, tm, w), lambda b, i: (b, i, 0))
    tok2 = pl.BlockSpec((2, 1, tm, w), lambda b, i: (0, b, i, 0))
    kern = functools.partial(_prep_rwkv_kernel, n_tiles=n_tiles, n_ctx_tiles=n_ctx // tm)
    return pl.pallas_call(
        kern,
        grid=(bsz, n_tiles),
        in_specs=specs + [_const_spec((4, w)), _const_spec((1, 2 * w)), _const_spec((1, 2 * w)),
                          _const_spec((2 * RW_LORA_W, 2 * w)), _const_spec((2 * RW_LORA_A, 2 * w)),
                          vec, vec, vec, _const_spec((w, w))],
        out_specs=[tok, tok, tok, tok2, tok2, tok2, tok],
        out_shape=[jax.ShapeDtypeStruct((bsz, lt, w), F32),
                   jax.ShapeDtypeStruct((bsz, lt, w), BF16),
                   jax.ShapeDtypeStruct((bsz, lt, w), F32),
                   jax.ShapeDtypeStruct((2, bsz, lt, w), F32),
                   jax.ShapeDtypeStruct((2, bsz, lt, w), F32),
                   jax.ShapeDtypeStruct((2, bsz, lt, w), F32),
                   jax.ShapeDtypeStruct((bsz, lt, w), F32)],
        compiler_params=_cparams(2),
        name="prep_rwkv",
    )(*([p3] * 12), mu4, w0.reshape(1, 2 * w), a0.reshape(1, 2 * w), w2cat, a2cat,
      k_k.reshape(1, w), k_a.reshape(1, w), r_k.reshape(1, w), _group_matrix(w, RW_DIM))


def _rwkv_kernel(tri_ref, mq_ref, r_ref, v_ref, a_ref, lw_ref, kd_ref, b_ref, y_ref, s_ref):
    c = RW_CHUNK
    nsub = r_ref.shape[1] // c
    d = pl.program_id(0)

    @pl.when(pl.program_id(2) == 0)
    def _():
        s_ref[...] = jnp.zeros_like(s_ref)

    tri = tri_ref[0]
    m_quad = mq_ref[0] > 0.5
    rows = lax.broadcasted_iota(jnp.int32, (c, c), 0)
    cols = lax.broadcasted_iota(jnp.int32, (c, c), 1)
    eye = (rows == cols).astype(F32)
    same_half = (rows >= c // 2) == (cols >= c // 2)
    heads = [slice(h * RW_DIM, (h + 1) * RW_DIM) for h in range(RW_HEADS)]

    row_sl, e_tot = [], []
    al, rh, rf, be, ka, bc, kc, vb = [], [], [], [], [], [], [], []
    for j in range(nsub):
        off = pl.multiple_of(jnp.where(d == 0, j, nsub - 1 - j) * c, c)
        rs = pl.ds(off, c)
        row_sl.append(rs)
        lw = lw_ref[0, 0, rs, :]
        cum = _dot_exact_lhs(tri, lw)
        tot = jnp.sum(lw, axis=0, keepdims=True)
        e_m = jnp.exp(-cum)
        e_t = jnp.exp(tot - cum)
        e_tot.append(jnp.exp(tot))
        b_in = b_ref[0, 0, rs, :]
        k_in = kd_ref[0, 0, rs, :]
        alpha = (a_ref[0, rs, :] * jnp.exp(cum - lw)).astype(BF16)
        rho_f = r_ref[0, rs, :] * jnp.exp(cum)
        rho = rho_f.astype(BF16)
        beta = (b_in * e_m).astype(BF16)
        kappa = (k_in * e_m).astype(BF16)
        beta_c = (b_in * e_t).astype(BF16)
        kappa_c = (k_in * e_t).astype(BF16)
        v_b = v_ref[0, rs, :]
        for sl in heads:
            al.append(alpha[:, sl])
            rh.append(rho[:, sl])
            rf.append(rho_f[:, sl])
            be.append(beta[:, sl])
            ka.append(kappa[:, sl])
            bc.append(beta_c[:, sl])
            kc.append(kappa_c[:, sl])
            vb.append(v_b[:, sl])

    units = range(nsub * RW_HEADS)
    cat = lambda x, y: jnp.concatenate([x, y], axis=0)
    zeros = jnp.zeros((c, RW_DIM), BF16)
    prod = [jnp.where(m_quad, _dot_nt(cat(al[u], rh[u]), cat(be[u], ka[u])), 0.0) for u in units]
    l_ab = [prod[u][0:c, 0:c] for u in units]
    top = [prod[u][0:c].astype(BF16) for u in units]
    bot = [prod[u][c:2 * c].astype(BF16) for u in units]
    l_d = [jnp.where(same_half, l_ab[u], 0.0) for u in units]
    l_o = [(l_ab[u] - l_d[u]).astype(BF16) for u in units]
    ld_b = [l_d[u].astype(BF16) for u in units]
    pw = [_dot(ld_b[u], ld_b[u]).astype(BF16) for u in units]
    td = [eye + l_d[u] for u in units]
    for _ in range(int(math.log2(c)) - 3):
        both = [_dot(cat(td[u].astype(BF16), pw[u]), pw[u]) for u in units]
        td = [td[u] + both[u][0:c] for u in units]
        pw = [both[u][c:2 * c].astype(BF16) for u in units]
    td = [td[u] + _dot(td[u].astype(BF16), pw[u]) for u in units]
    td_b = [td[u].astype(BF16) for u in units]
    x_o = [_dot(td_b[u], l_o[u]).astype(BF16) for u in units]
    tinv = [(td[u] + _dot(x_o[u], td_b[u])).astype(BF16) for u in units]
    akv = [_dot(top[u], cat(zeros, vb[u])).astype(BF16) for u in units]
    lcat = lambda x, y: jnp.concatenate([x, y], axis=1)
    av = [_dot(tinv[u], lcat(al[u], akv[u])).astype(BF16) for u in units]
    ry = [_dot(bot[u], cat(av[u], lcat(zeros, vb[u]))) for u in units]
    r_hat = [(rf[u] + ry[u][:, 0:RW_DIM]).astype(BF16) for u in units]
    y_hat = [ry[u][:, RW_DIM:2 * RW_DIM] for u in units]
    qn = [_dot_tn(av[u], bc[u]) for u in units]
    q_mat = [qn[u][0:RW_DIM].astype(BF16) for u in units]
    n_mat = [qn[u][RW_DIM:2 * RW_DIM] + _dot_tn(vb[u], kc[u]) for u in units]

    state = [s_ref[h] for h in range(RW_HEADS)]
    for j in range(nsub):
        for h, sl in enumerate(heads):
            u = j * RW_HEADS + h
            s_b = state[h].astype(BF16)
            y_ref[0, 0, row_sl[j], sl] = _dot_nt(r_hat[u], s_b) + y_hat[u]
            state[h] = state[h] * e_tot[j][:, sl] + _dot(s_b, q_mat[u]) + n_mat[u]
    for h in range(RW_HEADS):
        s_ref[h] = state[h]


def _scan_chunk_index(d, s, n_ctx_chunks, n_chunks):
    rev = jnp.where(s < n_ctx_chunks, n_ctx_chunks - 1 - s, n_chunks + n_ctx_chunks - 1 - s)
    return jnp.where(d == 0, s, rev)


def _direction_masks(c):
    i = np.arange(c)
    lower = (i[None, :] <= i[:, None]).astype(np.float32)
    tri = np.stack([lower, lower.T])
    strict = np.stack([lower - np.eye(c, dtype=np.float32), lower.T - np.eye(c, dtype=np.float32)])
    return tri, strict


def rwkv_scan(r, v, a, lw, kd, b, n_ctx):
    bsz, lt, w = r.shape
    c = RW_CHUNK
    blk = RW_NSUB * c
    nck = lt // blk
    ncc = n_ctx // blk
    tri, strict = _direction_masks(c)
    shared = pl.BlockSpec((1, blk, w), lambda d, bi, s: (bi, _scan_chunk_index(d, s, ncc, nck), 0))
    perdir = pl.BlockSpec((1, 1, blk, w), lambda d, bi, s: (d, bi, _scan_chunk_index(d, s, ncc, nck), 0))
    mask_spec = pl.BlockSpec((1, c, c), lambda d, bi, s: (d, 0, 0))
    quad_spec = pl.BlockSpec((1, 2 * c, 2 * c), lambda d, bi, s: (d, 0, 0))
    quad = np.concatenate([np.tile(strict, (1, 1, 2)), np.tile(tri, (1, 1, 2))], axis=1)
    return pl.pallas_call(
        _rwkv_kernel,
        grid=(2, bsz, nck),
        in_specs=[mask_spec, quad_spec, shared, shared, shared, perdir, perdir, perdir],
        out_specs=perdir,
        out_shape=jax.ShapeDtypeStruct((2, bsz, lt, w), F32),
        scratch_shapes=[pltpu.VMEM((RW_HEADS, RW_DIM, RW_DIM), F32)],
        compiler_params=_cparams(3),
        name="rwkv_scan",
    )(jnp.asarray(tri, BF16), jnp.asarray(quad, F32), r, v, a, lw, kd, b)


def _finish_rwkv_kernel(y_ref, bonus_ref, g_ref, lnw_ref, lnb_ref, g64_ref, o_ref):
    y = y_ref[0, 0] + y_ref[1, 0]
    g64 = g64_ref[...]
    mean = _dot_exact_rhs(y, g64) * (1.0 / RW_DIM)
    yc = y - mean
    var = _group_sum(yc * yc, g64) * (1.0 / RW_DIM)
    yn = yc * lax.rsqrt(var + RW_GN_EPS) * lnw_ref[...] + lnb_ref[...]
    o_ref[0] = ((yn + bonus_ref[0]) * _silu(g_ref[0])).astype(BF16)


def finish_rwkv(y, bonus, p3, ln_w, ln_b):
    _, bsz, lt, w = y.shape
    tm = _tile_rows(lt)
    tok = pl.BlockSpec((1, tm, w), lambda b, i: (b, i, 0))
    vec = _const_spec((1, w))
    return pl.pallas_call(
        _finish_rwkv_kernel,
        grid=(bsz, lt // tm),
        in_specs=[pl.BlockSpec((2, 1, tm, w), lambda b, i: (0, b, i, 0)), tok, _row_spec(CB_BG, tm),
                  vec, vec, _const_spec((w, w))],
        out_specs=tok,
        out_shape=jax.ShapeDtypeStruct((bsz, lt, w), BF16),
        compiler_params=_cparams(2),
        name="finish_rwkv",
    )(y, bonus, p3, ln_w.reshape(1, w), ln_b.reshape(1, w), _group_matrix(w, RW_DIM))


def _prep_ssd_kernel(x_ref, xp_ref, xn_ref, bc_ref, bcp_ref, bcn_ref, m_ref, cw_ref, cb_ref, dtb_ref,
                     aneg_ref, exp_ref, dsk_ref,
                     xq_ref, bco_ref, bt_ref, a_ref, at_ref, dskip_ref, *, n_tiles, n_ctx_tiles):
    i = pl.program_id(1)
    w = BRANCH_W

    def conv(x_ref, p_ref, n_ref, half):
        x = x_ref[0]
        xp, xn = _neighbours(x, p_ref[0], n_ref[0], i, n_tiles, n_ctx_tiles)
        lo = half * w
        y = (xp * cw_ref[0:1, lo:lo + w] + x * cw_ref[1:2, lo:lo + w] + xn * cw_ref[2:3, lo:lo + w]
             + cb_ref[:, lo:lo + w])
        return _silu(y)

    xs = conv(x_ref, xp_ref, xn_ref, 0)
    bc = conv(bc_ref, bcp_ref, bcn_ref, 1)
    bco_ref[0] = bc.astype(BF16)
    bm_f = bc[:, 0:SSM_GROUPS * SSM_STATE]
    q = SSM_CHUNK
    for j in range(bc.shape[0] // q):
        bt_ref[0, j] = bm_f[j * q:(j + 1) * q, :].T.astype(BF16)
    dt = _softplus(m_ref[0][:, MISC_DT:MISC_DT + LANES] + dtb_ref[...])
    lane = lax.broadcasted_iota(jnp.int32, (1, LANES), 1)
    dt = jnp.where(lane < 2 * SSM_HEADS, dt, 0.0)
    dtx = _dot_exact_rhs(dt, exp_ref[...])
    a_all = dt * aneg_ref[...]
    first = lane < SSM_HEADS
    a_dirs = [jnp.where(first, a_all, 0.0), jnp.where(first, pltpu.roll(a_all, LANES - SSM_HEADS, 1), 0.0)]
    for d in range(2):
        xq_ref[d, 0] = xs * dtx[:, d * w:(d + 1) * w]
        a_ref[d, 0] = a_dirs[d]
        for j in range(bc.shape[0] // q):
            at_ref[d, 0, j] = a_dirs[d][j * q:(j + 1) * q, :].T[0:2 * SUBLANES, :]
    dskip_ref[0] = xs * dsk_ref[...]


def prep_ssd(p3, conv_w, conv_b, dt_bias, a_log, d_skip, n_ctx):
    bsz, lt, _ = p3.shape
    tm = PREP_TM
    w = BRANCH_W
    n_tiles = lt // tm
    nh2 = 2 * SSM_HEADS
    pad_lanes = lambda v: jnp.pad(v.reshape(1, nh2), ((0, 0), (0, LANES - nh2)))
    expand = np.zeros((LANES, 2 * w), np.float32)
    for d in range(2):
        for h in range(SSM_HEADS):
            expand[d * SSM_HEADS + h, d * w + h * SSM_HEAD_DIM:d * w + (h + 1) * SSM_HEAD_DIM] = 1.0
    specs = [_row_spec(CB_CX), *_halo_specs(CB_CX, tm, lt), _row_spec(CB_CBC), *_halo_specs(CB_CBC, tm, lt),
             _row_spec(CB_MISC)]
    tok = pl.BlockSpec((1, tm, w), lambda b, i: (b, i, 0))
    gs = SSM_GROUPS * SSM_STATE
    kern = functools.partial(_prep_ssd_kernel, n_tiles=n_tiles, n_ctx_tiles=n_ctx // tm)
    return pl.pallas_call(
        kern,
        grid=(bsz, n_tiles),
        in_specs=specs + [_const_spec((3, 2 * w)), _const_spec((1, 2 * w)), _const_spec((1, LANES)),
                          _const_spec((1, LANES)), _const_spec((LANES, 2 * w)), _const_spec((1, w))],
        out_specs=[pl.BlockSpec((2, 1, tm, w), lambda b, i: (0, b, i, 0)), tok,
                   pl.BlockSpec((1, tm // SSM_CHUNK, gs, SSM_CHUNK), lambda b, i: (b, i, 0, 0)),
                   pl.BlockSpec((2, 1, tm, LANES), lambda b, i: (0, b, i, 0)),
                   pl.BlockSpec((2, 1, tm // SSM_CHUNK, 2 * SUBLANES, SSM_CHUNK), lambda b, i: (0, b, i, 0, 0)), tok],
        out_shape=[jax.ShapeDtypeStruct((2, bsz, lt, w), F32),
                   jax.ShapeDtypeStruct((bsz, lt, w), BF16),
                   jax.ShapeDtypeStruct((bsz, lt // SSM_CHUNK, gs, SSM_CHUNK), BF16),
                   jax.ShapeDtypeStruct((2, bsz, lt, LANES), F32),
                   jax.ShapeDtypeStruct((2, bsz, lt // SSM_CHUNK, 2 * SUBLANES, SSM_CHUNK), F32),
                   jax.ShapeDtypeStruct((bsz, lt, w), F32)],
        compiler_params=_cparams(2),
        name="prep_ssd",
    )(*([p3] * 7), conv_w, conv_b.reshape(1, 2 * w), pad_lanes(dt_bias), pad_lanes(-jnp.exp(a_log)),
      jnp.asarray(expand, BF16), jnp.repeat(d_skip, SSM_HEAD_DIM).reshape(1, w))


def _ssd_kernel(tri_ref, mi_ref, xq_ref, bc_ref, bt_ref, a_ref, at_ref, y_ref, s_ref):
    q = SSM_CHUNK
    nsub = bt_ref.shape[1]
    rep = SSM_HEADS // SSM_GROUPS
    gw = SSM_GROUPS * SSM_STATE
    p = SSM_HEAD_DIM
    d = pl.program_id(0)

    @pl.when(pl.program_id(2) == 0)
    def _():
        s_ref[...] = jnp.zeros_like(s_ref)

    tri = tri_ref[0]
    mask = mi_ref[0] > 0.5
    hs = range(SSM_HEADS)
    groups = [slice(g * SSM_STATE, (g + 1) * SSM_STATE) for g in range(SSM_GROUPS)]

    rows, cms, y_in, e_col, e_tot, upd = [], [], [], [], [], []
    for j in range(nsub):
        ch = jnp.where(d == 0, j, nsub - 1 - j)
        rs = pl.ds(pl.multiple_of(ch * q, q), q)
        rows.append(rs)
        a = a_ref[0, 0, rs, :]
        a_t = at_ref[0, 0, ch]
        xq = xq_ref[0, 0, rs, :]
        bm = bc_ref[0, rs, 0:gw]
        cm = bc_ref[0, rs, gw:2 * gw]
        bt = bt_ref[0, ch]
        acol = _dot_exact_lhs(tri, a)
        arow = _dot_nt_exact_lhs_f32(a_t, tri)
        tot = jnp.sum(a, axis=0, keepdims=True)
        cm_g = [cm[:, gs] for gs in groups]
        cb = [_dot_nt(cm_g[g], bm[:, gs]) for g, gs in enumerate(groups)]
        ac = [jnp.broadcast_to(acol[:, h:h + 1], (q, q)) for h in hs]
        ar = [jnp.broadcast_to(arow[h:h + 1, :], (q, q)) for h in hs]
        gmat = [(cb[h // rep] * jnp.exp(jnp.where(mask, ac[h] - ar[h], NEG_INF))).astype(BF16) for h in hs]
        xh = [xq[:, h * p:(h + 1) * p] for h in hs]
        tot_h = [jnp.broadcast_to(tot[:, h:h + 1], (1, p)) for h in hs]
        xd = [(xh[h] * jnp.exp(tot_h[h] - ac[h][:, :p])).astype(BF16) for h in hs]
        cms.append(cm_g)
        y_in.append([_dot(gmat[h], xh[h].astype(BF16)) for h in hs])
        e_col.append([jnp.exp(ac[h][:, :p]) for h in hs])
        e_tot.append([jnp.exp(tot_h[h]) for h in hs])
        upd.append([_dot(bt[groups[h // rep], :], xd[h]) for h in hs])

    state = [s_ref[h] for h in hs]
    for j in range(nsub):
        y_st = [_dot(cms[j][h // rep], state[h].astype(BF16)) for h in hs]
        for h in hs:
            y_ref[0, 0, rows[j], h * p:(h + 1) * p] = y_in[j][h] + y_st[h] * e_col[j][h]
            state[h] = state[h] * e_tot[j][h] + upd[j][h]
    for h in hs:
        s_ref[h] = state[h]


def ssd_scan(xq, bc, bt, a, a_t, n_ctx):
    _, bsz, lt, w = xq.shape
    q = SSM_CHUNK
    blk = SSM_NSUB * q
    nck = lt // blk
    ncc = n_ctx // blk
    gw = bt.shape[2]
    tri, _ = _direction_masks(q)

    def cidx(d, s):
        return _scan_chunk_index(d, s, ncc, nck)

    mask_spec = pl.BlockSpec((1, q, q), lambda d, bi, s: (d, 0, 0))
    return pl.pallas_call(
        _ssd_kernel,
        grid=(2, bsz, nck),
        in_specs=[mask_spec, mask_spec,
                  pl.BlockSpec((1, 1, blk, w), lambda d, bi, s: (d, bi, cidx(d, s), 0)),
                  pl.BlockSpec((1, blk, w), lambda d, bi, s: (bi, cidx(d, s), 0)),
                  pl.BlockSpec((1, SSM_NSUB, gw, q), lambda d, bi, s: (bi, cidx(d, s), 0, 0)),
                  pl.BlockSpec((1, 1, blk, LANES), lambda d, bi, s: (d, bi, cidx(d, s), 0)),
                  pl.BlockSpec((1, 1, SSM_NSUB, 2 * SUBLANES, q), lambda d, bi, s: (d, bi, cidx(d, s), 0, 0))],
        out_specs=pl.BlockSpec((1, 1, blk, w), lambda d, bi, s: (d, bi, cidx(d, s), 0)),
        out_shape=jax.ShapeDtypeStruct((2, bsz, lt, w), F32),
        scratch_shapes=[pltpu.VMEM((SSM_HEADS, SSM_STATE, SSM_HEAD_DIM), F32)],
        compiler_params=_cparams(3),
        name="ssd_scan",
    )(jnp.asarray(tri, BF16), jnp.asarray(tri, F32), xq, bc, bt, a, a_t)


def _finish_ssd_kernel(y_ref, dskip_ref, z_ref, nw_ref, g256_ref, o_ref):
    y = y_ref[0, 0] + y_ref[1, 0] + dskip_ref[0]
    g = y * _silu(z_ref[0])
    group = BRANCH_W // SSM_GROUPS
    o_ref[0] = _head_rms(g, g256_ref[...], group, nw_ref[...]).astype(BF16)


def finish_ssd(y, dskip, p3, norm_w):
    _, bsz, lt, w = y.shape
    tm = _tile_rows(lt)
    tok = pl.BlockSpec((1, tm, w), lambda b, i: (b, i, 0))
    return pl.pallas_call(
        _finish_ssd_kernel,
        grid=(bsz, lt // tm),
        in_specs=[pl.BlockSpec((2, 1, tm, w), lambda b, i: (0, b, i, 0)), tok, _row_spec(CB_CZ, tm),
                  _const_spec((1, w)), _const_spec((w, w))],
        out_specs=tok,
        out_shape=jax.ShapeDtypeStruct((bsz, lt, w), BF16),
        compiler_params=_cparams(2),
        name="finish_ssd",
    )(y, dskip, p3, norm_w.reshape(1, w), _group_matrix(w, w // SSM_GROUPS))


def _permute_w_in(w_in_l):
    w = BRANCH_W
    b0 = 4 * w
    lora0 = b0 + 3 * w
    bg0 = lora0 + MISC_DT
    c0 = bg0 + w
    dt0 = c0 + SSM_CONV_CH
    z0 = dt0 + 2 * SSM_HEADS
    d0 = z0 + w
    end = d0 + 4 * w
    parts = [w_in_l[:, 0:lora0], w_in_l[:, bg0:c0], w_in_l[:, c0:dt0], w_in_l[:, z0:d0], w_in_l[:, d0:end],
             w_in_l[:, lora0:bg0], w_in_l[:, dt0:z0]]
    used = sum(p.shape[1] for p in parts)
    pad = jnp.zeros((w_in_l.shape[0], IN_W_PAD - used), BF16)
    return jnp.concatenate([p.astype(BF16) for p in parts] + [pad], axis=1)


def kernel(x, c, ctx, c_ctx, norm_w, w_ada, b_ada, w_in, na_q_norm, na_k_norm, na_rpb, rw_mu, rw_w0, rw_w2, rw_a0, rw_a2, rw_k_k, rw_k_a, rw_r_k, rw_ln_w, rw_ln_b, ssm_conv_w, ssm_conv_b, ssm_dt_bias, ssm_A_log, ssm_D, ssm_norm_w, da_q_norm, da_k_norm, da_lq1, da_lk1, da_lq2, da_lk2, da_subln, w_gate, w_up, w_out):
    bsz, seq, d = x.shape
    n_ctx = ctx.shape[1]
    lt = n_ctx + seq
    depth = w_in.shape[0]
    assert n_ctx == CTX_LEN == PREP_TM == DA_TK and seq % (GRID_W * NA_WIN_R) == 0
    assert ((lt // DA_TK) - 1) % DA_NSUB == 0 and n_ctx % NORM_ROWS == 0
    cond = jnp.concatenate([_silu(c), _silu(c_ctx)[None], jnp.zeros((SUBLANES - bsz - 1, d), F32)], axis=0)
    xs2 = jnp.concatenate([ctx, x], axis=1).reshape(bsz * lt, d)
    cos, sin = rope_tables(lt, n_ctx)
    for l in range(depth):
        lam_init = 0.8 - 0.6 * math.exp(-0.3 * l)
        mod = small_matmul(cond, w_ada[l], b_ada[l])
        shift, scale, gate = jnp.split(mod[:bsz], 3, axis=-1)
        shift_c, scale_c, gate_c = jnp.split(mod[bsz], 3, axis=-1)
        p2, h = in_projection(xs2, norm_w[l], scale, shift, scale_c, shift_c, _permute_w_in(w_in[l]), lt, n_ctx)
        p3 = p2.reshape(bsz, lt, IN_W_PAD)

        kn, va, qt, kh, vt = prep_attention(p3, cos, sin, na_k_norm[l], da_q_norm[l], da_k_norm[l])
        oa = neighbourhood_attention(p3, kn, va, na_bias_table(na_rpb[l]), na_q_norm[l], n_ctx)
        lam = jnp.exp(jnp.sum(da_lq1[l] * da_lk1[l])) - jnp.exp(jnp.sum(da_lq2[l] * da_lk2[l])) + lam_init
        od = flash_diff_attention(qt, kh, vt, p3, lam, da_subln[l], lam_init)

        r, vb, a, lw, kd, b, bonus = prep_rwkv(p3, rw_mu[l], rw_w0[l], rw_w2[l], rw_a0[l], rw_a2[l],
                                              rw_k_k[l], rw_k_a[l], rw_r_k[l].reshape(-1), n_ctx)
        ob = finish_rwkv(rwkv_scan(r, vb, a, lw, kd, b, n_ctx), bonus, p3, rw_ln_w[l], rw_ln_b[l])

        xq, bc, bt, sa, sat, dskip = prep_ssd(p3, ssm_conv_w[l], ssm_conv_b[l], ssm_dt_bias[l], ssm_A_log[l],
                                              ssm_D[l], n_ctx)
        om = finish_ssd(ssd_scan(xq, bc, bt, sa, sat, n_ctx), dskip, p3, ssm_norm_w[l])

        ys = [t.reshape(bsz * lt, BRANCH_W) for t in (oa, ob, om, od)]
        acc = gate_up(h, ys, w_gate[l].astype(BF16), w_up[l].astype(BF16), lt)
        xs2 = out_projection(acc, w_out[l].astype(BF16), xs2, gate, gate_c, lt, n_ctx)
    return xs2.reshape(bsz, lt, d)[:, n_ctx:]
```

```python
import functools
import math

import numpy as np
import jax
import jax.numpy as jnp
from jax import lax
from jax.experimental import pallas as pl
from jax.experimental.pallas import tpu as pltpu

F32 = jnp.float32
BF16 = jnp.bfloat16

D_MODEL = 2048
GRID_W = 64
CTX_LEN = 256
N_BRANCH = 4
BRANCH_W = D_MODEL // N_BRANCH
NORM_EPS = 1e-6
NEG_INF = -1e30

NA_DIM = 64
NA_HEADS = BRANCH_W // NA_DIM
NA_WIN_R = 8
NA_WIN_C = 16
NA_NBLK = 2

RW_DIM = 64
RW_HEADS = BRANCH_W // RW_DIM
RW_LORA_W = 64
RW_LORA_A = 64
RW_GN_EPS = 64e-5
RW_KK_EPS = 1e-12
RW_CHUNK = 64
RW_NSUB = 4

SSM_HEAD_DIM = 64
SSM_HEADS = BRANCH_W // SSM_HEAD_DIM
SSM_GROUPS = 2
SSM_STATE = 128
SSM_CHUNK = 128
SSM_NSUB = 2
SSM_CONV_CH = BRANCH_W + 2 * SSM_GROUPS * SSM_STATE

DA_DIM = 64
DA_HEADS = BRANCH_W // (2 * DA_DIM)
ROPE_BASE = 10000.0
DA_TQ = 256
DA_TK = 256
DA_NSUB = 32
DA_ONES = 16

SUBLANES = 8
LANES = 128

CB_AQ, CB_AK, CB_AV, CB_AG = 0, 1, 2, 3
CB_BR, CB_BK, CB_BV, CB_BG = 4, 5, 6, 7
CB_CX, CB_CBC, CB_CZ = 8, 9, 10
CB_DQ, CB_DK, CB_DV, CB_DG = 11, 12, 13, 14
CB_MISC = 15
MISC_DT = 2 * RW_LORA_W + 2 * RW_LORA_A
N_COL_BLOCKS = 16
IN_W_PAD = N_COL_BLOCKS * BRANCH_W
PREP_TM = 256
MATMUL_TM = 768
NORM_ROWS = 32

V7X_VMEM_BYTES = 64 * 1024 * 1024
VMEM_LIMIT = V7X_VMEM_BYTES * 7 // 8


def _cparams(n_axes):
    return pltpu.CompilerParams(dimension_semantics=("arbitrary",) * n_axes, vmem_limit_bytes=VMEM_LIMIT)


def _dot(a, b):
    return jnp.dot(a, b, preferred_element_type=F32)


def _dot_nt(a, b):
    return lax.dot_general(a, b, (((1,), (1,)), ((), ())), preferred_element_type=F32)


def _dot_tn(a, b):
    return lax.dot_general(a, b, (((0,), (0,)), ((), ())), preferred_element_type=F32)


def _split3(x):
    hi = x.astype(BF16)
    r1 = x - hi.astype(F32)
    mid = r1.astype(BF16)
    lo = (r1 - mid.astype(F32)).astype(BF16)
    return hi, mid, lo


def _dot_exact_lhs(m_bf16, x):
    hi, mid, lo = _split3(x)
    return _dot(m_bf16, hi) + _dot(m_bf16, mid) + _dot(m_bf16, lo)


def _dot_exact_rhs(x, m_bf16):
    hi, mid, lo = _split3(x)
    return _dot(hi, m_bf16) + _dot(mid, m_bf16) + _dot(lo, m_bf16)


def _group_sum(x, gmat_bf16):
    hi = x.astype(BF16)
    lo = (x - hi.astype(F32)).astype(BF16)
    return _dot(hi, gmat_bf16) + _dot(lo, gmat_bf16)


def _dot_nt_exact_lhs_f32(x, m_bf16):
    hi, mid, lo = _split3(x)
    return _dot_nt(hi, m_bf16) + _dot_nt(mid, m_bf16) + _dot_nt(lo, m_bf16)


def _sigmoid(x):
    return 1.0 / (1.0 + jnp.exp(-x))


def _silu(x):
    return x * _sigmoid(x)


def _softplus(x):
    return jnp.maximum(x, 0.0) + jnp.log(1.0 + jnp.exp(-jnp.abs(x)))


def _group_matrix(width, group):
    g = np.arange(width) // group
    return jnp.asarray((g[:, None] == g[None, :]).astype(np.float32), BF16)


def _head_rms(x, gmat, group, w):
    ms = _group_sum(x * x, gmat) * (1.0 / group)
    return x * lax.rsqrt(ms + NORM_EPS) * w


def _tile_rows(n_rows):
    return MATMUL_TM if n_rows % MATMUL_TM == 0 else PREP_TM


def _row_spec(cb, tm=PREP_TM):
    return pl.BlockSpec((1, tm, BRANCH_W), lambda b, i: (b, i, cb))


def _halo_specs(cb, tm, lt):
    per = tm // SUBLANES
    last = lt // SUBLANES - 1
    prev = pl.BlockSpec((1, SUBLANES, BRANCH_W), lambda b, i: (b, jnp.maximum(i * per - 1, 0), cb))
    nxt = pl.BlockSpec((1, SUBLANES, BRANCH_W), lambda b, i: (b, jnp.minimum((i + 1) * per, last), cb))
    return prev, nxt


def _const_spec(shape):
    return pl.BlockSpec(shape, lambda *_: (0,) * len(shape))


def _neighbours(x, prev_blk, next_blk, i, n_tiles, n_ctx_tiles):
    tm = x.shape[0]
    row = lax.broadcasted_iota(jnp.int32, (tm, 1), 0)
    seg_start = jnp.logical_or(i == 0, i == n_ctx_tiles)
    seg_end = jnp.logical_or(i == n_ctx_tiles - 1, i == n_tiles - 1)
    prev_row = jnp.where(seg_start, 0.0, prev_blk[SUBLANES - 1:SUBLANES, :])
    next_row = jnp.where(seg_end, 0.0, next_blk[0:1, :])
    x_prev = jnp.where(row == 0, prev_row, pltpu.roll(x, 1, 0))
    x_next = jnp.where(row == tm - 1, next_row, pltpu.roll(x, tm - 1, 0))
    return x_prev, x_next


def _small_mm_kernel(a_ref, w_ref, b_ref, o_ref):
    o_ref[...] = _dot(a_ref[...].astype(BF16), w_ref[...].astype(BF16)) + b_ref[...]


def small_matmul(a, w, b, tn=512):
    m, k = a.shape
    n = w.shape[1]
    return pl.pallas_call(
        _small_mm_kernel,
        grid=(n // tn,),
        in_specs=[pl.BlockSpec((m, k), lambda j: (0, 0)),
                  pl.BlockSpec((k, tn), lambda j: (0, j)),
                  pl.BlockSpec((1, tn), lambda j: (0, j))],
        out_specs=pl.BlockSpec((m, tn), lambda j: (0, j)),
        out_shape=jax.ShapeDtypeStruct((m, n), F32),
        compiler_params=_cparams(1),
        name="adaln_mm",
    )(a, w, b.reshape(1, n))


def _inproj_kernel(x_ref, nw_ref, sc_ref, sh_ref, scc_ref, shc_ref, w_ref, p_ref, h_ref, hs_ref, *,
                   tiles_per_batch, n_ctx):
    @pl.when(pl.program_id(1) == 0)
    def _():
        tm = x_ref.shape[0]
        rc = NORM_ROWS
        base = (pl.program_id(0) % tiles_per_batch) * tm

        def norm_rows(k, carry):
            r0 = pl.multiple_of(k * rc, rc)
            xf = x_ref[pl.ds(r0, rc), :]
            ms = jnp.mean(xf * xf, axis=-1, keepdims=True)
            y = xf * lax.rsqrt(ms + NORM_EPS) * nw_ref[...]
            is_ctx = base + r0 < n_ctx
            sc = 1.0 + jnp.where(is_ctx, scc_ref[...], sc_ref[0])
            sh = jnp.where(is_ctx, shc_ref[...], sh_ref[0])
            h = (y * sc + sh).astype(BF16)
            hs_ref[pl.ds(r0, rc), :] = h
            h_ref[pl.ds(r0, rc), :] = h
            return carry

        lax.fori_loop(0, tm // rc, norm_rows, 0, unroll=4)

    p_ref[...] = _dot(hs_ref[...], w_ref[...])


def in_projection(xs2, norm_w, scale, shift, scale_c, shift_c, w_bf16, lt, n_ctx, tn=2048):
    m, d = xs2.shape
    n = w_bf16.shape[1]
    tm = _tile_rows(lt)
    tpb = lt // tm
    nb = scale.shape[0]
    kern = functools.partial(_inproj_kernel, tiles_per_batch=tpb, n_ctx=n_ctx)
    return pl.pallas_call(
        kern,
        grid=(m // tm, n // tn),
        in_specs=[pl.BlockSpec((tm, d), lambda i, j: (i, 0)),
                  pl.BlockSpec((1, d), lambda i, j: (0, 0)),
                  pl.BlockSpec((1, 1, d), lambda i, j: (i // tpb, 0, 0)),
                  pl.BlockSpec((1, 1, d), lambda i, j: (i // tpb, 0, 0)),
                  pl.BlockSpec((1, d), lambda i, j: (0, 0)),
                  pl.BlockSpec((1, d), lambda i, j: (0, 0)),
                  pl.BlockSpec((d, tn), lambda i, j: (0, j))],
        out_specs=[pl.BlockSpec((tm, tn), lambda i, j: (i, j)),
                   pl.BlockSpec((tm, d), lambda i, j: (i, 0))],
        out_shape=[jax.ShapeDtypeStruct((m, n), F32),
                   jax.ShapeDtypeStruct((m, d), BF16)],
        scratch_shapes=[pltpu.VMEM((tm, d), BF16)],
        compiler_params=_cparams(2),
        name="in_proj",
    )(xs2, norm_w.reshape(1, d), scale.reshape(nb, 1, d), shift.reshape(nb, 1, d),
      scale_c.reshape(1, d), shift_c.reshape(1, d), w_bf16)


def _gate_up_kernel(h_ref, ya_ref, yb_ref, yc_ref, yd_ref, wg_ref, wu_ref, o_ref):
    h = h_ref[...]
    acc = None
    for i, y_ref in enumerate((ya_ref, yb_ref, yc_ref, yd_ref)):
        g = _dot(h, wg_ref[i])
        u = _dot(y_ref[...], wu_ref[i])
        t = _sigmoid(g) * u
        acc = t if acc is None else acc + t
    o_ref[...] = acc.astype(BF16)


def gate_up(h, ys, wg_bf16, wu_bf16, lt, tn=512):
    m, d = h.shape
    w = ys[0].shape[1]
    tm = _tile_rows(lt)
    y_spec = pl.BlockSpec((tm, w), lambda i, j: (i, 0))
    return pl.pallas_call(
        _gate_up_kernel,
        grid=(m // tm, d // tn),
        in_specs=[pl.BlockSpec((tm, d), lambda i, j: (i, 0)), y_spec, y_spec, y_spec, y_spec,
                  pl.BlockSpec((N_BRANCH, d, tn), lambda i, j: (0, 0, j)),
                  pl.BlockSpec((N_BRANCH, w, tn), lambda i, j: (0, 0, j))],
        out_specs=pl.BlockSpec((tm, tn), lambda i, j: (i, j)),
        out_shape=jax.ShapeDtypeStruct((m, d), BF16),
        compiler_params=_cparams(2),
        name="gate_up",
    )(h, *ys, wg_bf16, wu_bf16)


def _out_proj_kernel(a_ref, w_ref, x_ref, g_ref, gc_ref, o_ref, *, tiles_per_batch, n_ctx):
    tm = x_ref.shape[0]
    row = (pl.program_id(0) % tiles_per_batch) * tm + lax.broadcasted_iota(jnp.int32, (tm, 1), 0)
    gate = jnp.where(row < n_ctx, gc_ref[...], g_ref[0])
    o_ref[...] = x_ref[...] + gate * _dot(a_ref[...], w_ref[...])


def out_projection(acc, w_bf16, xs2, gate, gate_c, lt, n_ctx, tn=1024):
    m, d = xs2.shape
    tm = _tile_rows(lt)
    tpb = lt // tm
    nb = gate.shape[0]
    kern = functools.partial(_out_proj_kernel, tiles_per_batch=tpb, n_ctx=n_ctx)
    return pl.pallas_call(
        kern,
        grid=(m // tm, d // tn),
        in_specs=[pl.BlockSpec((tm, d), lambda i, j: (i, 0)),
                  pl.BlockSpec((d, tn), lambda i, j: (0, j)),
                  pl.BlockSpec((tm, tn), lambda i, j: (i, j)),
                  pl.BlockSpec((1, 1, tn), lambda i, j: (i // tpb, 0, j)),
                  pl.BlockSpec((1, tn), lambda i, j: (0, j))],
        out_specs=pl.BlockSpec((tm, tn), lambda i, j: (i, j)),
        out_shape=jax.ShapeDtypeStruct((m, d), F32),
        compiler_params=_cparams(2),
        name="out_proj",
    )(acc, w_bf16, xs2, gate.reshape(nb, 1, d), gate_c.reshape(1, d))


def _rope(x, cos, sin):
    w = x.shape[1]
    lane = lax.broadcasted_iota(jnp.int32, (1, w), 1)
    first = ((lane // (DA_DIM // 4)) % 2) == 0
    rot = jnp.where(first, -pltpu.roll(x, w - DA_DIM // 4, 1), pltpu.roll(x, DA_DIM // 4, 1))
    return x * cos + rot * sin


def _prep_attn_kernel(ak_ref, av_ref, dq_ref, dk_ref, dv_ref, cos_ref, sin_ref, g64_ref,
                      wak_ref, wdq_ref, wdk_ref, kn_ref, va_ref, qt_ref, kh_ref, vt_ref):
    g64 = g64_ref[...]
    reps = BRANCH_W // LANES
    cos = jnp.concatenate([cos_ref[...]] * reps, axis=1)
    sin = jnp.concatenate([sin_ref[...]] * reps, axis=1)
    kn_ref[0] = _head_rms(ak_ref[0], g64, NA_DIM, wak_ref[...]).astype(BF16)
    va_ref[0] = av_ref[0].astype(BF16)
    q = _rope(_head_rms(dq_ref[0], g64, DA_DIM, wdq_ref[...]), cos, sin)
    k = _rope(_head_rms(dk_ref[0], g64, DA_DIM, wdk_ref[...]), cos, sin).astype(BF16)
    qt_ref[0] = q.T.astype(BF16)
    for hc in range(2 * DA_HEADS):
        kh_ref[0, hc] = k[:, hc * DA_DIM:(hc + 1) * DA_DIM]
    v = dv_ref[0]
    dv = 2 * DA_DIM
    for h in range(DA_HEADS):
        vt_ref[0, h, 0, 0:dv, :] = v[:, h * dv:(h + 1) * dv].T.astype(BF16)
        vt_ref[0, h, 0, dv:dv + DA_ONES, :] = jnp.ones((DA_ONES, v.shape[0]), BF16)


def prep_attention(p3, cos, sin, na_k_w, da_q_w, da_k_w):
    bsz, lt, _ = p3.shape
    tm = PREP_TM
    w = BRANCH_W
    tile = lambda v, s=1.0: (jnp.tile(v, w // v.shape[0]) * s).reshape(1, w)
    tab_spec = pl.BlockSpec((tm, LANES), lambda b, i: (i, 0))
    vec = _const_spec((1, w))
    return pl.pallas_call(
        _prep_attn_kernel,
        grid=(bsz, lt // tm),
        in_specs=[_row_spec(CB_AK), _row_spec(CB_AV), _row_spec(CB_DQ), _row_spec(CB_DK), _row_spec(CB_DV),
                  tab_spec, tab_spec, _const_spec((w, w)), vec, vec, vec],
        out_specs=[pl.BlockSpec((1, tm, w), lambda b, i: (b, i, 0)),
                   pl.BlockSpec((1, tm, w), lambda b, i: (b, i, 0)),
                   pl.BlockSpec((1, w, tm), lambda b, i: (b, 0, i)),
                   pl.BlockSpec((1, 2 * DA_HEADS, tm, DA_DIM), lambda b, i: (b, 0, i, 0)),
                   pl.BlockSpec((1, DA_HEADS, 1, 2 * DA_DIM + DA_ONES, tm), lambda b, i: (b, 0, i, 0, 0))],
        out_shape=[jax.ShapeDtypeStruct((bsz, lt, w), BF16),
                   jax.ShapeDtypeStruct((bsz, lt, w), BF16),
                   jax.ShapeDtypeStruct((bsz, w, lt), BF16),
                   jax.ShapeDtypeStruct((bsz, 2 * DA_HEADS, lt, DA_DIM), BF16),
                   jax.ShapeDtypeStruct((bsz, DA_HEADS, lt // tm, 2 * DA_DIM + DA_ONES, tm), BF16)],
        compiler_params=_cparams(2),
        name="prep_attn",
    )(p3, p3, p3, p3, p3, cos, sin, _group_matrix(w, DA_DIM),
      tile(na_k_w), tile(da_q_w, DA_DIM ** -0.5 * math.log2(math.e)), tile(da_k_w))


def rope_tables(lt, n_ctx):
    nf = DA_DIM // 4
    t = jnp.arange(lt - n_ctx, dtype=jnp.int32)
    rows, cols = t // GRID_W, t % GRID_W
    inv = ROPE_BASE ** (-jnp.arange(nf, dtype=F32) / nf)
    ang_r = rows.astype(F32)[:, None] * inv
    ang_c = cols.astype(F32)[:, None] * inv
    ang = jnp.concatenate([ang_r, ang_r, ang_c, ang_c], axis=-1)
    ang = jnp.concatenate([jnp.zeros((n_ctx, DA_DIM), F32), ang], axis=0)
    reps = LANES // DA_DIM
    return jnp.tile(jnp.cos(ang), (1, reps)), jnp.tile(jnp.sin(ang), (1, reps))


def _flash_kernel(lam_ref, qt_ref, k_ref, vt_ref, g_ref, sw_ref, o_ref, *, n_latent_iters, out_scale):
    tq = qt_ref.shape[2]
    dve = vt_ref.shape[3]
    dv = dve - DA_ONES
    tk = vt_ref.shape[4]
    qts = [qt_ref[0, c * DA_DIM:(c + 1) * DA_DIM, :] for c in range(2)]
    gate = _silu(g_ref[0])

    def attend(carry, first_chunk, n_chunks):
        chunks = [first_chunk + g for g in range(n_chunks)]
        sts = [[_dot(k_ref[0, c, pl.ds(pl.multiple_of(ch * tk, tk), tk), :], qts[c]) for c in range(2)]
               for ch in chunks]
        carry = list(carry)
        for g, ch in enumerate(chunks):
            vt = vt_ref[0, 0, ch]
            for c in range(2):
                m, acc = carry[c]
                st = sts[g][c].astype(BF16)
                m_new = jnp.maximum(m, jnp.max(st, axis=0, keepdims=True).astype(F32))
                pt = jnp.exp2(st - m_new.astype(BF16))
                carry[c] = (m_new, jnp.exp2(m - m_new) * acc + _dot(vt, pt))
        return tuple(carry)

    init = tuple((jnp.full((1, tq), NEG_INF, F32), jnp.zeros((dve, tq), F32)) for _ in range(2))
    carry = attend(init, 0, 1)
    n_iters = jnp.where(pl.program_id(2) == 0, 0, n_latent_iters)
    res = lax.fori_loop(0, n_iters, lambda i, cr: attend(cr, 1 + i * DA_NSUB, DA_NSUB), carry)
    outs = [acc[0:dv] * (1.0 / acc[dv:dv + 1]) for (_, acc) in res]
    ot = outs[0] - lam_ref[0] * outs[1]
    ot = ot * lax.rsqrt(jnp.mean(ot * ot, axis=0, keepdims=True) + NORM_EPS)
    o = ot.T * (sw_ref[...] * out_scale)
    o_ref[0] = (o * gate).astype(BF16)


def flash_diff_attention(qt, kh, vt, p3, lam, subln_w, lam_init):
    bsz, w, lt = qt.shape
    dv = 2 * DA_DIM
    nk = lt // DA_TK
    n_latent_iters = (nk - 1) // DA_NSUB
    kern = functools.partial(_flash_kernel, n_latent_iters=n_latent_iters, out_scale=1.0 - lam_init)
    g_blocks = BRANCH_W // dv
    return pl.pallas_call(
        kern,
        grid=(bsz, DA_HEADS, lt // DA_TQ),
        in_specs=[pl.BlockSpec(memory_space=pltpu.SMEM),
                  pl.BlockSpec((1, dv, DA_TQ), lambda b, h, qi: (b, h, qi)),
                  pl.BlockSpec((1, 2, lt, DA_DIM), lambda b, h, qi: (b, h, 0, 0)),
                  pl.BlockSpec((1, 1, nk, dv + DA_ONES, DA_TK), lambda b, h, qi: (b, h, 0, 0, 0)),
                  pl.BlockSpec((1, DA_TQ, dv), lambda b, h, qi: (b, qi, CB_DG * g_blocks + h)),
                  _const_spec((1, dv))],
        out_specs=pl.BlockSpec((1, DA_TQ, dv), lambda b, h, qi: (b, qi, h)),
        out_shape=jax.ShapeDtypeStruct((bsz, lt, BRANCH_W), BF16),
        compiler_params=_cparams(3),
        name="flash_attn",
    )(lam.reshape(1).astype(F32), qt, kh, vt, p3, subln_w.reshape(1, dv))


def _na_kernel(q_ref, g_ref, k_ref, v_ref, *rest, n_rows, n_ctx):
    bias_refs = rest[:NA_NBLK]
    g64_ref, wq_ref, o_ref = rest[NA_NBLK:]
    n_loc = NA_WIN_R * GRID_W
    q = _head_rms(q_ref[0], g64_ref[...], NA_DIM, wq_ref[...]).astype(BF16)
    g = g_ref[0]
    pair = [slice(hp * LANES, (hp + 1) * LANES) for hp in range(NA_HEADS // 2)]
    low_half = lax.broadcasted_iota(jnp.int32, (1, LANES), 1) < NA_DIM
    rows, starts, units = [], [], []
    for j in range(NA_NBLK):
        blk = pl.program_id(1) * NA_NBLK + j
        r = jnp.maximum(blk - n_ctx // GRID_W, 0)
        r0 = jnp.clip(r - NA_WIN_R // 2, 0, n_rows - NA_WIN_R)
        starts.append(pl.multiple_of(n_ctx + r0 * GRID_W, GRID_W))
        rows.append(slice(j * GRID_W, (j + 1) * GRID_W))
        units += [(j, hp) for hp in range(NA_HEADS // 2)]
    zero = jnp.zeros((), BF16)
    cat = lambda x, y: jnp.concatenate([x, y], axis=0)
    q2 = [cat(jnp.where(low_half, q[rows[j], pair[hp]], zero), jnp.where(low_half, zero, q[rows[j], pair[hp]]))
          for j, hp in units]
    s_loc = [_dot_nt(q2[u], k_ref[0, pl.ds(starts[j], n_loc), pair[hp]])
             + cat(bias_refs[j][0, 2 * hp], bias_refs[j][0, 2 * hp + 1]) for u, (j, hp) in enumerate(units)]
    s_ctx = [_dot_nt(q2[u], k_ref[0, 0:n_ctx, pair[hp]]) for u, (j, hp) in enumerate(units)]
    us = range(len(units))
    m = [jnp.maximum(jnp.max(s_loc[u], axis=-1, keepdims=True), jnp.max(s_ctx[u], axis=-1, keepdims=True))
         for u in us]
    p_loc = [jnp.exp(s_loc[u] - m[u]) for u in us]
    p_ctx = [jnp.exp(s_ctx[u] - m[u]) for u in us]
    l = [jnp.sum(p_loc[u], axis=-1, keepdims=True) + jnp.sum(p_ctx[u], axis=-1, keepdims=True) for u in us]
    o = [(_dot(p_loc[u].astype(BF16), v_ref[0, pl.ds(starts[j], n_loc), pair[hp]])
          + _dot(p_ctx[u].astype(BF16), v_ref[0, 0:n_ctx, pair[hp]])) / l[u]
         for u, (j, hp) in enumerate(units)]
    for u, (j, hp) in enumerate(units):
        o_pair = jnp.where(low_half, o[u][0:GRID_W], o[u][GRID_W:2 * GRID_W])
        o_ref[0, rows[j], pair[hp]] = (o_pair * _silu(g[rows[j], pair[hp]])).astype(BF16)


def neighbourhood_attention(p3, kn, va, bias_tbl, na_q_w, n_ctx):
    bsz, lt, w = kn.shape
    n_rows = (lt - n_ctx) // GRID_W
    ncb = n_ctx // GRID_W
    half = NA_WIN_R // 2

    def bias_spec(j):
        def idx(b, s):
            blk = s * NA_NBLK + j
            r = blk - ncb
            off = r - jnp.clip(r - half, 0, n_rows - NA_WIN_R)
            return (jnp.where(blk < ncb, NA_WIN_R, off), 0, 0, 0)
        return pl.BlockSpec((1, NA_HEADS, GRID_W, NA_WIN_R * GRID_W), idx)

    tq = NA_NBLK * GRID_W
    full_spec = pl.BlockSpec((1, lt, w), lambda b, s: (b, 0, 0))
    wq = (jnp.tile(na_q_w, w // NA_DIM) * NA_DIM ** -0.5).reshape(1, w)
    return pl.pallas_call(
        functools.partial(_na_kernel, n_rows=n_rows, n_ctx=n_ctx),
        grid=(bsz, lt // tq),
        in_specs=[_row_spec(CB_AQ, tq), _row_spec(CB_AG, tq), full_spec, full_spec,
                  *[bias_spec(j) for j in range(NA_NBLK)],
                  _const_spec((w, w)), _const_spec((1, w))],
        out_specs=pl.BlockSpec((1, tq, w), lambda b, s: (b, s, 0)),
        out_shape=jax.ShapeDtypeStruct((bsz, lt, w), BF16),
        compiler_params=_cparams(2),
        name="nbr_attn",
    )(p3, p3, kn, va, *([bias_tbl] * NA_NBLK), _group_matrix(w, NA_DIM), wq)


def na_bias_table(rpb):
    col = np.arange(GRID_W)
    c0 = np.clip(col - NA_WIN_C // 2, 0, GRID_W - NA_WIN_C)
    col_ok = (col[None, :] >= c0[:, None]) & (col[None, :] < c0[:, None] + NA_WIN_C)
    d_col = np.clip(col[None, :] - col[:, None] + (NA_WIN_C - 1), 0, 2 * NA_WIN_C - 2)
    onehot = (d_col[:, :, None] == np.arange(2 * NA_WIN_C - 1)).astype(np.float32)
    by_col = jnp.einsum('hrc,qwc->hrqw', rpb.astype(F32), jnp.asarray(onehot),
                        precision=lax.Precision.HIGHEST)
    by_col = jnp.where(jnp.asarray(col_ok)[None, None], by_col, NEG_INF)
    tbl = jnp.stack([by_col[:, NA_WIN_R - 1 - o:2 * NA_WIN_R - 1 - o] for o in range(NA_WIN_R)]
                    + [jnp.full((NA_HEADS, NA_WIN_R, GRID_W, GRID_W), NEG_INF, F32)])
    tbl = jnp.transpose(tbl, (0, 1, 3, 2, 4))
    return tbl.reshape(NA_WIN_R + 1, NA_HEADS, GRID_W, NA_WIN_R * GRID_W)


def _prep_rwkv_kernel(r_ref, rp_ref, rn_ref, k_ref, kp_ref, kn_ref, v_ref, vp_ref, vn_ref,
                      m_ref, mp_ref, mn_ref, mu_ref, w0_ref, a0_ref, w2_ref, a2_ref, kk_ref, ka_ref,
                      rk_ref, g64_ref,
                      ro_ref, vo_ref, ao_ref, lw_ref, kd_ref, bo_ref, bonus_ref, *, n_tiles, n_ctx_tiles):
    i = pl.program_id(1)
    w = BRANCH_W

    def shifted(x_ref, p_ref, n_ref, mu):
        x = x_ref[0]
        xp, xn = _neighbours(x, p_ref[0], n_ref[0], i, n_tiles, n_ctx_tiles)
        return x + (0.5 * (xp + xn) - x) * mu

    r = shifted(r_ref, rp_ref, rn_ref, mu_ref[0:1, :])
    k = shifted(k_ref, kp_ref, kn_ref, mu_ref[1:2, :])
    v = shifted(v_ref, vp_ref, vn_ref, mu_ref[2:3, :])
    misc = shifted(m_ref, mp_ref, mn_ref, mu_ref[3:4, :])
    wd = jnp.tanh(misc[:, 0:2 * RW_LORA_W]).astype(BF16)
    ad = misc[:, 2 * RW_LORA_W:MISC_DT].astype(BF16)
    w_log = w0_ref[...] + _dot(wd, w2_ref[...])
    gate = _sigmoid(a0_ref[...] + _dot(ad, a2_ref[...]))
    log_decay = -math.exp(-0.5) * _sigmoid(w_log)
    g64 = g64_ref[...]
    kk = k * kk_ref[...]
    kk = kk / jnp.maximum(jnp.sqrt(_group_sum(kk * kk, g64)), RW_KK_EPS)
    ro_ref[0] = r
    vo_ref[0] = v.astype(BF16)
    ao_ref[0] = -kk
    coef = None
    for d in range(2):
        a_d = gate[:, d * w:(d + 1) * w]
        kd = k * (1.0 + (a_d - 1.0) * ka_ref[...])
        lw_ref[d, 0] = log_decay[:, d * w:(d + 1) * w]
        kd_ref[d, 0] = kd
        bo_ref[d, 0] = kk * a_d
        coef = kd if coef is None else coef + kd
    bonus_ref[0] = _group_sum(r * coef * rk_ref[...], g64) * v


def prep_rwkv(p3, mu, w0, w2, a0, a2, k_k, k_a, r_k, n_ctx):
    bsz, lt, _ = p3.shape
    tm = PREP_TM
    w = BRANCH_W
    n_tiles = lt // tm
    mu4 = jnp.stack([mu[0:w], mu[w:2 * w], mu[2 * w:3 * w],
                     jnp.pad(mu[3 * w:], (0, w - (mu.shape[0] - 3 * w)))])
    zero = jnp.zeros((RW_LORA_W, w), F32)
    w2cat = jnp.concatenate([jnp.concatenate([w2[0], zero], axis=1),
                             jnp.concatenate([zero, w2[1]], axis=1)], axis=0).astype(BF16)
    a2cat = jnp.concatenate([jnp.concatenate([a2[0], zero], axis=1),
                             jnp.concatenate([zero, a2[1]], axis=1)], axis=0).astype(BF16)
    specs = []
    for cb in (CB_BR, CB_BK, CB_BV, CB_MISC):
        specs += [_row_spec(cb), *_halo_specs(cb, tm, lt)]
    vec = _const_spec((1, w))
    tok = pl.BlockSpec((1, tm, w), lambda b, i: (b, i, 0))
    tok2 = pl.BlockSpec((2, 1, tm, w), lambda b, i: (0, b, i, 0))
    kern = functools.partial(_prep_rwkv_kernel, n_tiles=n_tiles, n_ctx_tiles=n_ctx // tm)
    return pl.pallas_call(
        kern,
        grid=(bsz, n_tiles),
        in_specs=specs + [_const_spec((4, w)), _const_spec((1, 2 * w)), _const_spec((1, 2 * w)),
                          _const_spec((2 * RW_LORA_W, 2 * w)), _const_spec((2 * RW_LORA_A, 2 * w)),
                          vec, vec, vec, _const_spec((w, w))],
        out_specs=[tok, tok, tok, tok2, tok2, tok2, tok],
        out_shape=[jax.ShapeDtypeStruct((bsz, lt, w), F32),
                   jax.ShapeDtypeStruct((bsz, lt, w), BF16),
                   jax.ShapeDtypeStruct((bsz, lt, w), F32),
                   jax.ShapeDtypeStruct((2, bsz, lt, w), F32),
                   jax.ShapeDtypeStruct((2, bsz, lt, w), F32),
                   jax.ShapeDtypeStruct((2, bsz, lt, w), F32),
                   jax.ShapeDtypeStruct((bsz, lt, w), F32)],
        compiler_params=_cparams(2),
        name="prep_rwkv",
    )(*([p3] * 12), mu4, w0.reshape(1, 2 * w), a0.reshape(1, 2 * w), w2cat, a2cat,
      k_k.reshape(1, w), k_a.reshape(1, w), r_k.reshape(1, w), _group_matrix(w, RW_DIM))


def _rwkv_kernel(tri_ref, mq_ref, r_ref, v_ref, a_ref, lw_ref, kd_ref, b_ref, y_ref, s_ref):
    c = RW_CHUNK
    nsub = r_ref.shape[1] // c
    d = pl.program_id(0)

    @pl.when(pl.program_id(2) == 0)
    def _():
        s_ref[...] = jnp.zeros_like(s_ref)

    tri = tri_ref[0]
    m_quad = mq_ref[0] > 0.5
    rows = lax.broadcasted_iota(jnp.int32, (c, c), 0)
    cols = lax.broadcasted_iota(jnp.int32, (c, c), 1)
    eye = (rows == cols).astype(F32)
    same_half = (rows >= c // 2) == (cols >= c // 2)
    heads = [slice(h * RW_DIM, (h + 1) * RW_DIM) for h in range(RW_HEADS)]

    row_sl, e_tot = [], []
    al, rh, rf, be, ka, bc, kc, vb = [], [], [], [], [], [], [], []
    for j in range(nsub):
        off = pl.multiple_of(jnp.where(d == 0, j, nsub - 1 - j) * c, c)
        rs = pl.ds(off, c)
        row_sl.append(rs)
        lw = lw_ref[0, 0, rs, :]
        cum = _dot_exact_lhs(tri, lw)
        tot = jnp.sum(lw, axis=0, keepdims=True)
        e_m = jnp.exp(-cum)
        e_t = jnp.exp(tot - cum)
        e_tot.append(jnp.exp(tot))
        b_in = b_ref[0, 0, rs, :]
        k_in = kd_ref[0, 0, rs, :]
        alpha = (a_ref[0, rs, :] * jnp.exp(cum - lw)).astype(BF16)
        rho_f = r_ref[0, rs, :] * jnp.exp(cum)
        rho = rho_f.astype(BF16)
        beta = (b_in * e_m).astype(BF16)
        kappa = (k_in * e_m).astype(BF16)
        beta_c = (b_in * e_t).astype(BF16)
        kappa_c = (k_in * e_t).astype(BF16)
        v_b = v_ref[0, rs, :]
        for sl in heads:
            al.append(alpha[:, sl])
            rh.append(rho[:, sl])
            rf.append(rho_f[:, sl])
            be.append(beta[:, sl])
            ka.append(kappa[:, sl])
            bc.append(beta_c[:, sl])
            kc.append(kappa_c[:, sl])
            vb.append(v_b[:, sl])

    units = range(nsub * RW_HEADS)
    cat = lambda x, y: jnp.concatenate([x, y], axis=0)
    zeros = jnp.zeros((c, RW_DIM), BF16)
    prod = [jnp.where(m_quad, _dot_nt(cat(al[u], rh[u]), cat(be[u], ka[u])), 0.0) for u in units]
    l_ab = [prod[u][0:c, 0:c] for u in units]
    top = [prod[u][0:c].astype(BF16) for u in units]
    bot = [prod[u][c:2 * c].astype(BF16) for u in units]
    l_d = [jnp.where(same_half, l_ab[u], 0.0) for u in units]
    l_o = [(l_ab[u] - l_d[u]).astype(BF16) for u in units]
    ld_b = [l_d[u].astype(BF16) for u in units]
    pw = [_dot(ld_b[u], ld_b[u]).astype(BF16) for u in units]
    td = [eye + l_d[u] for u in units]
    for _ in range(int(math.log2(c)) - 3):
        both = [_dot(cat(td[u].astype(BF16), pw[u]), pw[u]) for u in units]
        td = [td[u] + both[u][0:c] for u in units]
        pw = [both[u][c:2 * c].astype(BF16) for u in units]
    td = [td[u] + _dot(td[u].astype(BF16), pw[u]) for u in units]
    td_b = [td[u].astype(BF16) for u in units]
    x_o = [_dot(td_b[u], l_o[u]).astype(BF16) for u in units]
    tinv = [(td[u] + _dot(x_o[u], td_b[u])).astype(BF16) for u in units]
    akv = [_dot(top[u], cat(zeros, vb[u])).astype(BF16) for u in units]
    lcat = lambda x, y: jnp.concatenate([x, y], axis=1)
    av = [_dot(tinv[u], lcat(al[u], akv[u])).astype(BF16) for u in units]
    ry = [_dot(bot[u], cat(av[u], lcat(zeros, vb[u]))) for u in units]
    r_hat = [(rf[u] + ry[u][:, 0:RW_DIM]).astype(BF16) for u in units]
    y_hat = [ry[u][:, RW_DIM:2 * RW_DIM] for u in units]
    qn = [_dot_tn(av[u], bc[u]) for u in units]
    q_mat = [qn[u][0:RW_DIM].astype(BF16) for u in units]
    n_mat = [qn[u][RW_DIM:2 * RW_DIM] + _dot_tn(vb[u], kc[u]) for u in units]

    state = [s_ref[h] for h in range(RW_HEADS)]
    for j in range(nsub):
        for h, sl in enumerate(heads):
            u = j * RW_HEADS + h
            s_b = state[h].astype(BF16)
            y_ref[0, 0, row_sl[j], sl] = _dot_nt(r_hat[u], s_b) + y_hat[u]
            state[h] = state[h] * e_tot[j][:, sl] + _dot(s_b, q_mat[u]) + n_mat[u]
    for h in range(RW_HEADS):
        s_ref[h] = state[h]


def _scan_chunk_index(d, s, n_ctx_chunks, n_chunks):
    rev = jnp.where(s < n_ctx_chunks, n_ctx_chunks - 1 - s, n_chunks + n_ctx_chunks - 1 - s)
    return jnp.where(d == 0, s, rev)


def _direction_masks(c):
    i = np.arange(c)
    lower = (i[None, :] <= i[:, None]).astype(np.float32)
    tri = np.stack([lower, lower.T])
    strict = np.stack([lower - np.eye(c, dtype=np.float32), lower.T - np.eye(c, dtype=np.float32)])
    return tri, strict


def rwkv_scan(r, v, a, lw, kd, b, n_ctx):
    bsz, lt, w = r.shape
    c = RW_CHUNK
    blk = RW_NSUB * c
    nck = lt // blk
    ncc = n_ctx // blk
    tri, strict = _direction_masks(c)
    shared = pl.BlockSpec((1, blk, w), lambda d, bi, s: (bi, _scan_chunk_index(d, s, ncc, nck), 0))
    perdir = pl.BlockSpec((1, 1, blk, w), lambda d, bi, s: (d, bi, _scan_chunk_index(d, s, ncc, nck), 0))
    mask_spec = pl.BlockSpec((1, c, c), lambda d, bi, s: (d, 0, 0))
    quad_spec = pl.BlockSpec((1, 2 * c, 2 * c), lambda d, bi, s: (d, 0, 0))
    quad = np.concatenate([np.tile(strict, (1, 1, 2)), np.tile(tri, (1, 1, 2))], axis=1)
    return pl.pallas_call(
        _rwkv_kernel,
        grid=(2, bsz, nck),
        in_specs=[mask_spec, quad_spec, shared, shared, shared, perdir, perdir, perdir],
        out_specs=perdir,
        out_shape=jax.ShapeDtypeStruct((2, bsz, lt, w), F32),
        scratch_shapes=[pltpu.VMEM((RW_HEADS, RW_DIM, RW_DIM), F32)],
        compiler_params=_cparams(3),
        name="rwkv_scan",
    )(jnp.asarray(tri, BF16), jnp.asarray(quad, F32), r, v, a, lw, kd, b)


def _finish_rwkv_kernel(y_ref, bonus_ref, g_ref, lnw_ref, lnb_ref, g64_ref, o_ref):
    y = y_ref[0, 0] + y_ref[1, 0]
    g64 = g64_ref[...]
    mean = _dot_exact_rhs(y, g64) * (1.0 / RW_DIM)
    yc = y - mean
    var = _group_sum(yc * yc, g64) * (1.0 / RW_DIM)
    yn = yc * lax.rsqrt(var + RW_GN_EPS) * lnw_ref[...] + lnb_ref[...]
    o_ref[0] = ((yn + bonus_ref[0]) * _silu(g_ref[0])).astype(BF16)


def finish_rwkv(y, bonus, p3, ln_w, ln_b):
    _, bsz, lt, w = y.shape
    tm = _tile_rows(lt)
    tok = pl.BlockSpec((1, tm, w), lambda b, i: (b, i, 0))
    vec = _const_spec((1, w))
    return pl.pallas_call(
        _finish_rwkv_kernel,
        grid=(bsz, lt // tm),
        in_specs=[pl.BlockSpec((2, 1, tm, w), lambda b, i: (0, b, i, 0)), tok, _row_spec(CB_BG, tm),
                  vec, vec, _const_spec((w, w))],
        out_specs=tok,
        out_shape=jax.ShapeDtypeStruct((bsz, lt, w), BF16),
        compiler_params=_cparams(2),
        name="finish_rwkv",
    )(y, bonus, p3, ln_w.reshape(1, w), ln_b.reshape(1, w), _group_matrix(w, RW_DIM))


def _prep_ssd_kernel(x_ref, xp_ref, xn_ref, bc_ref, bcp_ref, bcn_ref, m_ref, cw_ref, cb_ref, dtb_ref,
                     aneg_ref, exp_ref, dsk_ref,
                     xq_ref, bco_ref, bt_ref, a_ref, at_ref, dskip_ref, *, n_tiles, n_ctx_tiles):
    i = pl.program_id(1)
    w = BRANCH_W

    def conv(x_ref, p_ref, n_ref, half):
        x = x_ref[0]
        xp, xn = _neighbours(x, p_ref[0], n_ref[0], i, n_tiles, n_ctx_tiles)
        lo = half * w
        y = (xp * cw_ref[0:1, lo:lo + w] + x * cw_ref[1:2, lo:lo + w] + xn * cw_ref[2:3, lo:lo + w]
             + cb_ref[:, lo:lo + w])
        return _silu(y)

    xs = conv(x_ref, xp_ref, xn_ref, 0)
    bc = conv(bc_ref, bcp_ref, bcn_ref, 1)
    bco_ref[0] = bc.astype(BF16)
    bm_f = bc[:, 0:SSM_GROUPS * SSM_STATE]
    q = SSM_CHUNK
    for j in range(bc.shape[0] // q):
        bt_ref[0, j] = bm_f[j * q:(j + 1) * q, :].T.astype(BF16)
    dt = _softplus(m_ref[0][:, MISC_DT:MISC_DT + LANES] + dtb_ref[...])
    lane = lax.broadcasted_iota(jnp.int32, (1, LANES), 1)
    dt = jnp.where(lane < 2 * SSM_HEADS, dt, 0.0)
    dtx = _dot_exact_rhs(dt, exp_ref[...])
    a_all = dt * aneg_ref[...]
    first = lane < SSM_HEADS
    a_dirs = [jnp.where(first, a_all, 0.0), jnp.where(first, pltpu.roll(a_all, LANES - SSM_HEADS, 1), 0.0)]
    for d in range(2):
        xq_ref[d, 0] = xs * dtx[:, d * w:(d + 1) * w]
        a_ref[d, 0] = a_dirs[d]
        for j in range(bc.shape[0] // q):
            at_ref[d, 0, j] = a_dirs[d][j * q:(j + 1) * q, :].T[0:2 * SUBLANES, :]
    dskip_ref[0] = xs * dsk_ref[...]


def prep_ssd(p3, conv_w, conv_b, dt_bias, a_log, d_skip, n_ctx):
    bsz, lt, _ = p3.shape
    tm = PREP_TM
    w = BRANCH_W
    n_tiles = lt // tm
    nh2 = 2 * SSM_HEADS
    pad_lanes = lambda v: jnp.pad(v.reshape(1, nh2), ((0, 0), (0, LANES - nh2)))
    expand = np.zeros((LANES, 2 * w), np.float32)
    for d in range(2):
        for h in range(SSM_HEADS):
            expand[d * SSM_HEADS + h, d * w + h * SSM_HEAD_DIM:d * w + (h + 1) * SSM_HEAD_DIM] = 1.0
    specs = [_row_spec(CB_CX), *_halo_specs(CB_CX, tm, lt), _row_spec(CB_CBC), *_halo_specs(CB_CBC, tm, lt),
             _row_spec(CB_MISC)]
    tok = pl.BlockSpec((1, tm, w), lambda b, i: (b, i, 0))
    gs = SSM_GROUPS * SSM_STATE
    kern = functools.partial(_prep_ssd_kernel, n_tiles=n_tiles, n_ctx_tiles=n_ctx // tm)
    return pl.pallas_call(
        kern,
        grid=(bsz, n_tiles),
        in_specs=specs + [_const_spec((3, 2 * w)), _const_spec((1, 2 * w)), _const_spec((1, LANES)),
                          _const_spec((1, LANES)), _const_spec((LANES, 2 * w)), _const_spec((1, w))],
        out_specs=[pl.BlockSpec((2, 1, tm, w), lambda b, i: (0, b, i, 0)), tok,
                   pl.BlockSpec((1, tm // SSM_CHUNK, gs, SSM_CHUNK), lambda b, i: (b, i, 0, 0)),
                   pl.BlockSpec((2, 1, tm, LANES), lambda b, i: (0, b, i, 0)),
                   pl.BlockSpec((2, 1, tm // SSM_CHUNK, 2 * SUBLANES, SSM_CHUNK), lambda b, i: (0, b, i, 0, 0)), tok],
        out_shape=[jax.ShapeDtypeStruct((2, bsz, lt, w), F32),
                   jax.ShapeDtypeStruct((bsz, lt, w), BF16),
                   jax.ShapeDtypeStruct((bsz, lt // SSM_CHUNK, gs, SSM_CHUNK), BF16),
                   jax.ShapeDtypeStruct((2, bsz, lt, LANES), F32),
                   jax.ShapeDtypeStruct((2, bsz, lt // SSM_CHUNK, 2 * SUBLANES, SSM_CHUNK), F32),
                   jax.ShapeDtypeStruct((bsz, lt, w), F32)],
        compiler_params=_cparams(2),
        name="prep_ssd",
    )(*([p3] * 7), conv_w, conv_b.reshape(1, 2 * w), pad_lanes(dt_bias), pad_lanes(-jnp.exp(a_log)),
      jnp.asarray(expand, BF16), jnp.repeat(d_skip, SSM_HEAD_DIM).reshape(1, w))


def _ssd_kernel(tri_ref, mi_ref, xq_ref, bc_ref, bt_ref, a_ref, at_ref, y_ref, s_ref):
    q = SSM_CHUNK
    nsub = bt_ref.shape[1]
    rep = SSM_HEADS // SSM_GROUPS
    gw = SSM_GROUPS * SSM_STATE
    p = SSM_HEAD_DIM
    d = pl.program_id(0)

    @pl.when(pl.program_id(2) == 0)
    def _():
        s_ref[...] = jnp.zeros_like(s_ref)

    tri = tri_ref[0]
    mask = mi_ref[0] > 0.5
    hs = range(SSM_HEADS)
    groups = [slice(g * SSM_STATE, (g + 1) * SSM_STATE) for g in range(SSM_GROUPS)]

    rows, cms, y_in, e_col, e_tot, upd = [], [], [], [], [], []
    for j in range(nsub):
        ch = jnp.where(d == 0, j, nsub - 1 - j)
        rs = pl.ds(pl.multiple_of(ch * q, q), q)
        rows.append(rs)
        a = a_ref[0, 0, rs, :]
        a_t = at_ref[0, 0, ch]
        xq = xq_ref[0, 0, rs, :]
        bm = bc_ref[0, rs, 0:gw]
        cm = bc_ref[0, rs, gw:2 * gw]
        bt = bt_ref[0, ch]
        acol = _dot_exact_lhs(tri, a)
        arow = _dot_nt_exact_lhs_f32(a_t, tri)
        tot = jnp.sum(a, axis=0, keepdims=True)
        cm_g = [cm[:, gs] for gs in groups]
        cb = [_dot_nt(cm_g[g], bm[:, gs]) for g, gs in enumerate(groups)]
        ac = [jnp.broadcast_to(acol[:, h:h + 1], (q, q)) for h in hs]
        ar = [jnp.broadcast_to(arow[h:h + 1, :], (q, q)) for h in hs]
        gmat = [(cb[h // rep] * jnp.exp(jnp.where(mask, ac[h] - ar[h], NEG_INF))).astype(BF16) for h in hs]
        xh = [xq[:, h * p:(h + 1) * p] for h in hs]
        tot_h = [jnp.broadcast_to(tot[:, h:h + 1], (1, p)) for h in hs]
        xd = [(xh[h] * jnp.exp(tot_h[h] - ac[h][:, :p])).astype(BF16) for h in hs]
        cms.append(cm_g)
        y_in.append([_dot(gmat[h], xh[h].astype(BF16)) for h in hs])
        e_col.append([jnp.exp(ac[h][:, :p]) for h in hs])
        e_tot.append([jnp.exp(tot_h[h]) for h in hs])
        upd.append([_dot(bt[groups[h // rep], :], xd[h]) for h in hs])

    state = [s_ref[h] for h in hs]
    for j in range(nsub):
        y_st = [_dot(cms[j][h // rep], state[h].astype(BF16)) for h in hs]
        for h in hs:
            y_ref[0, 0, rows[j], h * p:(h + 1) * p] = y_in[j][h] + y_st[h] * e_col[j][h]
            state[h] = state[h] * e_tot[j][h] + upd[j][h]
    for h in hs:
        s_ref[h] = state[h]


def ssd_scan(xq, bc, bt, a, a_t, n_ctx):
    _, bsz, lt, w = xq.shape
    q = SSM_CHUNK
    blk = SSM_NSUB * q
    nck = lt // blk
    ncc = n_ctx // blk
    gw = bt.shape[2]
    tri, _ = _direction_masks(q)

    def cidx(d, s):
        return _scan_chunk_index(d, s, ncc, nck)

    mask_spec = pl.BlockSpec((1, q, q), lambda d, bi, s: (d, 0, 0))
    return pl.pallas_call(
        _ssd_kernel,
        grid=(2, bsz, nck),
        in_specs=[mask_spec, mask_spec,
                  pl.BlockSpec((1, 1, blk, w), lambda d, bi, s: (d, bi, cidx(d, s), 0)),
                  pl.BlockSpec((1, blk, w), lambda d, bi, s: (bi, cidx(d, s), 0)),
                  pl.BlockSpec((1, SSM_NSUB, gw, q), lambda d, bi, s: (bi, cidx(d, s), 0, 0)),
                  pl.BlockSpec((1, 1, blk, LANES), lambda d, bi, s: (d, bi, cidx(d, s), 0)),
                  pl.BlockSpec((1, 1, SSM_NSUB, 2 * SUBLANES, q), lambda d, bi, s: (d, bi, cidx(d, s), 0, 0))],
        out_specs=pl.BlockSpec((1, 1, blk, w), lambda d, bi, s: (d, bi, cidx(d, s), 0)),
        out_shape=jax.ShapeDtypeStruct((2, bsz, lt, w), F32),
        scratch_shapes=[pltpu.VMEM((SSM_HEADS, SSM_STATE, SSM_HEAD_DIM), F32)],
        compiler_params=_cparams(3),
        name="ssd_scan",
    )(jnp.asarray(tri, BF16), jnp.asarray(tri, F32), xq, bc, bt, a, a_t)


def _finish_ssd_kernel(y_ref, dskip_ref, z_ref, nw_ref, g256_ref, o_ref):
    y = y_ref[0, 0] + y_ref[1, 0] + dskip_ref[0]
    g = y * _silu(z_ref[0])
    group = BRANCH_W // SSM_GROUPS
    o_ref[0] = _head_rms(g, g256_ref[...], group, nw_ref[...]).astype(BF16)


def finish_ssd(y, dskip, p3, norm_w):
    _, bsz, lt, w = y.shape
    tm = _tile_rows(lt)
    tok = pl.BlockSpec((1, tm, w), lambda b, i: (b, i, 0))
    return pl.pallas_call(
        _finish_ssd_kernel,
        grid=(bsz, lt // tm),
        in_specs=[pl.BlockSpec((2, 1, tm, w), lambda b, i: (0, b, i, 0)), tok, _row_spec(CB_CZ, tm),
                  _const_spec((1, w)), _const_spec((w, w))],
        out_specs=tok,
        out_shape=jax.ShapeDtypeStruct((bsz, lt, w), BF16),
        compiler_params=_cparams(2),
        name="finish_ssd",
    )(y, dskip, p3, norm_w.reshape(1, w), _group_matrix(w, w // SSM_GROUPS))


def _permute_w_in(w_in_l):
    w = BRANCH_W
    b0 = 4 * w
    lora0 = b0 + 3 * w
    bg0 = lora0 + MISC_DT
    c0 = bg0 + w
    dt0 = c0 + SSM_CONV_CH
    z0 = dt0 + 2 * SSM_HEADS
    d0 = z0 + w
    end = d0 + 4 * w
    parts = [w_in_l[:, 0:lora0], w_in_l[:, bg0:c0], w_in_l[:, c0:dt0], w_in_l[:, z0:d0], w_in_l[:, d0:end],
             w_in_l[:, lora0:bg0], w_in_l[:, dt0:z0]]
    used = sum(p.shape[1] for p in parts)
    pad = jnp.zeros((w_in_l.shape[0], IN_W_PAD - used), BF16)
    return jnp.concatenate([p.astype(BF16) for p in parts] + [pad], axis=1)


def kernel(x, c, ctx, c_ctx, norm_w, w_ada, b_ada, w_in, na_q_norm, na_k_norm, na_rpb, rw_mu, rw_w0, rw_w2, rw_a0, rw_a2, rw_k_k, rw_k_a, rw_r_k, rw_ln_w, rw_ln_b, ssm_conv_w, ssm_conv_b, ssm_dt_bias, ssm_A_log, ssm_D, ssm_norm_w, da_q_norm, da_k_norm, da_lq1, da_lk1, da_lq2, da_lk2, da_subln, w_gate, w_up, w_out):
    bsz, seq, d = x.shape
    n_ctx = ctx.shape[1]
    lt = n_ctx + seq
    depth = w_in.shape[0]
    assert n_ctx == CTX_LEN == PREP_TM == DA_TK and seq % (GRID_W * NA_WIN_R) == 0
    assert ((lt // DA_TK) - 1) % DA_NSUB == 0 and n_ctx % NORM_ROWS == 0
    cond = jnp.concatenate([_silu(c), _silu(c_ctx)[None], jnp.zeros((SUBLANES - bsz - 1, d), F32)], axis=0)
    xs2 = jnp.concatenate([ctx, x], axis=1).reshape(bsz * lt, d)
    cos, sin = rope_tables(lt, n_ctx)
    for l in range(depth):
        lam_init = 0.8 - 0.6 * math.exp(-0.3 * l)
        mod = small_matmul(cond, w_ada[l], b_ada[l])
        shift, scale, gate = jnp.split(mod[:bsz], 3, axis=-1)
        shift_c, scale_c, gate_c = jnp.split(mod[bsz], 3, axis=-1)
        p2, h = in_projection(xs2, norm_w[l], scale, shift, scale_c, shift_c, _permute_w_in(w_in[l]), lt, n_ctx)
        p3 = p2.reshape(bsz, lt, IN_W_PAD)

        kn, va, qt, kh, vt = prep_attention(p3, cos, sin, na_k_norm[l], da_q_norm[l], da_k_norm[l])
        oa = neighbourhood_attention(p3, kn, va, na_bias_table(na_rpb[l]), na_q_norm[l], n_ctx)
        lam = jnp.exp(jnp.sum(da_lq1[l] * da_lk1[l])) - jnp.exp(jnp.sum(da_lq2[l] * da_lk2[l])) + lam_init
        od = flash_diff_attention(qt, kh, vt, p3, lam, da_subln[l], lam_init)

        r, vb, a, lw, kd, b, bonus = prep_rwkv(p3, rw_mu[l], rw_w0[l], rw_w2[l], rw_a0[l], rw_a2[l],
                                              rw_k_k[l], rw_k_a[l], rw_r_k[l].reshape(-1), n_ctx)
        ob = finish_rwkv(rwkv_scan(r, vb, a, lw, kd, b, n_ctx), bonus, p3, rw_ln_w[l], rw_ln_b[l])

        xq, bc, bt, sa, sat, dskip = prep_ssd(p3, ssm_conv_w[l], ssm_conv_b[l], ssm_dt_bias[l], ssm_A_log[l],
                                              ssm_D[l], n_ctx)
        om = finish_ssd(ssd_scan(xq, bc, bt, sa, sat, n_ctx), dskip, p3, ssm_norm_w[l])

        ys = [t.reshape(bsz * lt, BRANCH_W) for t in (oa, ob, om, od)]
        acc = gate_up(h, ys, w_gate[l].astype(BF16), w_up[l].astype(BF16), lt)
        xs2 = out_projection(acc, w_out[l].astype(BF16), xs2, gate, gate_c, lt, n_ctx)
    return xs2.reshape(bsz, lt, d)[:, n_ctx:]
```

```python
import functools
import math

import numpy as np
import jax
import jax.numpy as jnp
from jax import lax
from jax.experimental import pallas as pl
from jax.experimental.pallas import tpu as pltpu

F32 = jnp.float32
BF16 = jnp.bfloat16

D_MODEL = 2048
GRID_W = 64
CTX_LEN = 256
N_BRANCH = 4
BRANCH_W = D_MODEL // N_BRANCH
NORM_EPS = 1e-6
NEG_INF = -1e30

NA_DIM = 64
NA_HEADS = BRANCH_W // NA_DIM
NA_WIN_R = 8
NA_WIN_C = 16
NA_NBLK = 2

RW_DIM = 64
RW_HEADS = BRANCH_W // RW_DIM
RW_LORA_W = 64
RW_LORA_A = 64
RW_GN_EPS = 64e-5
RW_KK_EPS = 1e-12
RW_CHUNK = 64
RW_NSUB = 4

SSM_HEAD_DIM = 64
SSM_HEADS = BRANCH_W // SSM_HEAD_DIM
SSM_GROUPS = 2
SSM_STATE = 128
SSM_CHUNK = 128
SSM_NSUB = 2
SSM_CONV_CH = BRANCH_W + 2 * SSM_GROUPS * SSM_STATE

DA_DIM = 64
DA_HEADS = BRANCH_W // (2 * DA_DIM)
ROPE_BASE = 10000.0
DA_TQ = 256
DA_TK = 256
DA_NSUB = 16
DA_HSTEP = 2
DA_ONES = 16

SUBLANES = 8
LANES = 128

CB_AQ, CB_AK, CB_AV, CB_AG = 0, 1, 2, 3
CB_BR, CB_BK, CB_BV, CB_BG = 4, 5, 6, 7
CB_CX, CB_CBC, CB_CZ = 8, 9, 10
CB_DQ, CB_DK, CB_DV, CB_DG = 11, 12, 13, 14
CB_MISC = 15
MISC_DT = 2 * RW_LORA_W + 2 * RW_LORA_A
N_COL_BLOCKS = 16
IN_W_PAD = N_COL_BLOCKS * BRANCH_W
PREP_TM = 256
MATMUL_TM = 768
NORM_ROWS = 32

V7X_VMEM_BYTES = 64 * 1024 * 1024
VMEM_LIMIT = V7X_VMEM_BYTES * 7 // 8


def _cparams(n_axes):
    return pltpu.CompilerParams(dimension_semantics=("arbitrary",) * n_axes, vmem_limit_bytes=VMEM_LIMIT)


def _dot(a, b):
    return jnp.dot(a, b, preferred_element_type=F32)


def _dot_nt(a, b):
    return lax.dot_general(a, b, (((1,), (1,)), ((), ())), preferred_element_type=F32)


def _dot_tn(a, b):
    return lax.dot_general(a, b, (((0,), (0,)), ((), ())), preferred_element_type=F32)


def _split3(x):
    hi = x.astype(BF16)
    r1 = x - hi.astype(F32)
    mid = r1.astype(BF16)
    lo = (r1 - mid.astype(F32)).astype(BF16)
    return hi, mid, lo


def _dot_exact_lhs(m_bf16, x):
    hi, mid, lo = _split3(x)
    return _dot(m_bf16, hi) + _dot(m_bf16, mid) + _dot(m_bf16, lo)


def _dot_exact_rhs(x, m_bf16):
    hi, mid, lo = _split3(x)
    return _dot(hi, m_bf16) + _dot(mid, m_bf16) + _dot(lo, m_bf16)


def _group_sum(x, gmat_bf16):
    hi = x.astype(BF16)
    lo = (x - hi.astype(F32)).astype(BF16)
    return _dot(hi, gmat_bf16) + _dot(lo, gmat_bf16)


def _dot_nt_exact_lhs_f32(x, m_bf16):
    hi, mid, lo = _split3(x)
    return _dot_nt(hi, m_bf16) + _dot_nt(mid, m_bf16) + _dot_nt(lo, m_bf16)


def _sigmoid(x):
    return 1.0 / (1.0 + jnp.exp(-x))


def _silu(x):
    return x * _sigmoid(x)


def _softplus(x):
    return jnp.maximum(x, 0.0) + jnp.log(1.0 + jnp.exp(-jnp.abs(x)))


def _group_matrix(width, group):
    g = np.arange(width) // group
    return jnp.asarray((g[:, None] == g[None, :]).astype(np.float32), BF16)


def _head_rms(x, gmat, group, w):
    ms = _group_sum(x * x, gmat) * (1.0 / group)
    return x * lax.rsqrt(ms + NORM_EPS) * w


def _tile_rows(n_rows):
    return MATMUL_TM if n_rows % MATMUL_TM == 0 else PREP_TM


def _row_spec(cb, tm=PREP_TM):
    return pl.BlockSpec((1, tm, BRANCH_W), lambda b, i: (b, i, cb))


def _halo_specs(cb, tm, lt):
    per = tm // SUBLANES
    last = lt // SUBLANES - 1
    prev = pl.BlockSpec((1, SUBLANES, BRANCH_W), lambda b, i: (b, jnp.maximum(i * per - 1, 0), cb))
    nxt = pl.BlockSpec((1, SUBLANES, BRANCH_W), lambda b, i: (b, jnp.minimum((i + 1) * per, last), cb))
    return prev, nxt


def _const_spec(shape):
    return pl.BlockSpec(shape, lambda *_: (0,) * len(shape))


def _neighbours(x, prev_blk, next_blk, i, n_tiles, n_ctx_tiles):
    tm = x.shape[0]
    row = lax.broadcasted_iota(jnp.int32, (tm, 1), 0)
    seg_start = jnp.logical_or(i == 0, i == n_ctx_tiles)
    seg_end = jnp.logical_or(i == n_ctx_tiles - 1, i == n_tiles - 1)
    prev_row = jnp.where(seg_start, 0.0, prev_blk[SUBLANES - 1:SUBLANES, :])
    next_row = jnp.where(seg_end, 0.0, next_blk[0:1, :])
    x_prev = jnp.where(row == 0, prev_row, pltpu.roll(x, 1, 0))
    x_next = jnp.where(row == tm - 1, next_row, pltpu.roll(x, tm - 1, 0))
    return x_prev, x_next


def _small_mm_kernel(a_ref, w_ref, b_ref, o_ref):
    o_ref[...] = _dot(a_ref[...].astype(BF16), w_ref[...].astype(BF16)) + b_ref[...]


def small_matmul(a, w, b, tn=512):
    m, k = a.shape
    n = w.shape[1]
    return pl.pallas_call(
        _small_mm_kernel,
        grid=(n // tn,),
        in_specs=[pl.BlockSpec((m, k), lambda j: (0, 0)),
                  pl.BlockSpec((k, tn), lambda j: (0, j)),
                  pl.BlockSpec((1, tn), lambda j: (0, j))],
        out_specs=pl.BlockSpec((m, tn), lambda j: (0, j)),
        out_shape=jax.ShapeDtypeStruct((m, n), F32),
        compiler_params=_cparams(1),
        name="adaln_mm",
    )(a, w, b.reshape(1, n))


def _inproj_kernel(x_ref, nw_ref, sc_ref, sh_ref, scc_ref, shc_ref, w_ref, p_ref, h_ref, hs_ref, *,
                   tiles_per_batch, n_ctx):
    @pl.when(pl.program_id(1) == 0)
    def _():
        tm = x_ref.shape[0]
        rc = NORM_ROWS
        base = (pl.program_id(0) % tiles_per_batch) * tm

        def norm_rows(k, carry):
            r0 = pl.multiple_of(k * rc, rc)
            xf = x_ref[pl.ds(r0, rc), :]
            ms = jnp.mean(xf * xf, axis=-1, keepdims=True)
            y = xf * lax.rsqrt(ms + NORM_EPS) * nw_ref[...]
            is_ctx = base + r0 < n_ctx
            sc = 1.0 + jnp.where(is_ctx, scc_ref[...], sc_ref[0])
            sh = jnp.where(is_ctx, shc_ref[...], sh_ref[0])
            h = (y * sc + sh).astype(BF16)
            hs_ref[pl.ds(r0, rc), :] = h
            h_ref[pl.ds(r0, rc), :] = h
            return carry

        lax.fori_loop(0, tm // rc, norm_rows, 0, unroll=4)

    p_ref[...] = _dot(hs_ref[...], w_ref[...])


def in_projection(xs2, norm_w, scale, shift, scale_c, shift_c, w_bf16, lt, n_ctx, tn=2048):
    m, d = xs2.shape
    n = w_bf16.shape[1]
    tm = _tile_rows(lt)
    tpb = lt // tm
    nb = scale.shape[0]
    kern = functools.partial(_inproj_kernel, tiles_per_batch=tpb, n_ctx=n_ctx)
    return pl.pallas_call(
        kern,
        grid=(m // tm, n // tn),
        in_specs=[pl.BlockSpec((tm, d), lambda i, j: (i, 0)),
                  pl.BlockSpec((1, d), lambda i, j: (0, 0)),
                  pl.BlockSpec((1, 1, d), lambda i, j: (i // tpb, 0, 0)),
                  pl.BlockSpec((1, 1, d), lambda i, j: (i // tpb, 0, 0)),
                  pl.BlockSpec((1, d), lambda i, j: (0, 0)),
                  pl.BlockSpec((1, d), lambda i, j: (0, 0)),
                  pl.BlockSpec((d, tn), lambda i, j: (0, j))],
        out_specs=[pl.BlockSpec((tm, tn), lambda i, j: (i, j)),
                   pl.BlockSpec((tm, d), lambda i, j: (i, 0))],
        out_shape=[jax.ShapeDtypeStruct((m, n), F32),
                   jax.ShapeDtypeStruct((m, d), BF16)],
        scratch_shapes=[pltpu.VMEM((tm, d), BF16)],
        compiler_params=_cparams(2),
        name="in_proj",
    )(xs2, norm_w.reshape(1, d), scale.reshape(nb, 1, d), shift.reshape(nb, 1, d),
      scale_c.reshape(1, d), shift_c.reshape(1, d), w_bf16)


def _gate_up_kernel(h_ref, ya_ref, yb_ref, yc_ref, yd_ref, wg_ref, wu_ref, o_ref):
    h = h_ref[...]
    acc = None
    for i, y_ref in enumerate((ya_ref, yb_ref, yc_ref, yd_ref)):
        g = _dot(h, wg_ref[i])
        u = _dot(y_ref[...], wu_ref[i])
        t = _sigmoid(g) * u
        acc = t if acc is None else acc + t
    o_ref[...] = acc.astype(BF16)


def gate_up(h, ys, wg_bf16, wu_bf16, lt, tn=512):
    m, d = h.shape
    w = ys[0].shape[1]
    tm = _tile_rows(lt)
    y_spec = pl.BlockSpec((tm, w), lambda i, j: (i, 0))
    return pl.pallas_call(
        _gate_up_kernel,
        grid=(m // tm, d // tn),
        in_specs=[pl.BlockSpec((tm, d), lambda i, j: (i, 0)), y_spec, y_spec, y_spec, y_spec,
                  pl.BlockSpec((N_BRANCH, d, tn), lambda i, j: (0, 0, j)),
                  pl.BlockSpec((N_BRANCH, w, tn), lambda i, j: (0, 0, j))],
        out_specs=pl.BlockSpec((tm, tn), lambda i, j: (i, j)),
        out_shape=jax.ShapeDtypeStruct((m, d), BF16),
        compiler_params=_cparams(2),
        name="gate_up",
    )(h, *ys, wg_bf16, wu_bf16)


def _out_proj_kernel(a_ref, w_ref, x_ref, g_ref, gc_ref, o_ref, *, tiles_per_batch, n_ctx):
    tm = x_ref.shape[0]
    row = (pl.program_id(0) % tiles_per_batch) * tm + lax.broadcasted_iota(jnp.int32, (tm, 1), 0)
    gate = jnp.where(row < n_ctx, gc_ref[...], g_ref[0])
    o_ref[...] = x_ref[...] + gate * _dot(a_ref[...], w_ref[...])


def out_projection(acc, w_bf16, xs2, gate, gate_c, lt, n_ctx, tn=1024):
    m, d = xs2.shape
    tm = _tile_rows(lt)
    tpb = lt // tm
    nb = gate.shape[0]
    kern = functools.partial(_out_proj_kernel, tiles_per_batch=tpb, n_ctx=n_ctx)
    return pl.pallas_call(
        kern,
        grid=(m // tm, d // tn),
        in_specs=[pl.BlockSpec((tm, d), lambda i, j: (i, 0)),
                  pl.BlockSpec((d, tn), lambda i, j: (0, j)),
                  pl.BlockSpec((tm, tn), lambda i, j: (i, j)),
                  pl.BlockSpec((1, 1, tn), lambda i, j: (i // tpb, 0, j)),
                  pl.BlockSpec((1, tn), lambda i, j: (0, j))],
        out_specs=pl.BlockSpec((tm, tn), lambda i, j: (i, j)),
        out_shape=jax.ShapeDtypeStruct((m, d), F32),
        compiler_params=_cparams(2),
        name="out_proj",
    )(acc, w_bf16, xs2, gate.reshape(nb, 1, d), gate_c.reshape(1, d))


def _rope(x, cos, sin):
    w = x.shape[1]
    lane = lax.broadcasted_iota(jnp.int32, (1, w), 1)
    first = ((lane // (DA_DIM // 4)) % 2) == 0
    rot = jnp.where(first, -pltpu.roll(x, w - DA_DIM // 4, 1), pltpu.roll(x, DA_DIM // 4, 1))
    return x * cos + rot * sin


def _prep_attn_kernel(ak_ref, av_ref, dq_ref, dk_ref, dv_ref, cos_ref, sin_ref, g64_ref,
                      wak_ref, wdq_ref, wdk_ref, kn_ref, va_ref, qt_ref, kh_ref, vt_ref):
    g64 = g64_ref[...]
    reps = BRANCH_W // LANES
    cos = jnp.concatenate([cos_ref[...]] * reps, axis=1)
    sin = jnp.concatenate([sin_ref[...]] * reps, axis=1)
    kn_ref[0] = _head_rms(ak_ref[0], g64, NA_DIM, wak_ref[...]).astype(BF16)
    va_ref[0] = av_ref[0].astype(BF16)
    q = _rope(_head_rms(dq_ref[0], g64, DA_DIM, wdq_ref[...]), cos, sin)
    k = _rope(_head_rms(dk_ref[0], g64, DA_DIM, wdk_ref[...]), cos, sin).astype(BF16)
    qt_ref[0] = q.T.astype(BF16)
    for hc in range(2 * DA_HEADS):
        kh_ref[0, hc] = k[:, hc * DA_DIM:(hc + 1) * DA_DIM]
    v = dv_ref[0]
    dv = 2 * DA_DIM
    for h in range(DA_HEADS):
        vt_ref[0, h, 0, 0:dv, :] = v[:, h * dv:(h + 1) * dv].T.astype(BF16)
        vt_ref[0, h, 0, dv:dv + DA_ONES, :] = jnp.ones((DA_ONES, v.shape[0]), BF16)


def prep_attention(p3, cos, sin, na_k_w, da_q_w, da_k_w):
    bsz, lt, _ = p3.shape
    tm = PREP_TM
    w = BRANCH_W
    tile = lambda v, s=1.0: (jnp.tile(v, w // v.shape[0]) * s).reshape(1, w)
    tab_spec = pl.BlockSpec((tm, LANES), lambda b, i: (i, 0))
    vec = _const_spec((1, w))
    return pl.pallas_call(
        _prep_attn_kernel,
        grid=(bsz, lt // tm),
        in_specs=[_row_spec(CB_AK), _row_spec(CB_AV), _row_spec(CB_DQ), _row_spec(CB_DK), _row_spec(CB_DV),
                  tab_spec, tab_spec, _const_spec((w, w)), vec, vec, vec],
        out_specs=[pl.BlockSpec((1, tm, w), lambda b, i: (b, i, 0)),
                   pl.BlockSpec((1, tm, w), lambda b, i: (b, i, 0)),
                   pl.BlockSpec((1, w, tm), lambda b, i: (b, 0, i)),
                   pl.BlockSpec((1, 2 * DA_HEADS, tm, DA_DIM), lambda b, i: (b, 0, i, 0)),
                   pl.BlockSpec((1, DA_HEADS, 1, 2 * DA_DIM + DA_ONES, tm), lambda b, i: (b, 0, i, 0, 0))],
        out_shape=[jax.ShapeDtypeStruct((bsz, lt, w), BF16),
                   jax.ShapeDtypeStruct((bsz, lt, w), BF16),
                   jax.ShapeDtypeStruct((bsz, w, lt), BF16),
                   jax.ShapeDtypeStruct((bsz, 2 * DA_HEADS, lt, DA_DIM), BF16),
                   jax.ShapeDtypeStruct((bsz, DA_HEADS, lt // tm, 2 * DA_DIM + DA_ONES, tm), BF16)],
        compiler_params=_cparams(2),
        name="prep_attn",
    )(p3, p3, p3, p3, p3, cos, sin, _group_matrix(w, DA_DIM),
      tile(na_k_w), tile(da_q_w, DA_DIM ** -0.5 * math.log2(math.e)), tile(da_k_w))


def rope_tables(lt, n_ctx):
    nf = DA_DIM // 4
    t = jnp.arange(lt - n_ctx, dtype=jnp.int32)
    rows, cols = t // GRID_W, t % GRID_W
    inv = ROPE_BASE ** (-jnp.arange(nf, dtype=F32) / nf)
    ang_r = rows.astype(F32)[:, None] * inv
    ang_c = cols.astype(F32)[:, None] * inv
    ang = jnp.concatenate([ang_r, ang_r, ang_c, ang_c], axis=-1)
    ang = jnp.concatenate([jnp.zeros((n_ctx, DA_DIM), F32), ang], axis=0)
    reps = LANES // DA_DIM
    return jnp.tile(jnp.cos(ang), (1, reps)), jnp.tile(jnp.sin(ang), (1, reps))


def _flash_kernel(lam_ref, qt_ref, k_ref, vt_ref, g_ref, sw_ref, o_ref, *, n_latent_iters, out_scale):
    tq = qt_ref.shape[2]
    nh = vt_ref.shape[1]
    dve = vt_ref.shape[3]
    dv = dve - DA_ONES
    tk = vt_ref.shape[4]
    chains = range(2 * nh)
    qts = [qt_ref[0, n * DA_DIM:(n + 1) * DA_DIM, :] for n in chains]
    gate = _silu(g_ref[0])

    def attend(carry, first_chunk, n_chunks):
        chunks = [first_chunk + g for g in range(n_chunks)]
        sts = [[_dot(k_ref[0, n, pl.ds(pl.multiple_of(ch * tk, tk), tk), :], qts[n]) for n in chains]
               for ch in chunks]
        carry = list(carry)
        for g, ch in enumerate(chunks):
            for n in chains:
                m, acc = carry[n]
                st = sts[g][n].astype(BF16)
                m_new = jnp.maximum(m, jnp.max(st, axis=0, keepdims=True).astype(F32))
                pt = jnp.exp2(st - m_new.astype(BF16))
                carry[n] = (m_new, jnp.exp2(m - m_new) * acc + _dot(vt_ref[0, n // 2, ch], pt))
        return tuple(carry)

    init = tuple((jnp.full((1, tq), NEG_INF, F32), jnp.zeros((dve, tq), F32)) for _ in chains)
    carry = attend(init, 0, 1)
    n_iters = jnp.where(pl.program_id(2) == 0, 0, n_latent_iters)
    res = lax.fori_loop(0, n_iters, lambda i, cr: attend(cr, 1 + i * DA_NSUB, DA_NSUB), carry)
    outs = [acc[0:dv] * (1.0 / acc[dv:dv + 1]) for (_, acc) in res]
    for hh in range(nh):
        ot = outs[2 * hh] - lam_ref[0] * outs[2 * hh + 1]
        ot = ot * lax.rsqrt(jnp.mean(ot * ot, axis=0, keepdims=True) + NORM_EPS)
        o = ot.T * (sw_ref[...] * out_scale)
        o_ref[0, :, hh * dv:(hh + 1) * dv] = (o * gate[:, hh * dv:(hh + 1) * dv]).astype(BF16)


def flash_diff_attention(qt, kh, vt, p3, lam, subln_w, lam_init):
    bsz, w, lt = qt.shape
    dv = 2 * DA_DIM
    hs = DA_HSTEP
    nk = lt // DA_TK
    n_latent_iters = (nk - 1) // DA_NSUB
    kern = functools.partial(_flash_kernel, n_latent_iters=n_latent_iters, out_scale=1.0 - lam_init)
    g_blocks = BRANCH_W // (hs * dv)
    return pl.pallas_call(
        kern,
        grid=(bsz, DA_HEADS // hs, lt // DA_TQ),
        in_specs=[pl.BlockSpec(memory_space=pltpu.SMEM),
                  pl.BlockSpec((1, hs * dv, DA_TQ), lambda b, h, qi: (b, h, qi)),
                  pl.BlockSpec((1, 2 * hs, lt, DA_DIM), lambda b, h, qi: (b, h, 0, 0)),
                  pl.BlockSpec((1, hs, nk, dv + DA_ONES, DA_TK), lambda b, h, qi: (b, h, 0, 0, 0)),
                  pl.BlockSpec((1, DA_TQ, hs * dv), lambda b, h, qi: (b, qi, CB_DG * g_blocks + h)),
                  _const_spec((1, dv))],
        out_specs=pl.BlockSpec((1, DA_TQ, hs * dv), lambda b, h, qi: (b, qi, h)),
        out_shape=jax.ShapeDtypeStruct((bsz, lt, BRANCH_W), BF16),
        compiler_params=_cparams(3),
        name="flash_attn",
    )(lam.reshape(1).astype(F32), qt, kh, vt, p3, subln_w.reshape(1, dv))


def _na_kernel(q_ref, g_ref, k_ref, v_ref, *rest, n_rows, n_ctx):
    bias_refs = rest[:NA_NBLK]
    g64_ref, wq_ref, o_ref = rest[NA_NBLK:]
    n_loc = NA_WIN_R * GRID_W
    q = _head_rms(q_ref[0], g64_ref[...], NA_DIM, wq_ref[...]).astype(BF16)
    g = g_ref[0]
    pair = [slice(hp * LANES, (hp + 1) * LANES) for hp in range(NA_HEADS // 2)]
    low_half = lax.broadcasted_iota(jnp.int32, (1, LANES), 1) < NA_DIM
    rows, starts, units = [], [], []
    for j in range(NA_NBLK):
        blk = pl.program_id(1) * NA_NBLK + j
        r = jnp.maximum(blk - n_ctx // GRID_W, 0)
        r0 = jnp.clip(r - NA_WIN_R // 2, 0, n_rows - NA_WIN_R)
        starts.append(pl.multiple_of(n_ctx + r0 * GRID_W, GRID_W))
        rows.append(slice(j * GRID_W, (j + 1) * GRID_W))
        units += [(j, hp) for hp in range(NA_HEADS // 2)]
    zero = jnp.zeros((), BF16)
    cat = lambda x, y: jnp.concatenate([x, y], axis=0)
    q2 = [cat(jnp.where(low_half, q[rows[j], pair[hp]], zero), jnp.where(low_half, zero, q[rows[j], pair[hp]]))
          for j, hp in units]
    s_loc = [_dot_nt(q2[u], k_ref[0, pl.ds(starts[j], n_loc), pair[hp]])
             + cat(bias_refs[j][0, 2 * hp], bias_refs[j][0, 2 * hp + 1]) for u, (j, hp) in enumerate(units)]
    s_ctx = [_dot_nt(q2[u], k_ref[0, 0:n_ctx, pair[hp]]) for u, (j, hp) in enumerate(units)]
    us = range(len(units))
    m = [jnp.maximum(jnp.max(s_loc[u], axis=-1, keepdims=True), jnp.max(s_ctx[u], axis=-1, keepdims=True))
         for u in us]
    p_loc = [jnp.exp(s_loc[u] - m[u]) for u in us]
    p_ctx = [jnp.exp(s_ctx[u] - m[u]) for u in us]
    l = [jnp.sum(p_loc[u], axis=-1, keepdims=True) + jnp.sum(p_ctx[u], axis=-1, keepdims=True) for u in us]
    o = [(_dot(p_loc[u].astype(BF16), v_ref[0, pl.ds(starts[j], n_loc), pair[hp]])
          + _dot(p_ctx[u].astype(BF16), v_ref[0, 0:n_ctx, pair[hp]])) / l[u]
         for u, (j, hp) in enumerate(units)]
    for u, (j, hp) in enumerate(units):
        o_pair = jnp.where(low_half, o[u][0:GRID_W], o[u][GRID_W:2 * GRID_W])
        o_ref[0, rows[j], pair[hp]] = (o_pair * _silu(g[rows[j], pair[hp]])).astype(BF16)


def neighbourhood_attention(p3, kn, va, bias_tbl, na_q_w, n_ctx):
    bsz, lt, w = kn.shape
    n_rows = (lt - n_ctx) // GRID_W
    ncb = n_ctx // GRID_W
    half = NA_WIN_R // 2

    def bias_spec(j):
        def idx(b, s):
            blk = s * NA_NBLK + j
            r = blk - ncb
            off = r - jnp.clip(r - half, 0, n_rows - NA_WIN_R)
            return (jnp.where(blk < ncb, NA_WIN_R, off), 0, 0, 0)
        return pl.BlockSpec((1, NA_HEADS, GRID_W, NA_WIN_R * GRID_W), idx)

    tq = NA_NBLK * GRID_W
    full_spec = pl.BlockSpec((1, lt, w), lambda b, s: (b, 0, 0))
    wq = (jnp.tile(na_q_w, w // NA_DIM) * NA_DIM ** -0.5).reshape(1, w)
    return pl.pallas_call(
        functools.partial(_na_kernel, n_rows=n_rows, n_ctx=n_ctx),
        grid=(bsz, lt // tq),
        in_specs=[_row_spec(CB_AQ, tq), _row_spec(CB_AG, tq), full_spec, full_spec,
                  *[bias_spec(j) for j in range(NA_NBLK)],
                  _const_spec((w, w)), _const_spec((1, w))],
        out_specs=pl.BlockSpec((1, tq, w), lambda b, s: (b, s, 0)),
        out_shape=jax.ShapeDtypeStruct((bsz, lt, w), BF16),
        compiler_params=_cparams(2),
        name="nbr_attn",
    )(p3, p3, kn, va, *([bias_tbl] * NA_NBLK), _group_matrix(w, NA_DIM), wq)


def na_bias_table(rpb):
    col = np.arange(GRID_W)
    c0 = np.clip(col - NA_WIN_C // 2, 0, GRID_W - NA_WIN_C)
    col_ok = (col[None, :] >= c0[:, None]) & (col[None, :] < c0[:, None] + NA_WIN_C)
    d_col = np.clip(col[None, :] - col[:, None] + (NA_WIN_C - 1), 0, 2 * NA_WIN_C - 2)
    onehot = (d_col[:, :, None] == np.arange(2 * NA_WIN_C - 1)).astype(np.float32)
    by_col = jnp.einsum('hrc,qwc->hrqw', rpb.astype(F32), jnp.asarray(onehot),
                        precision=lax.Precision.HIGHEST)
    by_col = jnp.where(jnp.asarray(col_ok)[None, None], by_col, NEG_INF)
    tbl = jnp.stack([by_col[:, NA_WIN_R - 1 - o:2 * NA_WIN_R - 1 - o] for o in range(NA_WIN_R)]
                    + [jnp.full((NA_HEADS, NA_WIN_R, GRID_W, GRID_W), NEG_INF, F32)])
    tbl = jnp.transpose(tbl, (0, 1, 3, 2, 4))
    return tbl.reshape(NA_WIN_R + 1, NA_HEADS, GRID_W, NA_WIN_R * GRID_W)


def _prep_rwkv_kernel(r_ref, rp_ref, rn_ref, k_ref, kp_ref, kn_ref, v_ref, vp_ref, vn_ref,
                      m_ref, mp_ref, mn_ref, mu_ref, w0_ref, a0_ref, w2_ref, a2_ref, kk_ref, ka_ref,
                      rk_ref, g64_ref,
                      ro_ref, vo_ref, ao_ref, lw_ref, kd_ref, bo_ref, bonus_ref, *, n_tiles, n_ctx_tiles):
    i = pl.program_id(1)
    w = BRANCH_W

    def shifted(x_ref, p_ref, n_ref, mu):
        x = x_ref[0]
        xp, xn = _neighbours(x, p_ref[0], n_ref[0], i, n_tiles, n_ctx_tiles)
        return x + (0.5 * (xp + xn) - x) * mu

    r = shifted(r_ref, rp_ref, rn_ref, mu_ref[0:1, :])
    k = shifted(k_ref, kp_ref, kn_ref, mu_ref[1:2, :])
    v = shifted(v_ref, vp_ref, vn_ref, mu_ref[2:3, :])
    misc = shifted(m_ref, mp_ref, mn_ref, mu_ref[3:4, :])
    wd = jnp.tanh(misc[:, 0:2 * RW_LORA_W]).astype(BF16)
    ad = misc[:, 2 * RW_LORA_W:MISC_DT].astype(BF16)
    w_log = w0_ref[...] + _dot(wd, w2_ref[...])
    gate = _sigmoid(a0_ref[...] + _dot(ad, a2_ref[...]))
    log_decay = -math.exp(-0.5) * _sigmoid(w_log)
    g64 = g64_ref[...]
    kk = k * kk_ref[...]
    kk = kk / jnp.maximum(jnp.sqrt(_group_sum(kk * kk, g64)), RW_KK_EPS)
    ro_ref[0] = r
    vo_ref[0] = v.astype(BF16)
    ao_ref[0] = -kk
    coef = None
    for d in range(2):
        a_d = gate[:, d * w:(d + 1) * w]
        kd = k * (1.0 + (a_d - 1.0) * ka_ref[...])
        lw_ref[d, 0] = log_decay[:, d * w:(d + 1) * w]
        kd_ref[d, 0] = kd
        bo_ref[d, 0] = kk * a_d
        coef = kd if coef is None else coef + kd
    bonus_ref[0] = _group_sum(r * coef * rk_ref[...], g64) * v


def prep_rwkv(p3, mu, w0, w2, a0, a2, k_k, k_a, r_k, n_ctx):
    bsz, lt, _ = p3.shape
    tm = PREP_TM
    w = BRANCH_W
    n_tiles = lt // tm
    mu4 = jnp.stack([mu[0:w], mu[w:2 * w], mu[2 * w:3 * w],
                     jnp.pad(mu[3 * w:], (0, w - (mu.shape[0] - 3 * w)))])
    zero = jnp.zeros((RW_LORA_W, w), F32)
    w2cat = jnp.concatenate([jnp.concatenate([w2[0], zero], axis=1),
                             jnp.concatenate([zero, w2[1]], axis=1)], axis=0).astype(BF16)
    a2cat = jnp.concatenate([jnp.concatenate([a2[0], zero], axis=1),
                             jnp.concatenate([zero, a2[1]], axis=1)], axis=0).astype(BF16)
    specs = []
    for cb in (CB_BR, CB_BK, CB_BV, CB_MISC):
        specs += [_row_spec(cb), *_halo_specs(cb, tm, lt)]
    vec = _const_spec((1, w))
    tok = pl.BlockSpec((1, tm, w), lambda b, i: (b, i, 0))
    tok2 = pl.BlockSpec((2, 1, tm, w), lambda b, i: (0, b, i, 0))
    kern = functools.partial(_prep_rwkv_kernel, n_tiles=n_tiles, n_ctx_tiles=n_ctx // tm)
    return pl.pallas_call(
        kern,
        grid=(bsz, n_tiles),
        in_specs=specs + [_const_spec((4, w)), _const_spec((1, 2 * w)), _const_spec((1, 2 * w)),
                          _const_spec((2 * RW_LORA_W, 2 * w)), _const_spec((2 * RW_LORA_A, 2 * w)),
                          vec, vec, vec, _const_spec((w, w))],
        out_specs=[tok, tok, tok, tok2, tok2, tok2, tok],
        out_shape=[jax.ShapeDtypeStruct((bsz, lt, w), F32),
                   jax.ShapeDtypeStruct((bsz, lt, w), BF16),
                   jax.ShapeDtypeStruct((bsz, lt, w), F32),
                   jax.ShapeDtypeStruct((2, bsz, lt, w), F32),
                   jax.ShapeDtypeStruct((2, bsz, lt, w), F32),
                   jax.ShapeDtypeStruct((2, bsz, lt, w), F32),
                   jax.ShapeDtypeStruct((bsz, lt, w), F32)],
        compiler_params=_cparams(2),
        name="prep_rwkv",
    )(*([p3] * 12), mu4, w0.reshape(1, 2 * w), a0.reshape(1, 2 * w), w2cat, a2cat,
      k_k.reshape(1, w), k_a.reshape(1, w), r_k.reshape(1, w), _group_matrix(w, RW_DIM))


def _rwkv_kernel(tri_ref, mq_ref, r_ref, v_ref, a_ref, lw_ref, kd_ref, b_ref, y_ref, s_ref):
    c = RW_CHUNK
    nsub = r_ref.shape[1] // c
    d = pl.program_id(0)

    @pl.when(pl.program_id(2) == 0)
    def _():
        s_ref[...] = jnp.zeros_like(s_ref)

    tri = tri_ref[0]
    m_quad = mq_ref[0] > 0.5
    rows = lax.broadcasted_iota(jnp.int32, (c, c), 0)
    cols = lax.broadcasted_iota(jnp.int32, (c, c), 1)
    eye = (rows == cols).astype(F32)
    same_half = (rows >= c // 2) == (cols >= c // 2)
    heads = [slice(h * RW_DIM, (h + 1) * RW_DIM) for h in range(RW_HEADS)]

    row_sl, e_tot = [], []
    al, rh, rf, be, ka, bc, kc, vb = [], [], [], [], [], [], [], []
    for j in range(nsub):
        off = pl.multiple_of(jnp.where(d == 0, j, nsub - 1 - j) * c, c)
        rs = pl.ds(off, c)
        row_sl.append(rs)
        lw = lw_ref[0, 0, rs, :]
        cum = _dot_exact_lhs(tri, lw)
        tot = jnp.sum(lw, axis=0, keepdims=True)
        e_m = jnp.exp(-cum)
        e_t = jnp.exp(tot - cum)
        e_tot.append(jnp.exp(tot))
        b_in = b_ref[0, 0, rs, :]
        k_in = kd_ref[0, 0, rs, :]
        alpha = (a_ref[0, rs, :] * jnp.exp(cum - lw)).astype(BF16)
        rho_f = r_ref[0, rs, :] * jnp.exp(cum)
        rho = rho_f.astype(BF16)
        beta = (b_in * e_m).astype(BF16)
        kappa = (k_in * e_m).astype(BF16)
        beta_c = (b_in * e_t).astype(BF16)
        kappa_c = (k_in * e_t).astype(BF16)
        v_b = v_ref[0, rs, :]
        for sl in heads:
            al.append(alpha[:, sl])
            rh.append(rho[:, sl])
            rf.append(rho_f[:, sl])
            be.append(beta[:, sl])
            ka.append(kappa[:, sl])
            bc.append(beta_c[:, sl])
            kc.append(kappa_c[:, sl])
            vb.append(v_b[:, sl])

    units = range(nsub * RW_HEADS)
    cat = lambda x, y: jnp.concatenate([x, y], axis=0)
    zeros = jnp.zeros((c, RW_DIM), BF16)
    prod = [jnp.where(m_quad, _dot_nt(cat(al[u], rh[u]), cat(be[u], ka[u])), 0.0) for u in units]
    l_ab = [prod[u][0:c, 0:c] for u in units]
    top = [prod[u][0:c].astype(BF16) for u in units]
    bot = [prod[u][c:2 * c].astype(BF16) for u in units]
    l_d = [jnp.where(same_half, l_ab[u], 0.0) for u in units]
    l_o = [(l_ab[u] - l_d[u]).astype(BF16) for u in units]
    ld_b = [l_d[u].astype(BF16) for u in units]
    pw = [_dot(ld_b[u], ld_b[u]).astype(BF16) for u in units]
    td = [eye + l_d[u] for u in units]
    for _ in range(int(math.log2(c)) - 3):
        both = [_dot(cat(td[u].astype(BF16), pw[u]), pw[u]) for u in units]
        td = [td[u] + both[u][0:c] for u in units]
        pw = [both[u][c:2 * c].astype(BF16) for u in units]
    td = [td[u] + _dot(td[u].astype(BF16), pw[u]) for u in units]
    td_b = [td[u].astype(BF16) for u in units]
    x_o = [_dot(td_b[u], l_o[u]).astype(BF16) for u in units]
    tinv = [(td[u] + _dot(x_o[u], td_b[u])).astype(BF16) for u in units]
    akv = [_dot(top[u], cat(zeros, vb[u])).astype(BF16) for u in units]
    lcat = lambda x, y: jnp.concatenate([x, y], axis=1)
    av = [_dot(tinv[u], lcat(al[u], akv[u])).astype(BF16) for u in units]
    ry = [_dot(bot[u], cat(av[u], lcat(zeros, vb[u]))) for u in units]
    r_hat = [(rf[u] + ry[u][:, 0:RW_DIM]).astype(BF16) for u in units]
    y_hat = [ry[u][:, RW_DIM:2 * RW_DIM] for u in units]
    qn = [_dot_tn(av[u], bc[u]) for u in units]
    q_mat = [qn[u][0:RW_DIM].astype(BF16) for u in units]
    n_mat = [qn[u][RW_DIM:2 * RW_DIM] + _dot_tn(vb[u], kc[u]) for u in units]

    state = [s_ref[h] for h in range(RW_HEADS)]
    for j in range(nsub):
        for h, sl in enumerate(heads):
            u = j * RW_HEADS + h
            s_b = state[h].astype(BF16)
            y_ref[0, 0, row_sl[j], sl] = _dot_nt(r_hat[u], s_b) + y_hat[u]
            state[h] = state[h] * e_tot[j][:, sl] + _dot(s_b, q_mat[u]) + n_mat[u]
    for h in range(RW_HEADS):
        s_ref[h] = state[h]


def _scan_chunk_index(d, s, n_ctx_chunks, n_chunks):
    rev = jnp.where(s < n_ctx_chunks, n_ctx_chunks - 1 - s, n_chunks + n_ctx_chunks - 1 - s)
    return jnp.where(d == 0, s, rev)


def _direction_masks(c):
    i = np.arange(c)
    lower = (i[None, :] <= i[:, None]).astype(np.float32)
    tri = np.stack([lower, lower.T])
    strict = np.stack([lower - np.eye(c, dtype=np.float32), lower.T - np.eye(c, dtype=np.float32)])
    return tri, strict


def rwkv_scan(r, v, a, lw, kd, b, n_ctx):
    bsz, lt, w = r.shape
    c = RW_CHUNK
    blk = RW_NSUB * c
    nck = lt // blk
    ncc = n_ctx // blk
    tri, strict = _direction_masks(c)
    shared = pl.BlockSpec((1, blk, w), lambda d, bi, s: (bi, _scan_chunk_index(d, s, ncc, nck), 0))
    perdir = pl.BlockSpec((1, 1, blk, w), lambda d, bi, s: (d, bi, _scan_chunk_index(d, s, ncc, nck), 0))
    mask_spec = pl.BlockSpec((1, c, c), lambda d, bi, s: (d, 0, 0))
    quad_spec = pl.BlockSpec((1, 2 * c, 2 * c), lambda d, bi, s: (d, 0, 0))
    quad = np.concatenate([np.tile(strict, (1, 1, 2)), np.tile(tri, (1, 1, 2))], axis=1)
    return pl.pallas_call(
        _rwkv_kernel,
        grid=(2, bsz, nck),
        in_specs=[mask_spec, quad_spec, shared, shared, shared, perdir, perdir, perdir],
        out_specs=perdir,
        out_shape=jax.ShapeDtypeStruct((2, bsz, lt, w), F32),
        scratch_shapes=[pltpu.VMEM((RW_HEADS, RW_DIM, RW_DIM), F32)],
        compiler_params=_cparams(3),
        name="rwkv_scan",
    )(jnp.asarray(tri, BF16), jnp.asarray(quad, F32), r, v, a, lw, kd, b)


def _finish_rwkv_kernel(y_ref, bonus_ref, g_ref, lnw_ref, lnb_ref, g64_ref, o_ref):
    y = y_ref[0, 0] + y_ref[1, 0]
    g64 = g64_ref[...]
    mean = _dot_exact_rhs(y, g64) * (1.0 / RW_DIM)
    yc = y - mean
    var = _group_sum(yc * yc, g64) * (1.0 / RW_DIM)
    yn = yc * lax.rsqrt(var + RW_GN_EPS) * lnw_ref[...] + lnb_ref[...]
    o_ref[0] = ((yn + bonus_ref[0]) * _silu(g_ref[0])).astype(BF16)


def finish_rwkv(y, bonus, p3, ln_w, ln_b):
    _, bsz, lt, w = y.shape
    tm = _tile_rows(lt)
    tok = pl.BlockSpec((1, tm, w), lambda b, i: (b, i, 0))
    vec = _const_spec((1, w))
    return pl.pallas_call(
        _finish_rwkv_kernel,
        grid=(bsz, lt // tm),
        in_specs=[pl.BlockSpec((2, 1, tm, w), lambda b, i: (0, b, i, 0)), tok, _row_spec(CB_BG, tm),
                  vec, vec, _const_spec((w, w))],
        out_specs=tok,
        out_shape=jax.ShapeDtypeStruct((bsz, lt, w), BF16),
        compiler_params=_cparams(2),
        name="finish_rwkv",
    )(y, bonus, p3, ln_w.reshape(1, w), ln_b.reshape(1, w), _group_matrix(w, RW_DIM))


def _prep_ssd_kernel(x_ref, xp_ref, xn_ref, bc_ref, bcp_ref, bcn_ref, m_ref, cw_ref, cb_ref, dtb_ref,
                     aneg_ref, exp_ref, dsk_ref,
                     xq_ref, bco_ref, bt_ref, a_ref, at_ref, dskip_ref, *, n_tiles, n_ctx_tiles):
    i = pl.program_id(1)
    w = BRANCH_W

    def conv(x_ref, p_ref, n_ref, half):
        x = x_ref[0]
        xp, xn = _neighbours(x, p_ref[0], n_ref[0], i, n_tiles, n_ctx_tiles)
        lo = half * w
        y = (xp * cw_ref[0:1, lo:lo + w] + x * cw_ref[1:2, lo:lo + w] + xn * cw_ref[2:3, lo:lo + w]
             + cb_ref[:, lo:lo + w])
        return _silu(y)

    xs = conv(x_ref, xp_ref, xn_ref, 0)
    bc = conv(bc_ref, bcp_ref, bcn_ref, 1)
    bco_ref[0] = bc.astype(BF16)
    bm_f = bc[:, 0:SSM_GROUPS * SSM_STATE]
    q = SSM_CHUNK
    for j in range(bc.shape[0] // q):
        bt_ref[0, j] = bm_f[j * q:(j + 1) * q, :].T.astype(BF16)
    dt = _softplus(m_ref[0][:, MISC_DT:MISC_DT + LANES] + dtb_ref[...])
    lane = lax.broadcasted_iota(jnp.int32, (1, LANES), 1)
    dt = jnp.where(lane < 2 * SSM_HEADS, dt, 0.0)
    dtx = _dot_exact_rhs(dt, exp_ref[...])
    a_all = dt * aneg_ref[...]
    first = lane < SSM_HEADS
    a_dirs = [jnp.where(first, a_all, 0.0), jnp.where(first, pltpu.roll(a_all, LANES - SSM_HEADS, 1), 0.0)]
    for d in range(2):
        xq_ref[d, 0] = xs * dtx[:, d * w:(d + 1) * w]
        a_ref[d, 0] = a_dirs[d]
        for j in range(bc.shape[0] // q):
            at_ref[d, 0, j] = a_dirs[d][j * q:(j + 1) * q, :].T[0:2 * SUBLANES, :]
    dskip_ref[0] = xs * dsk_ref[...]


def prep_ssd(p3, conv_w, conv_b, dt_bias, a_log, d_skip, n_ctx):
    bsz, lt, _ = p3.shape
    tm = PREP_TM
    w = BRANCH_W
    n_tiles = lt // tm
    nh2 = 2 * SSM_HEADS
    pad_lanes = lambda v: jnp.pad(v.reshape(1, nh2), ((0, 0), (0, LANES - nh2)))
    expand = np.zeros((LANES, 2 * w), np.float32)
    for d in range(2):
        for h in range(SSM_HEADS):
            expand[d * SSM_HEADS + h, d * w + h * SSM_HEAD_DIM:d * w + (h + 1) * SSM_HEAD_DIM] = 1.0
    specs = [_row_spec(CB_CX), *_halo_specs(CB_CX, tm, lt), _row_spec(CB_CBC), *_halo_specs(CB_CBC, tm, lt),
             _row_spec(CB_MISC)]
    tok = pl.BlockSpec((1, tm, w), lambda b, i: (b, i, 0))
    gs = SSM_GROUPS * SSM_STATE
    kern = functools.partial(_prep_ssd_kernel, n_tiles=n_tiles, n_ctx_tiles=n_ctx // tm)
    return pl.pallas_call(
        kern,
        grid=(bsz, n_tiles),
        in_specs=specs + [_const_spec((3, 2 * w)), _const_spec((1, 2 * w)), _const_spec((1, LANES)),
                          _const_spec((1, LANES)), _const_spec((LANES, 2 * w)), _const_spec((1, w))],
        out_specs=[pl.BlockSpec((2, 1, tm, w), lambda b, i: (0, b, i, 0)), tok,
                   pl.BlockSpec((1, tm // SSM_CHUNK, gs, SSM_CHUNK), lambda b, i: (b, i, 0, 0)),
                   pl.BlockSpec((2, 1, tm, LANES), lambda b, i: (0, b, i, 0)),
                   pl.BlockSpec((2, 1, tm // SSM_CHUNK, 2 * SUBLANES, SSM_CHUNK), lambda b, i: (0, b, i, 0, 0)), tok],
        out_shape=[jax.ShapeDtypeStruct((2, bsz, lt, w), F32),
                   jax.ShapeDtypeStruct((bsz, lt, w), BF16),
                   jax.ShapeDtypeStruct((bsz, lt // SSM_CHUNK, gs, SSM_CHUNK), BF16),
                   jax.ShapeDtypeStruct((2, bsz, lt, LANES), F32),
                   jax.ShapeDtypeStruct((2, bsz, lt // SSM_CHUNK, 2 * SUBLANES, SSM_CHUNK), F32),
                   jax.ShapeDtypeStruct((bsz, lt, w), F32)],
        compiler_params=_cparams(2),
        name="prep_ssd",
    )(*([p3] * 7), conv_w, conv_b.reshape(1, 2 * w), pad_lanes(dt_bias), pad_lanes(-jnp.exp(a_log)),
      jnp.asarray(expand, BF16), jnp.repeat(d_skip, SSM_HEAD_DIM).reshape(1, w))


def _ssd_kernel(tri_ref, mi_ref, xq_ref, bc_ref, bt_ref, a_ref, at_ref, y_ref, s_ref):
    q = SSM_CHUNK
    nsub = bt_ref.shape[1]
    rep = SSM_HEADS // SSM_GROUPS
    gw = SSM_GROUPS * SSM_STATE
    p = SSM_HEAD_DIM
    d = pl.program_id(0)

    @pl.when(pl.program_id(2) == 0)
    def _():
        s_ref[...] = jnp.zeros_like(s_ref)

    tri = tri_ref[0]
    mask = mi_ref[0] > 0.5
    hs = range(SSM_HEADS)
    groups = [slice(g * SSM_STATE, (g + 1) * SSM_STATE) for g in range(SSM_GROUPS)]

    rows, cms, y_in, e_col, e_tot, upd = [], [], [], [], [], []
    for j in range(nsub):
        ch = jnp.where(d == 0, j, nsub - 1 - j)
        rs = pl.ds(pl.multiple_of(ch * q, q), q)
        rows.append(rs)
        a = a_ref[0, 0, rs, :]
        a_t = at_ref[0, 0, ch]
        xq = xq_ref[0, 0, rs, :]
        bm = bc_ref[0, rs, 0:gw]
        cm = bc_ref[0, rs, gw:2 * gw]
        bt = bt_ref[0, ch]
        acol = _dot_exact_lhs(tri, a)
        arow = _dot_nt_exact_lhs_f32(a_t, tri)
        tot = jnp.sum(a, axis=0, keepdims=True)
        cm_g = [cm[:, gs] for gs in groups]
        cb = [_dot_nt(cm_g[g], bm[:, gs]) for g, gs in enumerate(groups)]
        ac = [jnp.broadcast_to(acol[:, h:h + 1], (q, q)) for h in hs]
        ar = [jnp.broadcast_to(arow[h:h + 1, :], (q, q)) for h in hs]
        gmat = [(cb[h // rep] * jnp.exp(jnp.where(mask, ac[h] - ar[h], NEG_INF))).astype(BF16) for h in hs]
        xh = [xq[:, h * p:(h + 1) * p] for h in hs]
        tot_h = [jnp.broadcast_to(tot[:, h:h + 1], (1, p)) for h in hs]
        xd = [(xh[h] * jnp.exp(tot_h[h] - ac[h][:, :p])).astype(BF16) for h in hs]
        cms.append(cm_g)
        y_in.append([_dot(gmat[h], xh[h].astype(BF16)) for h in hs])
        e_col.append([jnp.exp(ac[h][:, :p]) for h in hs])
        e_tot.append([jnp.exp(tot_h[h]) for h in hs])
        upd.append([_dot(bt[groups[h // rep], :], xd[h]) for h in hs])

    state = [s_ref[h] for h in hs]
    for j in range(nsub):
        y_st = [_dot(cms[j][h // rep], state[h].astype(BF16)) for h in hs]
        for h in hs:
            y_ref[0, 0, rows[j], h * p:(h + 1) * p] = y_in[j][h] + y_st[h] * e_col[j][h]
            state[h] = state[h] * e_tot[j][h] + upd[j][h]
    for h in hs:
        s_ref[h] = state[h]


def ssd_scan(xq, bc, bt, a, a_t, n_ctx):
    _, bsz, lt, w = xq.shape
    q = SSM_CHUNK
    blk = SSM_NSUB * q
    nck = lt // blk
    ncc = n_ctx // blk
    gw = bt.shape[2]
    tri, _ = _direction_masks(q)

    def cidx(d, s):
        return _scan_chunk_index(d, s, ncc, nck)

    mask_spec = pl.BlockSpec((1, q, q), lambda d, bi, s: (d, 0, 0))
    return pl.pallas_call(
        _ssd_kernel,
        grid=(2, bsz, nck),
        in_specs=[mask_spec, mask_spec,
                  pl.BlockSpec((1, 1, blk, w), lambda d, bi, s: (d, bi, cidx(d, s), 0)),
                  pl.BlockSpec((1, blk, w), lambda d, bi, s: (bi, cidx(d, s), 0)),
                  pl.BlockSpec((1, SSM_NSUB, gw, q), lambda d, bi, s: (bi, cidx(d, s), 0, 0)),
                  pl.BlockSpec((1, 1, blk, LANES), lambda d, bi, s: (d, bi, cidx(d, s), 0)),
                  pl.BlockSpec((1, 1, SSM_NSUB, 2 * SUBLANES, q), lambda d, bi, s: (d, bi, cidx(d, s), 0, 0))],
        out_specs=pl.BlockSpec((1, 1, blk, w), lambda d, bi, s: (d, bi, cidx(d, s), 0)),
        out_shape=jax.ShapeDtypeStruct((2, bsz, lt, w), F32),
        scratch_shapes=[pltpu.VMEM((SSM_HEADS, SSM_STATE, SSM_HEAD_DIM), F32)],
        compiler_params=_cparams(3),
        name="ssd_scan",
    )(jnp.asarray(tri, BF16), jnp.asarray(tri, F32), xq, bc, bt, a, a_t)


def _finish_ssd_kernel(y_ref, dskip_ref, z_ref, nw_ref, g256_ref, o_ref):
    y = y_ref[0, 0] + y_ref[1, 0] + dskip_ref[0]
    g = y * _silu(z_ref[0])
    group = BRANCH_W // SSM_GROUPS
    o_ref[0] = _head_rms(g, g256_ref[...], group, nw_ref[...]).astype(BF16)


def finish_ssd(y, dskip, p3, norm_w):
    _, bsz, lt, w = y.shape
    tm = _tile_rows(lt)
    tok = pl.BlockSpec((1, tm, w), lambda b, i: (b, i, 0))
    return pl.pallas_call(
        _finish_ssd_kernel,
        grid=(bsz, lt // tm),
        in_specs=[pl.BlockSpec((2, 1, tm, w), lambda b, i: (0, b, i, 0)), tok, _row_spec(CB_CZ, tm),
                  _const_spec((1, w)), _const_spec((w, w))],
        out_specs=tok,
        out_shape=jax.ShapeDtypeStruct((bsz, lt, w), BF16),
        compiler_params=_cparams(2),
        name="finish_ssd",
    )(y, dskip, p3, norm_w.reshape(1, w), _group_matrix(w, w // SSM_GROUPS))


def _permute_w_in(w_in_l):
    w = BRANCH_W
    b0 = 4 * w
    lora0 = b0 + 3 * w
    bg0 = lora0 + MISC_DT
    c0 = bg0 + w
    dt0 = c0 + SSM_CONV_CH
    z0 = dt0 + 2 * SSM_HEADS
    d0 = z0 + w
    end = d0 + 4 * w
    parts = [w_in_l[:, 0:lora0], w_in_l[:, bg0:c0], w_in_l[:, c0:dt0], w_in_l[:, z0:d0], w_in_l[:, d0:end],
             w_in_l[:, lora0:bg0], w_in_l[:, dt0:z0]]
    used = sum(p.shape[1] for p in parts)
    pad = jnp.zeros((w_in_l.shape[0], IN_W_PAD - used), BF16)
    return jnp.concatenate([p.astype(BF16) for p in parts] + [pad], axis=1)


def kernel(x, c, ctx, c_ctx, norm_w, w_ada, b_ada, w_in, na_q_norm, na_k_norm, na_rpb, rw_mu, rw_w0, rw_w2, rw_a0, rw_a2, rw_k_k, rw_k_a, rw_r_k, rw_ln_w, rw_ln_b, ssm_conv_w, ssm_conv_b, ssm_dt_bias, ssm_A_log, ssm_D, ssm_norm_w, da_q_norm, da_k_norm, da_lq1, da_lk1, da_lq2, da_lk2, da_subln, w_gate, w_up, w_out):
    bsz, seq, d = x.shape
    n_ctx = ctx.shape[1]
    lt = n_ctx + seq
    depth = w_in.shape[0]
    assert n_ctx == CTX_LEN == PREP_TM == DA_TK and seq % (GRID_W * NA_WIN_R) == 0
    assert ((lt // DA_TK) - 1) % DA_NSUB == 0 and n_ctx % NORM_ROWS == 0
    cond = jnp.concatenate([_silu(c), _silu(c_ctx)[None], jnp.zeros((SUBLANES - bsz - 1, d), F32)], axis=0)
    xs2 = jnp.concatenate([ctx, x], axis=1).reshape(bsz * lt, d)
    cos, sin = rope_tables(lt, n_ctx)
    for l in range(depth):
        lam_init = 0.8 - 0.6 * math.exp(-0.3 * l)
        mod = small_matmul(cond, w_ada[l], b_ada[l])
        shift, scale, gate = jnp.split(mod[:bsz], 3, axis=-1)
        shift_c, scale_c, gate_c = jnp.split(mod[bsz], 3, axis=-1)
        p2, h = in_projection(xs2, norm_w[l], scale, shift, scale_c, shift_c, _permute_w_in(w_in[l]), lt, n_ctx)
        p3 = p2.reshape(bsz, lt, IN_W_PAD)

        kn, va, qt, kh, vt = prep_attention(p3, cos, sin, na_k_norm[l], da_q_norm[l], da_k_norm[l])
        oa = neighbourhood_attention(p3, kn, va, na_bias_table(na_rpb[l]), na_q_norm[l], n_ctx)
        lam = jnp.exp(jnp.sum(da_lq1[l] * da_lk1[l])) - jnp.exp(jnp.sum(da_lq2[l] * da_lk2[l])) + lam_init
        od = flash_diff_attention(qt, kh, vt, p3, lam, da_subln[l], lam_init)

        r, vb, a, lw, kd, b, bonus = prep_rwkv(p3, rw_mu[l], rw_w0[l], rw_w2[l], rw_a0[l], rw_a2[l],
                                              rw_k_k[l], rw_k_a[l], rw_r_k[l].reshape(-1), n_ctx)
        ob = finish_rwkv(rwkv_scan(r, vb, a, lw, kd, b, n_ctx), bonus, p3, rw_ln_w[l], rw_ln_b[l])

        xq, bc, bt, sa, sat, dskip = prep_ssd(p3, ssm_conv_w[l], ssm_conv_b[l], ssm_dt_bias[l], ssm_A_log[l],
                                              ssm_D[l], n_ctx)
        om = finish_ssd(ssd_scan(xq, bc, bt, sa, sat, n_ctx), dskip, p3, ssm_norm_w[l])

        ys = [t.reshape(bsz * lt, BRANCH_W) for t in (oa, ob, om, od)]
        acc = gate_up(h, ys, w_gate[l].astype(BF16), w_up[l].astype(BF16), lt)
        xs2 = out_projection(acc, w_out[l].astype(BF16), xs2, gate, gate_c, lt, n_ctx)
    return xs2.reshape(bsz, lt, d)[:, n_ctx:]
```

```python
import functools
import math

import numpy as np
import jax
import jax.numpy as jnp
from jax import lax
from jax.experimental import pallas as pl
from jax.experimental.pallas import tpu as pltpu

F32 = jnp.float32
BF16 = jnp.bfloat16

D_MODEL = 2048
GRID_W = 64
CTX_LEN = 256
N_BRANCH = 4
BRANCH_W = D_MODEL // N_BRANCH
NORM_EPS = 1e-6
NEG_INF = -1e30

NA_DIM = 64
NA_HEADS = BRANCH_W // NA_DIM
NA_WIN_R = 8
NA_WIN_C = 16
NA_NBLK = 4

RW_DIM = 64
RW_HEADS = BRANCH_W // RW_DIM
RW_LORA_W = 64
RW_LORA_A = 64
RW_GN_EPS = 64e-5
RW_KK_EPS = 1e-12
RW_CHUNK = 64
RW_NSUB = 4

SSM_HEAD_DIM = 64
SSM_HEADS = BRANCH_W // SSM_HEAD_DIM
SSM_GROUPS = 2
SSM_STATE = 128
SSM_CHUNK = 128
SSM_NSUB = 2
SSM_CONV_CH = BRANCH_W + 2 * SSM_GROUPS * SSM_STATE

DA_DIM = 64
DA_HEADS = BRANCH_W // (2 * DA_DIM)
ROPE_BASE = 10000.0
DA_TQ = 256
DA_TK = 256
DA_NSUB = 16
DA_HSTEP = 2
DA_ONES = 16

SUBLANES = 8
LANES = 128

CB_AQ, CB_AK, CB_AV, CB_AG = 0, 1, 2, 3
CB_BR, CB_BK, CB_BV, CB_BG = 4, 5, 6, 7
CB_CX, CB_CBC, CB_CZ = 8, 9, 10
CB_DQ, CB_DK, CB_DV, CB_DG = 11, 12, 13, 14
CB_MISC = 15
MISC_DT = 2 * RW_LORA_W + 2 * RW_LORA_A
N_COL_BLOCKS = 16
IN_W_PAD = N_COL_BLOCKS * BRANCH_W
PREP_TM = 256
MATMUL_TM = 768
NORM_ROWS = 32

V7X_VMEM_BYTES = 64 * 1024 * 1024
VMEM_LIMIT = V7X_VMEM_BYTES * 7 // 8


def _cparams(n_axes):
    return pltpu.CompilerParams(dimension_semantics=("arbitrary",) * n_axes, vmem_limit_bytes=VMEM_LIMIT)


def _dot(a, b):
    return jnp.dot(a, b, preferred_element_type=F32)


def _dot_nt(a, b):
    return lax.dot_general(a, b, (((1,), (1,)), ((), ())), preferred_element_type=F32)


def _dot_tn(a, b):
    return lax.dot_general(a, b, (((0,), (0,)), ((), ())), preferred_element_type=F32)


def _split3(x):
    hi = x.astype(BF16)
    r1 = x - hi.astype(F32)
    mid = r1.astype(BF16)
    lo = (r1 - mid.astype(F32)).astype(BF16)
    return hi, mid, lo


def _dot_exact_lhs(m_bf16, x):
    hi, mid, lo = _split3(x)
    return _dot(m_bf16, hi) + _dot(m_bf16, mid) + _dot(m_bf16, lo)


def _dot_exact_rhs(x, m_bf16):
    hi, mid, lo = _split3(x)
    return _dot(hi, m_bf16) + _dot(mid, m_bf16) + _dot(lo, m_bf16)


def _group_sum(x, gmat_bf16):
    hi = x.astype(BF16)
    lo = (x - hi.astype(F32)).astype(BF16)
    return _dot(hi, gmat_bf16) + _dot(lo, gmat_bf16)


def _dot_nt_exact_lhs_f32(x, m_bf16):
    hi, mid, lo = _split3(x)
    return _dot_nt(hi, m_bf16) + _dot_nt(mid, m_bf16) + _dot_nt(lo, m_bf16)


def _sigmoid(x):
    return 1.0 / (1.0 + jnp.exp(-x))


def _silu(x):
    return x * _sigmoid(x)


def _softplus(x):
    return jnp.maximum(x, 0.0) + jnp.log(1.0 + jnp.exp(-jnp.abs(x)))


def _group_matrix(width, group):
    g = np.arange(width) // group
    return jnp.asarray((g[:, None] == g[None, :]).astype(np.float32), BF16)


def _head_rms(x, gmat, group, w):
    ms = _group_sum(x * x, gmat) * (1.0 / group)
    return x * lax.rsqrt(ms + NORM_EPS) * w


def _tile_rows(n_rows):
    return MATMUL_TM if n_rows % MATMUL_TM == 0 else PREP_TM


def _row_spec(cb, tm=PREP_TM):
    return pl.BlockSpec((1, tm, BRANCH_W), lambda b, i: (b, i, cb))


def _halo_specs(cb, tm, lt):
    per = tm // SUBLANES
    last = lt // SUBLANES - 1
    prev = pl.BlockSpec((1, SUBLANES, BRANCH_W), lambda b, i: (b, jnp.maximum(i * per - 1, 0), cb))
    nxt = pl.BlockSpec((1, SUBLANES, BRANCH_W), lambda b, i: (b, jnp.minimum((i + 1) * per, last), cb))
    return prev, nxt


def _const_spec(shape):
    return pl.BlockSpec(shape, lambda *_: (0,) * len(shape))


def _neighbours(x, prev_blk, next_blk, i, n_tiles, n_ctx_tiles):
    tm = x.shape[0]
    row = lax.broadcasted_iota(jnp.int32, (tm, 1), 0)
    seg_start = jnp.logical_or(i == 0, i == n_ctx_tiles)
    seg_end = jnp.logical_or(i == n_ctx_tiles - 1, i == n_tiles - 1)
    prev_row = jnp.where(seg_start, 0.0, prev_blk[SUBLANES - 1:SUBLANES, :])
    next_row = jnp.where(seg_end, 0.0, next_blk[0:1, :])
    x_prev = jnp.where(row == 0, prev_row, pltpu.roll(x, 1, 0))
    x_next = jnp.where(row == tm - 1, next_row, pltpu.roll(x, tm - 1, 0))
    return x_prev, x_next


def _small_mm_kernel(a_ref, w_ref, b_ref, o_ref):
    o_ref[...] = _dot(a_ref[...].astype(BF16), w_ref[...].astype(BF16)) + b_ref[...]


def small_matmul(a, w, b, tn=512):
    m, k = a.shape
    n = w.shape[1]
    return pl.pallas_call(
        _small_mm_kernel,
        grid=(n // tn,),
        in_specs=[pl.BlockSpec((m, k), lambda j: (0, 0)),
                  pl.BlockSpec((k, tn), lambda j: (0, j)),
                  pl.BlockSpec((1, tn), lambda j: (0, j))],
        out_specs=pl.BlockSpec((m, tn), lambda j: (0, j)),
        out_shape=jax.ShapeDtypeStruct((m, n), F32),
        compiler_params=_cparams(1),
        name="adaln_mm",
    )(a, w, b.reshape(1, n))


def _inproj_kernel(x_ref, nw_ref, sc_ref, sh_ref, scc_ref, shc_ref, w_ref, p_ref, h_ref, hs_ref, *,
                   tiles_per_batch, n_ctx):
    @pl.when(pl.program_id(1) == 0)
    def _():
        tm = x_ref.shape[0]
        rc = NORM_ROWS
        base = (pl.program_id(0) % tiles_per_batch) * tm

        def norm_rows(k, carry):
            r0 = pl.multiple_of(k * rc, rc)
            xf = x_ref[pl.ds(r0, rc), :]
            ms = jnp.mean(xf * xf, axis=-1, keepdims=True)
            y = xf * lax.rsqrt(ms + NORM_EPS) * nw_ref[...]
            is_ctx = base + r0 < n_ctx
            sc = 1.0 + jnp.where(is_ctx, scc_ref[...], sc_ref[0])
            sh = jnp.where(is_ctx, shc_ref[...], sh_ref[0])
            h = (y * sc + sh).astype(BF16)
            hs_ref[pl.ds(r0, rc), :] = h
            h_ref[pl.ds(r0, rc), :] = h
            return carry

        lax.fori_loop(0, tm // rc, norm_rows, 0, unroll=4)

    p_ref[...] = _dot(hs_ref[...], w_ref[...])


def in_projection(xs2, norm_w, scale, shift, scale_c, shift_c, w_bf16, lt, n_ctx, tn=2048):
    m, d = xs2.shape
    n = w_bf16.shape[1]
    tm = _tile_rows(lt)
    tpb = lt // tm
    nb = scale.shape[0]
    kern = functools.partial(_inproj_kernel, tiles_per_batch=tpb, n_ctx=n_ctx)
    return pl.pallas_call(
        kern,
        grid=(m // tm, n // tn),
        in_specs=[pl.BlockSpec((tm, d), lambda i, j: (i, 0)),
                  pl.BlockSpec((1, d), lambda i, j: (0, 0)),
                  pl.BlockSpec((1, 1, d), lambda i, j: (i // tpb, 0, 0)),
                  pl.BlockSpec((1, 1, d), lambda i, j: (i // tpb, 0, 0)),
                  pl.BlockSpec((1, d), lambda i, j: (0, 0)),
                  pl.BlockSpec((1, d), lambda i, j: (0, 0)),
                  pl.BlockSpec((d, tn), lambda i, j: (0, j))],
        out_specs=[pl.BlockSpec((tm, tn), lambda i, j: (i, j)),
                   pl.BlockSpec((tm, d), lambda i, j: (i, 0))],
        out_shape=[jax.ShapeDtypeStruct((m, n), F32),
                   jax.ShapeDtypeStruct((m, d), BF16)],
        scratch_shapes=[pltpu.VMEM((tm, d), BF16)],
        compiler_params=_cparams(2),
        name="in_proj",
    )(xs2, norm_w.reshape(1, d), scale.reshape(nb, 1, d), shift.reshape(nb, 1, d),
      scale_c.reshape(1, d), shift_c.reshape(1, d), w_bf16)


def _gate_up_kernel(h_ref, ya_ref, yb_ref, yc_ref, yd_ref, wg_ref, wu_ref, o_ref):
    h = h_ref[...]
    acc = None
    for i, y_ref in enumerate((ya_ref, yb_ref, yc_ref, yd_ref)):
        g = _dot(h, wg_ref[i])
        u = _dot(y_ref[...], wu_ref[i])
        t = _sigmoid(g) * u
        acc = t if acc is None else acc + t
    o_ref[...] = acc.astype(BF16)


def gate_up(h, ys, wg_bf16, wu_bf16, lt, tn=512):
    m, d = h.shape
    w = ys[0].shape[1]
    tm = _tile_rows(lt)
    y_spec = pl.BlockSpec((tm, w), lambda i, j: (i, 0))
    return pl.pallas_call(
        _gate_up_kernel,
        grid=(m // tm, d // tn),
        in_specs=[pl.BlockSpec((tm, d), lambda i, j: (i, 0)), y_spec, y_spec, y_spec, y_spec,
                  pl.BlockSpec((N_BRANCH, d, tn), lambda i, j: (0, 0, j)),
                  pl.BlockSpec((N_BRANCH, w, tn), lambda i, j: (0, 0, j))],
        out_specs=pl.BlockSpec((tm, tn), lambda i, j: (i, j)),
        out_shape=jax.ShapeDtypeStruct((m, d), BF16),
        compiler_params=_cparams(2),
        name="gate_up",
    )(h, *ys, wg_bf16, wu_bf16)


def _out_proj_kernel(a_ref, w_ref, x_ref, g_ref, gc_ref, o_ref, *, tiles_per_batch, n_ctx):
    tm = x_ref.shape[0]
    row = (pl.program_id(0) % tiles_per_batch) * tm + lax.broadcasted_iota(jnp.int32, (tm, 1), 0)
    gate = jnp.where(row < n_ctx, gc_ref[...], g_ref[0])
    o_ref[...] = x_ref[...] + gate * _dot(a_ref[...], w_ref[...])


def out_projection(acc, w_bf16, xs2, gate, gate_c, lt, n_ctx, tn=1024):
    m, d = xs2.shape
    tm = _tile_rows(lt)
    tpb = lt // tm
    nb = gate.shape[0]
    kern = functools.partial(_out_proj_kernel, tiles_per_batch=tpb, n_ctx=n_ctx)
    return pl.pallas_call(
        kern,
        grid=(m // tm, d // tn),
        in_specs=[pl.BlockSpec((tm, d), lambda i, j: (i, 0)),
                  pl.BlockSpec((d, tn), lambda i, j: (0, j)),
                  pl.BlockSpec((tm, tn), lambda i, j: (i, j)),
                  pl.BlockSpec((1, 1, tn), lambda i, j: (i // tpb, 0, j)),
                  pl.BlockSpec((1, tn), lambda i, j: (0, j))],
        out_specs=pl.BlockSpec((tm, tn), lambda i, j: (i, j)),
        out_shape=jax.ShapeDtypeStruct((m, d), F32),
        compiler_params=_cparams(2),
        name="out_proj",
    )(acc, w_bf16, xs2, gate.reshape(nb, 1, d), gate_c.reshape(1, d))


def _rope(x, cos, sin):
    w = x.shape[1]
    lane = lax.broadcasted_iota(jnp.int32, (1, w), 1)
    first = ((lane // (DA_DIM // 4)) % 2) == 0
    rot = jnp.where(first, -pltpu.roll(x, w - DA_DIM // 4, 1), pltpu.roll(x, DA_DIM // 4, 1))
    return x * cos + rot * sin


def _prep_attn_kernel(ak_ref, av_ref, dq_ref, dk_ref, dv_ref, cos_ref, sin_ref, g64_ref,
                      wak_ref, wdq_ref, wdk_ref, kn_ref, va_ref, qt_ref, kh_ref, vt_ref):
    g64 = g64_ref[...]
    reps = BRANCH_W // LANES
    cos = jnp.concatenate([cos_ref[...]] * reps, axis=1)
    sin = jnp.concatenate([sin_ref[...]] * reps, axis=1)
    kn_ref[0] = _head_rms(ak_ref[0], g64, NA_DIM, wak_ref[...]).astype(BF16)
    va_ref[0] = av_ref[0].astype(BF16)
    q = _rope(_head_rms(dq_ref[0], g64, DA_DIM, wdq_ref[...]), cos, sin)
    k = _rope(_head_rms(dk_ref[0], g64, DA_DIM, wdk_ref[...]), cos, sin).astype(BF16)
    qt_ref[0] = q.T.astype(BF16)
    for hc in range(2 * DA_HEADS):
        kh_ref[0, hc] = k[:, hc * DA_DIM:(hc + 1) * DA_DIM]
    v = dv_ref[0]
    dv = 2 * DA_DIM
    for h in range(DA_HEADS):
        vt_ref[0, h, 0, 0:dv, :] = v[:, h * dv:(h + 1) * dv].T.astype(BF16)
        vt_ref[0, h, 0, dv:dv + DA_ONES, :] = jnp.ones((DA_ONES, v.shape[0]), BF16)


def prep_attention(p3, cos, sin, na_k_w, da_q_w, da_k_w):
    bsz, lt, _ = p3.shape
    tm = PREP_TM
    w = BRANCH_W
    tile = lambda v, s=1.0: (jnp.tile(v, w // v.shape[0]) * s).reshape(1, w)
    tab_spec = pl.BlockSpec((tm, LANES), lambda b, i: (i, 0))
    vec = _const_spec((1, w))
    return pl.pallas_call(
        _prep_attn_kernel,
        grid=(bsz, lt // tm),
        in_specs=[_row_spec(CB_AK), _row_spec(CB_AV), _row_spec(CB_DQ), _row_spec(CB_DK), _row_spec(CB_DV),
                  tab_spec, tab_spec, _const_spec((w, w)), vec, vec, vec],
        out_specs=[pl.BlockSpec((1, tm, w), lambda b, i: (b, i, 0)),
                   pl.BlockSpec((1, tm, w), lambda b, i: (b, i, 0)),
                   pl.BlockSpec((1, w, tm), lambda b, i: (b, 0, i)),
                   pl.BlockSpec((1, 2 * DA_HEADS, tm, DA_DIM), lambda b, i: (b, 0, i, 0)),
                   pl.BlockSpec((1, DA_HEADS, 1, 2 * DA_DIM + DA_ONES, tm), lambda b, i: (b, 0, i, 0, 0))],
        out_shape=[jax.ShapeDtypeStruct((bsz, lt, w), BF16),
                   jax.ShapeDtypeStruct((bsz, lt, w), BF16),
                   jax.ShapeDtypeStruct((bsz, w, lt), BF16),
                   jax.ShapeDtypeStruct((bsz, 2 * DA_HEADS, lt, DA_DIM), BF16),
                   jax.ShapeDtypeStruct((bsz, DA_HEADS, lt // tm, 2 * DA_DIM + DA_ONES, tm), BF16)],
        compiler_params=_cparams(2),
        name="prep_attn",
    )(p3, p3, p3, p3, p3, cos, sin, _group_matrix(w, DA_DIM),
      tile(na_k_w), tile(da_q_w, DA_DIM ** -0.5 * math.log2(math.e)), tile(da_k_w))


def rope_tables(lt, n_ctx):
    nf = DA_DIM // 4
    t = jnp.arange(lt - n_ctx, dtype=jnp.int32)
    rows, cols = t // GRID_W, t % GRID_W
    inv = ROPE_BASE ** (-jnp.arange(nf, dtype=F32) / nf)
    ang_r = rows.astype(F32)[:, None] * inv
    ang_c = cols.astype(F32)[:, None] * inv
    ang = jnp.concatenate([ang_r, ang_r, ang_c, ang_c], axis=-1)
    ang = jnp.concatenate([jnp.zeros((n_ctx, DA_DIM), F32), ang], axis=0)
    reps = LANES // DA_DIM
    return jnp.tile(jnp.cos(ang), (1, reps)), jnp.tile(jnp.sin(ang), (1, reps))


def _flash_kernel(lam_ref, qt_ref, k_ref, vt_ref, g_ref, sw_ref, o_ref, *, n_latent_iters, out_scale):
    tq = qt_ref.shape[2]
    nh = vt_ref.shape[1]
    dve = vt_ref.shape[3]
    dv = dve - DA_ONES
    tk = vt_ref.shape[4]
    chains = range(2 * nh)
    qts = [qt_ref[0, n * DA_DIM:(n + 1) * DA_DIM, :] for n in chains]
    gate = _silu(g_ref[0])

    def attend(carry, first_chunk, n_chunks):
        chunks = [first_chunk + g for g in range(n_chunks)]
        sts = [[_dot(k_ref[0, n, pl.ds(pl.multiple_of(ch * tk, tk), tk), :], qts[n]) for n in chains]
               for ch in chunks]
        carry = list(carry)
        for g, ch in enumerate(chunks):
            for n in chains:
                m, acc = carry[n]
                st = sts[g][n].astype(BF16)
                m_new = jnp.maximum(m, jnp.max(st, axis=0, keepdims=True).astype(F32))
                pt = jnp.exp2(st - m_new.astype(BF16))
                carry[n] = (m_new, jnp.exp2(m - m_new) * acc + _dot(vt_ref[0, n // 2, ch], pt))
        return tuple(carry)

    init = tuple((jnp.full((1, tq), NEG_INF, F32), jnp.zeros((dve, tq), F32)) for _ in chains)
    carry = attend(init, 0, 1)
    n_iters = jnp.where(pl.program_id(2) == 0, 0, n_latent_iters)
    res = lax.fori_loop(0, n_iters, lambda i, cr: attend(cr, 1 + i * DA_NSUB, DA_NSUB), carry)
    outs = [acc[0:dv] * (1.0 / acc[dv:dv + 1]) for (_, acc) in res]
    for hh in range(nh):
        ot = outs[2 * hh] - lam_ref[0] * outs[2 * hh + 1]
        ot = ot * lax.rsqrt(jnp.mean(ot * ot, axis=0, keepdims=True) + NORM_EPS)
        o = ot.T * (sw_ref[...] * out_scale)
        o_ref[0, :, hh * dv:(hh + 1) * dv] = (o * gate[:, hh * dv:(hh + 1) * dv]).astype(BF16)


def flash_diff_attention(qt, kh, vt, p3, lam, subln_w, lam_init):
    bsz, w, lt = qt.shape
    dv = 2 * DA_DIM
    hs = DA_HSTEP
    nk = lt // DA_TK
    n_latent_iters = (nk - 1) // DA_NSUB
    kern = functools.partial(_flash_kernel, n_latent_iters=n_latent_iters, out_scale=1.0 - lam_init)
    g_blocks = BRANCH_W // (hs * dv)
    return pl.pallas_call(
        kern,
        grid=(bsz, DA_HEADS // hs, lt // DA_TQ),
        in_specs=[pl.BlockSpec(memory_space=pltpu.SMEM),
                  pl.BlockSpec((1, hs * dv, DA_TQ), lambda b, h, qi: (b, h, qi)),
                  pl.BlockSpec((1, 2 * hs, lt, DA_DIM), lambda b, h, qi: (b, h, 0, 0)),
                  pl.BlockSpec((1, hs, nk, dv + DA_ONES, DA_TK), lambda b, h, qi: (b, h, 0, 0, 0)),
                  pl.BlockSpec((1, DA_TQ, hs * dv), lambda b, h, qi: (b, qi, CB_DG * g_blocks + h)),
                  _const_spec((1, dv))],
        out_specs=pl.BlockSpec((1, DA_TQ, hs * dv), lambda b, h, qi: (b, qi, h)),
        out_shape=jax.ShapeDtypeStruct((bsz, lt, BRANCH_W), BF16),
        compiler_params=_cparams(3),
        name="flash_attn",
    )(lam.reshape(1).astype(F32), qt, kh, vt, p3, subln_w.reshape(1, dv))


def _na_kernel(q_ref, g_ref, k_ref, v_ref, *rest, n_rows, n_ctx):
    bias_refs = rest[:NA_NBLK]
    g64_ref, wq_ref, o_ref = rest[NA_NBLK:]
    n_loc = NA_WIN_R * GRID_W
    q = _head_rms(q_ref[0], g64_ref[...], NA_DIM, wq_ref[...]).astype(BF16)
    g = g_ref[0]
    pair = [slice(hp * LANES, (hp + 1) * LANES) for hp in range(NA_HEADS // 2)]
    low_half = lax.broadcasted_iota(jnp.int32, (1, LANES), 1) < NA_DIM
    rows, starts, units = [], [], []
    for j in range(NA_NBLK):
        blk = pl.program_id(1) * NA_NBLK + j
        r = jnp.maximum(blk - n_ctx // GRID_W, 0)
        r0 = jnp.clip(r - NA_WIN_R // 2, 0, n_rows - NA_WIN_R)
        starts.append(pl.multiple_of(n_ctx + r0 * GRID_W, GRID_W))
        rows.append(slice(j * GRID_W, (j + 1) * GRID_W))
        units += [(j, hp) for hp in range(NA_HEADS // 2)]
    zero = jnp.zeros((), BF16)
    cat = lambda x, y: jnp.concatenate([x, y], axis=0)
    q2 = [cat(jnp.where(low_half, q[rows[j], pair[hp]], zero), jnp.where(low_half, zero, q[rows[j], pair[hp]]))
          for j, hp in units]
    s_loc = [_dot_nt(q2[u], k_ref[0, pl.ds(starts[j], n_loc), pair[hp]])
             + cat(bias_refs[j][0, 2 * hp], bias_refs[j][0, 2 * hp + 1]) for u, (j, hp) in enumerate(units)]
    s_ctx = [_dot_nt(q2[u], k_ref[0, 0:n_ctx, pair[hp]]) for u, (j, hp) in enumerate(units)]
    us = range(len(units))
    m = [jnp.maximum(jnp.max(s_loc[u], axis=-1, keepdims=True), jnp.max(s_ctx[u], axis=-1, keepdims=True))
         for u in us]
    p_loc = [jnp.exp(s_loc[u] - m[u]) for u in us]
    p_ctx = [jnp.exp(s_ctx[u] - m[u]) for u in us]
    l = [jnp.sum(p_loc[u], axis=-1, keepdims=True) + jnp.sum(p_ctx[u], axis=-1, keepdims=True) for u in us]
    o = [(_dot(p_loc[u].astype(BF16), v_ref[0, pl.ds(starts[j], n_loc), pair[hp]])
          + _dot(p_ctx[u].astype(BF16), v_ref[0, 0:n_ctx, pair[hp]])) / l[u]
         for u, (j, hp) in enumerate(units)]
    for u, (j, hp) in enumerate(units):
        o_pair = jnp.where(low_half, o[u][0:GRID_W], o[u][GRID_W:2 * GRID_W])
        o_ref[0, rows[j], pair[hp]] = (o_pair * _silu(g[rows[j], pair[hp]])).astype(BF16)


def neighbourhood_attention(p3, kn, va, bias_tbl, na_q_w, n_ctx):
    bsz, lt, w = kn.shape
    n_rows = (lt - n_ctx) // GRID_W
    ncb = n_ctx // GRID_W
    half = NA_WIN_R // 2

    def bias_spec(j):
        def idx(b, s):
            blk = s * NA_NBLK + j
            r = blk - ncb
            off = r - jnp.clip(r - half, 0, n_rows - NA_WIN_R)
            return (jnp.where(blk < ncb, NA_WIN_R, off), 0, 0, 0)
        return pl.BlockSpec((1, NA_HEADS, GRID_W, NA_WIN_R * GRID_W), idx)

    tq = NA_NBLK * GRID_W
    full_spec = pl.BlockSpec((1, lt, w), lambda b, s: (b, 0, 0))
    wq = (jnp.tile(na_q_w, w // NA_DIM) * NA_DIM ** -0.5).reshape(1, w)
    return pl.pallas_call(
        functools.partial(_na_kernel, n_rows=n_rows, n_ctx=n_ctx),
        grid=(bsz, lt // tq),
        in_specs=[_row_spec(CB_AQ, tq), _row_spec(CB_AG, tq), full_spec, full_spec,
                  *[bias_spec(j) for j in range(NA_NBLK)],
                  _const_spec((w, w)), _const_spec((1, w))],
        out_specs=pl.BlockSpec((1, tq, w), lambda b, s: (b, s, 0)),
        out_shape=jax.ShapeDtypeStruct((bsz, lt, w), BF16),
        compiler_params=_cparams(2),
        name="nbr_attn",
    )(p3, p3, kn, va, *([bias_tbl] * NA_NBLK), _group_matrix(w, NA_DIM), wq)


def na_bias_table(rpb):
    col = np.arange(GRID_W)
    c0 = np.clip(col - NA_WIN_C // 2, 0, GRID_W - NA_WIN_C)
    col_ok = (col[None, :] >= c0[:, None]) & (col[None, :] < c0[:, None] + NA_WIN_C)
    d_col = np.clip(col[None, :] - col[:, None] + (NA_WIN_C - 1), 0, 2 * NA_WIN_C - 2)
    onehot = (d_col[:, :, None] == np.arange(2 * NA_WIN_C - 1)).astype(np.float32)
    by_col = jnp.einsum('hrc,qwc->hrqw', rpb.astype(F32), jnp.asarray(onehot),
                        precision=lax.Precision.HIGHEST)
    by_col = jnp.where(jnp.asarray(col_ok)[None, None], by_col, NEG_INF)
    tbl = jnp.stack([by_col[:, NA_WIN_R - 1 - o:2 * NA_WIN_R - 1 - o] for o in range(NA_WIN_R)]
                    + [jnp.full((NA_HEADS, NA_WIN_R, GRID_W, GRID_W), NEG_INF, F32)])
    tbl = jnp.transpose(tbl, (0, 1, 3, 2, 4))
    return tbl.reshape(NA_WIN_R + 1, NA_HEADS, GRID_W, NA_WIN_R * GRID_W)


def _prep_rwkv_kernel(r_ref, rp_ref, rn_ref, k_ref, kp_ref, kn_ref, v_ref, vp_ref, vn_ref,
                      m_ref, mp_ref, mn_ref, mu_ref, w0_ref, a0_ref, w2_ref, a2_ref, kk_ref, ka_ref,
                      rk_ref, g64_ref,
                      ro_ref, vo_ref, ao_ref, lw_ref, kd_ref, bo_ref, bonus_ref, *, n_tiles, n_ctx_tiles):
    i = pl.program_id(1)
    w = BRANCH_W

    def shifted(x_ref, p_ref, n_ref, mu):
        x = x_ref[0]
        xp, xn = _neighbours(x, p_ref[0], n_ref[0], i, n_tiles, n_ctx_tiles)
        return x + (0.5 * (xp + xn) - x) * mu

    r = shifted(r_ref, rp_ref, rn_ref, mu_ref[0:1, :])
    k = shifted(k_ref, kp_ref, kn_ref, mu_ref[1:2, :])
    v = shifted(v_ref, vp_ref, vn_ref, mu_ref[2:3, :])
    misc = shifted(m_ref, mp_ref, mn_ref, mu_ref[3:4, :])
    wd = jnp.tanh(misc[:, 0:2 * RW_LORA_W]).astype(BF16)
    ad = misc[:, 2 * RW_LORA_W:MISC_DT].astype(BF16)
    w_log = w0_ref[...] + _dot(wd, w2_ref[...])
    gate = _sigmoid(a0_ref[...] + _dot(ad, a2_ref[...]))
    log_decay = -math.exp(-0.5) * _sigmoid(w_log)
    g64 = g64_ref[...]
    kk = k * kk_ref[...]
    kk = kk / jnp.maximum(jnp.sqrt(_group_sum(kk * kk, g64)), RW_KK_EPS)
    ro_ref[0] = r
    vo_ref[0] = v.astype(BF16)
    ao_ref[0] = -kk
    coef = None
    for d in range(2):
        a_d = gate[:, d * w:(d + 1) * w]
        kd = k * (1.0 + (a_d - 1.0) * ka_ref[...])
        lw_ref[d, 0] = log_decay[:, d * w:(d + 1) * w]
        kd_ref[d, 0] = kd
        bo_ref[d, 0] = kk * a_d
        coef = kd if coef is None else coef + kd
    bonus_ref[0] = _group_sum(r * coef * rk_ref[...], g64) * v


def prep_rwkv(p3, mu, w0, w2, a0, a2, k_k, k_a, r_k, n_ctx):
    bsz, lt, _ = p3.shape
    tm = PREP_TM
    w = BRANCH_W
    n_tiles = lt // tm
    mu4 = jnp.stack([mu[0:w], mu[w:2 * w], mu[2 * w:3 * w],
                     jnp.pad(mu[3 * w:], (0, w - (mu.shape[0] - 3 * w)))])
    zero = jnp.zeros((RW_LORA_W, w), F32)
    w2cat = jnp.concatenate([jnp.concatenate([w2[0], zero], axis=1),
                             jnp.concatenate([zero, w2[1]], axis=1)], axis=0).astype(BF16)
    a2cat = jnp.concatenate([jnp.concatenate([a2[0], zero], axis=1),
                             jnp.concatenate([zero, a2[1]], axis=1)], axis=0).astype(BF16)
    specs = []
    for cb in (CB_BR, CB_BK, CB_BV, CB_MISC):
        specs += [_row_spec(cb), *_halo_specs(cb, tm, lt)]
    vec = _const_spec((1, w))
    tok = pl.BlockSpec((1, tm, w), lambda b, i: (b, i, 0))
    tok2 = pl.BlockSpec((2, 1, tm, w), lambda b, i: (0, b, i, 0))
    kern = functools.partial(_prep_rwkv_kernel, n_tiles=n_tiles, n_ctx_tiles=n_ctx // tm)
    return pl.pallas_call(
        kern,
        grid=(bsz, n_tiles),
        in_specs=specs + [_const_spec((4, w)), _const_spec((1, 2 * w)), _const_spec((1, 2 * w)),
                          _const_spec((2 * RW_LORA_W, 2 * w)), _const_spec((2 * RW_LORA_A, 2 * w)),
                          vec, vec, vec, _const_spec((w, w))],
        out_specs=[tok, tok, tok, tok2, tok2, tok2, tok],
        out_shape=[jax.ShapeDtypeStruct((bsz, lt, w), F32),
                   jax.ShapeDtypeStruct((bsz, lt, w), BF16),
                   jax.ShapeDtypeStruct((bsz, lt, w), F32),
                   jax.ShapeDtypeStruct((2, bsz, lt, w), F32),
                   jax.ShapeDtypeStruct((2, bsz, lt, w), F32),
                   jax.ShapeDtypeStruct((2, bsz, lt, w), F32),
                   jax.ShapeDtypeStruct((bsz, lt, w), F32)],
        compiler_params=_cparams(2),
        name="prep_rwkv",
    )(*([p3] * 12), mu4, w0.reshape(1, 2 * w), a0.reshape(1, 2 * w), w2cat, a2cat,
      k_k.reshape(1, w), k_a.reshape(1, w), r_k.reshape(1, w), _group_matrix(w, RW_DIM))


def _rwkv_kernel(tri_ref, mq_ref, r_ref, v_ref, a_ref, lw_ref, kd_ref, b_ref, y_ref, s_ref):
    c = RW_CHUNK
    nsub = r_ref.shape[1] // c
    d = pl.program_id(0)

    @pl.when(pl.program_id(2) == 0)
    def _():
        s_ref[...] = jnp.zeros_like(s_ref)

    tri = tri_ref[0]
    m_quad = mq_ref[0] > 0.5
    rows = lax.broadcasted_iota(jnp.int32, (c, c), 0)
    cols = lax.broadcasted_iota(jnp.int32, (c, c), 1)
    eye = (rows == cols).astype(F32)
    same_half = (rows >= c // 2) == (cols >= c // 2)
    heads = [slice(h * RW_DIM, (h + 1) * RW_DIM) for h in range(RW_HEADS)]

    row_sl, e_tot = [], []
    al, rh, rf, be, ka, bc, kc, vb = [], [], [], [], [], [], [], []
    for j in range(nsub):
        off = pl.multiple_of(jnp.where(d == 0, j, nsub - 1 - j) * c, c)
        rs = pl.ds(off, c)
        row_sl.append(rs)
        lw = lw_ref[0, 0, rs, :]
        cum = _dot_exact_lhs(tri, lw)
        tot = jnp.sum(lw, axis=0, keepdims=True)
        e_m = jnp.exp(-cum)
        e_t = jnp.exp(tot - cum)
        e_tot.append(jnp.exp(tot))
        b_in = b_ref[0, 0, rs, :]
        k_in = kd_ref[0, 0, rs, :]
        alpha = (a_ref[0, rs, :] * jnp.exp(cum - lw)).astype(BF16)
        rho_f = r_ref[0, rs, :] * jnp.exp(cum)
        rho = rho_f.astype(BF16)
        beta = (b_in * e_m).astype(BF16)
        kappa = (k_in * e_m).astype(BF16)
        beta_c = (b_in * e_t).astype(BF16)
        kappa_c = (k_in * e_t).astype(BF16)
        v_b = v_ref[0, rs, :]
        for sl in heads:
            al.append(alpha[:, sl])
            rh.append(rho[:, sl])
            rf.append(rho_f[:, sl])
            be.append(beta[:, sl])
            ka.append(kappa[:, sl])
            bc.append(beta_c[:, sl])
            kc.append(kappa_c[:, sl])
            vb.append(v_b[:, sl])

    units = range(nsub * RW_HEADS)
    cat = lambda x, y: jnp.concatenate([x, y], axis=0)
    zeros = jnp.zeros((c, RW_DIM), BF16)
    prod = [jnp.where(m_quad, _dot_nt(cat(al[u], rh[u]), cat(be[u], ka[u])), 0.0) for u in units]
    l_ab = [prod[u][0:c, 0:c] for u in units]
    top = [prod[u][0:c].astype(BF16) for u in units]
    bot = [prod[u][c:2 * c].astype(BF16) for u in units]
    l_d = [jnp.where(same_half, l_ab[u], 0.0) for u in units]
    l_o = [(l_ab[u] - l_d[u]).astype(BF16) for u in units]
    ld_b = [l_d[u].astype(BF16) for u in units]
    pw = [_dot(ld_b[u], ld_b[u]).astype(BF16) for u in units]
    td = [eye + l_d[u] for u in units]
    for _ in range(int(math.log2(c)) - 3):
        both = [_dot(cat(td[u].astype(BF16), pw[u]), pw[u]) for u in units]
        td = [td[u] + both[u][0:c] for u in units]
        pw = [both[u][c:2 * c].astype(BF16) for u in units]
    td = [td[u] + _dot(td[u].astype(BF16), pw[u]) for u in units]
    td_b = [td[u].astype(BF16) for u in units]
    x_o = [_dot(td_b[u], l_o[u]).astype(BF16) for u in units]
    tinv = [(td[u] + _dot(x_o[u], td_b[u])).astype(BF16) for u in units]
    akv = [_dot(top[u], cat(zeros, vb[u])).astype(BF16) for u in units]
    lcat = lambda x, y: jnp.concatenate([x, y], axis=1)
    av = [_dot(tinv[u], lcat(al[u], akv[u])).astype(BF16) for u in units]
    ry = [_dot(bot[u], cat(av[u], lcat(zeros, vb[u]))) for u in units]
    r_hat = [(rf[u] + ry[u][:, 0:RW_DIM]).astype(BF16) for u in units]
    y_hat = [ry[u][:, RW_DIM:2 * RW_DIM] for u in units]
    qn = [_dot_tn(av[u], bc[u]) for u in units]
    q_mat = [qn[u][0:RW_DIM].astype(BF16) for u in units]
    n_mat = [qn[u][RW_DIM:2 * RW_DIM] + _dot_tn(vb[u], kc[u]) for u in units]

    state = [s_ref[h] for h in range(RW_HEADS)]
    for j in range(nsub):
        for h, sl in enumerate(heads):
            u = j * RW_HEADS + h
            s_b = state[h].astype(BF16)
            y_ref[0, 0, row_sl[j], sl] = _dot_nt(r_hat[u], s_b) + y_hat[u]
            state[h] = state[h] * e_tot[j][:, sl] + _dot(s_b, q_mat[u]) + n_mat[u]
    for h in range(RW_HEADS):
        s_ref[h] = state[h]


def _scan_chunk_index(d, s, n_ctx_chunks, n_chunks):
    rev = jnp.where(s < n_ctx_chunks, n_ctx_chunks - 1 - s, n_chunks + n_ctx_chunks - 1 - s)
    return jnp.where(d == 0, s, rev)


def _direction_masks(c):
    i = np.arange(c)
    lower = (i[None, :] <= i[:, None]).astype(np.float32)
    tri = np.stack([lower, lower.T])
    strict = np.stack([lower - np.eye(c, dtype=np.float32), lower.T - np.eye(c, dtype=np.float32)])
    return tri, strict


def rwkv_scan(r, v, a, lw, kd, b, n_ctx):
    bsz, lt, w = r.shape
    c = RW_CHUNK
    blk = RW_NSUB * c
    nck = lt // blk
    ncc = n_ctx // blk
    tri, strict = _direction_masks(c)
    shared = pl.BlockSpec((1, blk, w), lambda d, bi, s: (bi, _scan_chunk_index(d, s, ncc, nck), 0))
    perdir = pl.BlockSpec((1, 1, blk, w), lambda d, bi, s: (d, bi, _scan_chunk_index(d, s, ncc, nck), 0))
    mask_spec = pl.BlockSpec((1, c, c), lambda d, bi, s: (d, 0, 0))
    quad_spec = pl.BlockSpec((1, 2 * c, 2 * c), lambda d, bi, s: (d, 0, 0))
    quad = np.concatenate([np.tile(strict, (1, 1, 2)), np.tile(tri, (1, 1, 2))], axis=1)
    return pl.pallas_call(
        _rwkv_kernel,
        grid=(2, bsz, nck),
        in_specs=[mask_spec, quad_spec, shared, shared, shared, perdir, perdir, perdir],
        out_specs=perdir,
        out_shape=jax.ShapeDtypeStruct((2, bsz, lt, w), F32),
        scratch_shapes=[pltpu.VMEM((RW_HEADS, RW_DIM, RW_DIM), F32)],
        compiler_params=_cparams(3),
        name="rwkv_scan",
    )(jnp.asarray(tri, BF16), jnp.asarray(quad, F32), r, v, a, lw, kd, b)


def _finish_rwkv_kernel(y_ref, bonus_ref, g_ref, lnw_ref, lnb_ref, g64_ref, o_ref):
    y = y_ref[0, 0] + y_ref[1, 0]
    g64 = g64_ref[...]
    mean = _dot_exact_rhs(y, g64) * (1.0 / RW_DIM)
    yc = y - mean
    var = _group_sum(yc * yc, g64) * (1.0 / RW_DIM)
    yn = yc * lax.rsqrt(var + RW_GN_EPS) * lnw_ref[...] + lnb_ref[...]
    o_ref[0] = ((yn + bonus_ref[0]) * _silu(g_ref[0])).astype(BF16)


def finish_rwkv(y, bonus, p3, ln_w, ln_b):
    _, bsz, lt, w = y.shape
    tm = _tile_rows(lt)
    tok = pl.BlockSpec((1, tm, w), lambda b, i: (b, i, 0))
    vec = _const_spec((1, w))
    return pl.pallas_call(
        _finish_rwkv_kernel,
        grid=(bsz, lt // tm),
        in_specs=[pl.BlockSpec((2, 1, tm, w), lambda b, i: (0, b, i, 0)), tok, _row_spec(CB_BG, tm),
                  vec, vec, _const_spec((w, w))],
        out_specs=tok,
        out_shape=jax.ShapeDtypeStruct((bsz, lt, w), BF16),
        compiler_params=_cparams(2),
        name="finish_rwkv",
    )(y, bonus, p3, ln_w.reshape(1, w), ln_b.reshape(1, w), _group_matrix(w, RW_DIM))


def _prep_ssd_kernel(x_ref, xp_ref, xn_ref, bc_ref, bcp_ref, bcn_ref, m_ref, cw_ref, cb_ref, dtb_ref,
                     aneg_ref, exp_ref, dsk_ref,
                     xq_ref, bco_ref, bt_ref, a_ref, at_ref, dskip_ref, *, n_tiles, n_ctx_tiles):
    i = pl.program_id(1)
    w = BRANCH_W

    def conv(x_ref, p_ref, n_ref, half):
        x = x_ref[0]
        xp, xn = _neighbours(x, p_ref[0], n_ref[0], i, n_tiles, n_ctx_tiles)
        lo = half * w
        y = (xp * cw_ref[0:1, lo:lo + w] + x * cw_ref[1:2, lo:lo + w] + xn * cw_ref[2:3, lo:lo + w]
             + cb_ref[:, lo:lo + w])
        return _silu(y)

    xs = conv(x_ref, xp_ref, xn_ref, 0)
    bc = conv(bc_ref, bcp_ref, bcn_ref, 1)
    bco_ref[0] = bc.astype(BF16)
    bm_f = bc[:, 0:SSM_GROUPS * SSM_STATE]
    q = SSM_CHUNK
    for j in range(bc.shape[0] // q):
        bt_ref[0, j] = bm_f[j * q:(j + 1) * q, :].T.astype(BF16)
    dt = _softplus(m_ref[0][:, MISC_DT:MISC_DT + LANES] + dtb_ref[...])
    lane = lax.broadcasted_iota(jnp.int32, (1, LANES), 1)
    dt = jnp.where(lane < 2 * SSM_HEADS, dt, 0.0)
    dtx = _dot_exact_rhs(dt, exp_ref[...])
    a_all = dt * aneg_ref[...]
    first = lane < SSM_HEADS
    a_dirs = [jnp.where(first, a_all, 0.0), jnp.where(first, pltpu.roll(a_all, LANES - SSM_HEADS, 1), 0.0)]
    for d in range(2):
        xq_ref[d, 0] = xs * dtx[:, d * w:(d + 1) * w]
        a_ref[d, 0] = a_dirs[d]
        for j in range(bc.shape[0] // q):
            at_ref[d, 0, j] = a_dirs[d][j * q:(j + 1) * q, :].T[0:2 * SUBLANES, :]
    dskip_ref[0] = xs * dsk_ref[...]


def prep_ssd(p3, conv_w, conv_b, dt_bias, a_log, d_skip, n_ctx):
    bsz, lt, _ = p3.shape
    tm = PREP_TM
    w = BRANCH_W
    n_tiles = lt // tm
    nh2 = 2 * SSM_HEADS
    pad_lanes = lambda v: jnp.pad(v.reshape(1, nh2), ((0, 0), (0, LANES - nh2)))
    expand = np.zeros((LANES, 2 * w), np.float32)
    for d in range(2):
        for h in range(SSM_HEADS):
            expand[d * SSM_HEADS + h, d * w + h * SSM_HEAD_DIM:d * w + (h + 1) * SSM_HEAD_DIM] = 1.0
    specs = [_row_spec(CB_CX), *_halo_specs(CB_CX, tm, lt), _row_spec(CB_CBC), *_halo_specs(CB_CBC, tm, lt),
             _row_spec(CB_MISC)]
    tok = pl.BlockSpec((1, tm, w), lambda b, i: (b, i, 0))
    gs = SSM_GROUPS * SSM_STATE
    kern = functools.partial(_prep_ssd_kernel, n_tiles=n_tiles, n_ctx_tiles=n_ctx // tm)
    return pl.pallas_call(
        kern,
        grid=(bsz, n_tiles),
        in_specs=specs + [_const_spec((3, 2 * w)), _const_spec((1, 2 * w)), _const_spec((1, LANES)),
                          _const_spec((1, LANES)), _const_spec((LANES, 2 * w)), _const_spec((1, w))],
        out_specs=[pl.BlockSpec((2, 1, tm, w), lambda b, i: (0, b, i, 0)), tok,
                   pl.BlockSpec((1, tm // SSM_CHUNK, gs, SSM_CHUNK), lambda b, i: (b, i, 0, 0)),
                   pl.BlockSpec((2, 1, tm, LANES), lambda b, i: (0, b, i, 0)),
                   pl.BlockSpec((2, 1, tm // SSM_CHUNK, 2 * SUBLANES, SSM_CHUNK), lambda b, i: (0, b, i, 0, 0)), tok],
        out_shape=[jax.ShapeDtypeStruct((2, bsz, lt, w), F32),
                   jax.ShapeDtypeStruct((bsz, lt, w), BF16),
                   jax.ShapeDtypeStruct((bsz, lt // SSM_CHUNK, gs, SSM_CHUNK), BF16),
                   jax.ShapeDtypeStruct((2, bsz, lt, LANES), F32),
                   jax.ShapeDtypeStruct((2, bsz, lt // SSM_CHUNK, 2 * SUBLANES, SSM_CHUNK), F32),
                   jax.ShapeDtypeStruct((bsz, lt, w), F32)],
        compiler_params=_cparams(2),
        name="prep_ssd",
    )(*([p3] * 7), conv_w, conv_b.reshape(1, 2 * w), pad_lanes(dt_bias), pad_lanes(-jnp.exp(a_log)),
      jnp.asarray(expand, BF16), jnp.repeat(d_skip, SSM_HEAD_DIM).reshape(1, w))


def _ssd_kernel(tri_ref, mi_ref, xq_ref, bc_ref, bt_ref, a_ref, at_ref, y_ref, s_ref):
    q = SSM_CHUNK
    nsub = bt_ref.shape[1]
    rep = SSM_HEADS // SSM_GROUPS
    gw = SSM_GROUPS * SSM_STATE
    p = SSM_HEAD_DIM
    d = pl.program_id(0)

    @pl.when(pl.program_id(2) == 0)
    def _():
        s_ref[...] = jnp.zeros_like(s_ref)

    tri = tri_ref[0]
    mask = mi_ref[0] > 0.5
    hs = range(SSM_HEADS)
    groups = [slice(g * SSM_STATE, (g + 1) * SSM_STATE) for g in range(SSM_GROUPS)]

    rows, cms, y_in, e_col, e_tot, upd = [], [], [], [], [], []
    for j in range(nsub):
        ch = jnp.where(d == 0, j, nsub - 1 - j)
        rs = pl.ds(pl.multiple_of(ch * q, q), q)
        rows.append(rs)
        a = a_ref[0, 0, rs, :]
        a_t = at_ref[0, 0, ch]
        xq = xq_ref[0, 0, rs, :]
        bm = bc_ref[0, rs, 0:gw]
        cm = bc_ref[0, rs, gw:2 * gw]
        bt = bt_ref[0, ch]
        acol = _dot_exact_lhs(tri, a)
        arow = _dot_nt_exact_lhs_f32(a_t, tri)
        tot = jnp.sum(a, axis=0, keepdims=True)
        cm_g = [cm[:, gs] for gs in groups]
        cb = [_dot_nt(cm_g[g], bm[:, gs]) for g, gs in enumerate(groups)]
        ac = [jnp.broadcast_to(acol[:, h:h + 1], (q, q)) for h in hs]
        ar = [jnp.broadcast_to(arow[h:h + 1, :], (q, q)) for h in hs]
        gmat = [(cb[h // rep] * jnp.exp(jnp.where(mask, ac[h] - ar[h], NEG_INF))).astype(BF16) for h in hs]
        xh = [xq[:, h * p:(h + 1) * p] for h in hs]
        tot_h = [jnp.broadcast_to(tot[:, h:h + 1], (1, p)) for h in hs]
        xd = [(xh[h] * jnp.exp(tot_h[h] - ac[h][:, :p])).astype(BF16) for h in hs]
        cms.append(cm_g)
        y_in.append([_dot(gmat[h], xh[h].astype(BF16)) for h in hs])
        e_col.append([jnp.exp(ac[h][:, :p]) for h in hs])
        e_tot.append([jnp.exp(tot_h[h]) for h in hs])
        upd.append([_dot(bt[groups[h // rep], :], xd[h]) for h in hs])

    state = [s_ref[h] for h in hs]
    for j in range(nsub):
        y_st = [_dot(cms[j][h // rep], state[h].astype(BF16)) for h in hs]
        for h in hs:
            y_ref[0, 0, rows[j], h * p:(h + 1) * p] = y_in[j][h] + y_st[h] * e_col[j][h]
            state[h] = state[h] * e_tot[j][h] + upd[j][h]
    for h in hs:
        s_ref[h] = state[h]


def ssd_scan(xq, bc, bt, a, a_t, n_ctx):
    _, bsz, lt, w = xq.shape
    q = SSM_CHUNK
    blk = SSM_NSUB * q
    nck = lt // blk
    ncc = n_ctx // blk
    gw = bt.shape[2]
    tri, _ = _direction_masks(q)

    def cidx(d, s):
        return _scan_chunk_index(d, s, ncc, nck)

    mask_spec = pl.BlockSpec((1, q, q), lambda d, bi, s: (d, 0, 0))
    return pl.pallas_call(
        _ssd_kernel,
        grid=(2, bsz, nck),
        in_specs=[mask_spec, mask_spec,
                  pl.BlockSpec((1, 1, blk, w), lambda d, bi, s: (d, bi, cidx(d, s), 0)),
                  pl.BlockSpec((1, blk, w), lambda d, bi, s: (bi, cidx(d, s), 0)),
                  pl.BlockSpec((1, SSM_NSUB, gw, q), lambda d, bi, s: (bi, cidx(d, s), 0, 0)),
                  pl.BlockSpec((1, 1, blk, LANES), lambda d, bi, s: (d, bi, cidx(d, s), 0)),
                  pl.BlockSpec((1, 1, SSM_NSUB, 2 * SUBLANES, q), lambda d, bi, s: (d, bi, cidx(d, s), 0, 0))],
        out_specs=pl.BlockSpec((1, 1, blk, w), lambda d, bi, s: (d, bi, cidx(d, s), 0)),
        out_shape=jax.ShapeDtypeStruct((2, bsz, lt, w), F32),
        scratch_shapes=[pltpu.VMEM((SSM_HEADS, SSM_STATE, SSM_HEAD_DIM), F32)],
        compiler_params=_cparams(3),
        name="ssd_scan",
    )(jnp.asarray(tri, BF16), jnp.asarray(tri, F32), xq, bc, bt, a, a_t)


def _finish_ssd_kernel(y_ref, dskip_ref, z_ref, nw_ref, g256_ref, o_ref):
    y = y_ref[0, 0] + y_ref[1, 0] + dskip_ref[0]
    g = y * _silu(z_ref[0])
    group = BRANCH_W // SSM_GROUPS
    o_ref[0] = _head_rms(g, g256_ref[...], group, nw_ref[...]).astype(BF16)


def finish_ssd(y, dskip, p3, norm_w):
    _, bsz, lt, w = y.shape
    tm = _tile_rows(lt)
    tok = pl.BlockSpec((1, tm, w), lambda b, i: (b, i, 0))
    return pl.pallas_call(
        _finish_ssd_kernel,
        grid=(bsz, lt // tm),
        in_specs=[pl.BlockSpec((2, 1, tm, w), lambda b, i: (0, b, i, 0)), tok, _row_spec(CB_CZ, tm),
                  _const_spec((1, w)), _const_spec((w, w))],
        out_specs=tok,
        out_shape=jax.ShapeDtypeStruct((bsz, lt, w), BF16),
        compiler_params=_cparams(2),
        name="finish_ssd",
    )(y, dskip, p3, norm_w.reshape(1, w), _group_matrix(w, w // SSM_GROUPS))


def _permute_w_in(w_in_l):
    w = BRANCH_W
    b0 = 4 * w
    lora0 = b0 + 3 * w
    bg0 = lora0 + MISC_DT
    c0 = bg0 + w
    dt0 = c0 + SSM_CONV_CH
    z0 = dt0 + 2 * SSM_HEADS
    d0 = z0 + w
    end = d0 + 4 * w
    parts = [w_in_l[:, 0:lora0], w_in_l[:, bg0:c0], w_in_l[:, c0:dt0], w_in_l[:, z0:d0], w_in_l[:, d0:end],
             w_in_l[:, lora0:bg0], w_in_l[:, dt0:z0]]
    used = sum(p.shape[1] for p in parts)
    pad = jnp.zeros((w_in_l.shape[0], IN_W_PAD - used), BF16)
    return jnp.concatenate([p.astype(BF16) for p in parts] + [pad], axis=1)


def kernel(x, c, ctx, c_ctx, norm_w, w_ada, b_ada, w_in, na_q_norm, na_k_norm, na_rpb, rw_mu, rw_w0, rw_w2, rw_a0, rw_a2, rw_k_k, rw_k_a, rw_r_k, rw_ln_w, rw_ln_b, ssm_conv_w, ssm_conv_b, ssm_dt_bias, ssm_A_log, ssm_D, ssm_norm_w, da_q_norm, da_k_norm, da_lq1, da_lk1, da_lq2, da_lk2, da_subln, w_gate, w_up, w_out):
    bsz, seq, d = x.shape
    n_ctx = ctx.shape[1]
    lt = n_ctx + seq
    depth = w_in.shape[0]
    assert n_ctx == CTX_LEN == PREP_TM == DA_TK and seq % (GRID_W * NA_WIN_R) == 0
    assert ((lt // DA_TK) - 1) % DA_NSUB == 0 and n_ctx % NORM_ROWS == 0
    cond = jnp.concatenate([_silu(c), _silu(c_ctx)[None], jnp.zeros((SUBLANES - bsz - 1, d), F32)], axis=0)
    xs2 = jnp.concatenate([ctx, x], axis=1).reshape(bsz * lt, d)
    cos, sin = rope_tables(lt, n_ctx)
    for l in range(depth):
        lam_init = 0.8 - 0.6 * math.exp(-0.3 * l)
        mod = small_matmul(cond, w_ada[l], b_ada[l])
        shift, scale, gate = jnp.split(mod[:bsz], 3, axis=-1)
        shift_c, scale_c, gate_c = jnp.split(mod[bsz], 3, axis=-1)
        p2, h = in_projection(xs2, norm_w[l], scale, shift, scale_c, shift_c, _permute_w_in(w_in[l]), lt, n_ctx)
        p3 = p2.reshape(bsz, lt, IN_W_PAD)

        kn, va, qt, kh, vt = prep_attention(p3, cos, sin, na_k_norm[l], da_q_norm[l], da_k_norm[l])
        oa = neighbourhood_attention(p3, kn, va, na_bias_table(na_rpb[l]), na_q_norm[l], n_ctx)
        lam = jnp.exp(jnp.sum(da_lq1[l] * da_lk1[l])) - jnp.exp(jnp.sum(da_lq2[l] * da_lk2[l])) + lam_init
        od = flash_diff_attention(qt, kh, vt, p3, lam, da_subln[l], lam_init)

        r, vb, a, lw, kd, b, bonus = prep_rwkv(p3, rw_mu[l], rw_w0[l], rw_w2[l], rw_a0[l], rw_a2[l],
                                              rw_k_k[l], rw_k_a[l], rw_r_k[l].reshape(-1), n_ctx)
        ob = finish_rwkv(rwkv_scan(r, vb, a, lw, kd, b, n_ctx), bonus, p3, rw_ln_w[l], rw_ln_b[l])

        xq, bc, bt, sa, sat, dskip = prep_ssd(p3, ssm_conv_w[l], ssm_conv_b[l], ssm_dt_bias[l], ssm_A_log[l],
                                              ssm_D[l], n_ctx)
        om = finish_ssd(ssd_scan(xq, bc, bt, sa, sat, n_ctx), dskip, p3, ssm_norm_w[l])

        ys = [t.reshape(bsz * lt, BRANCH_W) for t in (oa, ob, om, od)]
        acc = gate_up(h, ys, w_gate[l].astype(BF16), w_up[l].astype(BF16), lt)
        xs2 = out_projection(acc, w_out[l].astype(BF16), xs2, gate, gate_c, lt, n_ctx)
    return xs2.reshape(bsz, lt, d)[:, n_ctx:]
```
